```python
import math
import jax, jax.numpy as jnp
from jax import lax
import numpy as np

D_MODEL = 1024
BATCH = 8
SEQ = 4096
DEPTH = 1
DEC_BATCH = 128
DEC_SEQ = 4
PAST_LEN = 16384
PAGE_SIZE = 128

MIX_W = D_MODEL
SSM_W = D_MODEL // 2
SSM_GC = 16
SSM_G = SSM_W // SSM_GC
SSM_P = 64
N_HEADS = 8
NOPE_DIM = 64
ROPE_DIM = 32
QK_DIM = NOPE_DIM + ROPE_DIM
V_DIM = (MIX_W - SSM_W) // N_HEADS
Q_LORA = 384
KV_LORA = 256
ROPE_THETA = 10000.0
IN_W = SSM_W + Q_LORA + KV_LORA + ROPE_DIM
N_EXPERTS = 32
TOP_K = 4
D_FF = D_MODEL
SWIGLU_ALPHA = 1.702
SWIGLU_LIMIT = 7.0
MOE_BLOCK = 128
PLE_DIM = 256
Q_BLOCK = 128
EPS = 1e-6
NEG = -1e30

kernel_name = "hybrid_s5_mla_moe_step"


def rmsnorm(x, g):
    xf = x.astype(jnp.float32)
    r = lax.rsqrt(jnp.mean(xf * xf, axis=-1, keepdims=True) + EPS)
    return (xf * r * g.astype(jnp.float32)).astype(x.dtype)


def rope_angles(pos):
    inv = ROPE_THETA ** (-jnp.arange(0, ROPE_DIM, 2, dtype=jnp.float32) / ROPE_DIM)
    ang = pos.astype(jnp.float32)[:, None] * inv[None, :]
    return jnp.cos(ang), jnp.sin(ang)


def apply_rope(x, cos, sin):
    half = ROPE_DIM // 2
    x1 = x[..., :half].astype(jnp.float32)
    x2 = x[..., half:].astype(jnp.float32)
    return jnp.concatenate([x1 * cos - x2 * sin, x1 * sin + x2 * cos], axis=-1).astype(x.dtype)


def project_in(h, g_mix, w_in):
    z = rmsnorm(h, g_mix) @ w_in
    return jnp.split(z, [SSM_W, SSM_W + Q_LORA, SSM_W + Q_LORA + KV_LORA], axis=-1)


def complex_affine_combine(e1, e2):
    a1r, a1i, b1r, b1i = e1
    a2r, a2i, b2r, b2i = e2
    return (a2r * a1r - a2i * a1i, a2r * a1i + a2i * a1r,
            a2r * b1r - a2i * b1i + b2r, a2r * b1i + a2i * b1r + b2i)


def s5_mixer(u, h0_re, h0_im, a_re, a_im, log_dt, b_re, b_im, c_re, c_im, d_skip, w_glu, b_glu):
    f32 = jnp.float32
    bsz, t = u.shape[0], u.shape[1]
    dt = jnp.exp(log_dt.astype(f32))[:, None]
    ar, ai = a_re.astype(f32), a_im.astype(f32)
    mag = jnp.exp(dt * ar)
    abar_re, abar_im = mag * jnp.cos(dt * ai), mag * jnp.sin(dt * ai)
    den = ar * ar + ai * ai
    nr, ni = abar_re - 1.0, abar_im
    coef_re = (nr * ar + ni * ai) / den
    coef_im = (ni * ar - nr * ai) / den
    br, bi = b_re.astype(f32), b_im.astype(f32)
    bbar_re = coef_re[..., None] * br - coef_im[..., None] * bi
    bbar_im = coef_re[..., None] * bi + coef_im[..., None] * br
    uf = u.astype(f32)
    ug = uf.reshape(bsz, t, SSM_G, SSM_GC)
    x_re = jnp.einsum('btgc,gpc->btgp', ug, bbar_re)
    x_im = jnp.einsum('btgc,gpc->btgp', ug, bbar_im)
    x_re = x_re.at[:, 0].add(abar_re * h0_re - abar_im * h0_im)
    x_im = x_im.at[:, 0].add(abar_re * h0_im + abar_im * h0_re)
    a_re_t = jnp.broadcast_to(abar_re, x_re.shape)
    a_im_t = jnp.broadcast_to(abar_im, x_im.shape)
    _, _, s_re, s_im = lax.associative_scan(complex_affine_combine, (a_re_t, a_im_t, x_re, x_im), axis=1)
    y = (jnp.einsum('btgp,gcp->btgc', s_re, c_re.astype(f32))
         - jnp.einsum('btgp,gcp->btgc', s_im, c_im.astype(f32)))
    y = y.reshape(bsz, t, SSM_W) + d_skip.astype(f32) * uf
    g = jax.nn.gelu(y)
    out = g * jax.nn.sigmoid(g @ w_glu.astype(f32) + b_glu.astype(f32))
    return out.astype(u.dtype), s_re[:, -1], s_im[:, -1]


def mla_queries(cq, cos, sin, g_q_lora, w_uq, g_qk_q):
    bsz, t = cq.shape[0], cq.shape[1]
    q = (rmsnorm(cq, g_q_lora) @ w_uq).reshape(bsz, t, N_HEADS, QK_DIM)
    q = jnp.concatenate([q[..., :NOPE_DIM], apply_rope(q[..., NOPE_DIM:], cos[:, None, :], sin[:, None, :])], axis=-1)
    return rmsnorm(q, g_qk_q)


def mla_key_rows(ckv, kr, cos, sin, g_kv_lora, w_uk):
    c = rmsnorm(ckv, g_kv_lora)
    kr = apply_rope(kr, cos, sin)
    k_nope = jnp.einsum('btc,chd->bthd', c, w_uk)
    k = jnp.concatenate([k_nope, jnp.broadcast_to(kr[:, :, None, :], kr.shape[:2] + (N_HEADS, ROPE_DIM))], axis=-1)
    kf = k.astype(jnp.float32)
    rstd = lax.rsqrt(jnp.mean(kf * kf, axis=-1) + EPS)
    return c, kr, k, rstd


def attend_prompt(q, k, rstd, g_qk_k, c, w_uv):
    bsz, t = q.shape[0], q.shape[1]
    scale = 1.0 / math.sqrt(QK_DIM)
    k_hat = (k.astype(jnp.float32) * rstd[..., None] * g_qk_k.astype(jnp.float32)).astype(q.dtype)
    v = jnp.einsum('btc,chd->bthd', c, w_uv)
    nb = t // Q_BLOCK
    qb = q.reshape(bsz, nb, Q_BLOCK, N_HEADS, QK_DIM).transpose(1, 0, 2, 3, 4)
    k_pos = jnp.arange(t)

    def block(args):
        q_i, i = args
        q_pos = i * Q_BLOCK + jnp.arange(Q_BLOCK)
        s = jnp.einsum('bqhd,bkhd->bhqk', q_i, k_hat).astype(jnp.float32) * scale
        s = jnp.where(k_pos[None, :] <= q_pos[:, None], s, NEG)
        p = jax.nn.softmax(s, axis=-1)
        return jnp.einsum('bhqk,bkhd->bqhd', p.astype(v.dtype), v)

    o = lax.map(block, (qb, jnp.arange(nb)))
    return o.transpose(1, 0, 2, 3, 4).reshape(bsz, t, N_HEADS * V_DIM)


def attend_sample(q, c_new, kr_new, rstd_new, cache_c, cache_kr, cache_rstd, page_table, g_qk_k, w_uk, w_uv):
    bsz, s_len = q.shape[0], q.shape[1]
    scale = 1.0 / math.sqrt(QK_DIM)
    qf = q.astype(jnp.float32) * g_qk_k.astype(jnp.float32)
    q_lat = jnp.einsum('bshd,chd->bshc', qf[..., :NOPE_DIM], w_uk.astype(jnp.float32))
    q_rp = qf[..., NOPE_DIM:]
    causal = jnp.arange(s_len)[None, :] <= jnp.arange(s_len)[:, None]

    def one(args):
        ql, qr, cn, krn, rn, pt = args
        c_p = cache_c[pt].reshape(-1, KV_LORA).astype(jnp.float32)
        kr_p = cache_kr[pt].reshape(-1, ROPE_DIM).astype(jnp.float32)
        r_p = cache_rstd[pt].reshape(-1, N_HEADS).astype(jnp.float32)
        cn = cn.astype(jnp.float32)
        s_p = (jnp.einsum('qhc,kc->hqk', ql, c_p) + jnp.einsum('qhr,kr->hqk', qr, kr_p)) * r_p.T[:, None, :]
        s_n = (jnp.einsum('qhc,kc->hqk', ql, cn) + jnp.einsum('qhr,kr->hqk', qr, krn.astype(jnp.float32))) * rn.T[:, None, :]
        s_n = jnp.where(causal[None], s_n, NEG)
        p = jax.nn.softmax(jnp.concatenate([s_p, s_n], axis=-1) * scale, axis=-1)
        n_past = c_p.shape[0]
        return jnp.einsum('hqk,kc->qhc', p[..., :n_past], c_p) + jnp.einsum('hqk,kc->qhc', p[..., n_past:], cn)

    o_lat = lax.map(one, (q_lat, q_rp, c_new, kr_new, rstd_new, page_table))
    o = jnp.einsum('bshc,chd->bshd', o_lat, w_uv.astype(jnp.float32))
    return o.reshape(bsz, s_len, N_HEADS * V_DIM).astype(q.dtype)


def mix_out(o_ssm, o_mla, g_out_ssm, g_out_mla, w_out):
    o = jnp.concatenate([rmsnorm(o_ssm, g_out_ssm), rmsnorm(o_mla.astype(o_ssm.dtype), g_out_mla)], axis=-1)
    return o @ w_out


def moe(x, w_router, b_router, w_gate_up, b_gate_up, w_down, b_down):
    shp = x.shape
    xt = x.reshape(-1, D_MODEL)
    t = xt.shape[0]
    logits = xt.astype(jnp.float32) @ w_router.astype(jnp.float32) + b_router.astype(jnp.float32)
    top_v, top_i = lax.top_k(logits, TOP_K)
    gates = jax.nn.softmax(top_v, axis=-1)
    flat_e = top_i.reshape(-1)
    flat_t = jnp.repeat(jnp.arange(t, dtype=jnp.int32), TOP_K)
    flat_g = gates.reshape(-1)
    order = jnp.argsort(flat_e)
    se, st, sg = flat_e[order], flat_t[order], flat_g[order]
    counts = jnp.bincount(flat_e, length=N_EXPERTS)
    starts = jnp.cumsum(counts) - counts
    padded = (counts + MOE_BLOCK - 1) // MOE_BLOCK * MOE_BLOCK
    pad_ends = jnp.cumsum(padded)
    pad_starts = pad_ends - padded
    n_blocks = -(-(t * TOP_K) // MOE_BLOCK) + N_EXPERTS
    n_rows = n_blocks * MOE_BLOCK
    dest = pad_starts[se] + jnp.arange(t * TOP_K) - starts[se]
    tok_buf = jnp.zeros((n_rows,), jnp.int32).at[dest].set(st)
    gate_buf = jnp.zeros((n_rows,), jnp.float32).at[dest].set(sg)
    blk_e = jnp.minimum(jnp.searchsorted(pad_ends, jnp.arange(n_blocks) * MOE_BLOCK, side='right'), N_EXPERTS - 1)
    xb = xt[tok_buf].reshape(n_blocks, MOE_BLOCK, D_MODEL)

    def expert_block(args):
        xi, e = args
        hdn = xi @ w_gate_up[e] + b_gate_up[e]
        x_glu = jnp.minimum(hdn[..., :D_FF], SWIGLU_LIMIT)
        x_lin = jnp.clip(hdn[..., D_FF:], -SWIGLU_LIMIT, SWIGLU_LIMIT)
        act = x_glu * jax.nn.sigmoid(SWIGLU_ALPHA * x_glu) * (x_lin + 1.0)
        return act @ w_down[e] + b_down[e]

    yb = lax.map(expert_block, (xb, blk_e)).reshape(n_rows, D_MODEL)
    y = jnp.zeros((t, D_MODEL), jnp.float32).at[tok_buf].add(yb.astype(jnp.float32) * gate_buf[:, None])
    return y.reshape(shp).astype(x.dtype)


def channel_and_ple(h, p, g_ffn, moe_w, ple_w):
    h = h + moe(rmsnorm(h, g_ffn), *moe_w)
    g_ple, w_ple_gate, w_ple = ple_w
    gate = jax.nn.sigmoid(rmsnorm(h, g_ple) @ w_ple_gate)
    return h + (p @ w_ple) * gate


def setup_inputs(seed: int = 0) -> dict:
    key = jax.random.key(seed)
    ks = iter(jax.random.split(key, 64))
    f32 = jnp.float32
    n_pages = PAST_LEN // PAGE_SIZE
    n_used = DEC_BATCH * n_pages
    n_pool = n_used + max(1, n_used // 4)

    def nrm(shape, scale):
        return jax.random.normal(next(ks), shape, f32) * scale

    def gain(shape):
        return 1.0 + nrm(shape, 0.02)

    n_idx = jnp.arange(SSM_P, dtype=f32)
    inp = {}
    inp['x_prompt'] = nrm((BATCH, SEQ, D_MODEL), 1.0)
    inp['x_sample'] = nrm((DEC_BATCH, DEC_SEQ, D_MODEL), 1.0)
    inp['cache_kv_latent'] = nrm((DEPTH, n_pool, PAGE_SIZE, KV_LORA), 1.0)
    inp['cache_k_rope'] = nrm((DEPTH, n_pool, PAGE_SIZE, ROPE_DIM), 1.0)
    inp['cache_k_rstd'] = jax.random.uniform(next(ks), (DEPTH, n_pool, PAGE_SIZE, N_HEADS), f32, 0.7, 1.3)
    inp['state_ssm_re'] = nrm((DEPTH, DEC_BATCH, SSM_G, SSM_P), 0.1)
    inp['state_ssm_im'] = nrm((DEPTH, DEC_BATCH, SSM_G, SSM_P), 0.1)
    inp['page_table'] = jax.random.permutation(next(ks), n_pool)[:n_used].reshape(DEC_BATCH, n_pages).astype(jnp.int32)
    inp['p_prompt'] = nrm((DEPTH, BATCH, SEQ, PLE_DIM), 1.0)
    inp['p_sample'] = nrm((DEPTH, DEC_BATCH, DEC_SEQ, PLE_DIM), 1.0)
    inp['g_mix'] = gain((DEPTH, D_MODEL))
    inp['w_in'] = nrm((DEPTH, D_MODEL, IN_W), D_MODEL ** -0.5)
    inp['s5_a_re'] = -0.5 + nrm((DEPTH, SSM_G, SSM_P), 0.01)
    inp['s5_a_im'] = math.pi * n_idx + nrm((DEPTH, SSM_G, SSM_P), 0.01)
    inp['s5_log_dt'] = jax.random.uniform(next(ks), (DEPTH, SSM_G), f32, math.log(1e-3), math.log(1e-1))
    inp['s5_b_re'] = nrm((DEPTH, SSM_G, SSM_P, SSM_GC), (2 * SSM_GC) ** -0.5)
    inp['s5_b_im'] = nrm((DEPTH, SSM_G, SSM_P, SSM_GC), (2 * SSM_GC) ** -0.5)
    inp['s5_c_re'] = nrm((DEPTH, SSM_G, SSM_GC, SSM_P), (2 * SSM_P) ** -0.5)
    inp['s5_c_im'] = nrm((DEPTH, SSM_G, SSM_GC, SSM_P), (2 * SSM_P) ** -0.5)
    inp['s5_d'] = nrm((DEPTH, SSM_W), 1.0)
    inp['w_glu'] = nrm((DEPTH, SSM_W, SSM_W), SSM_W ** -0.5)
    inp['b_glu'] = nrm((DEPTH, SSM_W), 0.01)
    inp['g_q_lora'] = gain((DEPTH, Q_LORA))
    inp['w_uq'] = nrm((DEPTH, Q_LORA, N_HEADS * QK_DIM), Q_LORA ** -0.5)
    inp['g_kv_lora'] = gain((DEPTH, KV_LORA))
    inp['w_uk'] = nrm((DEPTH, KV_LORA, N_HEADS, NOPE_DIM), KV_LORA ** -0.5)
    inp['w_uv'] = nrm((DEPTH, KV_LORA, N_HEADS, V_DIM), KV_LORA ** -0.5)
    inp['g_qk_q'] = gain((DEPTH, QK_DIM))
    inp['g_qk_k'] = gain((DEPTH, QK_DIM))
    inp['g_out_ssm'] = gain((DEPTH, SSM_W))
    inp['g_out_mla'] = gain((DEPTH, N_HEADS * V_DIM))
    inp['w_out'] = nrm((DEPTH, MIX_W, D_MODEL), MIX_W ** -0.5)
    inp['g_ffn'] = gain((DEPTH, D_MODEL))
    inp['w_router'] = nrm((DEPTH, D_MODEL, N_EXPERTS), D_MODEL ** -0.5)
    inp['b_router'] = nrm((DEPTH, N_EXPERTS), 0.01)
    inp['w_gate_up'] = nrm((DEPTH, N_EXPERTS, D_MODEL, 2 * D_FF), D_MODEL ** -0.5)
    inp['b_gate_up'] = nrm((DEPTH, N_EXPERTS, 2 * D_FF), 0.01)
    inp['w_down'] = nrm((DEPTH, N_EXPERTS, D_FF, D_MODEL), D_FF ** -0.5)
    inp['b_down'] = nrm((DEPTH, N_EXPERTS, D_MODEL), 0.01)
    inp['g_ple'] = gain((DEPTH, D_MODEL))
    inp['w_ple_gate'] = nrm((DEPTH, D_MODEL, D_MODEL), D_MODEL ** -0.5)
    inp['w_ple'] = nrm((DEPTH, PLE_DIM, D_MODEL), PLE_DIM ** -0.5)
    return inp


def reference(x_prompt, x_sample, cache_kv_latent, cache_k_rope, cache_k_rstd, state_ssm_re, state_ssm_im,
              page_table, p_prompt, p_sample, g_mix, w_in, s5_a_re, s5_a_im, s5_log_dt, s5_b_re, s5_b_im,
              s5_c_re, s5_c_im, s5_d, w_glu, b_glu, g_q_lora, w_uq, g_kv_lora, w_uk, w_uv, g_qk_q, g_qk_k,
              g_out_ssm, g_out_mla, w_out, g_ffn, w_router, b_router, w_gate_up, b_gate_up, w_down, b_down,
              g_ple, w_ple_gate, w_ple):
    cos_p, sin_p = rope_angles(jnp.arange(x_prompt.shape[1]))
    cos_s, sin_s = rope_angles(PAST_LEN + jnp.arange(x_sample.shape[1]))
    h_p, h_s = x_prompt, x_sample
    c_pl, kr_pl, rs_pl, sr_pl, si_pl = [], [], [], [], []
    c_sl, kr_sl, rs_sl, sr_sl, si_sl = [], [], [], [], []
    for i in range(DEPTH):
        s5 = (s5_a_re[i], s5_a_im[i], s5_log_dt[i], s5_b_re[i], s5_b_im[i], s5_c_re[i], s5_c_im[i],
              s5_d[i], w_glu[i], b_glu[i])
        moe_w = (w_router[i], b_router[i], w_gate_up[i], b_gate_up[i], w_down[i], b_down[i])
        ple_w = (g_ple[i], w_ple_gate[i], w_ple[i])

        u, cq, ckv, kr = project_in(h_p, g_mix[i], w_in[i])
        h0 = jnp.zeros((h_p.shape[0], SSM_G, SSM_P), jnp.float32)
        o_ssm, sr_p, si_p = s5_mixer(u, h0, h0, *s5)
        q = mla_queries(cq, cos_p, sin_p, g_q_lora[i], w_uq[i], g_qk_q[i])
        c_p, krr_p, k_p, rstd_p = mla_key_rows(ckv, kr, cos_p, sin_p, g_kv_lora[i], w_uk[i])
        o_mla = attend_prompt(q, k_p, rstd_p, g_qk_k[i], c_p, w_uv[i])
        h_p = h_p + mix_out(o_ssm, o_mla, g_out_ssm[i], g_out_mla[i], w_out[i])
        h_p = channel_and_ple(h_p, p_prompt[i], g_ffn[i], moe_w, ple_w)
        c_pl.append(c_p); kr_pl.append(krr_p); rs_pl.append(rstd_p); sr_pl.append(sr_p); si_pl.append(si_p)

        u, cq, ckv, kr = project_in(h_s, g_mix[i], w_in[i])
        o_ssm, sr_s, si_s = s5_mixer(u, state_ssm_re[i].astype(jnp.float32), state_ssm_im[i].astype(jnp.float32), *s5)
        q = mla_queries(cq, cos_s, sin_s, g_q_lora[i], w_uq[i], g_qk_q[i])
        c_s, krr_s, _, rstd_s = mla_key_rows(ckv, kr, cos_s, sin_s, g_kv_lora[i], w_uk[i])
        o_mla = attend_sample(q, c_s, krr_s, rstd_s, cache_kv_latent[i], cache_k_rope[i], cache_k_rstd[i],
                              page_table, g_qk_k[i], w_uk[i], w_uv[i])
        h_s = h_s + mix_out(o_ssm, o_mla, g_out_ssm[i], g_out_mla[i], w_out[i])
        h_s = channel_and_ple(h_s, p_sample[i], g_ffn[i], moe_w, ple_w)
        c_sl.append(c_s); kr_sl.append(krr_s); rs_sl.append(rstd_s); sr_sl.append(sr_s); si_sl.append(si_s)

    return (h_p, h_s,
            jnp.stack(c_pl), jnp.stack(kr_pl), jnp.stack(rs_pl), jnp.stack(sr_pl), jnp.stack(si_pl),
            jnp.stack(c_sl), jnp.stack(kr_sl), jnp.stack(rs_sl), jnp.stack(sr_sl), jnp.stack(si_sl))
```

```python
import functools
import math

import jax
import jax.numpy as jnp
from jax import lax
from jax.experimental import pallas as pl
from jax.experimental.pallas import tpu as pltpu

D_MODEL = 1024
SSM_W = 512
SSM_GC = 16
SSM_G = SSM_W // SSM_GC
SSM_P = 64
SSM_S = SSM_G * SSM_P
N_HEADS = 8
NOPE_DIM = 64
ROPE_DIM = 32
ROPE_HALF = ROPE_DIM // 2
QK_DIM = NOPE_DIM + ROPE_DIM
V_DIM = 64
Q_LORA = 384
KV_LORA = 256
ROPE_THETA = 10000.0
N_EXPERTS = 32
TOP_K = 4
D_FF = D_MODEL
SWIGLU_ALPHA = 1.702
SWIGLU_LIMIT = 7.0
PLE_DIM = 256
PAST_LEN = 16384
PAGE_SIZE = 128
EPS = 1e-6
NEG = -1e30

LANES = 128
SUBLANES = 8
VMEM_LIMIT = 56 * 1024 * 1024

HP = N_HEADS * LANES
BF16 = jnp.bfloat16
F32 = jnp.float32


def _cparams(*sem):
    return pltpu.CompilerParams(dimension_semantics=sem, vmem_limit_bytes=VMEM_LIMIT)


def _rms(x, g):
    r = lax.rsqrt(jnp.mean(x * x, axis=-1, keepdims=True) + EPS)
    return x * r * g


def _dot(a, b):
    return jnp.dot(a, b, preferred_element_type=F32)


def _dot_nt(a, b):
    return lax.dot_general(a, b, (((1,), (1,)), ((), ())), preferred_element_type=F32)


def _full(shape):
    nd = len(shape)
    return pl.BlockSpec(shape, lambda *_: (0,) * nd)


def _proj_kernel(x_ref, ct_ref, st_ref, gmix_ref, win_ref, gql_ref, wuq_ref, gkv_ref, wuk_ref,
                 w2_ref, gq_ref, gk_ref, u_ref, q_ref, k2_ref, *rest, sample):
    if sample:
        c_ref, kr_ref, rstd_ref = rest
    else:
        v2_ref, c_ref, kr_ref, rstd_ref = rest
    xn = _rms(x_ref[...], gmix_ref[...]).astype(BF16)
    z = _dot(xn, win_ref[...])
    u_ref[...] = z[:, :SSM_W]
    o = SSM_W
    cq = z[:, o:o + Q_LORA]
    o += Q_LORA
    ckv = z[:, o:o + KV_LORA]
    o += KV_LORA
    ct = ct_ref[...]
    st = st_ref[...]
    kblock = z[:, o:o + LANES] * ct + z[:, o + LANES:o + 2 * LANES] * st
    kr_ref[...] = kblock[:, NOPE_DIM:QK_DIM]

    qq = _dot(_rms(cq, gql_ref[...]).astype(BF16), wuq_ref[...])
    c = _rms(ckv, gkv_ref[...])
    c_ref[...] = c
    cb = c.astype(BF16)
    kn = _dot(cb, wuk_ref[...])
    if not sample:
        v2_ref[...] = _dot(cb, w2_ref[...]).astype(BF16)

    lane = lax.broadcasted_iota(jnp.int32, ct.shape, 1)
    rstd_all = jnp.zeros(ct.shape, F32)
    gq = gq_ref[...]
    gk = gk_ref[...]
    for h in range(N_HEADS):
        sl = slice(h * LANES, (h + 1) * LANES)
        qr = qq[:, sl] * ct + qq[:, HP + h * LANES:HP + (h + 1) * LANES] * st
        qn = qr * lax.rsqrt(jnp.sum(qr * qr, axis=-1, keepdims=True) * (1.0 / QK_DIM) + EPS) * gq
        kh = kn[:, sl] + kblock
        rs = lax.rsqrt(jnp.sum(kh * kh, axis=-1, keepdims=True) * (1.0 / QK_DIM) + EPS)
        rstd_all = jnp.where(lane == h, rs, rstd_all)
        if sample:
            qf = (qn * gk).astype(BF16)
            q_ref[:, sl] = qf
            k2_ref[:, h * KV_LORA:(h + 1) * KV_LORA] = _dot(qf, w2_ref[h]).astype(BF16)
        else:
            q_ref[:, sl] = qn.astype(BF16)
            k2_ref[:, sl] = (kh * rs * gk).astype(BF16)
    rstd_ref[...] = rstd_all[:, :N_HEADS]


def _proj(x2, ct, st, w, *, nb, nt, tm, sample):
    n = x2.shape[0]
    row = lambda b, t: (b * nt + t, 0)
    rows = lambda width: pl.BlockSpec((tm, width), row)
    k2_w = N_HEADS * KV_LORA if sample else HP
    w2 = w["wukT"] if sample else w["wuv"]
    out_shape = [
        jax.ShapeDtypeStruct((nt * tm, nb * SSM_W), F32),
        jax.ShapeDtypeStruct((n, HP), BF16),
        jax.ShapeDtypeStruct((n, k2_w), BF16),
        *([] if sample else [jax.ShapeDtypeStruct((n, HP), BF16)]),
        jax.ShapeDtypeStruct((n, KV_LORA), F32),
        jax.ShapeDtypeStruct((n, ROPE_DIM), F32),
        jax.ShapeDtypeStruct((n, N_HEADS), F32),
    ]
    out_specs = [
        pl.BlockSpec((tm, SSM_W), lambda b, t: (t, b)),
        rows(HP), rows(k2_w), *([] if sample else [rows(HP)]),
        rows(KV_LORA), rows(ROPE_DIM), rows(N_HEADS),
    ]
    in_specs = [
        rows(D_MODEL),
        pl.BlockSpec((tm, LANES), lambda b, t: (t, 0)),
        pl.BlockSpec((tm, LANES), lambda b, t: (t, 0)),
        _full(w["gmix"].shape), _full(w["win"].shape), _full(w["gql"].shape), _full(w["wuq"].shape),
        _full(w["gkv"].shape), _full(w["wuk"].shape), _full(w2.shape),
        _full(w["gq_s" if sample else "gq_p"].shape), _full(w["gk"].shape),
    ]
    return pl.pallas_call(
        functools.partial(_proj_kernel, sample=sample),
        grid=(nb, nt),
        in_specs=in_specs,
        out_specs=out_specs,
        out_shape=out_shape,
        compiler_params=_cparams("parallel", "parallel"),
        name="proj_s" if sample else "proj_p",
    )(x2, ct, st, w["gmix"], w["win"], w["gql"], w["wuq"], w["gkv"], w["wuk"], w2,
      w["gq_s" if sample else "gq_p"], w["gk"])


S5_HALF = SSM_S // 2
S5_QUARTER = SSM_S // 4


def _s5_kernel(u_ref, h0_ref, are_ref, aim_ref, bw_ref, cw_ref, dsk_ref, wglu_ref, bglu_ref, gout_ref,
               o_ref, sfin_ref, xs_ref, st_ref, *, steps):
    @pl.when(pl.program_id(1) == 0)
    def _():
        st_ref[...] = h0_ref[...]

    u = u_ref[...]
    ub = u.astype(BF16)
    half_w = SSM_W // 2
    for k in range(2):
        uk = ub[:, k * half_w:(k + 1) * half_w]
        xs_ref[:, k * S5_HALF:(k + 1) * S5_HALF] = _dot(uk, bw_ref[2 * k])
        xs_ref[:, SSM_S + k * S5_HALF:SSM_S + (k + 1) * S5_HALF] = _dot(uk, bw_ref[2 * k + 1])

    for q in range(4):
        lr = slice(q * S5_QUARTER, (q + 1) * S5_QUARTER)
        li = slice(SSM_S + q * S5_QUARTER, SSM_S + (q + 1) * S5_QUARTER)
        ar = are_ref[:, lr]
        ai = aim_ref[:, lr]

        def step(t, carry):
            sr, si = carry
            r0 = pl.multiple_of(t * SUBLANES, SUBLANES)
            nr = ar * sr - ai * si + xs_ref[pl.ds(r0, SUBLANES), lr]
            ni = ar * si + ai * sr + xs_ref[pl.ds(r0, SUBLANES), li]
            xs_ref[pl.ds(r0, SUBLANES), lr] = nr
            xs_ref[pl.ds(r0, SUBLANES), li] = ni
            return nr, ni

        sr, si = lax.fori_loop(0, steps, step, (st_ref[:, lr], st_ref[:, li]), unroll=4)
        st_ref[:, lr] = sr
        st_ref[:, li] = si
    sfin_ref[...] = st_ref[...]

    ys = []
    for k in range(2):
        sre = xs_ref[:, k * S5_HALF:(k + 1) * S5_HALF].astype(BF16)
        sim = xs_ref[:, SSM_S + k * S5_HALF:SSM_S + (k + 1) * S5_HALF].astype(BF16)
        ys.append(_dot(sre, cw_ref[2 * k]) + _dot(sim, cw_ref[2 * k + 1]))
    y = jnp.concatenate(ys, axis=-1) + dsk_ref[...] * u
    g = jax.nn.gelu(y)
    out = g * jax.nn.sigmoid(_dot(g.astype(BF16), wglu_ref[...]) + bglu_ref[...])
    o_ref[...] = _rms(out, gout_ref[...]).astype(BF16)


def _s5(u2, h0, w, *, ngroups, nchunks, steps):
    rows = steps * SUBLANES
    blk = lambda g, t: (g * nchunks + t, 0)
    return pl.pallas_call(
        functools.partial(_s5_kernel, steps=steps),
        grid=(ngroups, nchunks),
        in_specs=[
            pl.BlockSpec((rows, SSM_W), blk),
            pl.BlockSpec((SUBLANES, 2 * SSM_S), lambda g, t: (g, 0)),
            _full(w["s5_are"].shape), _full(w["s5_aim"].shape), _full(w["s5_bw"].shape),
            _full(w["s5_cw"].shape), _full(w["s5_d"].shape), _full(w["wglu"].shape),
            _full(w["bglu"].shape), _full(w["gout_ssm"].shape),
        ],
        out_specs=[
            pl.BlockSpec((rows, SSM_W), blk),
            pl.BlockSpec((SUBLANES, 2 * SSM_S), lambda g, t: (g, 0)),
        ],
        out_shape=[
            jax.ShapeDtypeStruct(u2.shape, BF16),
            jax.ShapeDtypeStruct(h0.shape, F32),
        ],
        scratch_shapes=[pltpu.VMEM((rows, 2 * SSM_S), F32), pltpu.VMEM((SUBLANES, 2 * SSM_S), F32)],
        compiler_params=_cparams("parallel", "arbitrary"),
        name="s5",
    )(u2, h0, w["s5_are"], w["s5_aim"], w["s5_bw"], w["s5_cw"], w["s5_d"], w["wglu"], w["bglu"],
      w["gout_ssm"])


def _attn_p_kernel(q_ref, k_ref, v_ref, o_ref, *, tq):
    i = pl.program_id(2)
    row = lax.broadcasted_iota(jnp.int32, (tq, tq), 0)
    col = lax.broadcasted_iota(jnp.int32, (tq, tq), 1)

    def block(j, carry, masked):
        k0 = pl.multiple_of(j * tq, tq)
        k2 = k_ref[pl.ds(k0, tq), :]
        v2 = v_ref[pl.ds(k0, tq), :]
        out = []
        for hh in range(2):
            m, l, acc = carry[hh]
            s = _dot_nt(q_ref[:, hh * LANES:(hh + 1) * LANES], k2[:, hh * LANES:(hh + 1) * LANES])
            if masked:
                s = jnp.where(col <= row, s, NEG)
            m_new = jnp.maximum(m, jnp.max(s, axis=-1, keepdims=True))
            alpha = jnp.exp(m - m_new)
            p = jnp.exp(s - m_new)
            l = alpha * l + jnp.sum(p, axis=-1, keepdims=True)
            acc = alpha * acc + _dot(p.astype(BF16), v2[:, hh * LANES:(hh + 1) * LANES])
            out.append((m_new, l, acc))
        return tuple(out)

    init = tuple((jnp.full((tq, 1), NEG, F32), jnp.zeros((tq, 1), F32), jnp.zeros((tq, 2 * V_DIM), F32))
                 for _ in range(2))
    carry = lax.fori_loop(0, i, lambda j, c: block(j, c, False), init)
    carry = block(i, carry, True)
    o_ref[...] = carry[0][2] / carry[0][1] + carry[1][2] / carry[1][1]


def _attn_p(q, k, v, *, nb, t, tq):
    nq = t // tq
    return pl.pallas_call(
        functools.partial(_attn_p_kernel, tq=tq),
        grid=(nb, N_HEADS // 2, nq),
        in_specs=[
            pl.BlockSpec((tq, 2 * LANES), lambda b, h, i: (b * nq + i, h)),
            pl.BlockSpec((t, 2 * LANES), lambda b, h, i: (b, h)),
            pl.BlockSpec((t, 2 * LANES), lambda b, h, i: (b, h)),
        ],
        out_specs=pl.BlockSpec((tq, 2 * V_DIM), lambda b, h, i: (b * nq + i, h)),
        out_shape=jax.ShapeDtypeStruct((nb * t, N_HEADS * V_DIM), F32),
        compiler_params=_cparams("parallel", "parallel", "arbitrary"),
        name="attn_p",
    )(q, k, v)


ATTN_S_PAGES = 8
NEW_KEYS = 16


def _attn_s_kernel(pt_ref, ql_ref, qr_ref, cn_ref, krn_ref, rnt_ref, cc_hbm, ckr_hbm, cr_hbm,
                   o_ref, cbuf, krbuf, rbuf, sems, *, n_pages, dec_seq):
    b = pl.program_id(0)
    nb = pl.num_programs(0)
    nch = n_pages // ATTN_S_PAGES
    caches = (cc_hbm, ckr_hbm, cr_hbm)
    bufs = (cbuf, krbuf, rbuf)
    keys = ATTN_S_PAGES * PAGE_SIZE
    nrow = dec_seq * N_HEADS
    scale = 1.0 / math.sqrt(QK_DIM)

    def copies(bb, j, slot):
        out = []
        for p in range(ATTN_S_PAGES):
            page = pt_ref[bb * n_pages + j * ATTN_S_PAGES + p]
            for a in range(3):
                out.append(pltpu.make_async_copy(caches[a].at[page], bufs[a].at[slot, p], sems.at[slot, a]))
        return out

    def start(bb, j, slot):
        for cp in copies(bb, j, slot):
            cp.start()

    def wait(slot):
        for cp in copies(0, 0, slot):
            cp.wait()

    @pl.when(b == 0)
    def _():
        start(0, 0, 0)

    ql = ql_ref[0]
    qr = qr_ref[0]
    ri = lax.broadcasted_iota(jnp.int32, (N_HEADS, N_HEADS), 0)
    ci = lax.broadcasted_iota(jnp.int32, (N_HEADS, N_HEADS), 1)
    eye = jnp.where(ri == ci, 1.0, 0.0).astype(BF16)

    def tile_heads(r_t):
        return jnp.concatenate([r_t] * dec_seq, axis=0)

    cnb = cn_ref[0].astype(BF16)
    s_n = (_dot_nt(ql, cnb) + _dot_nt(qr, krn_ref[0].astype(BF16))) * tile_heads(rnt_ref[0]) * scale
    qs = lax.broadcasted_iota(jnp.int32, (nrow, NEW_KEYS), 0) // N_HEADS
    kj = lax.broadcasted_iota(jnp.int32, (nrow, NEW_KEYS), 1)
    s_n = jnp.where(kj <= qs, s_n, NEG)
    m0 = jnp.max(s_n, axis=-1, keepdims=True)
    p_n = jnp.exp(s_n - m0)
    l0 = jnp.sum(p_n, axis=-1, keepdims=True)
    acc0 = _dot(p_n.astype(BF16), cnb)

    def consume(slot, carry):
        m, l, acc = carry
        cb = cbuf[slot].reshape(keys, KV_LORA).astype(BF16)
        raw = _dot_nt(ql, cb) + _dot_nt(qr, krbuf[slot].reshape(keys, ROPE_DIM).astype(BF16))
        r = rbuf[slot].reshape(keys, N_HEADS)
        r_hi = r.astype(BF16)
        r_lo = (r - r_hi.astype(F32)).astype(BF16)
        r_t = _dot_nt(eye, r_hi) + _dot_nt(eye, r_lo)
        s = raw * tile_heads(r_t) * scale
        m_new = jnp.maximum(m, jnp.max(s, axis=-1, keepdims=True))
        alpha = jnp.exp(m - m_new)
        p = jnp.exp(s - m_new)
        l = alpha * l + jnp.sum(p, axis=-1, keepdims=True)
        acc = alpha * acc + _dot(p.astype(BF16), cb)
        return m_new, l, acc

    def body(j2, carry):
        j = j2 * 2
        start(b, j + 1, 1)
        wait(0)
        carry = consume(0, carry)

        @pl.when(j2 < nch // 2 - 1)
        def _():
            start(b, j + 2, 0)

        @pl.when((j2 == nch // 2 - 1) & (b + 1 < nb))
        def _():
            start(b + 1, 0, 0)

        wait(1)
        return consume(1, carry)

    m, l, acc = lax.fori_loop(0, nch // 2, body, (m0, l0, acc0))
    o_ref[0] = acc / l


def _attn_s(page_table, ql, qr, cn, krn, rnt, cache_c, cache_kr, cache_r, *, dec_seq):
    nb, n_pages = page_table.shape
    nrow = dec_seq * N_HEADS
    b3 = lambda b, pt: (b, 0, 0)
    return pl.pallas_call(
        functools.partial(_attn_s_kernel, n_pages=n_pages, dec_seq=dec_seq),
        grid_spec=pltpu.PrefetchScalarGridSpec(
            num_scalar_prefetch=1,
            grid=(nb,),
            in_specs=[
                pl.BlockSpec((1, nrow, KV_LORA), b3),
                pl.BlockSpec((1, nrow, ROPE_DIM), b3),
                pl.BlockSpec((1, NEW_KEYS, KV_LORA), b3),
                pl.BlockSpec((1, NEW_KEYS, ROPE_DIM), b3),
                pl.BlockSpec((1, N_HEADS, NEW_KEYS), b3),
                pl.BlockSpec(memory_space=pl.ANY),
                pl.BlockSpec(memory_space=pl.ANY),
                pl.BlockSpec(memory_space=pl.ANY),
            ],
            out_specs=pl.BlockSpec((1, nrow, KV_LORA), b3),
            scratch_shapes=[
                pltpu.VMEM((2, ATTN_S_PAGES, PAGE_SIZE, KV_LORA), F32),
                pltpu.VMEM((2, ATTN_S_PAGES, PAGE_SIZE, ROPE_DIM), F32),
                pltpu.VMEM((2, ATTN_S_PAGES, PAGE_SIZE, N_HEADS), F32),
                pltpu.SemaphoreType.DMA((2, 3)),
            ],
        ),
        out_shape=jax.ShapeDtypeStruct((nb, nrow, KV_LORA), F32),
        compiler_params=_cparams("arbitrary"),
        name="attn_s",
    )(page_table.reshape(-1), ql, qr, cn, krn, rnt, cache_c, cache_kr, cache_r)


def _latent_out_kernel(ol_ref, wuv_ref, o_ref):
    for h in range(N_HEADS):
        o_ref[:, h * V_DIM:(h + 1) * V_DIM] = _dot(
            ol_ref[:, h * KV_LORA:(h + 1) * KV_LORA].astype(BF16), wuv_ref[h])


def _latent_out(ol, wuv_h):
    n = ol.shape[0]
    return pl.pallas_call(
        _latent_out_kernel,
        grid=(1,),
        in_specs=[_full(ol.shape), _full(wuv_h.shape)],
        out_specs=_full((n, N_HEADS * V_DIM)),
        out_shape=jax.ShapeDtypeStruct((n, N_HEADS * V_DIM), F32),
        compiler_params=_cparams("arbitrary"),
        name="latent_out",
    )(ol, wuv_h)


ROUTE_W = 16


def _mix_kernel(h_ref, os_ref, om_ref, gom_ref, wout_ref, gffn_ref, wr_ref, br_ref,
                h1_ref, xn_ref, route_ref, cnt_ref):
    @pl.when((pl.program_id(0) == 0) & (pl.program_id(1) == 0))
    def _():
        cnt_ref[...] = jnp.zeros_like(cnt_ref)

    tm = h_ref.shape[0]
    omn = _rms(om_ref[...], gom_ref[...]).astype(BF16)
    h1 = h_ref[...] + _dot(os_ref[...], wout_ref[:SSM_W, :]) + _dot(omn, wout_ref[SSM_W:, :])
    h1_ref[...] = h1
    xn = _rms(h1, gffn_ref[...]).astype(BF16)
    xn_ref[...] = xn
    work = _dot(xn, wr_ref[...]) + br_ref[...]

    lane = lax.broadcasted_iota(jnp.int32, work.shape, 1).astype(F32)
    route = jnp.zeros(work.shape, F32)
    sels, vals = [], []
    for k in range(TOP_K):
        mk = jnp.max(work, axis=-1, keepdims=True)
        ik = jnp.min(jnp.where(work == mk, lane, float(LANES)), axis=-1, keepdims=True)
        sel = lane == ik
        work = jnp.where(sel, -jnp.inf, work)
        route = jnp.where(lane == float(k), ik, route)
        sels.append(sel)
        vals.append(mk)
    es = [jnp.exp(v - vals[0]) for v in vals]
    den = es[0] + es[1] + es[2] + es[3]
    onehot = jnp.zeros(work.shape, F32)
    for k in range(TOP_K):
        route = jnp.where(lane == float(TOP_K + k), es[k] / den, route)
        onehot = jnp.where(sels[k], 1.0, onehot)

    r_i = lax.broadcasted_iota(jnp.int32, (tm, tm), 0)
    c_i = lax.broadcasted_iota(jnp.int32, (tm, tm), 1)
    tri = jnp.where(c_i < r_i, 1.0, 0.0).astype(BF16)
    cum = _dot(tri, onehot.astype(BF16)) + cnt_ref[...]
    for k in range(TOP_K):
        rank = jnp.sum(jnp.where(sels[k], cum, 0.0), axis=-1, keepdims=True)
        route = jnp.where(lane == float(2 * TOP_K + k), rank, route)
    route_ref[...] = route[:, :ROUTE_W]
    cnt_ref[...] = cnt_ref[...] + jnp.sum(onehot, axis=0, keepdims=True)


def _mix(h2, o_ssm, o_mla, w, *, nb, nt, tm):
    n = h2.shape[0]
    row = lambda b, t: (b * nt + t, 0)
    return pl.pallas_call(
        _mix_kernel,
        grid=(nb, nt),
        in_specs=[
            pl.BlockSpec((tm, D_MODEL), row),
            pl.BlockSpec((tm, SSM_W), lambda b, t: (t, b)),
            pl.BlockSpec((tm, N_HEADS * V_DIM), row),
            _full(w["gout_mla"].shape), _full(w["wout"].shape), _full(w["gffn"].shape),
            _full(w["wr"].shape), _full(w["br"].shape),
        ],
        out_specs=[
            pl.BlockSpec((tm, D_MODEL), row),
            pl.BlockSpec((tm, D_MODEL), row),
            pl.BlockSpec((tm, ROUTE_W), row),
            _full((1, LANES)),
        ],
        out_shape=[
            jax.ShapeDtypeStruct((n, D_MODEL), F32),
            jax.ShapeDtypeStruct((n, D_MODEL), BF16),
            jax.ShapeDtypeStruct((n, ROUTE_W), F32),
            jax.ShapeDtypeStruct((1, LANES), F32),
        ],
        compiler_params=_cparams("arbitrary", "arbitrary"),
        name="mix",
    )(h2, o_ssm, o_mla, w["gout_mla"], w["wout"], w["gffn"], w["wr"], w["br"])


def _experts_kernel(tile_ref, e_ref, first_ref, valid_ref, lo_ref, hi_ref,
                    x_ref, wgu_ref, bgu_ref, wd_ref, bd_ref, o_ref):
    i = pl.program_id(0)

    @pl.when(valid_ref[i] == 1)
    def _():
        bm = x_ref.shape[0]
        hdn = _dot(x_ref[...], wgu_ref[0]) + bgu_ref[0]
        x_glu = jnp.minimum(hdn[:, :D_FF], SWIGLU_LIMIT)
        x_lin = jnp.clip(hdn[:, D_FF:], -SWIGLU_LIMIT, SWIGLU_LIMIT)
        act = x_glu * jax.nn.sigmoid(SWIGLU_ALPHA * x_glu) * (x_lin + 1.0)
        y = _dot(act.astype(BF16), wd_ref[0]) + bd_ref[0]
        rows = tile_ref[i] * bm + lax.broadcasted_iota(jnp.int32, (bm, 1), 0)
        mine = (rows >= lo_ref[i]) & (rows < hi_ref[i])

        @pl.when(first_ref[i] == 1)
        def _():
            o_ref[...] = jnp.where(mine, y, 0.0)

        @pl.when(first_ref[i] == 0)
        def _():
            o_ref[...] = jnp.where(mine, y, o_ref[...])


def _experts(xs, items, w, *, bm):
    r = xs.shape[0]
    n_items = items[0].shape[0]
    xmap = lambda i, tile, e, first, valid, lo, hi: (tile[i], 0)
    emap = lambda i, tile, e, first, valid, lo, hi: (e[i], 0, 0)
    return pl.pallas_call(
        _experts_kernel,
        grid_spec=pltpu.PrefetchScalarGridSpec(
            num_scalar_prefetch=6,
            grid=(n_items,),
            in_specs=[
                pl.BlockSpec((bm, D_MODEL), xmap),
                pl.BlockSpec((1, D_MODEL, 2 * D_FF), emap),
                pl.BlockSpec((1, 1, 2 * D_FF), emap),
                pl.BlockSpec((1, D_FF, D_MODEL), emap),
                pl.BlockSpec((1, 1, D_MODEL), emap),
            ],
            out_specs=pl.BlockSpec((bm, D_MODEL), xmap),
        ),
        out_shape=jax.ShapeDtypeStruct((r, D_MODEL), F32),
        compiler_params=_cparams("arbitrary"),
        name="experts",
    )(*items, xs, w["wgu"], w["bgu"], w["wd"], w["bd"])


def _expert_items(counts, n_rows, bm):
    n_tiles = n_rows // bm
    n_items = n_tiles + N_EXPERTS - 1
    ends = jnp.cumsum(counts)
    starts = ends - counts
    first_tile = starts // bm
    per_e = jnp.where(counts > 0, (ends - 1) // bm - first_tile + 1, 0)
    item_end = jnp.cumsum(per_e)
    i = jnp.arange(n_items, dtype=jnp.int32)
    valid = i < item_end[-1]
    e = jnp.minimum(jnp.searchsorted(item_end, i, side="right"), N_EXPERTS - 1).astype(jnp.int32)
    tile = first_tile[e] + i - (item_end[e] - per_e[e])
    tile = jnp.where(valid, tile, n_tiles - 1).astype(jnp.int32)
    first = jnp.concatenate([jnp.ones((1,), jnp.int32), (tile[1:] != tile[:-1]).astype(jnp.int32)])
    return (tile, e, first, valid.astype(jnp.int32),
            starts[e].astype(jnp.int32), ends[e].astype(jnp.int32))


def _ple_kernel(h1_ref, ym_ref, p_ref, gple_ref, wg_ref, wp_ref, o_ref):
    h2 = h1_ref[...] + ym_ref[...]
    gate = jax.nn.sigmoid(_dot(_rms(h2, gple_ref[...]).astype(BF16), wg_ref[...]))
    o_ref[...] = h2 + _dot(p_ref[...].astype(BF16), wp_ref[...]) * gate


def _ple(h1, ym, p, w, *, tm):
    n = h1.shape[0]
    row = lambda t: (t, 0)
    return pl.pallas_call(
        _ple_kernel,
        grid=(n // tm,),
        in_specs=[
            pl.BlockSpec((tm, D_MODEL), row), pl.BlockSpec((tm, D_MODEL), row),
            pl.BlockSpec((tm, PLE_DIM), row),
            _full(w["gple"].shape), _full(w["wpg"].shape), _full(w["wp"].shape),
        ],
        out_specs=pl.BlockSpec((tm, D_MODEL), row),
        out_shape=jax.ShapeDtypeStruct((n, D_MODEL), F32),
        compiler_params=_cparams("parallel"),
        name="ple",
    )(h1, ym, p, w["gple"], w["wpg"], w["wp"])


def _rope_tables(pos):
    inv = ROPE_THETA ** (-jnp.arange(0, ROPE_DIM, 2, dtype=F32) / ROPE_DIM)
    ang = pos.astype(F32)[:, None] * inv[None, :]
    cos, sin = jnp.cos(ang), jnp.sin(ang)
    n = pos.shape[0]
    pad = jnp.zeros((n, LANES - QK_DIM), F32)
    ct = jnp.concatenate([jnp.ones((n, NOPE_DIM), F32), cos, cos, pad], axis=-1)
    st = jnp.concatenate([jnp.zeros((n, NOPE_DIM), F32), sin, sin, pad], axis=-1)
    return ct, st


def _rope_partner(w_rope):
    return jnp.concatenate([-w_rope[..., ROPE_HALF:], w_rope[..., :ROPE_HALF]], axis=-1)


def _head_tile(nope, rope):
    pad = jnp.zeros(rope.shape[:-1] + (LANES - QK_DIM,), rope.dtype)
    return jnp.concatenate([nope, rope, pad], axis=-1)


def _block_diag(m):
    g, a, b = m.shape
    hg = g // 2
    eye = jnp.eye(hg, dtype=m.dtype)
    mh = m.reshape(2, hg, a, b)
    return jnp.einsum("kgab,gh->kgahb", mh, eye).reshape(2, hg * a, hg * b)


def _prep_weights(g_mix, w_in, s5_a_re, s5_a_im, s5_log_dt, s5_b_re, s5_b_im, s5_c_re, s5_c_im, s5_d,
                  w_glu, b_glu, g_q_lora, w_uq, g_kv_lora, w_uk, w_uv, g_qk_q, g_qk_k, g_out_ssm,
                  g_out_mla, w_out, g_ffn, w_router, b_router, w_gate_up, b_gate_up, w_down, b_down,
                  g_ple, w_ple_gate, w_ple):
    w = {}
    row = lambda v: v.reshape(1, -1).astype(F32)
    o = SSM_W + Q_LORA + KV_LORA
    w_kr = w_in[:, o:o + ROPE_DIM]
    zeros_n = jnp.zeros((D_MODEL, NOPE_DIM), F32)
    w["win"] = jnp.concatenate(
        [w_in[:, :o], _head_tile(zeros_n, w_kr), _head_tile(zeros_n, _rope_partner(w_kr))], axis=-1).astype(BF16)
    w["gmix"] = row(g_mix)
    w["gql"] = row(g_q_lora)
    w["gkv"] = row(g_kv_lora)

    wq = w_uq.reshape(Q_LORA, N_HEADS, QK_DIM)
    q1 = _head_tile(wq[..., :NOPE_DIM], wq[..., NOPE_DIM:])
    q2 = _head_tile(jnp.zeros_like(wq[..., :NOPE_DIM]), _rope_partner(wq[..., NOPE_DIM:]))
    w["wuq"] = jnp.concatenate([q1.reshape(Q_LORA, HP), q2.reshape(Q_LORA, HP)], axis=-1).astype(BF16)
    w["wuk"] = jnp.concatenate(
        [w_uk, jnp.zeros((KV_LORA, N_HEADS, LANES - NOPE_DIM), F32)], axis=-1).reshape(KV_LORA, HP).astype(BF16)
    zv = jnp.zeros((KV_LORA, N_HEADS // 2, V_DIM), F32)
    w["wuv"] = jnp.stack([w_uv[:, 0::2], zv, zv, w_uv[:, 1::2]], axis=2).reshape(KV_LORA, HP).astype(BF16)
    w["wuv_h"] = w_uv.transpose(1, 0, 2).astype(BF16)
    wukT = w_uk.transpose(1, 2, 0)
    w["wukT"] = jnp.concatenate(
        [wukT, jnp.zeros((N_HEADS, LANES - NOPE_DIM, KV_LORA), F32)], axis=1).astype(BF16)
    gpad = jnp.zeros((LANES - QK_DIM,), F32)
    w["gq_p"] = row(jnp.concatenate([g_qk_q * (1.0 / math.sqrt(QK_DIM)), gpad]))
    w["gq_s"] = row(jnp.concatenate([g_qk_q, gpad]))
    w["gk"] = row(jnp.concatenate([g_qk_k, gpad]))

    dt = jnp.exp(s5_log_dt.astype(F32))[:, None]
    ar, ai = s5_a_re.astype(F32), s5_a_im.astype(F32)
    mag = jnp.exp(dt * ar)
    abar_re, abar_im = mag * jnp.cos(dt * ai), mag * jnp.sin(dt * ai)
    den = ar * ar + ai * ai
    nr, ni = abar_re - 1.0, abar_im
    coef_re = (nr * ar + ni * ai) / den
    coef_im = (ni * ar - nr * ai) / den
    br, bi = s5_b_re.astype(F32), s5_b_im.astype(F32)
    bbar_re = coef_re[..., None] * br - coef_im[..., None] * bi
    bbar_im = coef_re[..., None] * bi + coef_im[..., None] * br
    bre = _block_diag(bbar_re.transpose(0, 2, 1))
    bim = _block_diag(bbar_im.transpose(0, 2, 1))
    w["s5_bw"] = jnp.stack([bre[0], bim[0], bre[1], bim[1]]).astype(BF16)
    cre = _block_diag(s5_c_re.astype(F32).transpose(0, 2, 1))
    cim = _block_diag(-s5_c_im.astype(F32).transpose(0, 2, 1))
    w["s5_cw"] = jnp.stack([cre[0], cim[0], cre[1], cim[1]]).astype(BF16)
    w["s5_are"] = jnp.broadcast_to(abar_re.reshape(1, SSM_S), (SUBLANES, SSM_S))
    w["s5_aim"] = jnp.broadcast_to(abar_im.reshape(1, SSM_S), (SUBLANES, SSM_S))
    w["s5_d"] = row(s5_d)
    w["wglu"] = w_glu.astype(BF16)
    w["bglu"] = row(b_glu)
    w["gout_ssm"] = row(g_out_ssm)
    w["gout_mla"] = row(g_out_mla)
    w["wout"] = w_out.astype(BF16)
    w["gffn"] = row(g_ffn)
    w["wr"] = jnp.concatenate([w_router, jnp.zeros((D_MODEL, LANES - N_EXPERTS), F32)], axis=-1).astype(BF16)
    w["br"] = row(jnp.concatenate([b_router.astype(F32), jnp.full((LANES - N_EXPERTS,), NEG, F32)]))
    w["wgu"] = w_gate_up.astype(BF16)
    w["bgu"] = b_gate_up.reshape(N_EXPERTS, 1, 2 * D_FF).astype(F32)
    w["wd"] = w_down.astype(BF16)
    w["bd"] = b_down.reshape(N_EXPERTS, 1, D_MODEL).astype(F32)
    w["gple"] = row(g_ple)
    w["wpg"] = w_ple_gate.astype(BF16)
    w["wp"] = w_ple.astype(BF16)
    return w


EXPERT_ROWS = 512


def _channel(h2, o_ssm, o_mla, p2, w, *, nb, nt, tm):
    n = h2.shape[0]
    h1, xn, route, cnt = _mix(h2, o_ssm, o_mla, w, nb=nb, nt=nt, tm=tm)
    top_i = route[:, :TOP_K].astype(jnp.int32)
    gates = route[:, TOP_K:2 * TOP_K]
    rank = route[:, 2 * TOP_K:3 * TOP_K].astype(jnp.int32)
    counts = cnt[0, :N_EXPERTS].astype(jnp.int32)
    starts = jnp.cumsum(counts) - counts
    dest = starts[top_i] + rank
    n_rows = n * TOP_K
    tok = jnp.zeros((n_rows,), jnp.int32).at[dest.reshape(-1)].set(
        jnp.repeat(jnp.arange(n, dtype=jnp.int32), TOP_K))
    xs = xn[tok]
    ys = _experts(xs, _expert_items(counts, n_rows, EXPERT_ROWS), w, bm=EXPERT_ROWS)
    ym = jnp.sum(ys[dest] * gates[..., None], axis=1)
    return _ple(h1, ym, p2, w, tm=tm)


def kernel(x_prompt, x_sample, cache_kv_latent, cache_k_rope, cache_k_rstd, state_ssm_re, state_ssm_im,
           page_table, p_prompt, p_sample, g_mix, w_in, s5_a_re, s5_a_im, s5_log_dt, s5_b_re, s5_b_im,
           s5_c_re, s5_c_im, s5_d, w_glu, b_glu, g_q_lora, w_uq, g_kv_lora, w_uk, w_uv, g_qk_q, g_qk_k,
           g_out_ssm, g_out_mla, w_out, g_ffn, w_router, b_router, w_gate_up, b_gate_up, w_down, b_down,
           g_ple, w_ple_gate, w_ple):
    assert g_mix.shape[0] == 1, "single-layer step"
    nb, t, _ = x_prompt.shape
    db, ds, _ = x_sample.shape
    assert nb == SUBLANES and db % SUBLANES == 0
    w = _prep_weights(g_mix[0], w_in[0], s5_a_re[0], s5_a_im[0], s5_log_dt[0], s5_b_re[0], s5_b_im[0],
                      s5_c_re[0], s5_c_im[0], s5_d[0], w_glu[0], b_glu[0], g_q_lora[0], w_uq[0],
                      g_kv_lora[0], w_uk[0], w_uv[0], g_qk_q[0], g_qk_k[0], g_out_ssm[0], g_out_mla[0],
                      w_out[0], g_ffn[0], w_router[0], b_router[0], w_gate_up[0], b_gate_up[0],
                      w_down[0], b_down[0], g_ple[0], w_ple_gate[0], w_ple[0])

    tm = min(512, t)
    nt = t // tm
    xp = x_prompt.reshape(nb * t, D_MODEL)
    ct, st = _rope_tables(jnp.arange(t))
    u, q, k, v, c_p, kr_p, rstd_p = _proj(xp, ct, st, w, nb=nb, nt=nt, tm=tm, sample=False)
    steps = min(128, t)
    o_ssm, s_fin = _s5(u.reshape(t * nb, SSM_W), jnp.zeros((nb, 2 * SSM_S), F32), w,
                       ngroups=1, nchunks=t // steps, steps=steps)
    o_mla = _attn_p(q, k, v, nb=nb, t=t, tq=tm)
    y_p = _channel(xp, o_ssm.reshape(t, nb * SSM_W), o_mla, p_prompt[0].reshape(nb * t, PLE_DIM), w,
                   nb=nb, nt=nt, tm=tm)
    sr_p = s_fin[:, :SSM_S].reshape(1, nb, SSM_G, SSM_P)
    si_p = s_fin[:, SSM_S:].reshape(1, nb, SSM_G, SSM_P)

    ns = db * ds
    ng = db // SUBLANES
    xs_ = x_sample.reshape(ns, D_MODEL)
    ct_s, st_s = _rope_tables(PAST_LEN + jnp.tile(jnp.arange(ds), db))
    u_s, qf, qlat, c_s, kr_s, rstd_s = _proj(xs_, ct_s, st_s, w, nb=1, nt=1, tm=ns, sample=True)
    to_scan = lambda a: a.reshape(ng, SUBLANES, ds, -1).transpose(0, 2, 1, 3).reshape(ns, -1)
    from_scan = lambda a: a.reshape(ng, ds, SUBLANES, -1).transpose(0, 2, 1, 3).reshape(ns, -1)
    h0 = jnp.concatenate([state_ssm_re[0].reshape(db, SSM_S), state_ssm_im[0].reshape(db, SSM_S)],
                         axis=-1).astype(F32)
    o_ssm_s, s_fin_s = _s5(to_scan(u_s), h0, w, ngroups=ng, nchunks=1, steps=ds)
    pad_keys = lambda a: jnp.concatenate(
        [a, jnp.zeros((db, NEW_KEYS - ds, a.shape[-1]), a.dtype)], axis=1)
    rnt = pad_keys(rstd_s.reshape(db, ds, N_HEADS)).transpose(0, 2, 1)
    o_lat = _attn_s(page_table,
                    qlat.reshape(db, ds * N_HEADS, KV_LORA),
                    qf.reshape(db, ds * N_HEADS, LANES)[:, :, NOPE_DIM:QK_DIM],
                    pad_keys(c_s.reshape(db, ds, KV_LORA)), pad_keys(kr_s.reshape(db, ds, ROPE_DIM)), rnt,
                    cache_kv_latent[0], cache_k_rope[0], cache_k_rstd[0], dec_seq=ds)
    o_mla_s = _latent_out(o_lat.reshape(ns, N_HEADS * KV_LORA), w["wuv_h"])
    y_s = _channel(xs_, from_scan(o_ssm_s), o_mla_s, p_sample[0].reshape(ns, PLE_DIM), w,
                   nb=1, nt=1, tm=ns)

    return (y_p.reshape(nb, t, D_MODEL), y_s.reshape(db, ds, D_MODEL),
            c_p.reshape(1, nb, t, KV_LORA), kr_p.reshape(1, nb, t, ROPE_DIM), rstd_p.reshape(1, nb, t, N_HEADS),
            sr_p, si_p,
            c_s.reshape(1, db, ds, KV_LORA), kr_s.reshape(1, db, ds, ROPE_DIM), rstd_s.reshape(1, db, ds, N_HEADS),
            s_fin_s[:, :SSM_S].reshape(1, db, SSM_G, SSM_P), s_fin_s[:, SSM_S:].reshape(1, db, SSM_G, SSM_P))
```

```python
import functools
import math

import jax
import jax.numpy as jnp
from jax import lax
from jax.experimental import pallas as pl
from jax.experimental.pallas import tpu as pltpu

D_MODEL = 1024
SSM_W = 512
SSM_GC = 16
SSM_G = SSM_W // SSM_GC
SSM_P = 64
SSM_S = SSM_G * SSM_P
N_HEADS = 8
NOPE_DIM = 64
ROPE_DIM = 32
ROPE_HALF = ROPE_DIM // 2
QK_DIM = NOPE_DIM + ROPE_DIM
V_DIM = 64
Q_LORA = 384
KV_LORA = 256
ROPE_THETA = 10000.0
N_EXPERTS = 32
TOP_K = 4
D_FF = D_MODEL
SWIGLU_ALPHA = 1.702
SWIGLU_LIMIT = 7.0
PLE_DIM = 256
PAST_LEN = 16384
PAGE_SIZE = 128
EPS = 1e-6
NEG = -1e30

LANES = 128
SUBLANES = 8
VMEM_LIMIT = 56 * 1024 * 1024

HP = N_HEADS * LANES
BF16 = jnp.bfloat16
F32 = jnp.float32


def _cparams(*sem):
    return pltpu.CompilerParams(dimension_semantics=sem, vmem_limit_bytes=VMEM_LIMIT)


def _rms(x, g):
    r = lax.rsqrt(jnp.mean(x * x, axis=-1, keepdims=True) + EPS)
    return x * r * g


def _dot(a, b):
    return jnp.dot(a, b, preferred_element_type=F32)


def _dot_nt(a, b):
    return lax.dot_general(a, b, (((1,), (1,)), ((), ())), preferred_element_type=F32)


def _full(shape):
    nd = len(shape)
    return pl.BlockSpec(shape, lambda *_: (0,) * nd)


def _proj_kernel(x_ref, ct_ref, st_ref, gmix_ref, win_ref, gql_ref, wuq_ref, gkv_ref, wuk_ref,
                 w2_ref, gq_ref, gk_ref, u_ref, q_ref, k2_ref, *rest, sample):
    if sample:
        c_ref, kr_ref, rstd_ref = rest
    else:
        v2_ref, c_ref, kr_ref, rstd_ref = rest
    xn = _rms(x_ref[...], gmix_ref[...]).astype(BF16)
    z = _dot(xn, win_ref[...])
    u_ref[...] = z[:, :SSM_W]
    o = SSM_W
    cq = z[:, o:o + Q_LORA]
    o += Q_LORA
    ckv = z[:, o:o + KV_LORA]
    o += KV_LORA
    ct = ct_ref[...]
    st = st_ref[...]
    kblock = z[:, o:o + LANES] * ct + z[:, o + LANES:o + 2 * LANES] * st
    kr_ref[...] = kblock[:, NOPE_DIM:QK_DIM]

    qq = _dot(_rms(cq, gql_ref[...]).astype(BF16), wuq_ref[...])
    c = _rms(ckv, gkv_ref[...])
    c_ref[...] = c
    cb = c.astype(BF16)
    kn = _dot(cb, wuk_ref[...])
    if not sample:
        v2_ref[...] = _dot(cb, w2_ref[...]).astype(BF16)

    lane = lax.broadcasted_iota(jnp.int32, ct.shape, 1)
    rstd_all = jnp.zeros(ct.shape, F32)
    gq = gq_ref[...]
    gk = gk_ref[...]
    for h in range(N_HEADS):
        sl = slice(h * LANES, (h + 1) * LANES)
        qr = qq[:, sl] * ct + qq[:, HP + h * LANES:HP + (h + 1) * LANES] * st
        qn = qr * lax.rsqrt(jnp.sum(qr * qr, axis=-1, keepdims=True) * (1.0 / QK_DIM) + EPS) * gq
        kh = kn[:, sl] + kblock
        rs = lax.rsqrt(jnp.sum(kh * kh, axis=-1, keepdims=True) * (1.0 / QK_DIM) + EPS)
        rstd_all = jnp.where(lane == h, rs, rstd_all)
        if sample:
            qf = (qn * gk).astype(BF16)
            q_ref[:, sl] = qf
            k2_ref[:, h * KV_LORA:(h + 1) * KV_LORA] = _dot(qf, w2_ref[h]).astype(BF16)
        else:
            q_ref[:, sl] = qn.astype(BF16)
            k2_ref[:, sl] = (kh * rs * gk).astype(BF16)
    rstd_ref[...] = rstd_all[:, :N_HEADS]


def _proj(x2, ct, st, w, *, nb, nt, tm, sample):
    n = x2.shape[0]
    row = lambda b, t: (b * nt + t, 0)
    rows = lambda width: pl.BlockSpec((tm, width), row)
    k2_w = N_HEADS * KV_LORA if sample else HP
    w2 = w["wukT"] if sample else w["wuv"]
    out_shape = [
        jax.ShapeDtypeStruct((nt * tm, nb * SSM_W), F32),
        jax.ShapeDtypeStruct((n, HP), BF16),
        jax.ShapeDtypeStruct((n, k2_w), BF16),
        *([] if sample else [jax.ShapeDtypeStruct((n, HP), BF16)]),
        jax.ShapeDtypeStruct((n, KV_LORA), F32),
        jax.ShapeDtypeStruct((n, ROPE_DIM), F32),
        jax.ShapeDtypeStruct((n, N_HEADS), F32),
    ]
    out_specs = [
        pl.BlockSpec((tm, SSM_W), lambda b, t: (t, b)),
        rows(HP), rows(k2_w), *([] if sample else [rows(HP)]),
        rows(KV_LORA), rows(ROPE_DIM), rows(N_HEADS),
    ]
    in_specs = [
        rows(D_MODEL),
        pl.BlockSpec((tm, LANES), lambda b, t: (t, 0)),
        pl.BlockSpec((tm, LANES), lambda b, t: (t, 0)),
        _full(w["gmix"].shape), _full(w["win"].shape), _full(w["gql"].shape), _full(w["wuq"].shape),
        _full(w["gkv"].shape), _full(w["wuk"].shape), _full(w2.shape),
        _full(w["gq_s" if sample else "gq_p"].shape), _full(w["gk"].shape),
    ]
    return pl.pallas_call(
        functools.partial(_proj_kernel, sample=sample),
        grid=(nb, nt),
        in_specs=in_specs,
        out_specs=out_specs,
        out_shape=out_shape,
        compiler_params=_cparams("parallel", "parallel"),
        name="proj_s" if sample else "proj_p",
    )(x2, ct, st, w["gmix"], w["win"], w["gql"], w["wuq"], w["gkv"], w["wuk"], w2,
      w["gq_s" if sample else "gq_p"], w["gk"])


S5_HALF = SSM_S // 2
S5_QUARTER = SSM_S // 4


def _s5_kernel(u_ref, h0_ref, are_ref, aim_ref, bw_ref, cw_ref, dsk_ref, wglu_ref, bglu_ref, gout_ref,
               o_ref, sfin_ref, xs_ref, st_ref, *, steps):
    @pl.when(pl.program_id(1) == 0)
    def _():
        st_ref[...] = h0_ref[...]

    u = u_ref[...]
    ub = u.astype(BF16)
    half_w = SSM_W // 2
    for k in range(2):
        uk = ub[:, k * half_w:(k + 1) * half_w]
        xs_ref[:, k * S5_HALF:(k + 1) * S5_HALF] = _dot(uk, bw_ref[2 * k])
        xs_ref[:, SSM_S + k * S5_HALF:SSM_S + (k + 1) * S5_HALF] = _dot(uk, bw_ref[2 * k + 1])

    for q in range(4):
        lr = slice(q * S5_QUARTER, (q + 1) * S5_QUARTER)
        li = slice(SSM_S + q * S5_QUARTER, SSM_S + (q + 1) * S5_QUARTER)
        ar = are_ref[:, lr]
        ai = aim_ref[:, lr]

        def step(t, carry):
            sr, si = carry
            r0 = pl.multiple_of(t * SUBLANES, SUBLANES)
            nr = ar * sr - ai * si + xs_ref[pl.ds(r0, SUBLANES), lr]
            ni = ar * si + ai * sr + xs_ref[pl.ds(r0, SUBLANES), li]
            xs_ref[pl.ds(r0, SUBLANES), lr] = nr
            xs_ref[pl.ds(r0, SUBLANES), li] = ni
            return nr, ni

        sr, si = lax.fori_loop(0, steps, step, (st_ref[:, lr], st_ref[:, li]), unroll=4)
        st_ref[:, lr] = sr
        st_ref[:, li] = si
    sfin_ref[...] = st_ref[...]

    ys = []
    for k in range(2):
        sre = xs_ref[:, k * S5_HALF:(k + 1) * S5_HALF].astype(BF16)
        sim = xs_ref[:, SSM_S + k * S5_HALF:SSM_S + (k + 1) * S5_HALF].astype(BF16)
        ys.append(_dot(sre, cw_ref[2 * k]) + _dot(sim, cw_ref[2 * k + 1]))
    y = jnp.concatenate(ys, axis=-1) + dsk_ref[...] * u
    g = jax.nn.gelu(y)
    out = g * jax.nn.sigmoid(_dot(g.astype(BF16), wglu_ref[...]) + bglu_ref[...])
    o_ref[...] = _rms(out, gout_ref[...]).astype(BF16)


def _s5(u2, h0, w, *, ngroups, nchunks, steps):
    rows = steps * SUBLANES
    blk = lambda g, t: (g * nchunks + t, 0)
    return pl.pallas_call(
        functools.partial(_s5_kernel, steps=steps),
        grid=(ngroups, nchunks),
        in_specs=[
            pl.BlockSpec((rows, SSM_W), blk),
            pl.BlockSpec((SUBLANES, 2 * SSM_S), lambda g, t: (g, 0)),
            _full(w["s5_are"].shape), _full(w["s5_aim"].shape), _full(w["s5_bw"].shape),
            _full(w["s5_cw"].shape), _full(w["s5_d"].shape), _full(w["wglu"].shape),
            _full(w["bglu"].shape), _full(w["gout_ssm"].shape),
        ],
        out_specs=[
            pl.BlockSpec((rows, SSM_W), blk),
            pl.BlockSpec((SUBLANES, 2 * SSM_S), lambda g, t: (g, 0)),
        ],
        out_shape=[
            jax.ShapeDtypeStruct(u2.shape, BF16),
            jax.ShapeDtypeStruct(h0.shape, F32),
        ],
        scratch_shapes=[pltpu.VMEM((rows, 2 * SSM_S), F32), pltpu.VMEM((SUBLANES, 2 * SSM_S), F32)],
        compiler_params=_cparams("parallel", "arbitrary"),
        name="s5",
    )(u2, h0, w["s5_are"], w["s5_aim"], w["s5_bw"], w["s5_cw"], w["s5_d"], w["wglu"], w["bglu"],
      w["gout_ssm"])


def _attn_p_kernel(q_ref, k_ref, v_ref, o_ref, *, tq):
    i = pl.program_id(2)
    row = lax.broadcasted_iota(jnp.int32, (tq, tq), 0)
    col = lax.broadcasted_iota(jnp.int32, (tq, tq), 1)

    def block(j, carry, masked):
        k0 = pl.multiple_of(j * tq, tq)
        k2 = k_ref[pl.ds(k0, tq), :]
        v2 = v_ref[pl.ds(k0, tq), :]
        out = []
        for hh in range(2):
            m, l, acc = carry[hh]
            s = _dot_nt(q_ref[:, hh * LANES:(hh + 1) * LANES], k2[:, hh * LANES:(hh + 1) * LANES])
            if masked:
                s = jnp.where(col <= row, s, NEG)
            m_new = jnp.maximum(m, jnp.max(s, axis=-1, keepdims=True))
            alpha = jnp.exp(m - m_new)
            p = jnp.exp(s - m_new)
            l = alpha * l + jnp.sum(p, axis=-1, keepdims=True)
            acc = alpha * acc + _dot(p.astype(BF16), v2[:, hh * LANES:(hh + 1) * LANES])
            out.append((m_new, l, acc))
        return tuple(out)

    init = tuple((jnp.full((tq, 1), NEG, F32), jnp.zeros((tq, 1), F32), jnp.zeros((tq, 2 * V_DIM), F32))
                 for _ in range(2))
    carry = lax.fori_loop(0, i, lambda j, c: block(j, c, False), init)
    carry = block(i, carry, True)
    o_ref[...] = carry[0][2] / carry[0][1] + carry[1][2] / carry[1][1]


def _attn_p(q, k, v, *, nb, t, tq):
    nq = t // tq
    return pl.pallas_call(
        functools.partial(_attn_p_kernel, tq=tq),
        grid=(nb, N_HEADS // 2, nq),
        in_specs=[
            pl.BlockSpec((tq, 2 * LANES), lambda b, h, i: (b * nq + i, h)),
            pl.BlockSpec((t, 2 * LANES), lambda b, h, i: (b, h)),
            pl.BlockSpec((t, 2 * LANES), lambda b, h, i: (b, h)),
        ],
        out_specs=pl.BlockSpec((tq, 2 * V_DIM), lambda b, h, i: (b * nq + i, h)),
        out_shape=jax.ShapeDtypeStruct((nb * t, N_HEADS * V_DIM), F32),
        compiler_params=_cparams("parallel", "parallel", "arbitrary"),
        name="attn_p",
    )(q, k, v)


ATTN_S_PAGES = 32
NEW_KEYS = 16


def _attn_s_kernel(pt_ref, ql_ref, qr_ref, cn_ref, krn_ref, rnt_ref, cc_hbm, ckr_hbm, cr_hbm,
                   o_ref, cbuf, krbuf, rbuf, cb_ref, sc_ref, sems, *, n_pages, dec_seq):
    b = pl.program_id(0)
    nb = pl.num_programs(0)
    bp = min(ATTN_S_PAGES, n_pages)
    nblk = n_pages // bp
    keys = bp * PAGE_SIZE
    nrow = dec_seq * N_HEADS
    scale = 1.0 / math.sqrt(QK_DIM)
    slot = b % 2

    def start_block(bb, i, dst):
        for p in range(bp):
            pg = i * bp + p
            page = pt_ref[bb * n_pages + pg]
            lanes = pl.ds(pl.multiple_of(pg * PAGE_SIZE, PAGE_SIZE), PAGE_SIZE)
            pltpu.make_async_copy(cc_hbm.at[page], cbuf.at[dst, pg], sems.at[dst, 0]).start()
            pltpu.make_async_copy(ckr_hbm.at[page], krbuf.at[dst, :, lanes], sems.at[dst, 1]).start()
            pltpu.make_async_copy(cr_hbm.at[page], rbuf.at[dst, :, lanes], sems.at[dst, 2]).start()

    @pl.when(b == 0)
    def _():
        lax.fori_loop(0, nblk, lambda i, c: (start_block(0, i, 0), c)[1], 0)

    pltpu.make_async_copy(cbuf.at[slot], cbuf.at[slot], sems.at[slot, 0]).wait()
    pltpu.make_async_copy(krbuf.at[slot], krbuf.at[slot], sems.at[slot, 1]).wait()
    pltpu.make_async_copy(rbuf.at[slot], rbuf.at[slot], sems.at[slot, 2]).wait()

    ql = ql_ref[0]
    qr = qr_ref[0]

    def tile_heads(r_t):
        return jnp.concatenate([r_t] * dec_seq, axis=0)

    cnb = cn_ref[0].astype(BF16)
    s_n = (_dot_nt(ql, cnb) + _dot(qr, krn_ref[0].astype(BF16))) * tile_heads(rnt_ref[0]) * scale
    qs = lax.broadcasted_iota(jnp.int32, (nrow, NEW_KEYS), 0) // N_HEADS
    kj = lax.broadcasted_iota(jnp.int32, (nrow, NEW_KEYS), 1)
    s_n = jnp.where(kj <= qs, s_n, NEG)

    def score_block(i, m):
        @pl.when(b + 1 < nb)
        def _():
            start_block(b + 1, i, 1 - slot)

        k0 = pl.multiple_of(i * keys, keys)
        cb = cbuf[slot, pl.ds(i * bp, bp)].reshape(keys, KV_LORA).astype(BF16)
        cb_ref[pl.ds(k0, keys), :] = cb
        raw = _dot_nt(ql, cb) + _dot(qr, krbuf[slot, :, pl.ds(k0, keys)].astype(BF16))
        s = raw * tile_heads(rbuf[slot, :, pl.ds(k0, keys)]) * scale
        sc_ref[:, pl.ds(k0, keys)] = s
        return jnp.maximum(m, jnp.max(s, axis=-1, keepdims=True))

    m = lax.fori_loop(0, nblk, score_block, jnp.max(s_n, axis=-1, keepdims=True))
    p_n = jnp.exp(s_n - m)

    def value_block(i, carry):
        l, acc = carry
        k0 = pl.multiple_of(i * keys, keys)
        p = jnp.exp(sc_ref[:, pl.ds(k0, keys)] - m)
        return (l + jnp.sum(p, axis=-1, keepdims=True),
                acc + _dot(p.astype(BF16), cb_ref[pl.ds(k0, keys), :]))

    l, acc = lax.fori_loop(0, nblk, value_block,
                           (jnp.sum(p_n, axis=-1, keepdims=True), _dot(p_n.astype(BF16), cnb)))
    o_ref[0] = acc / l


def _attn_s(page_table, ql, qr, cn, krn, rnt, cache_c, cache_kr, cache_r, *, dec_seq):
    nb, n_pages = page_table.shape
    nrow = dec_seq * N_HEADS
    b3 = lambda b, pt: (b, 0, 0)
    return pl.pallas_call(
        functools.partial(_attn_s_kernel, n_pages=n_pages, dec_seq=dec_seq),
        grid_spec=pltpu.PrefetchScalarGridSpec(
            num_scalar_prefetch=1,
            grid=(nb,),
            in_specs=[
                pl.BlockSpec((1, nrow, KV_LORA), b3),
                pl.BlockSpec((1, nrow, ROPE_DIM), b3),
                pl.BlockSpec((1, NEW_KEYS, KV_LORA), b3),
                pl.BlockSpec((1, ROPE_DIM, NEW_KEYS), b3),
                pl.BlockSpec((1, N_HEADS, NEW_KEYS), b3),
                pl.BlockSpec(memory_space=pl.ANY),
                pl.BlockSpec(memory_space=pl.ANY),
                pl.BlockSpec(memory_space=pl.ANY),
            ],
            out_specs=pl.BlockSpec((1, nrow, KV_LORA), b3),
            scratch_shapes=[
                pltpu.VMEM((2, n_pages, PAGE_SIZE, KV_LORA), F32),
                pltpu.VMEM((2, ROPE_DIM, n_pages * PAGE_SIZE), F32),
                pltpu.VMEM((2, N_HEADS, n_pages * PAGE_SIZE), F32),
                pltpu.VMEM((n_pages * PAGE_SIZE, KV_LORA), BF16),
                pltpu.VMEM((nrow, n_pages * PAGE_SIZE), F32),
                pltpu.SemaphoreType.DMA((2, 3)),
            ],
        ),
        out_shape=jax.ShapeDtypeStruct((nb, nrow, KV_LORA), F32),
        compiler_params=_cparams("arbitrary"),
        name="attn_s",
    )(page_table.reshape(-1), ql, qr, cn, krn, rnt, cache_c, cache_kr, cache_r)


def _latent_out_kernel(ol_ref, wuv_ref, o_ref):
    for h in range(N_HEADS):
        o_ref[:, h * V_DIM:(h + 1) * V_DIM] = _dot(
            ol_ref[:, h * KV_LORA:(h + 1) * KV_LORA].astype(BF16), wuv_ref[h])


def _latent_out(ol, wuv_h):
    n = ol.shape[0]
    return pl.pallas_call(
        _latent_out_kernel,
        grid=(1,),
        in_specs=[_full(ol.shape), _full(wuv_h.shape)],
        out_specs=_full((n, N_HEADS * V_DIM)),
        out_shape=jax.ShapeDtypeStruct((n, N_HEADS * V_DIM), F32),
        compiler_params=_cparams("arbitrary"),
        name="latent_out",
    )(ol, wuv_h)


ROUTE_W = 16


def _mix_kernel(h_ref, os_ref, om_ref, gom_ref, wout_ref, gffn_ref, wr_ref, br_ref,
                h1_ref, xn_ref, route_ref, cnt_ref):
    @pl.when((pl.program_id(0) == 0) & (pl.program_id(1) == 0))
    def _():
        cnt_ref[...] = jnp.zeros_like(cnt_ref)

    tm = h_ref.shape[0]
    omn = _rms(om_ref[...], gom_ref[...]).astype(BF16)
    h1 = h_ref[...] + _dot(os_ref[...], wout_ref[:SSM_W, :]) + _dot(omn, wout_ref[SSM_W:, :])
    h1_ref[...] = h1
    xn = _rms(h1, gffn_ref[...]).astype(BF16)
    xn_ref[...] = xn
    work = _dot(xn, wr_ref[...]) + br_ref[...]

    lane = lax.broadcasted_iota(jnp.int32, work.shape, 1).astype(F32)
    route = jnp.zeros(work.shape, F32)
    sels, vals = [], []
    for k in range(TOP_K):
        mk = jnp.max(work, axis=-1, keepdims=True)
        ik = jnp.min(jnp.where(work == mk, lane, float(LANES)), axis=-1, keepdims=True)
        sel = lane == ik
        work = jnp.where(sel, -jnp.inf, work)
        route = jnp.where(lane == float(k), ik, route)
        sels.append(sel)
        vals.append(mk)
    es = [jnp.exp(v - vals[0]) for v in vals]
    den = es[0] + es[1] + es[2] + es[3]
    onehot = jnp.zeros(work.shape, F32)
    for k in range(TOP_K):
        route = jnp.where(lane == float(TOP_K + k), es[k] / den, route)
        onehot = jnp.where(sels[k], 1.0, onehot)

    r_i = lax.broadcasted_iota(jnp.int32, (tm, tm), 0)
    c_i = lax.broadcasted_iota(jnp.int32, (tm, tm), 1)
    tri = jnp.where(c_i < r_i, 1.0, 0.0).astype(BF16)
    cum = _dot(tri, onehot.astype(BF16)) + cnt_ref[...]
    for k in range(TOP_K):
        rank = jnp.sum(jnp.where(sels[k], cum, 0.0), axis=-1, keepdims=True)
        route = jnp.where(lane == float(2 * TOP_K + k), rank, route)
    route_ref[...] = route[:, :ROUTE_W]
    cnt_ref[...] = cnt_ref[...] + jnp.sum(onehot, axis=0, keepdims=True)


def _mix(h2, o_ssm, o_mla, w, *, nb, nt, tm):
    n = h2.shape[0]
    row = lambda b, t: (b * nt + t, 0)
    return pl.pallas_call(
        _mix_kernel,
        grid=(nb, nt),
        in_specs=[
            pl.BlockSpec((tm, D_MODEL), row),
            pl.BlockSpec((tm, SSM_W), lambda b, t: (t, b)),
            pl.BlockSpec((tm, N_HEADS * V_DIM), row),
            _full(w["gout_mla"].shape), _full(w["wout"].shape), _full(w["gffn"].shape),
            _full(w["wr"].shape), _full(w["br"].shape),
        ],
        out_specs=[
            pl.BlockSpec((tm, D_MODEL), row),
            pl.BlockSpec((tm, D_MODEL), row),
            pl.BlockSpec((tm, ROUTE_W), row),
            _full((1, LANES)),
        ],
        out_shape=[
            jax.ShapeDtypeStruct((n, D_MODEL), F32),
            jax.ShapeDtypeStruct((n, D_MODEL), BF16),
            jax.ShapeDtypeStruct((n, ROUTE_W), F32),
            jax.ShapeDtypeStruct((1, LANES), F32),
        ],
        compiler_params=_cparams("arbitrary", "arbitrary"),
        name="mix",
    )(h2, o_ssm, o_mla, w["gout_mla"], w["wout"], w["gffn"], w["wr"], w["br"])


def _experts_kernel(tile_ref, e_ref, first_ref, valid_ref, lo_ref, hi_ref,
                    x_ref, wgu_ref, bgu_ref, wd_ref, bd_ref, o_ref):
    i = pl.program_id(0)

    @pl.when(valid_ref[i] == 1)
    def _():
        bm = x_ref.shape[0]
        hdn = _dot(x_ref[...], wgu_ref[0]) + bgu_ref[0]
        x_glu = jnp.minimum(hdn[:, :D_FF], SWIGLU_LIMIT)
        x_lin = jnp.clip(hdn[:, D_FF:], -SWIGLU_LIMIT, SWIGLU_LIMIT)
        act = x_glu * jax.nn.sigmoid(SWIGLU_ALPHA * x_glu) * (x_lin + 1.0)
        y = _dot(act.astype(BF16), wd_ref[0]) + bd_ref[0]
        rows = tile_ref[i] * bm + lax.broadcasted_iota(jnp.int32, (bm, 1), 0)
        mine = (rows >= lo_ref[i]) & (rows < hi_ref[i])

        @pl.when(first_ref[i] == 1)
        def _():
            o_ref[...] = jnp.where(mine, y, 0.0)

        @pl.when(first_ref[i] == 0)
        def _():
            o_ref[...] = jnp.where(mine, y, o_ref[...])


def _experts(xs, items, w, *, bm):
    r = xs.shape[0]
    n_items = items[0].shape[0]
    xmap = lambda i, tile, e, first, valid, lo, hi: (tile[i], 0)
    emap = lambda i, tile, e, first, valid, lo, hi: (e[i], 0, 0)
    return pl.pallas_call(
        _experts_kernel,
        grid_spec=pltpu.PrefetchScalarGridSpec(
            num_scalar_prefetch=6,
            grid=(n_items,),
            in_specs=[
                pl.BlockSpec((bm, D_MODEL), xmap),
                pl.BlockSpec((1, D_MODEL, 2 * D_FF), emap),
                pl.BlockSpec((1, 1, 2 * D_FF), emap),
                pl.BlockSpec((1, D_FF, D_MODEL), emap),
                pl.BlockSpec((1, 1, D_MODEL), emap),
            ],
            out_specs=pl.BlockSpec((bm, D_MODEL), xmap),
        ),
        out_shape=jax.ShapeDtypeStruct((r, D_MODEL), F32),
        compiler_params=_cparams("arbitrary"),
        name="experts",
    )(*items, xs, w["wgu"], w["bgu"], w["wd"], w["bd"])


def _expert_items(counts, n_rows, bm):
    n_tiles = n_rows // bm
    n_items = n_tiles + N_EXPERTS - 1
    ends = jnp.cumsum(counts)
    starts = ends - counts
    first_tile = starts // bm
    per_e = jnp.where(counts > 0, (ends - 1) // bm - first_tile + 1, 0)
    item_end = jnp.cumsum(per_e)
    i = jnp.arange(n_items, dtype=jnp.int32)
    valid = i < item_end[-1]
    e = jnp.minimum(jnp.sum(i[:, None] >= item_end[None, :], axis=1), N_EXPERTS - 1).astype(jnp.int32)
    tile = first_tile[e] + i - (item_end[e] - per_e[e])
    tile = jnp.where(valid, tile, n_tiles - 1).astype(jnp.int32)
    first = jnp.concatenate([jnp.ones((1,), jnp.int32), (tile[1:] != tile[:-1]).astype(jnp.int32)])
    return (tile, e, first, valid.astype(jnp.int32),
            starts[e].astype(jnp.int32), ends[e].astype(jnp.int32))


def _ple_kernel(h1_ref, ym_ref, p_ref, gple_ref, wg_ref, wp_ref, o_ref):
    h2 = h1_ref[...] + ym_ref[...]
    gate = jax.nn.sigmoid(_dot(_rms(h2, gple_ref[...]).astype(BF16), wg_ref[...]))
    o_ref[...] = h2 + _dot(p_ref[...].astype(BF16), wp_ref[...]) * gate


def _ple(h1, ym, p, w, *, tm):
    n = h1.shape[0]
    row = lambda t: (t, 0)
    return pl.pallas_call(
        _ple_kernel,
        grid=(n // tm,),
        in_specs=[
            pl.BlockSpec((tm, D_MODEL), row), pl.BlockSpec((tm, D_MODEL), row),
            pl.BlockSpec((tm, PLE_DIM), row),
            _full(w["gple"].shape), _full(w["wpg"].shape), _full(w["wp"].shape),
        ],
        out_specs=pl.BlockSpec((tm, D_MODEL), row),
        out_shape=jax.ShapeDtypeStruct((n, D_MODEL), F32),
        compiler_params=_cparams("parallel"),
        name="ple",
    )(h1, ym, p, w["gple"], w["wpg"], w["wp"])


def _rope_tables(pos):
    inv = ROPE_THETA ** (-jnp.arange(0, ROPE_DIM, 2, dtype=F32) / ROPE_DIM)
    ang = pos.astype(F32)[:, None] * inv[None, :]
    cos, sin = jnp.cos(ang), jnp.sin(ang)
    n = pos.shape[0]
    pad = jnp.zeros((n, LANES - QK_DIM), F32)
    ct = jnp.concatenate([jnp.ones((n, NOPE_DIM), F32), cos, cos, pad], axis=-1)
    st = jnp.concatenate([jnp.zeros((n, NOPE_DIM), F32), sin, sin, pad], axis=-1)
    return ct, st


def _rope_partner(w_rope):
    return jnp.concatenate([-w_rope[..., ROPE_HALF:], w_rope[..., :ROPE_HALF]], axis=-1)


def _head_tile(nope, rope):
    pad = jnp.zeros(rope.shape[:-1] + (LANES - QK_DIM,), rope.dtype)
    return jnp.concatenate([nope, rope, pad], axis=-1)


def _block_diag(m):
    g, a, b = m.shape
    hg = g // 2
    eye = jnp.eye(hg, dtype=m.dtype)
    mh = m.reshape(2, hg, a, b)
    return jnp.einsum("kgab,gh->kgahb", mh, eye).reshape(2, hg * a, hg * b)


def _prep_weights(g_mix, w_in, s5_a_re, s5_a_im, s5_log_dt, s5_b_re, s5_b_im, s5_c_re, s5_c_im, s5_d,
                  w_glu, b_glu, g_q_lora, w_uq, g_kv_lora, w_uk, w_uv, g_qk_q, g_qk_k, g_out_ssm,
                  g_out_mla, w_out, g_ffn, w_router, b_router, w_gate_up, b_gate_up, w_down, b_down,
                  g_ple, w_ple_gate, w_ple):
    w = {}
    row = lambda v: v.reshape(1, -1).astype(F32)
    o = SSM_W + Q_LORA + KV_LORA
    w_kr = w_in[:, o:o + ROPE_DIM]
    zeros_n = jnp.zeros((D_MODEL, NOPE_DIM), F32)
    w["win"] = jnp.concatenate(
        [w_in[:, :o], _head_tile(zeros_n, w_kr), _head_tile(zeros_n, _rope_partner(w_kr))], axis=-1).astype(BF16)
    w["gmix"] = row(g_mix)
    w["gql"] = row(g_q_lora)
    w["gkv"] = row(g_kv_lora)

    wq = w_uq.reshape(Q_LORA, N_HEADS, QK_DIM)
    q1 = _head_tile(wq[..., :NOPE_DIM], wq[..., NOPE_DIM:])
    q2 = _head_tile(jnp.zeros_like(wq[..., :NOPE_DIM]), _rope_partner(wq[..., NOPE_DIM:]))
    w["wuq"] = jnp.concatenate([q1.reshape(Q_LORA, HP), q2.reshape(Q_LORA, HP)], axis=-1).astype(BF16)
    w["wuk"] = jnp.concatenate(
        [w_uk, jnp.zeros((KV_LORA, N_HEADS, LANES - NOPE_DIM), F32)], axis=-1).reshape(KV_LORA, HP).astype(BF16)
    zv = jnp.zeros((KV_LORA, N_HEADS // 2, V_DIM), F32)
    w["wuv"] = jnp.stack([w_uv[:, 0::2], zv, zv, w_uv[:, 1::2]], axis=2).reshape(KV_LORA, HP).astype(BF16)
    w["wuv_h"] = w_uv.transpose(1, 0, 2).astype(BF16)
    wukT = w_uk.transpose(1, 2, 0)
    w["wukT"] = jnp.concatenate(
        [wukT, jnp.zeros((N_HEADS, LANES - NOPE_DIM, KV_LORA), F32)], axis=1).astype(BF16)
    gpad = jnp.zeros((LANES - QK_DIM,), F32)
    w["gq_p"] = row(jnp.concatenate([g_qk_q * (1.0 / math.sqrt(QK_DIM)), gpad]))
    w["gq_s"] = row(jnp.concatenate([g_qk_q, gpad]))
    w["gk"] = row(jnp.concatenate([g_qk_k, gpad]))

    dt = jnp.exp(s5_log_dt.astype(F32))[:, None]
    ar, ai = s5_a_re.astype(F32), s5_a_im.astype(F32)
    mag = jnp.exp(dt * ar)
    abar_re, abar_im = mag * jnp.cos(dt * ai), mag * jnp.sin(dt * ai)
    den = ar * ar + ai * ai
    nr, ni = abar_re - 1.0, abar_im
    coef_re = (nr * ar + ni * ai) / den
    coef_im = (ni * ar - nr * ai) / den
    br, bi = s5_b_re.astype(F32), s5_b_im.astype(F32)
    bbar_re = coef_re[..., None] * br - coef_im[..., None] * bi
    bbar_im = coef_re[..., None] * bi + coef_im[..., None] * br
    bre = _block_diag(bbar_re.transpose(0, 2, 1))
    bim = _block_diag(bbar_im.transpose(0, 2, 1))
    w["s5_bw"] = jnp.stack([bre[0], bim[0], bre[1], bim[1]]).astype(BF16)
    cre = _block_diag(s5_c_re.astype(F32).transpose(0, 2, 1))
    cim = _block_diag(-s5_c_im.astype(F32).transpose(0, 2, 1))
    w["s5_cw"] = jnp.stack([cre[0], cim[0], cre[1], cim[1]]).astype(BF16)
    w["s5_are"] = jnp.broadcast_to(abar_re.reshape(1, SSM_S), (SUBLANES, SSM_S))
    w["s5_aim"] = jnp.broadcast_to(abar_im.reshape(1, SSM_S), (SUBLANES, SSM_S))
    w["s5_d"] = row(s5_d)
    w["wglu"] = w_glu.astype(BF16)
    w["bglu"] = row(b_glu)
    w["gout_ssm"] = row(g_out_ssm)
    w["gout_mla"] = row(g_out_mla)
    w["wout"] = w_out.astype(BF16)
    w["gffn"] = row(g_ffn)
    w["wr"] = jnp.concatenate([w_router, jnp.zeros((D_MODEL, LANES - N_EXPERTS), F32)], axis=-1).astype(BF16)
    w["br"] = row(jnp.concatenate([b_router.astype(F32), jnp.full((LANES - N_EXPERTS,), NEG, F32)]))
    w["wgu"] = w_gate_up.astype(BF16)
    w["bgu"] = b_gate_up.reshape(N_EXPERTS, 1, 2 * D_FF).astype(F32)
    w["wd"] = w_down.astype(BF16)
    w["bd"] = b_down.reshape(N_EXPERTS, 1, D_MODEL).astype(F32)
    w["gple"] = row(g_ple)
    w["wpg"] = w_ple_gate.astype(BF16)
    w["wp"] = w_ple.astype(BF16)
    return w


EXPERT_ROWS = 512


def _channel(h2, o_ssm, o_mla, p2, w, *, nb, nt, tm):
    n = h2.shape[0]
    h1, xn, route, cnt = _mix(h2, o_ssm, o_mla, w, nb=nb, nt=nt, tm=tm)
    top_i = route[:, :TOP_K].astype(jnp.int32)
    gates = route[:, TOP_K:2 * TOP_K]
    rank = route[:, 2 * TOP_K:3 * TOP_K].astype(jnp.int32)
    counts = cnt[0, :N_EXPERTS].astype(jnp.int32)
    starts = jnp.cumsum(counts) - counts
    dest = starts[top_i] + rank
    n_rows = n * TOP_K
    tok = jnp.zeros((n_rows,), jnp.int32).at[dest.reshape(-1)].set(
        jnp.repeat(jnp.arange(n, dtype=jnp.int32), TOP_K))
    xs = xn[tok]
    ys = _experts(xs, _expert_items(counts, n_rows, EXPERT_ROWS), w, bm=EXPERT_ROWS)
    ym = jnp.sum(ys[dest] * gates[..., None], axis=1)
    return _ple(h1, ym, p2, w, tm=tm)


def kernel(x_prompt, x_sample, cache_kv_latent, cache_k_rope, cache_k_rstd, state_ssm_re, state_ssm_im,
           page_table, p_prompt, p_sample, g_mix, w_in, s5_a_re, s5_a_im, s5_log_dt, s5_b_re, s5_b_im,
           s5_c_re, s5_c_im, s5_d, w_glu, b_glu, g_q_lora, w_uq, g_kv_lora, w_uk, w_uv, g_qk_q, g_qk_k,
           g_out_ssm, g_out_mla, w_out, g_ffn, w_router, b_router, w_gate_up, b_gate_up, w_down, b_down,
           g_ple, w_ple_gate, w_ple):
    assert g_mix.shape[0] == 1, "single-layer step"
    nb, t, _ = x_prompt.shape
    db, ds, _ = x_sample.shape
    assert nb == SUBLANES and db % SUBLANES == 0
    w = _prep_weights(g_mix[0], w_in[0], s5_a_re[0], s5_a_im[0], s5_log_dt[0], s5_b_re[0], s5_b_im[0],
                      s5_c_re[0], s5_c_im[0], s5_d[0], w_glu[0], b_glu[0], g_q_lora[0], w_uq[0],
                      g_kv_lora[0], w_uk[0], w_uv[0], g_qk_q[0], g_qk_k[0], g_out_ssm[0], g_out_mla[0],
                      w_out[0], g_ffn[0], w_router[0], b_router[0], w_gate_up[0], b_gate_up[0],
                      w_down[0], b_down[0], g_ple[0], w_ple_gate[0], w_ple[0])

    tm = min(512, t)
    nt = t // tm
    xp = x_prompt.reshape(nb * t, D_MODEL)
    ct, st = _rope_tables(jnp.arange(t))
    u, q, k, v, c_p, kr_p, rstd_p = _proj(xp, ct, st, w, nb=nb, nt=nt, tm=tm, sample=False)
    steps = min(128, t)
    o_ssm, s_fin = _s5(u.reshape(t * nb, SSM_W), jnp.zeros((nb, 2 * SSM_S), F32), w,
                       ngroups=1, nchunks=t // steps, steps=steps)
    o_mla = _attn_p(q, k, v, nb=nb, t=t, tq=tm)
    y_p = _channel(xp, o_ssm.reshape(t, nb * SSM_W), o_mla, p_prompt[0].reshape(nb * t, PLE_DIM), w,
                   nb=nb, nt=nt, tm=tm)
    sr_p = s_fin[:, :SSM_S].reshape(1, nb, SSM_G, SSM_P)
    si_p = s_fin[:, SSM_S:].reshape(1, nb, SSM_G, SSM_P)

    ns = db * ds
    ng = db // SUBLANES
    xs_ = x_sample.reshape(ns, D_MODEL)
    ct_s, st_s = _rope_tables(PAST_LEN + jnp.tile(jnp.arange(ds), db))
    u_s, qf, qlat, c_s, kr_s, rstd_s = _proj(xs_, ct_s, st_s, w, nb=1, nt=1, tm=ns, sample=True)
    to_scan = lambda a: a.reshape(ng, SUBLANES, ds, -1).transpose(0, 2, 1, 3).reshape(ns, -1)
    from_scan = lambda a: a.reshape(ng, ds, SUBLANES, -1).transpose(0, 2, 1, 3).reshape(ns, -1)
    h0 = jnp.concatenate([state_ssm_re[0].reshape(db, SSM_S), state_ssm_im[0].reshape(db, SSM_S)],
                         axis=-1).astype(F32)
    o_ssm_s, s_fin_s = _s5(to_scan(u_s), h0, w, ngroups=ng, nchunks=1, steps=ds)
    pad_keys = lambda a: jnp.concatenate(
        [a, jnp.zeros((db, NEW_KEYS - ds, a.shape[-1]), a.dtype)], axis=1)
    key_major = lambda a: a.transpose(0, 2, 1)
    o_lat = _attn_s(page_table,
                    qlat.reshape(db, ds * N_HEADS, KV_LORA),
                    qf.reshape(db, ds * N_HEADS, LANES)[:, :, NOPE_DIM:QK_DIM],
                    pad_keys(c_s.reshape(db, ds, KV_LORA)),
                    key_major(pad_keys(kr_s.reshape(db, ds, ROPE_DIM))),
                    key_major(pad_keys(rstd_s.reshape(db, ds, N_HEADS))),
                    cache_kv_latent[0], key_major(cache_k_rope[0]), key_major(cache_k_rstd[0]), dec_seq=ds)
    o_mla_s = _latent_out(o_lat.reshape(ns, N_HEADS * KV_LORA), w["wuv_h"])
    y_s = _channel(xs_, from_scan(o_ssm_s), o_mla_s, p_sample[0].reshape(ns, PLE_DIM), w,
                   nb=1, nt=1, tm=ns)

    return (y_p.reshape(nb, t, D_MODEL), y_s.reshape(db, ds, D_MODEL),
            c_p.reshape(1, nb, t, KV_LORA), kr_p.reshape(1, nb, t, ROPE_DIM), rstd_p.reshape(1, nb, t, N_HEADS),
            sr_p, si_p,
            c_s.reshape(1, db, ds, KV_LORA), kr_s.reshape(1, db, ds, ROPE_DIM), rstd_s.reshape(1, db, ds, N_HEADS),
            s_fin_s[:, :SSM_S].reshape(1, db, SSM_G, SSM_P), s_fin_s[:, SSM_S:].reshape(1, db, SSM_G, SSM_P))
```

```python
import functools
import math

import jax
import jax.numpy as jnp
from jax import lax
from jax.experimental import pallas as pl
from jax.experimental.pallas import tpu as pltpu
from jax.experimental.pallas import tpu_sc as plsc

D_MODEL = 1024
SSM_W = 512
SSM_GC = 16
SSM_G = SSM_W // SSM_GC
SSM_P = 64
SSM_S = SSM_G * SSM_P
N_HEADS = 8
NOPE_DIM = 64
ROPE_DIM = 32
ROPE_HALF = ROPE_DIM // 2
QK_DIM = NOPE_DIM + ROPE_DIM
V_DIM = 64
Q_LORA = 384
KV_LORA = 256
ROPE_THETA = 10000.0
N_EXPERTS = 32
TOP_K = 4
D_FF = D_MODEL
SWIGLU_ALPHA = 1.702
SWIGLU_LIMIT = 7.0
PLE_DIM = 256
PAST_LEN = 16384
PAGE_SIZE = 128
EPS = 1e-6
NEG = -1e30

LANES = 128
SUBLANES = 8
VMEM_LIMIT = 56 * 1024 * 1024

HP = N_HEADS * LANES
BF16 = jnp.bfloat16
F32 = jnp.float32


def _cparams(*sem):
    return pltpu.CompilerParams(dimension_semantics=sem, vmem_limit_bytes=VMEM_LIMIT)


def _rms(x, g):
    r = lax.rsqrt(jnp.mean(x * x, axis=-1, keepdims=True) + EPS)
    return x * r * g


def _dot(a, b):
    return jnp.dot(a, b, preferred_element_type=F32)


def _dot_nt(a, b):
    return lax.dot_general(a, b, (((1,), (1,)), ((), ())), preferred_element_type=F32)


def _full(shape):
    nd = len(shape)
    return pl.BlockSpec(shape, lambda *_: (0,) * nd)


def _proj_kernel(x_ref, ct_ref, st_ref, gmix_ref, win_ref, gql_ref, wuq_ref, gkv_ref, wuk_ref,
                 w2_ref, gq_ref, gk_ref, u_ref, q_ref, k2_ref, *rest, sample):
    if sample:
        c_ref, kr_ref, rstd_ref = rest
    else:
        v2_ref, c_ref, kr_ref, rstd_ref = rest
    xn = _rms(x_ref[...], gmix_ref[...]).astype(BF16)
    z = _dot(xn, win_ref[...])
    u_ref[...] = z[:, :SSM_W]
    o = SSM_W
    cq = z[:, o:o + Q_LORA]
    o += Q_LORA
    ckv = z[:, o:o + KV_LORA]
    o += KV_LORA
    ct = ct_ref[...]
    st = st_ref[...]
    kblock = z[:, o:o + LANES] * ct + z[:, o + LANES:o + 2 * LANES] * st
    kr_ref[...] = kblock[:, NOPE_DIM:QK_DIM]

    qq = _dot(_rms(cq, gql_ref[...]).astype(BF16), wuq_ref[...])
    c = _rms(ckv, gkv_ref[...])
    c_ref[...] = c
    cb = c.astype(BF16)
    kn = _dot(cb, wuk_ref[...])
    if not sample:
        v2_ref[...] = _dot(cb, w2_ref[...]).astype(BF16)

    lane = lax.broadcasted_iota(jnp.int32, ct.shape, 1)
    rstd_all = jnp.zeros(ct.shape, F32)
    gq = gq_ref[...]
    gk = gk_ref[...]
    for h in range(N_HEADS):
        sl = slice(h * LANES, (h + 1) * LANES)
        qr = qq[:, sl] * ct + qq[:, HP + h * LANES:HP + (h + 1) * LANES] * st
        qn = qr * lax.rsqrt(jnp.sum(qr * qr, axis=-1, keepdims=True) * (1.0 / QK_DIM) + EPS) * gq
        kh = kn[:, sl] + kblock
        rs = lax.rsqrt(jnp.sum(kh * kh, axis=-1, keepdims=True) * (1.0 / QK_DIM) + EPS)
        rstd_all = jnp.where(lane == h, rs, rstd_all)
        if sample:
            qf = (qn * gk).astype(BF16)
            q_ref[:, sl] = qf
            k2_ref[:, h * KV_LORA:(h + 1) * KV_LORA] = _dot(qf, w2_ref[h]).astype(BF16)
        else:
            q_ref[:, sl] = qn.astype(BF16)
            k2_ref[:, sl] = (kh * rs * gk).astype(BF16)
    rstd_ref[...] = rstd_all[:, :N_HEADS]


def _proj(x2, ct, st, w, *, nb, nt, tm, sample):
    n = x2.shape[0]
    row = lambda b, t: (b * nt + t, 0)
    rows = lambda width: pl.BlockSpec((tm, width), row)
    k2_w = N_HEADS * KV_LORA if sample else HP
    w2 = w["wukT"] if sample else w["wuv"]
    out_shape = [
        jax.ShapeDtypeStruct((nt * tm, nb * SSM_W), F32),
        jax.ShapeDtypeStruct((n, HP), BF16),
        jax.ShapeDtypeStruct((n, k2_w), BF16),
        *([] if sample else [jax.ShapeDtypeStruct((n, HP), BF16)]),
        jax.ShapeDtypeStruct((n, KV_LORA), F32),
        jax.ShapeDtypeStruct((n, ROPE_DIM), F32),
        jax.ShapeDtypeStruct((n, N_HEADS), F32),
    ]
    out_specs = [
        pl.BlockSpec((tm, SSM_W), lambda b, t: (t, b)),
        rows(HP), rows(k2_w), *([] if sample else [rows(HP)]),
        rows(KV_LORA), rows(ROPE_DIM), rows(N_HEADS),
    ]
    in_specs = [
        rows(D_MODEL),
        pl.BlockSpec((tm, LANES), lambda b, t: (t, 0)),
        pl.BlockSpec((tm, LANES), lambda b, t: (t, 0)),
        _full(w["gmix"].shape), _full(w["win"].shape), _full(w["gql"].shape), _full(w["wuq"].shape),
        _full(w["gkv"].shape), _full(w["wuk"].shape), _full(w2.shape),
        _full(w["gq_s" if sample else "gq_p"].shape), _full(w["gk"].shape),
    ]
    return pl.pallas_call(
        functools.partial(_proj_kernel, sample=sample),
        grid=(nb, nt),
        in_specs=in_specs,
        out_specs=out_specs,
        out_shape=out_shape,
        compiler_params=_cparams("parallel", "parallel"),
        name="proj_s" if sample else "proj_p",
    )(x2, ct, st, w["gmix"], w["win"], w["gql"], w["wuq"], w["gkv"], w["wuk"], w2,
      w["gq_s" if sample else "gq_p"], w["gk"])


S5_HALF = SSM_S // 2
S5_QUARTER = SSM_S // 4


def _s5_kernel(u_ref, h0_ref, are_ref, aim_ref, bw_ref, cw_ref, dsk_ref, wglu_ref, bglu_ref, gout_ref,
               o_ref, sfin_ref, xs_ref, st_ref, *, steps):
    @pl.when(pl.program_id(1) == 0)
    def _():
        st_ref[...] = h0_ref[...]

    u = u_ref[...]
    ub = u.astype(BF16)
    half_w = SSM_W // 2
    for k in range(2):
        uk = ub[:, k * half_w:(k + 1) * half_w]
        xs_ref[:, k * S5_HALF:(k + 1) * S5_HALF] = _dot(uk, bw_ref[2 * k])
        xs_ref[:, SSM_S + k * S5_HALF:SSM_S + (k + 1) * S5_HALF] = _dot(uk, bw_ref[2 * k + 1])

    for q in range(4):
        lr = slice(q * S5_QUARTER, (q + 1) * S5_QUARTER)
        li = slice(SSM_S + q * S5_QUARTER, SSM_S + (q + 1) * S5_QUARTER)
        ar = are_ref[:, lr]
        ai = aim_ref[:, lr]

        def step(t, carry):
            sr, si = carry
            r0 = pl.multiple_of(t * SUBLANES, SUBLANES)
            nr = ar * sr - ai * si + xs_ref[pl.ds(r0, SUBLANES), lr]
            ni = ar * si + ai * sr + xs_ref[pl.ds(r0, SUBLANES), li]
            xs_ref[pl.ds(r0, SUBLANES), lr] = nr
            xs_ref[pl.ds(r0, SUBLANES), li] = ni
            return nr, ni

        sr, si = lax.fori_loop(0, steps, step, (st_ref[:, lr], st_ref[:, li]), unroll=4)
        st_ref[:, lr] = sr
        st_ref[:, li] = si
    sfin_ref[...] = st_ref[...]

    ys = []
    for k in range(2):
        sre = xs_ref[:, k * S5_HALF:(k + 1) * S5_HALF].astype(BF16)
        sim = xs_ref[:, SSM_S + k * S5_HALF:SSM_S + (k + 1) * S5_HALF].astype(BF16)
        ys.append(_dot(sre, cw_ref[2 * k]) + _dot(sim, cw_ref[2 * k + 1]))
    y = jnp.concatenate(ys, axis=-1) + dsk_ref[...] * u
    g = jax.nn.gelu(y)
    out = g * jax.nn.sigmoid(_dot(g.astype(BF16), wglu_ref[...]) + bglu_ref[...])
    o_ref[...] = _rms(out, gout_ref[...]).astype(BF16)


def _s5(u2, h0, w, *, ngroups, nchunks, steps):
    rows = steps * SUBLANES
    blk = lambda g, t: (g * nchunks + t, 0)
    return pl.pallas_call(
        functools.partial(_s5_kernel, steps=steps),
        grid=(ngroups, nchunks),
        in_specs=[
            pl.BlockSpec((rows, SSM_W), blk),
            pl.BlockSpec((SUBLANES, 2 * SSM_S), lambda g, t: (g, 0)),
            _full(w["s5_are"].shape), _full(w["s5_aim"].shape), _full(w["s5_bw"].shape),
            _full(w["s5_cw"].shape), _full(w["s5_d"].shape), _full(w["wglu"].shape),
            _full(w["bglu"].shape), _full(w["gout_ssm"].shape),
        ],
        out_specs=[
            pl.BlockSpec((rows, SSM_W), blk),
            pl.BlockSpec((SUBLANES, 2 * SSM_S), lambda g, t: (g, 0)),
        ],
        out_shape=[
            jax.ShapeDtypeStruct(u2.shape, BF16),
            jax.ShapeDtypeStruct(h0.shape, F32),
        ],
        scratch_shapes=[pltpu.VMEM((rows, 2 * SSM_S), F32), pltpu.VMEM((SUBLANES, 2 * SSM_S), F32)],
        compiler_params=_cparams("parallel", "arbitrary"),
        name="s5",
    )(u2, h0, w["s5_are"], w["s5_aim"], w["s5_bw"], w["s5_cw"], w["s5_d"], w["wglu"], w["bglu"],
      w["gout_ssm"])


def _attn_p_kernel(q_ref, k_ref, v_ref, o_ref, *, tq):
    i = pl.program_id(2)
    row = lax.broadcasted_iota(jnp.int32, (tq, tq), 0)
    col = lax.broadcasted_iota(jnp.int32, (tq, tq), 1)

    def block(j, carry, masked):
        k0 = pl.multiple_of(j * tq, tq)
        k2 = k_ref[pl.ds(k0, tq), :]
        v2 = v_ref[pl.ds(k0, tq), :]
        out = []
        for hh in range(2):
            m, l, acc = carry[hh]
            s = _dot_nt(q_ref[:, hh * LANES:(hh + 1) * LANES], k2[:, hh * LANES:(hh + 1) * LANES])
            if masked:
                s = jnp.where(col <= row, s, NEG)
            m_new = jnp.maximum(m, jnp.max(s, axis=-1, keepdims=True))
            alpha = jnp.exp(m - m_new)
            p = jnp.exp(s - m_new)
            l = alpha * l + jnp.sum(p, axis=-1, keepdims=True)
            acc = alpha * acc + _dot(p.astype(BF16), v2[:, hh * LANES:(hh + 1) * LANES])
            out.append((m_new, l, acc))
        return tuple(out)

    init = tuple((jnp.full((tq, 1), NEG, F32), jnp.zeros((tq, 1), F32), jnp.zeros((tq, 2 * V_DIM), F32))
                 for _ in range(2))
    carry = lax.fori_loop(0, i, lambda j, c: block(j, c, False), init)
    carry = block(i, carry, True)
    o_ref[...] = carry[0][2] / carry[0][1] + carry[1][2] / carry[1][1]


def _attn_p(q, k, v, *, nb, t, tq):
    nq = t // tq
    return pl.pallas_call(
        functools.partial(_attn_p_kernel, tq=tq),
        grid=(nb, N_HEADS // 2, nq),
        in_specs=[
            pl.BlockSpec((tq, 2 * LANES), lambda b, h, i: (b * nq + i, h)),
            pl.BlockSpec((t, 2 * LANES), lambda b, h, i: (b, h)),
            pl.BlockSpec((t, 2 * LANES), lambda b, h, i: (b, h)),
        ],
        out_specs=pl.BlockSpec((tq, 2 * V_DIM), lambda b, h, i: (b * nq + i, h)),
        out_shape=jax.ShapeDtypeStruct((nb * t, N_HEADS * V_DIM), F32),
        compiler_params=_cparams("parallel", "parallel", "arbitrary"),
        name="attn_p",
    )(q, k, v)


ATTN_S_PAGES = 32
NEW_KEYS = 16


def _attn_s_kernel(pt_ref, ql_ref, qr_ref, cn_ref, krn_ref, rnt_ref, cc_hbm, ckr_hbm, cr_hbm,
                   o_ref, cbuf, krbuf, rbuf, cb_ref, sc_ref, sems, *, n_pages, dec_seq):
    b = pl.program_id(0)
    nb = pl.num_programs(0)
    bp = min(ATTN_S_PAGES, n_pages)
    nblk = n_pages // bp
    keys = bp * PAGE_SIZE
    nrow = dec_seq * N_HEADS
    scale = 1.0 / math.sqrt(QK_DIM)
    slot = b % 2

    def start_block(bb, i, dst):
        for p in range(bp):
            pg = i * bp + p
            page = pt_ref[bb * n_pages + pg]
            lanes = pl.ds(pl.multiple_of(pg * PAGE_SIZE, PAGE_SIZE), PAGE_SIZE)
            pltpu.make_async_copy(cc_hbm.at[page], cbuf.at[dst, pg], sems.at[dst, 0]).start()
            pltpu.make_async_copy(ckr_hbm.at[page], krbuf.at[dst, :, lanes], sems.at[dst, 1]).start()
            pltpu.make_async_copy(cr_hbm.at[page], rbuf.at[dst, :, lanes], sems.at[dst, 2]).start()

    @pl.when(b == 0)
    def _():
        lax.fori_loop(0, nblk, lambda i, c: (start_block(0, i, 0), c)[1], 0)

    pltpu.make_async_copy(cbuf.at[slot], cbuf.at[slot], sems.at[slot, 0]).wait()
    pltpu.make_async_copy(krbuf.at[slot], krbuf.at[slot], sems.at[slot, 1]).wait()
    pltpu.make_async_copy(rbuf.at[slot], rbuf.at[slot], sems.at[slot, 2]).wait()

    ql = ql_ref[0]
    qr = qr_ref[0]

    def tile_heads(r_t):
        return jnp.concatenate([r_t] * dec_seq, axis=0)

    cnb = cn_ref[0].astype(BF16)
    s_n = (_dot_nt(ql, cnb) + _dot(qr, krn_ref[0].astype(BF16))) * tile_heads(rnt_ref[0]) * scale
    qs = lax.broadcasted_iota(jnp.int32, (nrow, NEW_KEYS), 0) // N_HEADS
    kj = lax.broadcasted_iota(jnp.int32, (nrow, NEW_KEYS), 1)
    s_n = jnp.where(kj <= qs, s_n, NEG)

    def score_block(i, m):
        @pl.when(b + 1 < nb)
        def _():
            start_block(b + 1, i, 1 - slot)

        k0 = pl.multiple_of(i * keys, keys)
        cb = cbuf[slot, pl.ds(i * bp, bp)].reshape(keys, KV_LORA).astype(BF16)
        cb_ref[pl.ds(k0, keys), :] = cb
        raw = _dot_nt(ql, cb) + _dot(qr, krbuf[slot, :, pl.ds(k0, keys)].astype(BF16))
        s = raw * tile_heads(rbuf[slot, :, pl.ds(k0, keys)]) * scale
        sc_ref[:, pl.ds(k0, keys)] = s
        return jnp.maximum(m, jnp.max(s, axis=-1, keepdims=True))

    m = lax.fori_loop(0, nblk, score_block, jnp.max(s_n, axis=-1, keepdims=True))
    p_n = jnp.exp(s_n - m)

    def value_block(i, carry):
        l, acc = carry
        k0 = pl.multiple_of(i * keys, keys)
        p = jnp.exp(sc_ref[:, pl.ds(k0, keys)] - m)
        return (l + jnp.sum(p, axis=-1, keepdims=True),
                acc + _dot(p.astype(BF16), cb_ref[pl.ds(k0, keys), :]))

    l, acc = lax.fori_loop(0, nblk, value_block,
                           (jnp.sum(p_n, axis=-1, keepdims=True), _dot(p_n.astype(BF16), cnb)))
    o_ref[0] = acc / l


def _attn_s(page_table, ql, qr, cn, krn, rnt, cache_c, cache_kr, cache_r, *, dec_seq):
    nb, n_pages = page_table.shape
    nrow = dec_seq * N_HEADS
    b3 = lambda b, pt: (b, 0, 0)
    return pl.pallas_call(
        functools.partial(_attn_s_kernel, n_pages=n_pages, dec_seq=dec_seq),
        grid_spec=pltpu.PrefetchScalarGridSpec(
            num_scalar_prefetch=1,
            grid=(nb,),
            in_specs=[
                pl.BlockSpec((1, nrow, KV_LORA), b3),
                pl.BlockSpec((1, nrow, ROPE_DIM), b3),
                pl.BlockSpec((1, NEW_KEYS, KV_LORA), b3),
                pl.BlockSpec((1, ROPE_DIM, NEW_KEYS), b3),
                pl.BlockSpec((1, N_HEADS, NEW_KEYS), b3),
                pl.BlockSpec(memory_space=pl.ANY),
                pl.BlockSpec(memory_space=pl.ANY),
                pl.BlockSpec(memory_space=pl.ANY),
            ],
            out_specs=pl.BlockSpec((1, nrow, KV_LORA), b3),
            scratch_shapes=[
                pltpu.VMEM((2, n_pages, PAGE_SIZE, KV_LORA), F32),
                pltpu.VMEM((2, ROPE_DIM, n_pages * PAGE_SIZE), F32),
                pltpu.VMEM((2, N_HEADS, n_pages * PAGE_SIZE), F32),
                pltpu.VMEM((n_pages * PAGE_SIZE, KV_LORA), BF16),
                pltpu.VMEM((nrow, n_pages * PAGE_SIZE), F32),
                pltpu.SemaphoreType.DMA((2, 3)),
            ],
        ),
        out_shape=jax.ShapeDtypeStruct((nb, nrow, KV_LORA), F32),
        compiler_params=_cparams("arbitrary"),
        name="attn_s",
    )(page_table.reshape(-1), ql, qr, cn, krn, rnt, cache_c, cache_kr, cache_r)


def _latent_out_kernel(ol_ref, wuv_ref, o_ref):
    for h in range(N_HEADS):
        o_ref[:, h * V_DIM:(h + 1) * V_DIM] = _dot(
            ol_ref[:, h * KV_LORA:(h + 1) * KV_LORA].astype(BF16), wuv_ref[h])


def _latent_out(ol, wuv_h):
    n = ol.shape[0]
    return pl.pallas_call(
        _latent_out_kernel,
        grid=(1,),
        in_specs=[_full(ol.shape), _full(wuv_h.shape)],
        out_specs=_full((n, N_HEADS * V_DIM)),
        out_shape=jax.ShapeDtypeStruct((n, N_HEADS * V_DIM), F32),
        compiler_params=_cparams("arbitrary"),
        name="latent_out",
    )(ol, wuv_h)


ROUTE_W = 16


def _mix_kernel(h_ref, os_ref, om_ref, gom_ref, wout_ref, gffn_ref, wr_ref, br_ref,
                h1_ref, xn_ref, route_ref, cnt_ref):
    @pl.when((pl.program_id(0) == 0) & (pl.program_id(1) == 0))
    def _():
        cnt_ref[...] = jnp.zeros_like(cnt_ref)

    tm = h_ref.shape[0]
    omn = _rms(om_ref[...], gom_ref[...]).astype(BF16)
    h1 = h_ref[...] + _dot(os_ref[...], wout_ref[:SSM_W, :]) + _dot(omn, wout_ref[SSM_W:, :])
    h1_ref[...] = h1
    xn = _rms(h1, gffn_ref[...])
    xn_ref[...] = xn
    work = _dot(xn.astype(BF16), wr_ref[...]) + br_ref[...]

    lane = lax.broadcasted_iota(jnp.int32, work.shape, 1).astype(F32)
    route = jnp.zeros(work.shape, F32)
    sels, vals = [], []
    for k in range(TOP_K):
        mk = jnp.max(work, axis=-1, keepdims=True)
        ik = jnp.min(jnp.where(work == mk, lane, float(LANES)), axis=-1, keepdims=True)
        sel = lane == ik
        work = jnp.where(sel, -jnp.inf, work)
        route = jnp.where(lane == float(k), ik, route)
        sels.append(sel)
        vals.append(mk)
    es = [jnp.exp(v - vals[0]) for v in vals]
    den = es[0] + es[1] + es[2] + es[3]
    onehot = jnp.zeros(work.shape, F32)
    for k in range(TOP_K):
        route = jnp.where(lane == float(TOP_K + k), es[k] / den, route)
        onehot = jnp.where(sels[k], 1.0, onehot)

    r_i = lax.broadcasted_iota(jnp.int32, (tm, tm), 0)
    c_i = lax.broadcasted_iota(jnp.int32, (tm, tm), 1)
    tri = jnp.where(c_i < r_i, 1.0, 0.0).astype(BF16)
    cum = _dot(tri, onehot.astype(BF16)) + cnt_ref[...]
    for k in range(TOP_K):
        rank = jnp.sum(jnp.where(sels[k], cum, 0.0), axis=-1, keepdims=True)
        route = jnp.where(lane == float(2 * TOP_K + k), rank, route)
    route_ref[...] = route[:, :ROUTE_W]
    cnt_ref[...] = cnt_ref[...] + jnp.sum(onehot, axis=0, keepdims=True)


def _mix(h2, o_ssm, o_mla, w, *, nb, nt, tm):
    n = h2.shape[0]
    row = lambda b, t: (b * nt + t, 0)
    return pl.pallas_call(
        _mix_kernel,
        grid=(nb, nt),
        in_specs=[
            pl.BlockSpec((tm, D_MODEL), row),
            pl.BlockSpec((tm, SSM_W), lambda b, t: (t, b)),
            pl.BlockSpec((tm, N_HEADS * V_DIM), row),
            _full(w["gout_mla"].shape), _full(w["wout"].shape), _full(w["gffn"].shape),
            _full(w["wr"].shape), _full(w["br"].shape),
        ],
        out_specs=[
            pl.BlockSpec((tm, D_MODEL), row),
            pl.BlockSpec((tm, D_MODEL), row),
            pl.BlockSpec((tm, ROUTE_W), row),
            _full((1, LANES)),
        ],
        out_shape=[
            jax.ShapeDtypeStruct((n, D_MODEL), F32),
            jax.ShapeDtypeStruct((n, D_MODEL), F32),
            jax.ShapeDtypeStruct((n, ROUTE_W), F32),
            jax.ShapeDtypeStruct((1, LANES), F32),
        ],
        compiler_params=_cparams("arbitrary", "arbitrary"),
        name="mix",
    )(h2, o_ssm, o_mla, w["gout_mla"], w["wout"], w["gffn"], w["wr"], w["br"])


def _experts_kernel(tile_ref, e_ref, first_ref, valid_ref, lo_ref, hi_ref,
                    x_ref, wgu_ref, bgu_ref, wd_ref, bd_ref, o_ref):
    i = pl.program_id(0)

    @pl.when(valid_ref[i] == 1)
    def _():
        bm = x_ref.shape[0]
        hdn = _dot(x_ref[...].astype(BF16), wgu_ref[0]) + bgu_ref[0]
        x_glu = jnp.minimum(hdn[:, :D_FF], SWIGLU_LIMIT)
        x_lin = jnp.clip(hdn[:, D_FF:], -SWIGLU_LIMIT, SWIGLU_LIMIT)
        act = x_glu * jax.nn.sigmoid(SWIGLU_ALPHA * x_glu) * (x_lin + 1.0)
        y = _dot(act.astype(BF16), wd_ref[0]) + bd_ref[0]
        rows = tile_ref[i] * bm + lax.broadcasted_iota(jnp.int32, (bm, 1), 0)
        mine = (rows >= lo_ref[i]) & (rows < hi_ref[i])

        @pl.when(first_ref[i] == 1)
        def _():
            o_ref[...] = jnp.where(mine, y, 0.0)

        @pl.when(first_ref[i] == 0)
        def _():
            o_ref[...] = jnp.where(mine, y, o_ref[...])


def _experts(xs, items, w, *, bm):
    r = xs.shape[0]
    n_items = items[0].shape[0]
    xmap = lambda i, tile, e, first, valid, lo, hi: (tile[i], 0)
    emap = lambda i, tile, e, first, valid, lo, hi: (e[i], 0, 0)
    return pl.pallas_call(
        _experts_kernel,
        grid_spec=pltpu.PrefetchScalarGridSpec(
            num_scalar_prefetch=6,
            grid=(n_items,),
            in_specs=[
                pl.BlockSpec((bm, D_MODEL), xmap),
                pl.BlockSpec((1, D_MODEL, 2 * D_FF), emap),
                pl.BlockSpec((1, 1, 2 * D_FF), emap),
                pl.BlockSpec((1, D_FF, D_MODEL), emap),
                pl.BlockSpec((1, 1, D_MODEL), emap),
            ],
            out_specs=pl.BlockSpec((bm, D_MODEL), xmap),
        ),
        out_shape=jax.ShapeDtypeStruct((r, D_MODEL), F32),
        compiler_params=_cparams("arbitrary"),
        name="experts",
    )(*items, xs, w["wgu"], w["bgu"], w["wd"], w["bd"])


def _expert_items(counts, n_rows, bm):
    n_tiles = n_rows // bm
    n_items = n_tiles + N_EXPERTS - 1
    ends = jnp.cumsum(counts)
    starts = ends - counts
    first_tile = starts // bm
    per_e = jnp.where(counts > 0, (ends - 1) // bm - first_tile + 1, 0)
    item_end = jnp.cumsum(per_e)
    i = jnp.arange(n_items, dtype=jnp.int32)
    valid = i < item_end[-1]
    e = jnp.minimum(jnp.sum(i[:, None] >= item_end[None, :], axis=1), N_EXPERTS - 1).astype(jnp.int32)
    tile = first_tile[e] + i - (item_end[e] - per_e[e])
    tile = jnp.where(valid, tile, n_tiles - 1).astype(jnp.int32)
    first = jnp.concatenate([jnp.ones((1,), jnp.int32), (tile[1:] != tile[:-1]).astype(jnp.int32)])
    return (tile, e, first, valid.astype(jnp.int32),
            starts[e].astype(jnp.int32), ends[e].astype(jnp.int32))


def _ple_kernel(h1_ref, yg_ref, route_ref, p_ref, gple_ref, wg_ref, wp_ref, o_ref):
    h2 = h1_ref[...]
    route = route_ref[...]
    for k in range(TOP_K):
        h2 = h2 + yg_ref[k] * route[:, TOP_K + k:TOP_K + k + 1]
    gate = jax.nn.sigmoid(_dot(_rms(h2, gple_ref[...]).astype(BF16), wg_ref[...]))
    o_ref[...] = h2 + _dot(p_ref[...].astype(BF16), wp_ref[...]) * gate


def _ple(h1, yg, route, p, w, *, tm):
    n = h1.shape[0]
    row = lambda t: (t, 0)
    return pl.pallas_call(
        _ple_kernel,
        grid=(n // tm,),
        in_specs=[
            pl.BlockSpec((tm, D_MODEL), row),
            pl.BlockSpec((TOP_K, tm, D_MODEL), lambda t: (0, t, 0)),
            pl.BlockSpec((tm, ROUTE_W), row),
            pl.BlockSpec((tm, PLE_DIM), row),
            _full(w["gple"].shape), _full(w["wpg"].shape), _full(w["wp"].shape),
        ],
        out_specs=pl.BlockSpec((tm, D_MODEL), row),
        out_shape=jax.ShapeDtypeStruct((n, D_MODEL), F32),
        compiler_params=_cparams("parallel"),
        name="ple",
    )(h1, yg, route, p, w["gple"], w["wpg"], w["wp"])


SC_INDEX_ROW = 128
SC_SCATTER_TOKENS = SC_INDEX_ROW // TOP_K
SC_GATHER_TOKENS = 8


def _sc_mesh():
    return plsc.VectorSubcoreMesh(core_axis_name="c", subcore_axis_name="s")


def _sc_dispatch(xn, dest):
    n = xn.shape[0]
    w = SC_SCATTER_TOKENS
    idx = dest.reshape(n // w, w, TOP_K).transpose(0, 2, 1).reshape(n // w, SC_INDEX_ROW)

    @pl.kernel(out_type=jax.ShapeDtypeStruct((n * TOP_K, D_MODEL), xn.dtype),
               mesh=_sc_mesh(), scratch_types=[])
    def scatter(x_hbm, i_hbm, o_hbm):
        def body(x_vmem, i_vmem):
            for k in range(TOP_K):
                pltpu.sync_copy(x_vmem, o_hbm.at[i_vmem.at[0, pl.ds(k * w, w)]])

        pltpu.emit_pipeline(
            body,
            grid=(n // w,),
            in_specs=[pl.BlockSpec((w, D_MODEL), lambda i: (i, 0)),
                      pl.BlockSpec((1, SC_INDEX_ROW), lambda i: (i, 0))],
            out_specs=[],
            core_axis_name=("c", "s"),
            dimension_semantics=(pltpu.PARALLEL,),
        )(x_hbm, i_hbm)

    return scatter(xn, idx)


def _sc_combine(ys, dest):
    n = dest.shape[0]
    w = SC_GATHER_TOKENS
    idx = dest.reshape(n // w, w, TOP_K).transpose(0, 2, 1).reshape(n // w, TOP_K * w)
    idx = jnp.concatenate([idx, jnp.zeros((n // w, SC_INDEX_ROW - TOP_K * w), jnp.int32)], axis=-1)

    @pl.kernel(out_type=jax.ShapeDtypeStruct((TOP_K, n, D_MODEL), ys.dtype),
               mesh=_sc_mesh(), scratch_types=[])
    def gather(y_hbm, i_hbm, o_hbm):
        def body(i_vmem, o_vmem):
            for k in range(TOP_K):
                pltpu.sync_copy(y_hbm.at[i_vmem.at[0, pl.ds(k * w, w)]], o_vmem.at[k])

        pltpu.emit_pipeline(
            body,
            grid=(n // w,),
            in_specs=[pl.BlockSpec((1, SC_INDEX_ROW), lambda i: (i, 0))],
            out_specs=[pl.BlockSpec((TOP_K, w, D_MODEL), lambda i: (0, i, 0))],
            core_axis_name=("c", "s"),
            dimension_semantics=(pltpu.PARALLEL,),
        )(i_hbm, o_hbm)

    return gather(ys, idx)


def _rope_tables(pos):
    inv = ROPE_THETA ** (-jnp.arange(0, ROPE_DIM, 2, dtype=F32) / ROPE_DIM)
    ang = pos.astype(F32)[:, None] * inv[None, :]
    cos, sin = jnp.cos(ang), jnp.sin(ang)
    n = pos.shape[0]
    pad = jnp.zeros((n, LANES - QK_DIM), F32)
    ct = jnp.concatenate([jnp.ones((n, NOPE_DIM), F32), cos, cos, pad], axis=-1)
    st = jnp.concatenate([jnp.zeros((n, NOPE_DIM), F32), sin, sin, pad], axis=-1)
    return ct, st


def _rope_partner(w_rope):
    return jnp.concatenate([-w_rope[..., ROPE_HALF:], w_rope[..., :ROPE_HALF]], axis=-1)


def _head_tile(nope, rope):
    pad = jnp.zeros(rope.shape[:-1] + (LANES - QK_DIM,), rope.dtype)
    return jnp.concatenate([nope, rope, pad], axis=-1)


def _block_diag(m):
    g, a, b = m.shape
    hg = g // 2
    eye = jnp.eye(hg, dtype=m.dtype)
    mh = m.reshape(2, hg, a, b)
    return jnp.einsum("kgab,gh->kgahb", mh, eye).reshape(2, hg * a, hg * b)


def _prep_weights(g_mix, w_in, s5_a_re, s5_a_im, s5_log_dt, s5_b_re, s5_b_im, s5_c_re, s5_c_im, s5_d,
                  w_glu, b_glu, g_q_lora, w_uq, g_kv_lora, w_uk, w_uv, g_qk_q, g_qk_k, g_out_ssm,
                  g_out_mla, w_out, g_ffn, w_router, b_router, w_gate_up, b_gate_up, w_down, b_down,
                  g_ple, w_ple_gate, w_ple):
    w = {}
    row = lambda v: v.reshape(1, -1).astype(F32)
    o = SSM_W + Q_LORA + KV_LORA
    w_kr = w_in[:, o:o + ROPE_DIM]
    zeros_n = jnp.zeros((D_MODEL, NOPE_DIM), F32)
    w["win"] = jnp.concatenate(
        [w_in[:, :o], _head_tile(zeros_n, w_kr), _head_tile(zeros_n, _rope_partner(w_kr))], axis=-1).astype(BF16)
    w["gmix"] = row(g_mix)
    w["gql"] = row(g_q_lora)
    w["gkv"] = row(g_kv_lora)

    wq = w_uq.reshape(Q_LORA, N_HEADS, QK_DIM)
    q1 = _head_tile(wq[..., :NOPE_DIM], wq[..., NOPE_DIM:])
    q2 = _head_tile(jnp.zeros_like(wq[..., :NOPE_DIM]), _rope_partner(wq[..., NOPE_DIM:]))
    w["wuq"] = jnp.concatenate([q1.reshape(Q_LORA, HP), q2.reshape(Q_LORA, HP)], axis=-1).astype(BF16)
    w["wuk"] = jnp.concatenate(
        [w_uk, jnp.zeros((KV_LORA, N_HEADS, LANES - NOPE_DIM), F32)], axis=-1).reshape(KV_LORA, HP).astype(BF16)
    zv = jnp.zeros((KV_LORA, N_HEADS // 2, V_DIM), F32)
    w["wuv"] = jnp.stack([w_uv[:, 0::2], zv, zv, w_uv[:, 1::2]], axis=2).reshape(KV_LORA, HP).astype(BF16)
    w["wuv_h"] = w_uv.transpose(1, 0, 2).astype(BF16)
    wukT = w_uk.transpose(1, 2, 0)
    w["wukT"] = jnp.concatenate(
        [wukT, jnp.zeros((N_HEADS, LANES - NOPE_DIM, KV_LORA), F32)], axis=1).astype(BF16)
    gpad = jnp.zeros((LANES - QK_DIM,), F32)
    w["gq_p"] = row(jnp.concatenate([g_qk_q * (1.0 / math.sqrt(QK_DIM)), gpad]))
    w["gq_s"] = row(jnp.concatenate([g_qk_q, gpad]))
    w["gk"] = row(jnp.concatenate([g_qk_k, gpad]))

    dt = jnp.exp(s5_log_dt.astype(F32))[:, None]
    ar, ai = s5_a_re.astype(F32), s5_a_im.astype(F32)
    mag = jnp.exp(dt * ar)
    abar_re, abar_im = mag * jnp.cos(dt * ai), mag * jnp.sin(dt * ai)
    den = ar * ar + ai * ai
    nr, ni = abar_re - 1.0, abar_im
    coef_re = (nr * ar + ni * ai) / den
    coef_im = (ni * ar - nr * ai) / den
    br, bi = s5_b_re.astype(F32), s5_b_im.astype(F32)
    bbar_re = coef_re[..., None] * br - coef_im[..., None] * bi
    bbar_im = coef_re[..., None] * bi + coef_im[..., None] * br
    bre = _block_diag(bbar_re.transpose(0, 2, 1))
    bim = _block_diag(bbar_im.transpose(0, 2, 1))
    w["s5_bw"] = jnp.stack([bre[0], bim[0], bre[1], bim[1]]).astype(BF16)
    cre = _block_diag(s5_c_re.astype(F32).transpose(0, 2, 1))
    cim = _block_diag(-s5_c_im.astype(F32).transpose(0, 2, 1))
    w["s5_cw"] = jnp.stack([cre[0], cim[0], cre[1], cim[1]]).astype(BF16)
    w["s5_are"] = jnp.broadcast_to(abar_re.reshape(1, SSM_S), (SUBLANES, SSM_S))
    w["s5_aim"] = jnp.broadcast_to(abar_im.reshape(1, SSM_S), (SUBLANES, SSM_S))
    w["s5_d"] = row(s5_d)
    w["wglu"] = w_glu.astype(BF16)
    w["bglu"] = row(b_glu)
    w["gout_ssm"] = row(g_out_ssm)
    w["gout_mla"] = row(g_out_mla)
    w["wout"] = w_out.astype(BF16)
    w["gffn"] = row(g_ffn)
    w["wr"] = jnp.concatenate([w_router, jnp.zeros((D_MODEL, LANES - N_EXPERTS), F32)], axis=-1).astype(BF16)
    w["br"] = row(jnp.concatenate([b_router.astype(F32), jnp.full((LANES - N_EXPERTS,), NEG, F32)]))
    w["wgu"] = w_gate_up.astype(BF16)
    w["bgu"] = b_gate_up.reshape(N_EXPERTS, 1, 2 * D_FF).astype(F32)
    w["wd"] = w_down.astype(BF16)
    w["bd"] = b_down.reshape(N_EXPERTS, 1, D_MODEL).astype(F32)
    w["gple"] = row(g_ple)
    w["wpg"] = w_ple_gate.astype(BF16)
    w["wp"] = w_ple.astype(BF16)
    return w


EXPERT_ROWS = 512


def _channel(mixed, p2, w, *, tm):
    h1, xn, route, cnt = mixed
    n = h1.shape[0]
    top_i = route[:, :TOP_K].astype(jnp.int32)
    rank = route[:, 2 * TOP_K:3 * TOP_K].astype(jnp.int32)
    counts = cnt[0, :N_EXPERTS].astype(jnp.int32)
    starts = jnp.cumsum(counts) - counts
    dest = starts[top_i] + rank
    xs = _sc_dispatch(xn, dest)
    ys = _experts(xs, _expert_items(counts, n * TOP_K, EXPERT_ROWS), w, bm=EXPERT_ROWS)
    return _ple(h1, _sc_combine(ys, dest), route, p2, w, tm=tm)


def kernel(x_prompt, x_sample, cache_kv_latent, cache_k_rope, cache_k_rstd, state_ssm_re, state_ssm_im,
           page_table, p_prompt, p_sample, g_mix, w_in, s5_a_re, s5_a_im, s5_log_dt, s5_b_re, s5_b_im,
           s5_c_re, s5_c_im, s5_d, w_glu, b_glu, g_q_lora, w_uq, g_kv_lora, w_uk, w_uv, g_qk_q, g_qk_k,
           g_out_ssm, g_out_mla, w_out, g_ffn, w_router, b_router, w_gate_up, b_gate_up, w_down, b_down,
           g_ple, w_ple_gate, w_ple):
    assert g_mix.shape[0] == 1, "single-layer step"
    nb, t, _ = x_prompt.shape
    db, ds, _ = x_sample.shape
    assert nb == SUBLANES and db % SUBLANES == 0
    w = _prep_weights(g_mix[0], w_in[0], s5_a_re[0], s5_a_im[0], s5_log_dt[0], s5_b_re[0], s5_b_im[0],
                      s5_c_re[0], s5_c_im[0], s5_d[0], w_glu[0], b_glu[0], g_q_lora[0], w_uq[0],
                      g_kv_lora[0], w_uk[0], w_uv[0], g_qk_q[0], g_qk_k[0], g_out_ssm[0], g_out_mla[0],
                      w_out[0], g_ffn[0], w_router[0], b_router[0], w_gate_up[0], b_gate_up[0],
                      w_down[0], b_down[0], g_ple[0], w_ple_gate[0], w_ple[0])

    tm = min(512, t)
    nt = t // tm
    xp = x_prompt.reshape(nb * t, D_MODEL)
    ct, st = _rope_tables(jnp.arange(t))
    u, q, k, v, c_p, kr_p, rstd_p = _proj(xp, ct, st, w, nb=nb, nt=nt, tm=tm, sample=False)
    steps = min(128, t)
    o_ssm, s_fin = _s5(u.reshape(t * nb, SSM_W), jnp.zeros((nb, 2 * SSM_S), F32), w,
                       ngroups=1, nchunks=t // steps, steps=steps)
    o_mla = _attn_p(q, k, v, nb=nb, t=t, tq=tm)
    mixed_p = _mix(xp, o_ssm.reshape(t, nb * SSM_W), o_mla, w, nb=nb, nt=nt, tm=tm)
    sr_p = s_fin[:, :SSM_S].reshape(1, nb, SSM_G, SSM_P)
    si_p = s_fin[:, SSM_S:].reshape(1, nb, SSM_G, SSM_P)

    ns = db * ds
    ng = db // SUBLANES
    xs_ = x_sample.reshape(ns, D_MODEL)
    ct_s, st_s = _rope_tables(PAST_LEN + jnp.tile(jnp.arange(ds), db))
    u_s, qf, qlat, c_s, kr_s, rstd_s = _proj(xs_, ct_s, st_s, w, nb=1, nt=1, tm=ns, sample=True)
    to_scan = lambda a: a.reshape(ng, SUBLANES, ds, -1).transpose(0, 2, 1, 3).reshape(ns, -1)
    from_scan = lambda a: a.reshape(ng, ds, SUBLANES, -1).transpose(0, 2, 1, 3).reshape(ns, -1)
    h0 = jnp.concatenate([state_ssm_re[0].reshape(db, SSM_S), state_ssm_im[0].reshape(db, SSM_S)],
                         axis=-1).astype(F32)
    o_ssm_s, s_fin_s = _s5(to_scan(u_s), h0, w, ngroups=ng, nchunks=1, steps=ds)
    pad_keys = lambda a: jnp.concatenate(
        [a, jnp.zeros((db, NEW_KEYS - ds, a.shape[-1]), a.dtype)], axis=1)
    key_major = lambda a: a.transpose(0, 2, 1)
    o_lat = _attn_s(page_table,
                    qlat.reshape(db, ds * N_HEADS, KV_LORA),
                    qf.reshape(db, ds * N_HEADS, LANES)[:, :, NOPE_DIM:QK_DIM],
                    pad_keys(c_s.reshape(db, ds, KV_LORA)),
                    key_major(pad_keys(kr_s.reshape(db, ds, ROPE_DIM))),
                    key_major(pad_keys(rstd_s.reshape(db, ds, N_HEADS))),
                    cache_kv_latent[0], key_major(cache_k_rope[0]), key_major(cache_k_rstd[0]), dec_seq=ds)
    o_mla_s = _latent_out(o_lat.reshape(ns, N_HEADS * KV_LORA), w["wuv_h"])
    mixed_s = _mix(xs_, from_scan(o_ssm_s), o_mla_s, w, nb=1, nt=1, tm=ns)

    y_p = _channel(mixed_p, p_prompt[0].reshape(nb * t, PLE_DIM), w, tm=tm)
    y_s = _channel(mixed_s, p_sample[0].reshape(ns, PLE_DIM), w, tm=ns)

    return (y_p.reshape(nb, t, D_MODEL), y_s.reshape(db, ds, D_MODEL),
            c_p.reshape(1, nb, t, KV_LORA), kr_p.reshape(1, nb, t, ROPE_DIM), rstd_p.reshape(1, nb, t, N_HEADS),
            sr_p, si_p,
            c_s.reshape(1, db, ds, KV_LORA), kr_s.reshape(1, db, ds, ROPE_DIM), rstd_s.reshape(1, db, ds, N_HEADS),
            s_fin_s[:, :SSM_S].reshape(1, db, SSM_G, SSM_P), s_fin_s[:, SSM_S:].reshape(1, db, SSM_G, SSM_P))
```

```python
import functools
import math

import jax
import jax.numpy as jnp
from jax import lax
from jax.experimental import pallas as pl
from jax.experimental.pallas import tpu as pltpu
from jax.experimental.pallas import tpu_sc as plsc

D_MODEL = 1024
SSM_W = 512
SSM_GC = 16
SSM_G = SSM_W // SSM_GC
SSM_P = 64
SSM_S = SSM_G * SSM_P
N_HEADS = 8
NOPE_DIM = 64
ROPE_DIM = 32
ROPE_HALF = ROPE_DIM // 2
QK_DIM = NOPE_DIM + ROPE_DIM
V_DIM = 64
Q_LORA = 384
KV_LORA = 256
ROPE_THETA = 10000.0
N_EXPERTS = 32
TOP_K = 4
D_FF = D_MODEL
SWIGLU_ALPHA = 1.702
SWIGLU_LIMIT = 7.0
PLE_DIM = 256
PAST_LEN = 16384
PAGE_SIZE = 128
EPS = 1e-6
NEG = -1e30

LANES = 128
SUBLANES = 8
VMEM_LIMIT = 56 * 1024 * 1024

HP = N_HEADS * LANES
BF16 = jnp.bfloat16
F32 = jnp.float32


def _cparams(*sem):
    return pltpu.CompilerParams(dimension_semantics=sem, vmem_limit_bytes=VMEM_LIMIT)


def _rms(x, g):
    r = lax.rsqrt(jnp.mean(x * x, axis=-1, keepdims=True) + EPS)
    return x * r * g


def _dot(a, b):
    return jnp.dot(a, b, preferred_element_type=F32)


def _dot_nt(a, b):
    return lax.dot_general(a, b, (((1,), (1,)), ((), ())), preferred_element_type=F32)


def _full(shape):
    nd = len(shape)
    return pl.BlockSpec(shape, lambda *_: (0,) * nd)


def _proj_kernel(x_ref, ct_ref, st_ref, gmix_ref, win_ref, gql_ref, wuq_ref, gkv_ref, wuk_ref,
                 w2_ref, gq_ref, gk_ref, u_ref, q_ref, k2_ref, *rest, sample):
    if sample:
        c_ref, kr_ref, rstd_ref = rest
    else:
        v2_ref, c_ref, kr_ref, rstd_ref = rest
    xn = _rms(x_ref[...], gmix_ref[...]).astype(BF16)
    z = _dot(xn, win_ref[...])
    u_ref[...] = z[:, :SSM_W]
    o = SSM_W
    cq = z[:, o:o + Q_LORA]
    o += Q_LORA
    ckv = z[:, o:o + KV_LORA]
    o += KV_LORA
    ct = ct_ref[...]
    st = st_ref[...]
    kblock = z[:, o:o + LANES] * ct + z[:, o + LANES:o + 2 * LANES] * st
    kr_ref[...] = kblock[:, NOPE_DIM:QK_DIM]

    qq = _dot(_rms(cq, gql_ref[...]).astype(BF16), wuq_ref[...])
    c = _rms(ckv, gkv_ref[...])
    c_ref[...] = c
    cb = c.astype(BF16)
    kn = _dot(cb, wuk_ref[...])
    if not sample:
        v2_ref[...] = _dot(cb, w2_ref[...]).astype(BF16)

    lane = lax.broadcasted_iota(jnp.int32, ct.shape, 1)
    rstd_all = jnp.zeros(ct.shape, F32)
    gq = gq_ref[...]
    gk = gk_ref[...]
    for h in range(N_HEADS):
        sl = slice(h * LANES, (h + 1) * LANES)
        qr = qq[:, sl] * ct + qq[:, HP + h * LANES:HP + (h + 1) * LANES] * st
        qn = qr * lax.rsqrt(jnp.sum(qr * qr, axis=-1, keepdims=True) * (1.0 / QK_DIM) + EPS) * gq
        kh = kn[:, sl] + kblock
        rs = lax.rsqrt(jnp.sum(kh * kh, axis=-1, keepdims=True) * (1.0 / QK_DIM) + EPS)
        rstd_all = jnp.where(lane == h, rs, rstd_all)
        if sample:
            qf = (qn * gk).astype(BF16)
            q_ref[:, sl] = qf
            k2_ref[:, h * KV_LORA:(h + 1) * KV_LORA] = _dot(qf, w2_ref[h]).astype(BF16)
        else:
            q_ref[:, sl] = qn.astype(BF16)
            k2_ref[:, sl] = (kh * rs * gk).astype(BF16)
    rstd_ref[...] = rstd_all[:, :N_HEADS]


def _proj(x2, ct, st, w, *, nb, nt, tm, sample):
    n = x2.shape[0]
    row = lambda b, t: (b * nt + t, 0)
    rows = lambda width: pl.BlockSpec((tm, width), row)
    k2_w = N_HEADS * KV_LORA if sample else HP
    w2 = w["wukT"] if sample else w["wuv"]
    out_shape = [
        jax.ShapeDtypeStruct((nt * tm, nb * SSM_W), F32),
        jax.ShapeDtypeStruct((n, HP), BF16),
        jax.ShapeDtypeStruct((n, k2_w), BF16),
        *([] if sample else [jax.ShapeDtypeStruct((n, HP), BF16)]),
        jax.ShapeDtypeStruct((n, KV_LORA), F32),
        jax.ShapeDtypeStruct((n, ROPE_DIM), F32),
        jax.ShapeDtypeStruct((n, N_HEADS), F32),
    ]
    out_specs = [
        pl.BlockSpec((tm, SSM_W), lambda b, t: (t, b)),
        rows(HP), rows(k2_w), *([] if sample else [rows(HP)]),
        rows(KV_LORA), rows(ROPE_DIM), rows(N_HEADS),
    ]
    in_specs = [
        rows(D_MODEL),
        pl.BlockSpec((tm, LANES), lambda b, t: (t, 0)),
        pl.BlockSpec((tm, LANES), lambda b, t: (t, 0)),
        _full(w["gmix"].shape), _full(w["win"].shape), _full(w["gql"].shape), _full(w["wuq"].shape),
        _full(w["gkv"].shape), _full(w["wuk"].shape), _full(w2.shape),
        _full(w["gq_s" if sample else "gq_p"].shape), _full(w["gk"].shape),
    ]
    return pl.pallas_call(
        functools.partial(_proj_kernel, sample=sample),
        grid=(nb, nt),
        in_specs=in_specs,
        out_specs=out_specs,
        out_shape=out_shape,
        compiler_params=_cparams("parallel", "parallel"),
        name="proj_s" if sample else "proj_p",
    )(x2, ct, st, w["gmix"], w["win"], w["gql"], w["wuq"], w["gkv"], w["wuk"], w2,
      w["gq_s" if sample else "gq_p"], w["gk"])


S5_HALF = SSM_S // 2
S5_QUARTER = SSM_S // 4


def _s5_kernel(u_ref, h0_ref, are_ref, aim_ref, bw_ref, cw_ref, dsk_ref, wglu_ref, bglu_ref, gout_ref,
               o_ref, sfin_ref, xs_ref, st_ref, *, steps):
    @pl.when(pl.program_id(1) == 0)
    def _():
        st_ref[...] = h0_ref[...]

    u = u_ref[...]
    ub = u.astype(BF16)
    half_w = SSM_W // 2
    for k in range(2):
        uk = ub[:, k * half_w:(k + 1) * half_w]
        xs_ref[:, k * S5_HALF:(k + 1) * S5_HALF] = _dot(uk, bw_ref[2 * k])
        xs_ref[:, SSM_S + k * S5_HALF:SSM_S + (k + 1) * S5_HALF] = _dot(uk, bw_ref[2 * k + 1])

    for q in range(4):
        lr = slice(q * S5_QUARTER, (q + 1) * S5_QUARTER)
        li = slice(SSM_S + q * S5_QUARTER, SSM_S + (q + 1) * S5_QUARTER)
        ar = are_ref[:, lr]
        ai = aim_ref[:, lr]

        def step(t, carry):
            sr, si = carry
            r0 = pl.multiple_of(t * SUBLANES, SUBLANES)
            nr = ar * sr - ai * si + xs_ref[pl.ds(r0, SUBLANES), lr]
            ni = ar * si + ai * sr + xs_ref[pl.ds(r0, SUBLANES), li]
            xs_ref[pl.ds(r0, SUBLANES), lr] = nr
            xs_ref[pl.ds(r0, SUBLANES), li] = ni
            return nr, ni

        sr, si = lax.fori_loop(0, steps, step, (st_ref[:, lr], st_ref[:, li]), unroll=4)
        st_ref[:, lr] = sr
        st_ref[:, li] = si
    sfin_ref[...] = st_ref[...]

    ys = []
    for k in range(2):
        sre = xs_ref[:, k * S5_HALF:(k + 1) * S5_HALF].astype(BF16)
        sim = xs_ref[:, SSM_S + k * S5_HALF:SSM_S + (k + 1) * S5_HALF].astype(BF16)
        ys.append(_dot(sre, cw_ref[2 * k]) + _dot(sim, cw_ref[2 * k + 1]))
    y = jnp.concatenate(ys, axis=-1) + dsk_ref[...] * u
    g = jax.nn.gelu(y)
    out = g * jax.nn.sigmoid(_dot(g.astype(BF16), wglu_ref[...]) + bglu_ref[...])
    o_ref[...] = _rms(out, gout_ref[...]).astype(BF16)


def _s5(u2, h0, w, *, ngroups, nchunks, steps):
    rows = steps * SUBLANES
    blk = lambda g, t: (g * nchunks + t, 0)
    return pl.pallas_call(
        functools.partial(_s5_kernel, steps=steps),
        grid=(ngroups, nchunks),
        in_specs=[
            pl.BlockSpec((rows, SSM_W), blk),
            pl.BlockSpec((SUBLANES, 2 * SSM_S), lambda g, t: (g, 0)),
            _full(w["s5_are"].shape), _full(w["s5_aim"].shape), _full(w["s5_bw"].shape),
            _full(w["s5_cw"].shape), _full(w["s5_d"].shape), _full(w["wglu"].shape),
            _full(w["bglu"].shape), _full(w["gout_ssm"].shape),
        ],
        out_specs=[
            pl.BlockSpec((rows, SSM_W), blk),
            pl.BlockSpec((SUBLANES, 2 * SSM_S), lambda g, t: (g, 0)),
        ],
        out_shape=[
            jax.ShapeDtypeStruct(u2.shape, BF16),
            jax.ShapeDtypeStruct(h0.shape, F32),
        ],
        scratch_shapes=[pltpu.VMEM((rows, 2 * SSM_S), F32), pltpu.VMEM((SUBLANES, 2 * SSM_S), F32)],
        compiler_params=_cparams("parallel", "arbitrary"),
        name="s5",
    )(u2, h0, w["s5_are"], w["s5_aim"], w["s5_bw"], w["s5_cw"], w["s5_d"], w["wglu"], w["bglu"],
      w["gout_ssm"])


def _attn_p_kernel(q_ref, k_ref, v_ref, o_ref, *, tq):
    i = pl.program_id(2)
    row = lax.broadcasted_iota(jnp.int32, (tq, tq), 0)
    col = lax.broadcasted_iota(jnp.int32, (tq, tq), 1)

    def block(j, carry, masked):
        k0 = pl.multiple_of(j * tq, tq)
        k2 = k_ref[pl.ds(k0, tq), :]
        v2 = v_ref[pl.ds(k0, tq), :]
        out = []
        for hh in range(2):
            m, l, acc = carry[hh]
            s = _dot_nt(q_ref[:, hh * LANES:(hh + 1) * LANES], k2[:, hh * LANES:(hh + 1) * LANES])
            if masked:
                s = jnp.where(col <= row, s, NEG)
            m_new = jnp.maximum(m, jnp.max(s, axis=-1, keepdims=True))
            alpha = jnp.exp(m - m_new)
            p = jnp.exp(s - m_new)
            l = alpha * l + jnp.sum(p, axis=-1, keepdims=True)
            acc = alpha * acc + _dot(p.astype(BF16), v2[:, hh * LANES:(hh + 1) * LANES])
            out.append((m_new, l, acc))
        return tuple(out)

    init = tuple((jnp.full((tq, 1), NEG, F32), jnp.zeros((tq, 1), F32), jnp.zeros((tq, 2 * V_DIM), F32))
                 for _ in range(2))
    carry = lax.fori_loop(0, i, lambda j, c: block(j, c, False), init)
    carry = block(i, carry, True)
    o_ref[...] = carry[0][2] / carry[0][1] + carry[1][2] / carry[1][1]


def _attn_p(q, k, v, *, nb, t, tq):
    nq = t // tq
    return pl.pallas_call(
        functools.partial(_attn_p_kernel, tq=tq),
        grid=(nb, N_HEADS // 2, nq),
        in_specs=[
            pl.BlockSpec((tq, 2 * LANES), lambda b, h, i: (b * nq + i, h)),
            pl.BlockSpec((t, 2 * LANES), lambda b, h, i: (b, h)),
            pl.BlockSpec((t, 2 * LANES), lambda b, h, i: (b, h)),
        ],
        out_specs=pl.BlockSpec((tq, 2 * V_DIM), lambda b, h, i: (b * nq + i, h)),
        out_shape=jax.ShapeDtypeStruct((nb * t, N_HEADS * V_DIM), F32),
        compiler_params=_cparams("parallel", "parallel", "arbitrary"),
        name="attn_p",
    )(q, k, v)


ATTN_S_PAGES = 32
NEW_KEYS = 16


def _attn_s_kernel(pt_ref, ql_ref, qr_ref, cn_ref, krn_ref, rnt_ref, cc_hbm, ckr_hbm, cr_hbm,
                   o_ref, cbuf, krbuf, rbuf, cb_ref, sc_ref, sems, *, n_pages, dec_seq):
    b = pl.program_id(0)
    nb = pl.num_programs(0)
    bp = min(ATTN_S_PAGES, n_pages)
    nblk = n_pages // bp
    keys = bp * PAGE_SIZE
    nrow = dec_seq * N_HEADS
    scale = 1.0 / math.sqrt(QK_DIM)
    slot = b % 2

    def start_block(bb, i, dst):
        for p in range(bp):
            pg = i * bp + p
            page = pt_ref[bb * n_pages + pg]
            lanes = pl.ds(pl.multiple_of(pg * PAGE_SIZE, PAGE_SIZE), PAGE_SIZE)
            pltpu.make_async_copy(cc_hbm.at[page], cbuf.at[dst, pg], sems.at[dst, 0]).start()
            pltpu.make_async_copy(ckr_hbm.at[page], krbuf.at[dst, :, lanes], sems.at[dst, 1]).start()
            pltpu.make_async_copy(cr_hbm.at[page], rbuf.at[dst, :, lanes], sems.at[dst, 2]).start()

    @pl.when(b == 0)
    def _():
        lax.fori_loop(0, nblk, lambda i, c: (start_block(0, i, 0), c)[1], 0)

    pltpu.make_async_copy(cbuf.at[slot], cbuf.at[slot], sems.at[slot, 0]).wait()
    pltpu.make_async_copy(krbuf.at[slot], krbuf.at[slot], sems.at[slot, 1]).wait()
    pltpu.make_async_copy(rbuf.at[slot], rbuf.at[slot], sems.at[slot, 2]).wait()

    ql = ql_ref[0]
    qr = qr_ref[0]

    def tile_heads(r_t):
        return jnp.concatenate([r_t] * dec_seq, axis=0)

    cnb = cn_ref[0].astype(BF16)
    s_n = (_dot_nt(ql, cnb) + _dot(qr, krn_ref[0].astype(BF16))) * tile_heads(rnt_ref[0]) * scale
    qs = lax.broadcasted_iota(jnp.int32, (nrow, NEW_KEYS), 0) // N_HEADS
    kj = lax.broadcasted_iota(jnp.int32, (nrow, NEW_KEYS), 1)
    s_n = jnp.where(kj <= qs, s_n, NEG)

    def score_block(i, m):
        @pl.when(b + 1 < nb)
        def _():
            start_block(b + 1, i, 1 - slot)

        k0 = pl.multiple_of(i * keys, keys)
        cb = cbuf[slot, pl.ds(i * bp, bp)].reshape(keys, KV_LORA).astype(BF16)
        cb_ref[pl.ds(k0, keys), :] = cb
        raw = _dot_nt(ql, cb) + _dot(qr, krbuf[slot, :, pl.ds(k0, keys)].astype(BF16))
        s = raw * tile_heads(rbuf[slot, :, pl.ds(k0, keys)]) * scale
        sc_ref[:, pl.ds(k0, keys)] = s
        return jnp.maximum(m, jnp.max(s, axis=-1, keepdims=True))

    m = lax.fori_loop(0, nblk, score_block, jnp.max(s_n, axis=-1, keepdims=True))
    p_n = jnp.exp(s_n - m)

    def value_block(i, carry):
        l, acc = carry
        k0 = pl.multiple_of(i * keys, keys)
        p = jnp.exp(sc_ref[:, pl.ds(k0, keys)] - m)
        return (l + jnp.sum(p, axis=-1, keepdims=True),
                acc + _dot(p.astype(BF16), cb_ref[pl.ds(k0, keys), :]))

    l, acc = lax.fori_loop(0, nblk, value_block,
                           (jnp.sum(p_n, axis=-1, keepdims=True), _dot(p_n.astype(BF16), cnb)))
    o_ref[0] = acc / l


def _attn_s(page_table, ql, qr, cn, krn, rnt, cache_c, cache_kr, cache_r, *, dec_seq):
    nb, n_pages = page_table.shape
    nrow = dec_seq * N_HEADS
    b3 = lambda b, pt: (b, 0, 0)
    return pl.pallas_call(
        functools.partial(_attn_s_kernel, n_pages=n_pages, dec_seq=dec_seq),
        grid_spec=pltpu.PrefetchScalarGridSpec(
            num_scalar_prefetch=1,
            grid=(nb,),
            in_specs=[
                pl.BlockSpec((1, nrow, KV_LORA), b3),
                pl.BlockSpec((1, nrow, ROPE_DIM), b3),
                pl.BlockSpec((1, NEW_KEYS, KV_LORA), b3),
                pl.BlockSpec((1, ROPE_DIM, NEW_KEYS), b3),
                pl.BlockSpec((1, N_HEADS, NEW_KEYS), b3),
                pl.BlockSpec(memory_space=pl.ANY),
                pl.BlockSpec(memory_space=pl.ANY),
                pl.BlockSpec(memory_space=pl.ANY),
            ],
            out_specs=pl.BlockSpec((1, nrow, KV_LORA), b3),
            scratch_shapes=[
                pltpu.VMEM((2, n_pages, PAGE_SIZE, KV_LORA), F32),
                pltpu.VMEM((2, ROPE_DIM, n_pages * PAGE_SIZE), F32),
                pltpu.VMEM((2, N_HEADS, n_pages * PAGE_SIZE), F32),
                pltpu.VMEM((n_pages * PAGE_SIZE, KV_LORA), BF16),
                pltpu.VMEM((nrow, n_pages * PAGE_SIZE), F32),
                pltpu.SemaphoreType.DMA((2, 3)),
            ],
        ),
        out_shape=jax.ShapeDtypeStruct((nb, nrow, KV_LORA), F32),
        compiler_params=_cparams("arbitrary"),
        name="attn_s",
    )(page_table.reshape(-1), ql, qr, cn, krn, rnt, cache_c, cache_kr, cache_r)


def _latent_out_kernel(ol_ref, wuv_ref, o_ref):
    for h in range(N_HEADS):
        o_ref[:, h * V_DIM:(h + 1) * V_DIM] = _dot(
            ol_ref[:, h * KV_LORA:(h + 1) * KV_LORA].astype(BF16), wuv_ref[h])


def _latent_out(ol, wuv_h):
    n = ol.shape[0]
    return pl.pallas_call(
        _latent_out_kernel,
        grid=(1,),
        in_specs=[_full(ol.shape), _full(wuv_h.shape)],
        out_specs=_full((n, N_HEADS * V_DIM)),
        out_shape=jax.ShapeDtypeStruct((n, N_HEADS * V_DIM), F32),
        compiler_params=_cparams("arbitrary"),
        name="latent_out",
    )(ol, wuv_h)


ROUTE_W = 16


def _mix_kernel(h_ref, os_ref, om_ref, gom_ref, wout_ref, gffn_ref, wr_ref, br_ref,
                h1_ref, xn_ref, route_ref, cnt_ref):
    @pl.when((pl.program_id(0) == 0) & (pl.program_id(1) == 0))
    def _():
        cnt_ref[...] = jnp.zeros_like(cnt_ref)

    tm = h_ref.shape[0]
    omn = _rms(om_ref[...], gom_ref[...]).astype(BF16)
    h1 = h_ref[...] + _dot(os_ref[...], wout_ref[:SSM_W, :]) + _dot(omn, wout_ref[SSM_W:, :])
    h1_ref[...] = h1
    xn = _rms(h1, gffn_ref[...])
    xn_ref[...] = xn
    work = _dot(xn.astype(BF16), wr_ref[...]) + br_ref[...]

    lane = lax.broadcasted_iota(jnp.int32, work.shape, 1).astype(F32)
    route = jnp.zeros(work.shape, F32)
    sels, vals = [], []
    for k in range(TOP_K):
        mk = jnp.max(work, axis=-1, keepdims=True)
        ik = jnp.min(jnp.where(work == mk, lane, float(LANES)), axis=-1, keepdims=True)
        sel = lane == ik
        work = jnp.where(sel, -jnp.inf, work)
        route = jnp.where(lane == float(k), ik, route)
        sels.append(sel)
        vals.append(mk)
    es = [jnp.exp(v - vals[0]) for v in vals]
    den = es[0] + es[1] + es[2] + es[3]
    onehot = jnp.zeros(work.shape, F32)
    for k in range(TOP_K):
        route = jnp.where(lane == float(TOP_K + k), es[k] / den, route)
        onehot = jnp.where(sels[k], 1.0, onehot)

    r_i = lax.broadcasted_iota(jnp.int32, (tm, tm), 0)
    c_i = lax.broadcasted_iota(jnp.int32, (tm, tm), 1)
    tri = jnp.where(c_i < r_i, 1.0, 0.0).astype(BF16)
    cum = _dot(tri, onehot.astype(BF16)) + cnt_ref[...]
    for k in range(TOP_K):
        rank = jnp.sum(jnp.where(sels[k], cum, 0.0), axis=-1, keepdims=True)
        route = jnp.where(lane == float(2 * TOP_K + k), rank, route)
    route_ref[...] = route[:, :ROUTE_W]
    cnt_ref[...] = cnt_ref[...] + jnp.sum(onehot, axis=0, keepdims=True)


def _mix(h2, o_ssm, o_mla, w, *, nb, nt, tm):
    n = h2.shape[0]
    row = lambda b, t: (b * nt + t, 0)
    return pl.pallas_call(
        _mix_kernel,
        grid=(nb, nt),
        in_specs=[
            pl.BlockSpec((tm, D_MODEL), row),
            pl.BlockSpec((tm, SSM_W), lambda b, t: (t, b)),
            pl.BlockSpec((tm, N_HEADS * V_DIM), row),
            _full(w["gout_mla"].shape), _full(w["wout"].shape), _full(w["gffn"].shape),
            _full(w["wr"].shape), _full(w["br"].shape),
        ],
        out_specs=[
            pl.BlockSpec((tm, D_MODEL), row),
            pl.BlockSpec((tm, D_MODEL), row),
            pl.BlockSpec((tm, ROUTE_W), row),
            _full((1, LANES)),
        ],
        out_shape=[
            jax.ShapeDtypeStruct((n, D_MODEL), F32),
            jax.ShapeDtypeStruct((n, D_MODEL), F32),
            jax.ShapeDtypeStruct((n, ROUTE_W), F32),
            jax.ShapeDtypeStruct((1, LANES), F32),
        ],
        compiler_params=_cparams("arbitrary", "arbitrary"),
        name="mix",
    )(h2, o_ssm, o_mla, w["gout_mla"], w["wout"], w["gffn"], w["wr"], w["br"])


def _experts_kernel(tile_ref, e_ref, first_ref, valid_ref, lo_ref, hi_ref, newe_ref,
                    x_ref, wgu_ref, bgu_ref, wd_ref, bd_ref, o_ref, wgu_bf, wd_bf):
    i = pl.program_id(0)

    @pl.when(newe_ref[i] == 1)
    def _():
        wgu_bf[...] = wgu_ref[0].astype(BF16)
        wd_bf[...] = wd_ref[0].astype(BF16)

    @pl.when(valid_ref[i] == 1)
    def _():
        bm = x_ref.shape[0]
        hdn = _dot(x_ref[...].astype(BF16), wgu_bf[...]) + bgu_ref[0]
        x_glu = jnp.minimum(hdn[:, :D_FF], SWIGLU_LIMIT)
        x_lin = jnp.clip(hdn[:, D_FF:], -SWIGLU_LIMIT, SWIGLU_LIMIT)
        act = x_glu * jax.nn.sigmoid(SWIGLU_ALPHA * x_glu) * (x_lin + 1.0)
        y = _dot(act.astype(BF16), wd_bf[...]) + bd_ref[0]
        rows = tile_ref[i] * bm + lax.broadcasted_iota(jnp.int32, (bm, 1), 0)
        mine = (rows >= lo_ref[i]) & (rows < hi_ref[i])

        @pl.when(first_ref[i] == 1)
        def _():
            o_ref[...] = jnp.where(mine, y, 0.0)

        @pl.when(first_ref[i] == 0)
        def _():
            o_ref[...] = jnp.where(mine, y, o_ref[...])


def _experts(xs, items, w, *, bm):
    r = xs.shape[0]
    n_items = items[0].shape[0]
    xmap = lambda i, tile, *_: (tile[i], 0)
    emap = lambda i, tile, e, *_: (e[i], 0, 0)
    return pl.pallas_call(
        _experts_kernel,
        grid_spec=pltpu.PrefetchScalarGridSpec(
            num_scalar_prefetch=len(items),
            grid=(n_items,),
            in_specs=[
                pl.BlockSpec((bm, D_MODEL), xmap),
                pl.BlockSpec((1, D_MODEL, 2 * D_FF), emap),
                pl.BlockSpec((1, 1, 2 * D_FF), emap),
                pl.BlockSpec((1, D_FF, D_MODEL), emap),
                pl.BlockSpec((1, 1, D_MODEL), emap),
            ],
            out_specs=pl.BlockSpec((bm, D_MODEL), xmap),
            scratch_shapes=[pltpu.VMEM((D_MODEL, 2 * D_FF), BF16), pltpu.VMEM((D_FF, D_MODEL), BF16)],
        ),
        out_shape=jax.ShapeDtypeStruct((r, D_MODEL), F32),
        compiler_params=_cparams("arbitrary"),
        name="experts",
    )(*items, xs, w["wgu"], w["bgu"], w["wd"], w["bd"])


def _expert_items(counts, n_rows, bm):
    n_tiles = n_rows // bm
    n_items = n_tiles + N_EXPERTS - 1
    ends = jnp.cumsum(counts)
    starts = ends - counts
    first_tile = starts // bm
    per_e = jnp.where(counts > 0, (ends - 1) // bm - first_tile + 1, 0)
    item_end = jnp.cumsum(per_e)
    i = jnp.arange(n_items, dtype=jnp.int32)
    valid = i < item_end[-1]
    e = jnp.minimum(jnp.sum(i[:, None] >= item_end[None, :], axis=1), N_EXPERTS - 1).astype(jnp.int32)
    tile = first_tile[e] + i - (item_end[e] - per_e[e])
    tile = jnp.where(valid, tile, n_tiles - 1).astype(jnp.int32)
    e = jnp.where(valid, e, e[item_end[-1] - 1])
    one = jnp.ones((1,), jnp.int32)
    first = jnp.concatenate([one, (tile[1:] != tile[:-1]).astype(jnp.int32)])
    new_e = jnp.concatenate([one, (e[1:] != e[:-1]).astype(jnp.int32)])
    return (tile, e, first, valid.astype(jnp.int32),
            starts[e].astype(jnp.int32), ends[e].astype(jnp.int32), new_e)


def _ple_kernel(h1_ref, yg_ref, route_ref, p_ref, gple_ref, wg_ref, wp_ref, o_ref):
    h2 = h1_ref[...]
    route = route_ref[...]
    w = SC_GATHER_TOKENS
    for k in range(TOP_K):
        yk = yg_ref[:, k * w:(k + 1) * w, :].reshape(h2.shape)
        h2 = h2 + yk * route[:, TOP_K + k:TOP_K + k + 1]
    gate = jax.nn.sigmoid(_dot(_rms(h2, gple_ref[...]).astype(BF16), wg_ref[...]))
    o_ref[...] = h2 + _dot(p_ref[...].astype(BF16), wp_ref[...]) * gate


def _ple(h1, yg, route, p, w, *, tm):
    n = h1.shape[0]
    row = lambda t: (t, 0)
    return pl.pallas_call(
        _ple_kernel,
        grid=(n // tm,),
        in_specs=[
            pl.BlockSpec((tm, D_MODEL), row),
            pl.BlockSpec((tm // SC_GATHER_TOKENS, TOP_K * SC_GATHER_TOKENS, D_MODEL), lambda t: (t, 0, 0)),
            pl.BlockSpec((tm, ROUTE_W), row),
            pl.BlockSpec((tm, PLE_DIM), row),
            _full(w["gple"].shape), _full(w["wpg"].shape), _full(w["wp"].shape),
        ],
        out_specs=pl.BlockSpec((tm, D_MODEL), row),
        out_shape=jax.ShapeDtypeStruct((n, D_MODEL), F32),
        compiler_params=_cparams("parallel"),
        name="ple",
    )(h1, yg, route, p, w["gple"], w["wpg"], w["wp"])


SC_INDEX_ROW = 128
SC_SCATTER_TOKENS = SC_INDEX_ROW // TOP_K
SC_GATHER_TOKENS = 8


def _sc_mesh():
    return plsc.VectorSubcoreMesh(core_axis_name="c", subcore_axis_name="s")


def _sc_dispatch(xn, dest):
    n = xn.shape[0]
    w = SC_SCATTER_TOKENS
    idx = dest.reshape(n // w, w, TOP_K).transpose(0, 2, 1).reshape(n // w, SC_INDEX_ROW)

    @pl.kernel(out_type=jax.ShapeDtypeStruct((n * TOP_K, D_MODEL), xn.dtype),
               mesh=_sc_mesh(), scratch_types=[pltpu.SemaphoreType.DMA])
    def scatter(x_hbm, i_hbm, o_hbm, sem):
        def body(x_vmem, i_vmem):
            copies = [pltpu.make_async_copy(x_vmem, o_hbm.at[i_vmem.at[0, pl.ds(k * w, w)]], sem)
                      for k in range(TOP_K)]
            for cp in copies:
                cp.start()
            for cp in copies:
                cp.wait()

        pltpu.emit_pipeline(
            body,
            grid=(n // w,),
            in_specs=[pl.BlockSpec((w, D_MODEL), lambda i: (i, 0)),
                      pl.BlockSpec((1, SC_INDEX_ROW), lambda i: (i, 0))],
            out_specs=[],
            core_axis_name=("c", "s"),
            dimension_semantics=(pltpu.PARALLEL,),
        )(x_hbm, i_hbm)

    return scatter(xn, idx)


def _sc_combine(ys, dest):
    n = dest.shape[0]
    w = SC_GATHER_TOKENS
    rows = TOP_K * w
    idx = dest.reshape(n // w, w, TOP_K).transpose(0, 2, 1).reshape(n // w, rows)
    idx = jnp.concatenate([idx, jnp.zeros((n // w, SC_INDEX_ROW - rows), jnp.int32)], axis=-1)

    @pl.kernel(out_type=jax.ShapeDtypeStruct((n // w, rows, D_MODEL), ys.dtype),
               mesh=_sc_mesh(), scratch_types=[])
    def gather(y_hbm, i_hbm, o_hbm):
        def body(i_vmem, o_vmem):
            pltpu.sync_copy(y_hbm.at[i_vmem.at[0, pl.ds(0, rows)]], o_vmem.at[0])

        pltpu.emit_pipeline(
            body,
            grid=(n // w,),
            in_specs=[pl.BlockSpec((1, SC_INDEX_ROW), lambda i: (i, 0))],
            out_specs=[pl.BlockSpec((1, rows, D_MODEL), lambda i: (i, 0, 0))],
            core_axis_name=("c", "s"),
            dimension_semantics=(pltpu.PARALLEL,),
        )(i_hbm, o_hbm)

    return gather(ys, idx)


def _rope_tables(pos):
    inv = ROPE_THETA ** (-jnp.arange(0, ROPE_DIM, 2, dtype=F32) / ROPE_DIM)
    ang = pos.astype(F32)[:, None] * inv[None, :]
    cos, sin = jnp.cos(ang), jnp.sin(ang)
    n = pos.shape[0]
    pad = jnp.zeros((n, LANES - QK_DIM), F32)
    ct = jnp.concatenate([jnp.ones((n, NOPE_DIM), F32), cos, cos, pad], axis=-1)
    st = jnp.concatenate([jnp.zeros((n, NOPE_DIM), F32), sin, sin, pad], axis=-1)
    return ct, st


def _rope_partner(w_rope):
    return jnp.concatenate([-w_rope[..., ROPE_HALF:], w_rope[..., :ROPE_HALF]], axis=-1)


def _head_tile(nope, rope):
    pad = jnp.zeros(rope.shape[:-1] + (LANES - QK_DIM,), rope.dtype)
    return jnp.concatenate([nope, rope, pad], axis=-1)


def _block_diag(m):
    g, a, b = m.shape
    hg = g // 2
    eye = jnp.eye(hg, dtype=m.dtype)
    mh = m.reshape(2, hg, a, b)
    return jnp.einsum("kgab,gh->kgahb", mh, eye).reshape(2, hg * a, hg * b)


def _prep_weights(g_mix, w_in, s5_a_re, s5_a_im, s5_log_dt, s5_b_re, s5_b_im, s5_c_re, s5_c_im, s5_d,
                  w_glu, b_glu, g_q_lora, w_uq, g_kv_lora, w_uk, w_uv, g_qk_q, g_qk_k, g_out_ssm,
                  g_out_mla, w_out, g_ffn, w_router, b_router, w_gate_up, b_gate_up, w_down, b_down,
                  g_ple, w_ple_gate, w_ple):
    w = {}
    row = lambda v: v.reshape(1, -1).astype(F32)
    o = SSM_W + Q_LORA + KV_LORA
    w_kr = w_in[:, o:o + ROPE_DIM]
    zeros_n = jnp.zeros((D_MODEL, NOPE_DIM), F32)
    w["win"] = jnp.concatenate(
        [w_in[:, :o], _head_tile(zeros_n, w_kr), _head_tile(zeros_n, _rope_partner(w_kr))], axis=-1).astype(BF16)
    w["gmix"] = row(g_mix)
    w["gql"] = row(g_q_lora)
    w["gkv"] = row(g_kv_lora)

    wq = w_uq.reshape(Q_LORA, N_HEADS, QK_DIM)
    q1 = _head_tile(wq[..., :NOPE_DIM], wq[..., NOPE_DIM:])
    q2 = _head_tile(jnp.zeros_like(wq[..., :NOPE_DIM]), _rope_partner(wq[..., NOPE_DIM:]))
    w["wuq"] = jnp.concatenate([q1.reshape(Q_LORA, HP), q2.reshape(Q_LORA, HP)], axis=-1).astype(BF16)
    w["wuk"] = jnp.concatenate(
        [w_uk, jnp.zeros((KV_LORA, N_HEADS, LANES - NOPE_DIM), F32)], axis=-1).reshape(KV_LORA, HP).astype(BF16)
    zv = jnp.zeros((KV_LORA, N_HEADS // 2, V_DIM), F32)
    w["wuv"] = jnp.stack([w_uv[:, 0::2], zv, zv, w_uv[:, 1::2]], axis=2).reshape(KV_LORA, HP).astype(BF16)
    w["wuv_h"] = w_uv.transpose(1, 0, 2).astype(BF16)
    wukT = w_uk.transpose(1, 2, 0)
    w["wukT"] = jnp.concatenate(
        [wukT, jnp.zeros((N_HEADS, LANES - NOPE_DIM, KV_LORA), F32)], axis=1).astype(BF16)
    gpad = jnp.zeros((LANES - QK_DIM,), F32)
    w["gq_p"] = row(jnp.concatenate([g_qk_q * (1.0 / math.sqrt(QK_DIM)), gpad]))
    w["gq_s"] = row(jnp.concatenate([g_qk_q, gpad]))
    w["gk"] = row(jnp.concatenate([g_qk_k, gpad]))

    dt = jnp.exp(s5_log_dt.astype(F32))[:, None]
    ar, ai = s5_a_re.astype(F32), s5_a_im.astype(F32)
    mag = jnp.exp(dt * ar)
    abar_re, abar_im = mag * jnp.cos(dt * ai), mag * jnp.sin(dt * ai)
    den = ar * ar + ai * ai
    nr, ni = abar_re - 1.0, abar_im
    coef_re = (nr * ar + ni * ai) / den
    coef_im = (ni * ar - nr * ai) / den
    br, bi = s5_b_re.astype(F32), s5_b_im.astype(F32)
    bbar_re = coef_re[..., None] * br - coef_im[..., None] * bi
    bbar_im = coef_re[..., None] * bi + coef_im[..., None] * br
    bre = _block_diag(bbar_re.transpose(0, 2, 1))
    bim = _block_diag(bbar_im.transpose(0, 2, 1))
    w["s5_bw"] = jnp.stack([bre[0], bim[0], bre[1], bim[1]]).astype(BF16)
    cre = _block_diag(s5_c_re.astype(F32).transpose(0, 2, 1))
    cim = _block_diag(-s5_c_im.astype(F32).transpose(0, 2, 1))
    w["s5_cw"] = jnp.stack([cre[0], cim[0], cre[1], cim[1]]).astype(BF16)
    w["s5_are"] = jnp.broadcast_to(abar_re.reshape(1, SSM_S), (SUBLANES, SSM_S))
    w["s5_aim"] = jnp.broadcast_to(abar_im.reshape(1, SSM_S), (SUBLANES, SSM_S))
    w["s5_d"] = row(s5_d)
    w["wglu"] = w_glu.astype(BF16)
    w["bglu"] = row(b_glu)
    w["gout_ssm"] = row(g_out_ssm)
    w["gout_mla"] = row(g_out_mla)
    w["wout"] = w_out.astype(BF16)
    w["gffn"] = row(g_ffn)
    w["wr"] = jnp.concatenate([w_router, jnp.zeros((D_MODEL, LANES - N_EXPERTS), F32)], axis=-1).astype(BF16)
    w["br"] = row(jnp.concatenate([b_router.astype(F32), jnp.full((LANES - N_EXPERTS,), NEG, F32)]))
    w["wgu"] = w_gate_up
    w["bgu"] = b_gate_up.reshape(N_EXPERTS, 1, 2 * D_FF).astype(F32)
    w["wd"] = w_down
    w["bd"] = b_down.reshape(N_EXPERTS, 1, D_MODEL).astype(F32)
    w["gple"] = row(g_ple)
    w["wpg"] = w_ple_gate.astype(BF16)
    w["wp"] = w_ple.astype(BF16)
    return w


EXPERT_ROWS = 512


def _moe_experts(mixed, w):
    h1, xn, route, cnt = mixed
    n = h1.shape[0]
    top_i = route[:, :TOP_K].astype(jnp.int32)
    rank = route[:, 2 * TOP_K:3 * TOP_K].astype(jnp.int32)
    counts = cnt[0, :N_EXPERTS].astype(jnp.int32)
    starts = jnp.cumsum(counts) - counts
    dest = starts[top_i] + rank
    xs = _sc_dispatch(xn, dest)
    return _experts(xs, _expert_items(counts, n * TOP_K, EXPERT_ROWS), w, bm=EXPERT_ROWS), dest


def _moe_combine(mixed, ys, dest, p2, w, *, tm):
    h1, _, route, _ = mixed
    return _ple(h1, _sc_combine(ys, dest), route, p2, w, tm=tm)


def _after(x, anchor):
    return lax.optimization_barrier((x, anchor))[0]


def kernel(x_prompt, x_sample, cache_kv_latent, cache_k_rope, cache_k_rstd, state_ssm_re, state_ssm_im,
           page_table, p_prompt, p_sample, g_mix, w_in, s5_a_re, s5_a_im, s5_log_dt, s5_b_re, s5_b_im,
           s5_c_re, s5_c_im, s5_d, w_glu, b_glu, g_q_lora, w_uq, g_kv_lora, w_uk, w_uv, g_qk_q, g_qk_k,
           g_out_ssm, g_out_mla, w_out, g_ffn, w_router, b_router, w_gate_up, b_gate_up, w_down, b_down,
           g_ple, w_ple_gate, w_ple):
    assert g_mix.shape[0] == 1, "single-layer step"
    nb, t, _ = x_prompt.shape
    db, ds, _ = x_sample.shape
    assert nb == SUBLANES and db % SUBLANES == 0
    w = _prep_weights(g_mix[0], w_in[0], s5_a_re[0], s5_a_im[0], s5_log_dt[0], s5_b_re[0], s5_b_im[0],
                      s5_c_re[0], s5_c_im[0], s5_d[0], w_glu[0], b_glu[0], g_q_lora[0], w_uq[0],
                      g_kv_lora[0], w_uk[0], w_uv[0], g_qk_q[0], g_qk_k[0], g_out_ssm[0], g_out_mla[0],
                      w_out[0], g_ffn[0], w_router[0], b_router[0], w_gate_up[0], b_gate_up[0],
                      w_down[0], b_down[0], g_ple[0], w_ple_gate[0], w_ple[0])

    tm = min(512, t)
    nt = t // tm
    xp = x_prompt.reshape(nb * t, D_MODEL)
    ct, st = _rope_tables(jnp.arange(t))
    u, q, k, v, c_p, kr_p, rstd_p = _proj(xp, ct, st, w, nb=nb, nt=nt, tm=tm, sample=False)
    steps = min(128, t)
    o_ssm, s_fin = _s5(u.reshape(t * nb, SSM_W), jnp.zeros((nb, 2 * SSM_S), F32), w,
                       ngroups=1, nchunks=t // steps, steps=steps)
    o_mla = _attn_p(q, k, v, nb=nb, t=t, tq=tm)
    mixed_p = _mix(xp, o_ssm.reshape(t, nb * SSM_W), o_mla, w, nb=nb, nt=nt, tm=tm)
    sr_p = s_fin[:, :SSM_S].reshape(1, nb, SSM_G, SSM_P)
    si_p = s_fin[:, SSM_S:].reshape(1, nb, SSM_G, SSM_P)

    ns = db * ds
    ng = db // SUBLANES
    xs_ = _after(x_sample.reshape(ns, D_MODEL), mixed_p[2])
    ct_s, st_s = _rope_tables(PAST_LEN + jnp.tile(jnp.arange(ds), db))
    u_s, qf, qlat, c_s, kr_s, rstd_s = _proj(xs_, ct_s, st_s, w, nb=1, nt=1, tm=ns, sample=True)
    to_scan = lambda a: a.reshape(ng, SUBLANES, ds, -1).transpose(0, 2, 1, 3).reshape(ns, -1)
    from_scan = lambda a: a.reshape(ng, ds, SUBLANES, -1).transpose(0, 2, 1, 3).reshape(ns, -1)
    h0 = jnp.concatenate([state_ssm_re[0].reshape(db, SSM_S), state_ssm_im[0].reshape(db, SSM_S)],
                         axis=-1).astype(F32)
    o_ssm_s, s_fin_s = _s5(to_scan(u_s), h0, w, ngroups=ng, nchunks=1, steps=ds)
    pad_keys = lambda a: jnp.concatenate(
        [a, jnp.zeros((db, NEW_KEYS - ds, a.shape[-1]), a.dtype)], axis=1)
    key_major = lambda a: a.transpose(0, 2, 1)
    o_lat = _attn_s(page_table,
                    qlat.reshape(db, ds * N_HEADS, KV_LORA),
                    qf.reshape(db, ds * N_HEADS, LANES)[:, :, NOPE_DIM:QK_DIM],
                    pad_keys(c_s.reshape(db, ds, KV_LORA)),
                    key_major(pad_keys(kr_s.reshape(db, ds, ROPE_DIM))),
                    key_major(pad_keys(rstd_s.reshape(db, ds, N_HEADS))),
                    cache_kv_latent[0], key_major(cache_k_rope[0]), key_major(cache_k_rstd[0]), dec_seq=ds)
    o_mla_s = _latent_out(o_lat.reshape(ns, N_HEADS * KV_LORA), w["wuv_h"])

    ys_p, dest_p = _moe_experts(mixed_p, w)
    mixed_s = _mix(xs_, from_scan(o_ssm_s), _after(o_mla_s, ys_p), w, nb=1, nt=1, tm=ns)
    ys_s, dest_s = _moe_experts(mixed_s, w)

    y_p = _moe_combine(mixed_p, ys_p, dest_p, p_prompt[0].reshape(nb * t, PLE_DIM), w, tm=tm)
    y_s = _moe_combine(mixed_s, ys_s, dest_s, p_sample[0].reshape(ns, PLE_DIM), w, tm=ns)

    return (y_p.reshape(nb, t, D_MODEL), y_s.reshape(db, ds, D_MODEL),
            c_p.reshape(1, nb, t, KV_LORA), kr_p.reshape(1, nb, t, ROPE_DIM), rstd_p.reshape(1, nb, t, N_HEADS),
            sr_p, si_p,
            c_s.reshape(1, db, ds, KV_LORA), kr_s.reshape(1, db, ds, ROPE_DIM), rstd_s.reshape(1, db, ds, N_HEADS),
            s_fin_s[:, :SSM_S].reshape(1, db, SSM_G, SSM_P), s_fin_s[:, SSM_S:].reshape(1, db, SSM_G, SSM_P))
```

```python
import functools
import math

import jax
import jax.numpy as jnp
from jax import lax
from jax.experimental import pallas as pl
from jax.experimental.pallas import tpu as pltpu
from jax.experimental.pallas import tpu_sc as plsc

D_MODEL = 1024
SSM_W = 512
SSM_GC = 16
SSM_G = SSM_W // SSM_GC
SSM_P = 64
SSM_S = SSM_G * SSM_P
N_HEADS = 8
NOPE_DIM = 64
ROPE_DIM = 32
ROPE_HALF = ROPE_DIM // 2
QK_DIM = NOPE_DIM + ROPE_DIM
V_DIM = 64
Q_LORA = 384
KV_LORA = 256
ROPE_THETA = 10000.0
N_EXPERTS = 32
TOP_K = 4
D_FF = D_MODEL
SWIGLU_ALPHA = 1.702
SWIGLU_LIMIT = 7.0
PLE_DIM = 256
PAST_LEN = 16384
PAGE_SIZE = 128
EPS = 1e-6
NEG = -1e30

LANES = 128
SUBLANES = 8
VMEM_LIMIT = 56 * 1024 * 1024

HP = N_HEADS * LANES
BF16 = jnp.bfloat16
F32 = jnp.float32


def _cparams(*sem):
    return pltpu.CompilerParams(dimension_semantics=sem, vmem_limit_bytes=VMEM_LIMIT)


def _rms(x, g):
    r = lax.rsqrt(jnp.mean(x * x, axis=-1, keepdims=True) + EPS)
    return x * r * g


def _dot(a, b):
    return jnp.dot(a, b, preferred_element_type=F32)


def _dot_nt(a, b):
    return lax.dot_general(a, b, (((1,), (1,)), ((), ())), preferred_element_type=F32)


def _full(shape):
    nd = len(shape)
    return pl.BlockSpec(shape, lambda *_: (0,) * nd)


def _proj_kernel(x_ref, ct_ref, st_ref, gmix_ref, win_ref, gql_ref, wuq_ref, gkv_ref, wuk_ref,
                 w2_ref, gq_ref, gk_ref, u_ref, q_ref, k2_ref, *rest, sample):
    if sample:
        c_ref, kr_ref, rstd_ref = rest
    else:
        v2_ref, c_ref, kr_ref, rstd_ref = rest
    xn = _rms(x_ref[...], gmix_ref[...]).astype(BF16)
    z = _dot(xn, win_ref[...])
    u_ref[...] = z[:, :SSM_W]
    o = SSM_W
    cq = z[:, o:o + Q_LORA]
    o += Q_LORA
    ckv = z[:, o:o + KV_LORA]
    o += KV_LORA
    ct = ct_ref[...]
    st = st_ref[...]
    kblock = z[:, o:o + LANES] * ct + z[:, o + LANES:o + 2 * LANES] * st
    kr_ref[...] = kblock[:, NOPE_DIM:QK_DIM]

    qq = _dot(_rms(cq, gql_ref[...]).astype(BF16), wuq_ref[...])
    c = _rms(ckv, gkv_ref[...])
    c_ref[...] = c
    cb = c.astype(BF16)
    kn = _dot(cb, wuk_ref[...])
    if not sample:
        hl = lax.broadcasted_iota(jnp.int32, (1, HP), 1) & (LANES - 1)
        v2_ref[...] = (_dot(cb, w2_ref[...]) + jnp.where(hl >= V_DIM, 1.0, 0.0)).astype(BF16)

    lane = lax.broadcasted_iota(jnp.int32, ct.shape, 1)
    rstd_all = jnp.zeros(ct.shape, F32)
    gq = gq_ref[...]
    gk = gk_ref[...]
    for h in range(N_HEADS):
        sl = slice(h * LANES, (h + 1) * LANES)
        qr = qq[:, sl] * ct + qq[:, HP + h * LANES:HP + (h + 1) * LANES] * st
        qn = qr * lax.rsqrt(jnp.sum(qr * qr, axis=-1, keepdims=True) * (1.0 / QK_DIM) + EPS) * gq
        kh = kn[:, sl] + kblock
        rs = lax.rsqrt(jnp.sum(kh * kh, axis=-1, keepdims=True) * (1.0 / QK_DIM) + EPS)
        rstd_all = jnp.where(lane == h, rs, rstd_all)
        if sample:
            qf = (qn * gk).astype(BF16)
            q_ref[:, sl] = qf
            k2_ref[:, h * KV_LORA:(h + 1) * KV_LORA] = _dot(qf, w2_ref[h]).astype(BF16)
        else:
            q_ref[:, sl] = qn.astype(BF16)
            k2_ref[:, sl] = (kh * rs * gk).astype(BF16)
    rstd_ref[...] = rstd_all[:, :N_HEADS]


def _proj(x2, ct, st, w, *, nb, nt, tm, sample):
    n = x2.shape[0]
    row = lambda b, t: (b * nt + t, 0)
    rows = lambda width: pl.BlockSpec((tm, width), row)
    k2_w = N_HEADS * KV_LORA if sample else HP
    w2 = w["wukT"] if sample else w["wuv"]
    out_shape = [
        jax.ShapeDtypeStruct((nt * tm, nb * SSM_W), F32),
        jax.ShapeDtypeStruct((n, HP), BF16),
        jax.ShapeDtypeStruct((n, k2_w), BF16),
        *([] if sample else [jax.ShapeDtypeStruct((n, HP), BF16)]),
        jax.ShapeDtypeStruct((n, KV_LORA), F32),
        jax.ShapeDtypeStruct((n, ROPE_DIM), F32),
        jax.ShapeDtypeStruct((n, N_HEADS), F32),
    ]
    out_specs = [
        pl.BlockSpec((tm, SSM_W), lambda b, t: (t, b)),
        rows(HP), rows(k2_w), *([] if sample else [rows(HP)]),
        rows(KV_LORA), rows(ROPE_DIM), rows(N_HEADS),
    ]
    in_specs = [
        rows(D_MODEL),
        pl.BlockSpec((tm, LANES), lambda b, t: (t, 0)),
        pl.BlockSpec((tm, LANES), lambda b, t: (t, 0)),
        _full(w["gmix"].shape), _full(w["win"].shape), _full(w["gql"].shape), _full(w["wuq"].shape),
        _full(w["gkv"].shape), _full(w["wuk"].shape), _full(w2.shape),
        _full(w["gq_s" if sample else "gq_p"].shape), _full(w["gk"].shape),
    ]
    return pl.pallas_call(
        functools.partial(_proj_kernel, sample=sample),
        grid=(nb, nt),
        in_specs=in_specs,
        out_specs=out_specs,
        out_shape=out_shape,
        compiler_params=_cparams("parallel", "parallel"),
        name="proj_s" if sample else "proj_p",
    )(x2, ct, st, w["gmix"], w["win"], w["gql"], w["wuq"], w["gkv"], w["wuk"], w2,
      w["gq_s" if sample else "gq_p"], w["gk"])


S5_HALF = SSM_S // 2
S5_QUARTER = SSM_S // 4


def _s5_kernel(u_ref, h0_ref, are_ref, aim_ref, bw_ref, cw_ref, dsk_ref, wglu_ref, bglu_ref, gout_ref,
               o_ref, sfin_ref, xs_ref, st_ref, *, steps):
    @pl.when(pl.program_id(1) == 0)
    def _():
        st_ref[...] = h0_ref[...]

    u = u_ref[...]
    ub = u.astype(BF16)
    half_w = SSM_W // 2
    for k in range(2):
        uk = ub[:, k * half_w:(k + 1) * half_w]
        xs_ref[:, k * S5_HALF:(k + 1) * S5_HALF] = _dot(uk, bw_ref[2 * k])
        xs_ref[:, SSM_S + k * S5_HALF:SSM_S + (k + 1) * S5_HALF] = _dot(uk, bw_ref[2 * k + 1])

    for q in range(4):
        lr = slice(q * S5_QUARTER, (q + 1) * S5_QUARTER)
        li = slice(SSM_S + q * S5_QUARTER, SSM_S + (q + 1) * S5_QUARTER)
        ar = are_ref[:, lr]
        ai = aim_ref[:, lr]

        def step(t, carry):
            sr, si = carry
            r0 = pl.multiple_of(t * SUBLANES, SUBLANES)
            nr = ar * sr - ai * si + xs_ref[pl.ds(r0, SUBLANES), lr]
            ni = ar * si + ai * sr + xs_ref[pl.ds(r0, SUBLANES), li]
            xs_ref[pl.ds(r0, SUBLANES), lr] = nr
            xs_ref[pl.ds(r0, SUBLANES), li] = ni
            return nr, ni

        sr, si = lax.fori_loop(0, steps, step, (st_ref[:, lr], st_ref[:, li]), unroll=4)
        st_ref[:, lr] = sr
        st_ref[:, li] = si
    sfin_ref[...] = st_ref[...]

    ys = []
    for k in range(2):
        sre = xs_ref[:, k * S5_HALF:(k + 1) * S5_HALF].astype(BF16)
        sim = xs_ref[:, SSM_S + k * S5_HALF:SSM_S + (k + 1) * S5_HALF].astype(BF16)
        ys.append(_dot(sre, cw_ref[2 * k]) + _dot(sim, cw_ref[2 * k + 1]))
    y = jnp.concatenate(ys, axis=-1) + dsk_ref[...] * u
    g = jax.nn.gelu(y)
    out = g * jax.nn.sigmoid(_dot(g.astype(BF16), wglu_ref[...]) + bglu_ref[...])
    o_ref[...] = _rms(out, gout_ref[...]).astype(BF16)


def _s5(u2, h0, w, *, ngroups, nchunks, steps):
    rows = steps * SUBLANES
    blk = lambda g, t: (g * nchunks + t, 0)
    return pl.pallas_call(
        functools.partial(_s5_kernel, steps=steps),
        grid=(ngroups, nchunks),
        in_specs=[
            pl.BlockSpec((rows, SSM_W), blk),
            pl.BlockSpec((SUBLANES, 2 * SSM_S), lambda g, t: (g, 0)),
            _full(w["s5_are"].shape), _full(w["s5_aim"].shape), _full(w["s5_bw"].shape),
            _full(w["s5_cw"].shape), _full(w["s5_d"].shape), _full(w["wglu"].shape),
            _full(w["bglu"].shape), _full(w["gout_ssm"].shape),
        ],
        out_specs=[
            pl.BlockSpec((rows, SSM_W), blk),
            pl.BlockSpec((SUBLANES, 2 * SSM_S), lambda g, t: (g, 0)),
        ],
        out_shape=[
            jax.ShapeDtypeStruct(u2.shape, BF16),
            jax.ShapeDtypeStruct(h0.shape, F32),
        ],
        scratch_shapes=[pltpu.VMEM((rows, 2 * SSM_S), F32), pltpu.VMEM((SUBLANES, 2 * SSM_S), F32)],
        compiler_params=_cparams("parallel", "arbitrary"),
        name="s5",
    )(u2, h0, w["s5_are"], w["s5_aim"], w["s5_bw"], w["s5_cw"], w["s5_d"], w["wglu"], w["bglu"],
      w["gout_ssm"])


def _attn_p_kernel(q_ref, k_ref, v_ref, o_ref, *, tq, tk):
    i = pl.program_id(2)
    nd = tq // tk
    row = lax.broadcasted_iota(jnp.int32, (tk, tk), 0)
    col = lax.broadcasted_iota(jnp.int32, (tk, tk), 1)

    def block(j, carry, diag):
        k0 = pl.multiple_of(j * tk, tk)
        k2 = k_ref[pl.ds(k0, tk), :]
        v2 = v_ref[pl.ds(k0, tk), :]
        r0 = 0 if diag is None else diag * tk
        out = []
        for hh in range(2):
            m, acc = carry[hh]
            s = _dot_nt(q_ref[r0:, hh * LANES:(hh + 1) * LANES], k2[:, hh * LANES:(hh + 1) * LANES])
            if diag is not None:
                tri = jnp.where(col <= row, s[:tk], NEG)
                s = tri if r0 + tk == tq else jnp.concatenate([tri, s[tk:]], axis=0)
            m_new = jnp.maximum(m[r0:], jnp.max(s, axis=-1, keepdims=True))
            p = jnp.exp2(s - m_new).astype(BF16)
            acc_new = jnp.exp2(m[r0:] - m_new) * acc[r0:] + _dot(p, v2[:, hh * LANES:(hh + 1) * LANES])
            if r0:
                m_new = jnp.concatenate([m[:r0], m_new], axis=0)
                acc_new = jnp.concatenate([acc[:r0], acc_new], axis=0)
            out.append((m_new, acc_new))
        return tuple(out)

    init = tuple((jnp.full((tq, 1), NEG, F32), jnp.zeros((tq, LANES), F32)) for _ in range(2))
    carry = lax.fori_loop(0, i * nd, lambda j, c: block(j, c, None), init)
    for d in range(nd):
        carry = block(i * nd + d, carry, d)
    o_even, o_odd = (acc / pltpu.roll(acc, V_DIM, 1) for _, acc in carry)
    lane = lax.broadcasted_iota(jnp.int32, (tq, LANES), 1)
    o_ref[...] = jnp.where(lane < V_DIM, o_even, pltpu.roll(o_odd, V_DIM, 1))


def _attn_p(q, k, v, *, nb, t, tq, tk):
    nq = t // tq
    return pl.pallas_call(
        functools.partial(_attn_p_kernel, tq=tq, tk=tk),
        grid=(nb, N_HEADS // 2, nq),
        in_specs=[
            pl.BlockSpec((tq, 2 * LANES), lambda b, h, i: (b * nq + i, h)),
            pl.BlockSpec((t, 2 * LANES), lambda b, h, i: (b, h)),
            pl.BlockSpec((t, 2 * LANES), lambda b, h, i: (b, h)),
        ],
        out_specs=pl.BlockSpec((tq, 2 * V_DIM), lambda b, h, i: (b * nq + i, h)),
        out_shape=jax.ShapeDtypeStruct((nb * t, N_HEADS * V_DIM), F32),
        compiler_params=_cparams("parallel", "parallel", "arbitrary"),
        name="attn_p",
    )(q, k, v)


ATTN_S_PAGES = 32
NEW_KEYS = 16


def _attn_s_kernel(pt_ref, ql_ref, qr_ref, cn_ref, krn_ref, rnt_ref, cc_hbm, ckr_hbm, cr_hbm,
                   o_ref, cbuf, krbuf, rbuf, cb_ref, sc_ref, sems, *, n_pages, dec_seq):
    b = pl.program_id(0)
    nb = pl.num_programs(0)
    bp = min(ATTN_S_PAGES, n_pages)
    nblk = n_pages // bp
    keys = bp * PAGE_SIZE
    nrow = dec_seq * N_HEADS
    scale = 1.0 / math.sqrt(QK_DIM)
    slot = b % 2

    def start_block(bb, i, dst):
        for p in range(bp):
            pg = i * bp + p
            page = pt_ref[bb * n_pages + pg]
            lanes = pl.ds(pl.multiple_of(pg * PAGE_SIZE, PAGE_SIZE), PAGE_SIZE)
            pltpu.make_async_copy(cc_hbm.at[page], cbuf.at[dst, pg], sems.at[dst, 0]).start()
            pltpu.make_async_copy(ckr_hbm.at[page], krbuf.at[dst, :, lanes], sems.at[dst, 1]).start()
            pltpu.make_async_copy(cr_hbm.at[page], rbuf.at[dst, :, lanes], sems.at[dst, 2]).start()

    @pl.when(b == 0)
    def _():
        lax.fori_loop(0, nblk, lambda i, c: (start_block(0, i, 0), c)[1], 0)

    pltpu.make_async_copy(cbuf.at[slot], cbuf.at[slot], sems.at[slot, 0]).wait()
    pltpu.make_async_copy(krbuf.at[slot], krbuf.at[slot], sems.at[slot, 1]).wait()
    pltpu.make_async_copy(rbuf.at[slot], rbuf.at[slot], sems.at[slot, 2]).wait()

    ql = ql_ref[0]
    qr = qr_ref[0]

    def tile_heads(r_t):
        return jnp.concatenate([r_t] * dec_seq, axis=0)

    cnb = cn_ref[0].astype(BF16)
    s_n = (_dot_nt(ql, cnb) + _dot(qr, krn_ref[0].astype(BF16))) * tile_heads(rnt_ref[0]) * scale
    qs = lax.broadcasted_iota(jnp.int32, (nrow, NEW_KEYS), 0) // N_HEADS
    kj = lax.broadcasted_iota(jnp.int32, (nrow, NEW_KEYS), 1)
    s_n = jnp.where(kj <= qs, s_n, NEG)

    def score_block(i, m):
        @pl.when(b + 1 < nb)
        def _():
            start_block(b + 1, i, 1 - slot)

        k0 = pl.multiple_of(i * keys, keys)
        cb = cbuf[slot, pl.ds(i * bp, bp)].reshape(keys, KV_LORA).astype(BF16)
        cb_ref[pl.ds(k0, keys), :] = cb
        raw = _dot_nt(ql, cb) + _dot(qr, krbuf[slot, :, pl.ds(k0, keys)].astype(BF16))
        s = raw * tile_heads(rbuf[slot, :, pl.ds(k0, keys)]) * scale
        sc_ref[:, pl.ds(k0, keys)] = s
        return jnp.maximum(m, jnp.max(s, axis=-1, keepdims=True))

    m = lax.fori_loop(0, nblk, score_block, jnp.max(s_n, axis=-1, keepdims=True))
    p_n = jnp.exp(s_n - m)

    def value_block(i, carry):
        l, acc = carry
        k0 = pl.multiple_of(i * keys, keys)
        p = jnp.exp(sc_ref[:, pl.ds(k0, keys)] - m)
        return (l + jnp.sum(p, axis=-1, keepdims=True),
                acc + _dot(p.astype(BF16), cb_ref[pl.ds(k0, keys), :]))

    l, acc = lax.fori_loop(0, nblk, value_block,
                           (jnp.sum(p_n, axis=-1, keepdims=True), _dot(p_n.astype(BF16), cnb)))
    o_ref[0] = acc / l


def _attn_s(page_table, ql, qr, cn, krn, rnt, cache_c, cache_kr, cache_r, *, dec_seq):
    nb, n_pages = page_table.shape
    nrow = dec_seq * N_HEADS
    b3 = lambda b, pt: (b, 0, 0)
    return pl.pallas_call(
        functools.partial(_attn_s_kernel, n_pages=n_pages, dec_seq=dec_seq),
        grid_spec=pltpu.PrefetchScalarGridSpec(
            num_scalar_prefetch=1,
            grid=(nb,),
            in_specs=[
                pl.BlockSpec((1, nrow, KV_LORA), b3),
                pl.BlockSpec((1, nrow, ROPE_DIM), b3),
                pl.BlockSpec((1, NEW_KEYS, KV_LORA), b3),
                pl.BlockSpec((1, ROPE_DIM, NEW_KEYS), b3),
                pl.BlockSpec((1, N_HEADS, NEW_KEYS), b3),
                pl.BlockSpec(memory_space=pl.ANY),
                pl.BlockSpec(memory_space=pl.ANY),
                pl.BlockSpec(memory_space=pl.ANY),
            ],
            out_specs=pl.BlockSpec((1, nrow, KV_LORA), b3),
            scratch_shapes=[
                pltpu.VMEM((2, n_pages, PAGE_SIZE, KV_LORA), F32),
                pltpu.VMEM((2, ROPE_DIM, n_pages * PAGE_SIZE), F32),
                pltpu.VMEM((2, N_HEADS, n_pages * PAGE_SIZE), F32),
                pltpu.VMEM((n_pages * PAGE_SIZE, KV_LORA), BF16),
                pltpu.VMEM((nrow, n_pages * PAGE_SIZE), F32),
                pltpu.SemaphoreType.DMA((2, 3)),
            ],
        ),
        out_shape=jax.ShapeDtypeStruct((nb, nrow, KV_LORA), F32),
        compiler_params=_cparams("arbitrary"),
        name="attn_s",
    )(page_table.reshape(-1), ql, qr, cn, krn, rnt, cache_c, cache_kr, cache_r)


def _latent_out_kernel(ol_ref, wuv_ref, o_ref):
    for h in range(N_HEADS):
        o_ref[:, h * V_DIM:(h + 1) * V_DIM] = _dot(
            ol_ref[:, h * KV_LORA:(h + 1) * KV_LORA].astype(BF16), wuv_ref[h])


def _latent_out(ol, wuv_h):
    n = ol.shape[0]
    return pl.pallas_call(
        _latent_out_kernel,
        grid=(1,),
        in_specs=[_full(ol.shape), _full(wuv_h.shape)],
        out_specs=_full((n, N_HEADS * V_DIM)),
        out_shape=jax.ShapeDtypeStruct((n, N_HEADS * V_DIM), F32),
        compiler_params=_cparams("arbitrary"),
        name="latent_out",
    )(ol, wuv_h)


ROUTE_W = 16


def _mix_kernel(h_ref, os_ref, om_ref, gom_ref, wout_ref, gffn_ref, wr_ref, br_ref,
                h1_ref, xn_ref, route_ref, cnt_ref):
    @pl.when((pl.program_id(0) == 0) & (pl.program_id(1) == 0))
    def _():
        cnt_ref[...] = jnp.zeros_like(cnt_ref)

    tm = h_ref.shape[0]
    omn = _rms(om_ref[...], gom_ref[...]).astype(BF16)
    h1 = h_ref[...] + _dot(os_ref[...], wout_ref[:SSM_W, :]) + _dot(omn, wout_ref[SSM_W:, :])
    h1_ref[...] = h1
    xn = _rms(h1, gffn_ref[...])
    xn_ref[...] = xn
    work = _dot(xn.astype(BF16), wr_ref[...]) + br_ref[...]

    lane = lax.broadcasted_iota(jnp.int32, work.shape, 1).astype(F32)
    route = jnp.zeros(work.shape, F32)
    sels, vals = [], []
    for k in range(TOP_K):
        mk = jnp.max(work, axis=-1, keepdims=True)
        ik = jnp.min(jnp.where(work == mk, lane, float(LANES)), axis=-1, keepdims=True)
        sel = lane == ik
        work = jnp.where(sel, -jnp.inf, work)
        route = jnp.where(lane == float(k), ik, route)
        sels.append(sel)
        vals.append(mk)
    es = [jnp.exp(v - vals[0]) for v in vals]
    den = es[0] + es[1] + es[2] + es[3]
    onehot = jnp.zeros(work.shape, F32)
    for k in range(TOP_K):
        route = jnp.where(lane == float(TOP_K + k), es[k] / den, route)
        onehot = jnp.where(sels[k], 1.0, onehot)

    r_i = lax.broadcasted_iota(jnp.int32, (tm, tm), 0)
    c_i = lax.broadcasted_iota(jnp.int32, (tm, tm), 1)
    tri = jnp.where(c_i < r_i, 1.0, 0.0).astype(BF16)
    cum = _dot(tri, onehot.astype(BF16)) + cnt_ref[...]
    for k in range(TOP_K):
        rank = jnp.sum(jnp.where(sels[k], cum, 0.0), axis=-1, keepdims=True)
        route = jnp.where(lane == float(2 * TOP_K + k), rank, route)
    route_ref[...] = route[:, :ROUTE_W]
    cnt_ref[...] = cnt_ref[...] + jnp.sum(onehot, axis=0, keepdims=True)


def _mix(h2, o_ssm, o_mla, w, *, nb, nt, tm):
    n = h2.shape[0]
    row = lambda b, t: (b * nt + t, 0)
    return pl.pallas_call(
        _mix_kernel,
        grid=(nb, nt),
        in_specs=[
            pl.BlockSpec((tm, D_MODEL), row),
            pl.BlockSpec((tm, SSM_W), lambda b, t: (t, b)),
            pl.BlockSpec((tm, N_HEADS * V_DIM), row),
            _full(w["gout_mla"].shape), _full(w["wout"].shape), _full(w["gffn"].shape),
            _full(w["wr"].shape), _full(w["br"].shape),
        ],
        out_specs=[
            pl.BlockSpec((tm, D_MODEL), row),
            pl.BlockSpec((tm, D_MODEL), row),
            pl.BlockSpec((tm, ROUTE_W), row),
            _full((1, LANES)),
        ],
        out_shape=[
            jax.ShapeDtypeStruct((n, D_MODEL), F32),
            jax.ShapeDtypeStruct((n, D_MODEL), F32),
            jax.ShapeDtypeStruct((n, ROUTE_W), F32),
            jax.ShapeDtypeStruct((1, LANES), F32),
        ],
        compiler_params=_cparams("arbitrary", "arbitrary"),
        name="mix",
    )(h2, o_ssm, o_mla, w["gout_mla"], w["wout"], w["gffn"], w["wr"], w["br"])


def _experts_kernel(tile_ref, e_ref, first_ref, valid_ref, lo_ref, hi_ref, newe_ref,
                    x_ref, wgu_ref, bgu_ref, wd_ref, bd_ref, o_ref, wgu_bf, wd_bf):
    i = pl.program_id(0)

    @pl.when(newe_ref[i] == 1)
    def _():
        wgu_bf[...] = wgu_ref[0].astype(BF16)
        wd_bf[...] = wd_ref[0].astype(BF16)

    @pl.when(valid_ref[i] == 1)
    def _():
        bm = x_ref.shape[0]
        hdn = _dot(x_ref[...].astype(BF16), wgu_bf[...]) + bgu_ref[0]
        x_glu = jnp.minimum(hdn[:, :D_FF], SWIGLU_LIMIT)
        x_lin = jnp.clip(hdn[:, D_FF:], -SWIGLU_LIMIT, SWIGLU_LIMIT)
        act = x_glu * jax.nn.sigmoid(SWIGLU_ALPHA * x_glu) * (x_lin + 1.0)
        y = _dot(act.astype(BF16), wd_bf[...]) + bd_ref[0]
        rows = tile_ref[i] * bm + lax.broadcasted_iota(jnp.int32, (bm, 1), 0)
        mine = (rows >= lo_ref[i]) & (rows < hi_ref[i])

        @pl.when(first_ref[i] == 1)
        def _():
            o_ref[...] = jnp.where(mine, y, 0.0)

        @pl.when(first_ref[i] == 0)
        def _():
            o_ref[...] = jnp.where(mine, y, o_ref[...])


def _experts(xs, items, w, *, bm):
    r = xs.shape[0]
    n_items = items[0].shape[0]
    xmap = lambda i, tile, *_: (tile[i], 0)
    emap = lambda i, tile, e, *_: (e[i], 0, 0)
    return pl.pallas_call(
        _experts_kernel,
        grid_spec=pltpu.PrefetchScalarGridSpec(
            num_scalar_prefetch=len(items),
            grid=(n_items,),
            in_specs=[
                pl.BlockSpec((bm, D_MODEL), xmap),
                pl.BlockSpec((1, D_MODEL, 2 * D_FF), emap),
                pl.BlockSpec((1, 1, 2 * D_FF), emap),
                pl.BlockSpec((1, D_FF, D_MODEL), emap),
                pl.BlockSpec((1, 1, D_MODEL), emap),
            ],
            out_specs=pl.BlockSpec((bm, D_MODEL), xmap),
            scratch_shapes=[pltpu.VMEM((D_MODEL, 2 * D_FF), BF16), pltpu.VMEM((D_FF, D_MODEL), BF16)],
        ),
        out_shape=jax.ShapeDtypeStruct((r, D_MODEL), F32),
        compiler_params=_cparams("arbitrary"),
        name="experts",
    )(*items, xs, w["wgu"], w["bgu"], w["wd"], w["bd"])


def _expert_items(counts, n_rows, bm):
    n_tiles = n_rows // bm
    n_items = n_tiles + N_EXPERTS - 1
    ends = jnp.cumsum(counts)
    starts = ends - counts
    first_tile = starts // bm
    per_e = jnp.where(counts > 0, (ends - 1) // bm - first_tile + 1, 0)
    item_end = jnp.cumsum(per_e)
    i = jnp.arange(n_items, dtype=jnp.int32)
    valid = i < item_end[-1]
    e = jnp.minimum(jnp.sum(i[:, None] >= item_end[None, :], axis=1), N_EXPERTS - 1).astype(jnp.int32)
    tile = first_tile[e] + i - (item_end[e] - per_e[e])
    tile = jnp.where(valid, tile, n_tiles - 1).astype(jnp.int32)
    e = jnp.where(valid, e, e[item_end[-1] - 1])
    one = jnp.ones((1,), jnp.int32)
    first = jnp.concatenate([one, (tile[1:] != tile[:-1]).astype(jnp.int32)])
    new_e = jnp.concatenate([one, (e[1:] != e[:-1]).astype(jnp.int32)])
    return (tile, e, first, valid.astype(jnp.int32),
            starts[e].astype(jnp.int32), ends[e].astype(jnp.int32), new_e)


def _ple_kernel(h1_ref, yg_ref, route_ref, p_ref, gple_ref, wg_ref, wp_ref, o_ref):
    h2 = h1_ref[...]
    route = route_ref[...]
    w = SC_GATHER_TOKENS
    for k in range(TOP_K):
        yk = yg_ref[:, k * w:(k + 1) * w, :].reshape(h2.shape)
        h2 = h2 + yk * route[:, TOP_K + k:TOP_K + k + 1]
    gate = jax.nn.sigmoid(_dot(_rms(h2, gple_ref[...]).astype(BF16), wg_ref[...]))
    o_ref[...] = h2 + _dot(p_ref[...].astype(BF16), wp_ref[...]) * gate


def _ple(h1, yg, route, p, w, *, tm):
    n = h1.shape[0]
    row = lambda t: (t, 0)
    return pl.pallas_call(
        _ple_kernel,
        grid=(n // tm,),
        in_specs=[
            pl.BlockSpec((tm, D_MODEL), row),
            pl.BlockSpec((tm // SC_GATHER_TOKENS, TOP_K * SC_GATHER_TOKENS, D_MODEL), lambda t: (t, 0, 0)),
            pl.BlockSpec((tm, ROUTE_W), row),
            pl.BlockSpec((tm, PLE_DIM), row),
            _full(w["gple"].shape), _full(w["wpg"].shape), _full(w["wp"].shape),
        ],
        out_specs=pl.BlockSpec((tm, D_MODEL), row),
        out_shape=jax.ShapeDtypeStruct((n, D_MODEL), F32),
        compiler_params=_cparams("parallel"),
        name="ple",
    )(h1, yg, route, p, w["gple"], w["wpg"], w["wp"])


SC_INDEX_ROW = 128
SC_SCATTER_TOKENS = SC_INDEX_ROW // TOP_K
SC_GATHER_TOKENS = 8


def _sc_mesh():
    return plsc.VectorSubcoreMesh(core_axis_name="c", subcore_axis_name="s")


def _sc_dispatch(xn, dest):
    n = xn.shape[0]
    w = SC_SCATTER_TOKENS
    idx = dest.reshape(n // w, w, TOP_K).transpose(0, 2, 1).reshape(n // w, SC_INDEX_ROW)

    @pl.kernel(out_type=jax.ShapeDtypeStruct((n * TOP_K, D_MODEL), xn.dtype),
               mesh=_sc_mesh(), scratch_types=[pltpu.SemaphoreType.DMA])
    def scatter(x_hbm, i_hbm, o_hbm, sem):
        def body(x_vmem, i_vmem):
            copies = [pltpu.make_async_copy(x_vmem, o_hbm.at[i_vmem.at[0, pl.ds(k * w, w)]], sem)
                      for k in range(TOP_K)]
            for cp in copies:
                cp.start()
            for cp in copies:
                cp.wait()

        pltpu.emit_pipeline(
            body,
            grid=(n // w,),
            in_specs=[pl.BlockSpec((w, D_MODEL), lambda i: (i, 0)),
                      pl.BlockSpec((1, SC_INDEX_ROW), lambda i: (i, 0))],
            out_specs=[],
            core_axis_name=("c", "s"),
            dimension_semantics=(pltpu.PARALLEL,),
        )(x_hbm, i_hbm)

    return scatter(xn, idx)


def _sc_combine(ys, dest):
    n = dest.shape[0]
    w = SC_GATHER_TOKENS
    rows = TOP_K * w
    idx = dest.reshape(n // w, w, TOP_K).transpose(0, 2, 1).reshape(n // w, rows)
    idx = jnp.concatenate([idx, jnp.zeros((n // w, SC_INDEX_ROW - rows), jnp.int32)], axis=-1)

    @pl.kernel(out_type=jax.ShapeDtypeStruct((n // w, rows, D_MODEL), ys.dtype),
               mesh=_sc_mesh(), scratch_types=[])
    def gather(y_hbm, i_hbm, o_hbm):
        def body(i_vmem, o_vmem):
            pltpu.sync_copy(y_hbm.at[i_vmem.at[0, pl.ds(0, rows)]], o_vmem.at[0])

        pltpu.emit_pipeline(
            body,
            grid=(n // w,),
            in_specs=[pl.BlockSpec((1, SC_INDEX_ROW), lambda i: (i, 0))],
            out_specs=[pl.BlockSpec((1, rows, D_MODEL), lambda i: (i, 0, 0))],
            core_axis_name=("c", "s"),
            dimension_semantics=(pltpu.PARALLEL,),
        )(i_hbm, o_hbm)

    return gather(ys, idx)


def _rope_tables(pos):
    inv = ROPE_THETA ** (-jnp.arange(0, ROPE_DIM, 2, dtype=F32) / ROPE_DIM)
    ang = pos.astype(F32)[:, None] * inv[None, :]
    cos, sin = jnp.cos(ang), jnp.sin(ang)
    n = pos.shape[0]
    pad = jnp.zeros((n, LANES - QK_DIM), F32)
    ct = jnp.concatenate([jnp.ones((n, NOPE_DIM), F32), cos, cos, pad], axis=-1)
    st = jnp.concatenate([jnp.zeros((n, NOPE_DIM), F32), sin, sin, pad], axis=-1)
    return ct, st


def _rope_partner(w_rope):
    return jnp.concatenate([-w_rope[..., ROPE_HALF:], w_rope[..., :ROPE_HALF]], axis=-1)


def _head_tile(nope, rope):
    pad = jnp.zeros(rope.shape[:-1] + (LANES - QK_DIM,), rope.dtype)
    return jnp.concatenate([nope, rope, pad], axis=-1)


def _block_diag(m):
    g, a, b = m.shape
    hg = g // 2
    eye = jnp.eye(hg, dtype=m.dtype)
    mh = m.reshape(2, hg, a, b)
    return jnp.einsum("kgab,gh->kgahb", mh, eye).reshape(2, hg * a, hg * b)


def _prep_weights(g_mix, w_in, s5_a_re, s5_a_im, s5_log_dt, s5_b_re, s5_b_im, s5_c_re, s5_c_im, s5_d,
                  w_glu, b_glu, g_q_lora, w_uq, g_kv_lora, w_uk, w_uv, g_qk_q, g_qk_k, g_out_ssm,
                  g_out_mla, w_out, g_ffn, w_router, b_router, w_gate_up, b_gate_up, w_down, b_down,
                  g_ple, w_ple_gate, w_ple):
    w = {}
    row = lambda v: v.reshape(1, -1).astype(F32)
    o = SSM_W + Q_LORA + KV_LORA
    w_kr = w_in[:, o:o + ROPE_DIM]
    zeros_n = jnp.zeros((D_MODEL, NOPE_DIM), F32)
    w["win"] = jnp.concatenate(
        [w_in[:, :o], _head_tile(zeros_n, w_kr), _head_tile(zeros_n, _rope_partner(w_kr))], axis=-1).astype(BF16)
    w["gmix"] = row(g_mix)
    w["gql"] = row(g_q_lora)
    w["gkv"] = row(g_kv_lora)

    wq = w_uq.reshape(Q_LORA, N_HEADS, QK_DIM)
    q1 = _head_tile(wq[..., :NOPE_DIM], wq[..., NOPE_DIM:])
    q2 = _head_tile(jnp.zeros_like(wq[..., :NOPE_DIM]), _rope_partner(wq[..., NOPE_DIM:]))
    w["wuq"] = jnp.concatenate([q1.reshape(Q_LORA, HP), q2.reshape(Q_LORA, HP)], axis=-1).astype(BF16)
    w["wuk"] = jnp.concatenate(
        [w_uk, jnp.zeros((KV_LORA, N_HEADS, LANES - NOPE_DIM), F32)], axis=-1).reshape(KV_LORA, HP).astype(BF16)
    w["wuv"] = jnp.concatenate(
        [w_uv, jnp.zeros((KV_LORA, N_HEADS, LANES - V_DIM), F32)], axis=-1).reshape(KV_LORA, HP).astype(BF16)
    w["wuv_h"] = w_uv.transpose(1, 0, 2).astype(BF16)
    wukT = w_uk.transpose(1, 2, 0)
    w["wukT"] = jnp.concatenate(
        [wukT, jnp.zeros((N_HEADS, LANES - NOPE_DIM, KV_LORA), F32)], axis=1).astype(BF16)
    gpad = jnp.zeros((LANES - QK_DIM,), F32)
    w["gq_p"] = row(jnp.concatenate([g_qk_q * (math.log2(math.e) / math.sqrt(QK_DIM)), gpad]))
    w["gq_s"] = row(jnp.concatenate([g_qk_q, gpad]))
    w["gk"] = row(jnp.concatenate([g_qk_k, gpad]))

    dt = jnp.exp(s5_log_dt.astype(F32))[:, None]
    ar, ai = s5_a_re.astype(F32), s5_a_im.astype(F32)
    mag = jnp.exp(dt * ar)
    abar_re, abar_im = mag * jnp.cos(dt * ai), mag * jnp.sin(dt * ai)
    den = ar * ar + ai * ai
    nr, ni = abar_re - 1.0, abar_im
    coef_re = (nr * ar + ni * ai) / den
    coef_im = (ni * ar - nr * ai) / den
    br, bi = s5_b_re.astype(F32), s5_b_im.astype(F32)
    bbar_re = coef_re[..., None] * br - coef_im[..., None] * bi
    bbar_im = coef_re[..., None] * bi + coef_im[..., None] * br
    bre = _block_diag(bbar_re.transpose(0, 2, 1))
    bim = _block_diag(bbar_im.transpose(0, 2, 1))
    w["s5_bw"] = jnp.stack([bre[0], bim[0], bre[1], bim[1]]).astype(BF16)
    cre = _block_diag(s5_c_re.astype(F32).transpose(0, 2, 1))
    cim = _block_diag(-s5_c_im.astype(F32).transpose(0, 2, 1))
    w["s5_cw"] = jnp.stack([cre[0], cim[0], cre[1], cim[1]]).astype(BF16)
    w["s5_are"] = jnp.broadcast_to(abar_re.reshape(1, SSM_S), (SUBLANES, SSM_S))
    w["s5_aim"] = jnp.broadcast_to(abar_im.reshape(1, SSM_S), (SUBLANES, SSM_S))
    w["s5_d"] = row(s5_d)
    w["wglu"] = w_glu.astype(BF16)
    w["bglu"] = row(b_glu)
    w["gout_ssm"] = row(g_out_ssm)
    w["gout_mla"] = row(g_out_mla)
    w["wout"] = w_out.astype(BF16)
    w["gffn"] = row(g_ffn)
    w["wr"] = jnp.concatenate([w_router, jnp.zeros((D_MODEL, LANES - N_EXPERTS), F32)], axis=-1).astype(BF16)
    w["br"] = row(jnp.concatenate([b_router.astype(F32), jnp.full((LANES - N_EXPERTS,), NEG, F32)]))
    w["wgu"] = w_gate_up
    w["bgu"] = b_gate_up.reshape(N_EXPERTS, 1, 2 * D_FF).astype(F32)
    w["wd"] = w_down
    w["bd"] = b_down.reshape(N_EXPERTS, 1, D_MODEL).astype(F32)
    w["gple"] = row(g_ple)
    w["wpg"] = w_ple_gate.astype(BF16)
    w["wp"] = w_ple.astype(BF16)
    return w


EXPERT_ROWS = 512


def _moe_experts(mixed, w):
    h1, xn, route, cnt = mixed
    n = h1.shape[0]
    top_i = route[:, :TOP_K].astype(jnp.int32)
    rank = route[:, 2 * TOP_K:3 * TOP_K].astype(jnp.int32)
    counts = cnt[0, :N_EXPERTS].astype(jnp.int32)
    starts = jnp.cumsum(counts) - counts
    dest = starts[top_i] + rank
    xs = _sc_dispatch(xn, dest)
    bm = min(EXPERT_ROWS, max(LANES, n * TOP_K // N_EXPERTS))
    return _experts(xs, _expert_items(counts, n * TOP_K, bm), w, bm=bm), dest


def _moe_combine(mixed, ys, dest, p2, w, *, tm):
    h1, _, route, _ = mixed
    return _ple(h1, _sc_combine(ys, dest), route, p2, w, tm=tm)


def _after(x, anchor):
    return lax.optimization_barrier((x, anchor))[0]


def kernel(x_prompt, x_sample, cache_kv_latent, cache_k_rope, cache_k_rstd, state_ssm_re, state_ssm_im,
           page_table, p_prompt, p_sample, g_mix, w_in, s5_a_re, s5_a_im, s5_log_dt, s5_b_re, s5_b_im,
           s5_c_re, s5_c_im, s5_d, w_glu, b_glu, g_q_lora, w_uq, g_kv_lora, w_uk, w_uv, g_qk_q, g_qk_k,
           g_out_ssm, g_out_mla, w_out, g_ffn, w_router, b_router, w_gate_up, b_gate_up, w_down, b_down,
           g_ple, w_ple_gate, w_ple):
    assert g_mix.shape[0] == 1, "single-layer step"
    nb, t, _ = x_prompt.shape
    db, ds, _ = x_sample.shape
    assert nb == SUBLANES and db % SUBLANES == 0
    w = _prep_weights(g_mix[0], w_in[0], s5_a_re[0], s5_a_im[0], s5_log_dt[0], s5_b_re[0], s5_b_im[0],
                      s5_c_re[0], s5_c_im[0], s5_d[0], w_glu[0], b_glu[0], g_q_lora[0], w_uq[0],
                      g_kv_lora[0], w_uk[0], w_uv[0], g_qk_q[0], g_qk_k[0], g_out_ssm[0], g_out_mla[0],
                      w_out[0], g_ffn[0], w_router[0], b_router[0], w_gate_up[0], b_gate_up[0],
                      w_down[0], b_down[0], g_ple[0], w_ple_gate[0], w_ple[0])

    tm = min(512, t)
    nt = t // tm
    xp = x_prompt.reshape(nb * t, D_MODEL)
    ct, st = _rope_tables(jnp.arange(t))
    u, q, k, v, c_p, kr_p, rstd_p = _proj(xp, ct, st, w, nb=nb, nt=nt, tm=tm, sample=False)
    steps = min(128, t)
    o_ssm, s_fin = _s5(u.reshape(t * nb, SSM_W), jnp.zeros((nb, 2 * SSM_S), F32), w,
                       ngroups=1, nchunks=t // steps, steps=steps)
    o_mla = _attn_p(q, k, v, nb=nb, t=t, tq=min(1024, t), tk=min(512, t))
    mixed_p = _mix(xp, o_ssm.reshape(t, nb * SSM_W), o_mla, w, nb=nb, nt=nt, tm=tm)
    sr_p = s_fin[:, :SSM_S].reshape(1, nb, SSM_G, SSM_P)
    si_p = s_fin[:, SSM_S:].reshape(1, nb, SSM_G, SSM_P)

    ns = db * ds
    ng = db // SUBLANES
    xs_ = _after(x_sample.reshape(ns, D_MODEL), mixed_p[2])
    ct_s, st_s = _rope_tables(PAST_LEN + jnp.tile(jnp.arange(ds), db))
    u_s, qf, qlat, c_s, kr_s, rstd_s = _proj(xs_, ct_s, st_s, w, nb=1, nt=1, tm=ns, sample=True)
    to_scan = lambda a: a.reshape(ng, SUBLANES, ds, -1).transpose(0, 2, 1, 3).reshape(ns, -1)
    from_scan = lambda a: a.reshape(ng, ds, SUBLANES, -1).transpose(0, 2, 1, 3).reshape(ns, -1)
    h0 = jnp.concatenate([state_ssm_re[0].reshape(db, SSM_S), state_ssm_im[0].reshape(db, SSM_S)],
                         axis=-1).astype(F32)
    o_ssm_s, s_fin_s = _s5(to_scan(u_s), h0, w, ngroups=ng, nchunks=1, steps=ds)
    pad_keys = lambda a: jnp.concatenate(
        [a, jnp.zeros((db, NEW_KEYS - ds, a.shape[-1]), a.dtype)], axis=1)
    key_major = lambda a: a.transpose(0, 2, 1)
    attn_in = (page_table,
               qlat.reshape(db, ds * N_HEADS, KV_LORA),
               qf.reshape(db, ds * N_HEADS, LANES)[:, :, NOPE_DIM:QK_DIM],
               pad_keys(c_s.reshape(db, ds, KV_LORA)),
               key_major(pad_keys(kr_s.reshape(db, ds, ROPE_DIM))),
               key_major(pad_keys(rstd_s.reshape(db, ds, N_HEADS))))
    caches = (cache_kv_latent[0], key_major(cache_k_rope[0]), key_major(cache_k_rstd[0]))

    def attn_rows(lo, hi, anchor=None):
        pt, *rest = (a[lo:hi] for a in attn_in)
        if anchor is not None:
            pt = _after(pt, anchor)
        return _attn_s(pt, *rest, *caches, dec_seq=ds)

    o_lat_a = attn_rows(0, db // 2)
    ys_p, dest_p = _moe_experts(mixed_p, w)
    o_lat_b = attn_rows(db // 2, db, anchor=ys_p)
    o_lat = jnp.concatenate([o_lat_a, o_lat_b], axis=0)
    o_mla_s = _latent_out(o_lat.reshape(ns, N_HEADS * KV_LORA), w["wuv_h"])

    mixed_s = _mix(xs_, from_scan(o_ssm_s), o_mla_s, w, nb=1, nt=1, tm=ns)
    ys_s, dest_s = _moe_experts(mixed_s, w)

    y_p = _moe_combine(mixed_p, ys_p, dest_p, p_prompt[0].reshape(nb * t, PLE_DIM), w, tm=tm)
    y_s = _moe_combine(mixed_s, ys_s, dest_s, p_sample[0].reshape(ns, PLE_DIM), w, tm=ns)

    return (y_p.reshape(nb, t, D_MODEL), y_s.reshape(db, ds, D_MODEL),
            c_p.reshape(1, nb, t, KV_LORA), kr_p.reshape(1, nb, t, ROPE_DIM), rstd_p.reshape(1, nb, t, N_HEADS),
            sr_p, si_p,
            c_s.reshape(1, db, ds, KV_LORA), kr_s.reshape(1, db, ds, ROPE_DIM), rstd_s.reshape(1, db, ds, N_HEADS),
            s_fin_s[:, :SSM_S].reshape(1, db, SSM_G, SSM_P), s_fin_s[:, SSM_S:].reshape(1, db, SSM_G, SSM_P))
```

```python
import functools
import math

import jax
import jax.numpy as jnp
from jax import lax
from jax.experimental import pallas as pl
from jax.experimental.pallas import tpu as pltpu
from jax.experimental.pallas import tpu_sc as plsc

D_MODEL = 1024
SSM_W = 512
SSM_GC = 16
SSM_G = SSM_W // SSM_GC
SSM_P = 64
SSM_S = SSM_G * SSM_P
N_HEADS = 8
NOPE_DIM = 64
ROPE_DIM = 32
ROPE_HALF = ROPE_DIM // 2
QK_DIM = NOPE_DIM + ROPE_DIM
V_DIM = 64
Q_LORA = 384
KV_LORA = 256
ROPE_THETA = 10000.0
N_EXPERTS = 32
TOP_K = 4
D_FF = D_MODEL
SWIGLU_ALPHA = 1.702
SWIGLU_LIMIT = 7.0
PLE_DIM = 256
PAST_LEN = 16384
PAGE_SIZE = 128
EPS = 1e-6
NEG = -1e30

LANES = 128
SUBLANES = 8
VMEM_LIMIT = 56 * 1024 * 1024

HP = N_HEADS * LANES
BF16 = jnp.bfloat16
F32 = jnp.float32


def _cparams(*sem):
    return pltpu.CompilerParams(dimension_semantics=sem, vmem_limit_bytes=VMEM_LIMIT)


def _rms(x, g):
    r = lax.rsqrt(jnp.mean(x * x, axis=-1, keepdims=True) + EPS)
    return x * r * g


def _dot(a, b):
    return jnp.dot(a, b, preferred_element_type=F32)


def _dot_nt(a, b):
    return lax.dot_general(a, b, (((1,), (1,)), ((), ())), preferred_element_type=F32)


def _full(shape):
    nd = len(shape)
    return pl.BlockSpec(shape, lambda *_: (0,) * nd)


def _proj_kernel(x_ref, ct_ref, st_ref, gmix_ref, win_ref, gql_ref, wuq_ref, gkv_ref, wuk_ref,
                 w2_ref, gq_ref, gk_ref, u_ref, q_ref, k2_ref, *rest, sample):
    if sample:
        c_ref, kr_ref, rstd_ref = rest
    else:
        v2_ref, c_ref, kr_ref, rstd_ref = rest
    xn = _rms(x_ref[...], gmix_ref[...]).astype(BF16)
    z = _dot(xn, win_ref[...])
    u_ref[...] = z[:, :SSM_W]
    o = SSM_W
    cq = z[:, o:o + Q_LORA]
    o += Q_LORA
    ckv = z[:, o:o + KV_LORA]
    o += KV_LORA
    ct = ct_ref[...]
    st = st_ref[...]
    kblock = z[:, o:o + LANES] * ct + z[:, o + LANES:o + 2 * LANES] * st
    kr_ref[...] = kblock[:, NOPE_DIM:QK_DIM]

    qq = _dot(_rms(cq, gql_ref[...]).astype(BF16), wuq_ref[...])
    c = _rms(ckv, gkv_ref[...])
    c_ref[...] = c
    cb = c.astype(BF16)
    kn = _dot(cb, wuk_ref[...])
    if not sample:
        hl = lax.broadcasted_iota(jnp.int32, (1, HP), 1) & (LANES - 1)
        v2_ref[...] = (_dot(cb, w2_ref[...]) + jnp.where(hl >= V_DIM, 1.0, 0.0)).astype(BF16)

    lane = lax.broadcasted_iota(jnp.int32, ct.shape, 1)
    rstd_all = jnp.zeros(ct.shape, F32)
    gq = gq_ref[...]
    gk = gk_ref[...]
    for h in range(N_HEADS):
        sl = slice(h * LANES, (h + 1) * LANES)
        qr = qq[:, sl] * ct + qq[:, HP + h * LANES:HP + (h + 1) * LANES] * st
        qn = qr * lax.rsqrt(jnp.sum(qr * qr, axis=-1, keepdims=True) * (1.0 / QK_DIM) + EPS) * gq
        kh = kn[:, sl] + kblock
        rs = lax.rsqrt(jnp.sum(kh * kh, axis=-1, keepdims=True) * (1.0 / QK_DIM) + EPS)
        rstd_all = jnp.where(lane == h, rs, rstd_all)
        if sample:
            qf = (qn * gk).astype(BF16)
            q_ref[:, sl] = qf
            k2_ref[:, h * KV_LORA:(h + 1) * KV_LORA] = _dot(qf, w2_ref[h]).astype(BF16)
        else:
            q_ref[:, sl] = qn.astype(BF16)
            k2_ref[:, sl] = (kh * rs * gk).astype(BF16)
    rstd_ref[...] = rstd_all[:, :N_HEADS]


def _proj(x2, ct, st, w, *, nb, nt, tm, sample):
    n = x2.shape[0]
    row = lambda b, t: (b * nt + t, 0)
    rows = lambda width: pl.BlockSpec((tm, width), row)
    k2_w = N_HEADS * KV_LORA if sample else HP
    w2 = w["wukT"] if sample else w["wuv"]
    out_shape = [
        jax.ShapeDtypeStruct((nt * tm, nb * SSM_W), F32),
        jax.ShapeDtypeStruct((n, HP), BF16),
        jax.ShapeDtypeStruct((n, k2_w), BF16),
        *([] if sample else [jax.ShapeDtypeStruct((n, HP), BF16)]),
        jax.ShapeDtypeStruct((n, KV_LORA), F32),
        jax.ShapeDtypeStruct((n, ROPE_DIM), F32),
        jax.ShapeDtypeStruct((n, N_HEADS), F32),
    ]
    out_specs = [
        pl.BlockSpec((tm, SSM_W), lambda b, t: (t, b)),
        rows(HP), rows(k2_w), *([] if sample else [rows(HP)]),
        rows(KV_LORA), rows(ROPE_DIM), rows(N_HEADS),
    ]
    in_specs = [
        rows(D_MODEL),
        pl.BlockSpec((tm, LANES), lambda b, t: (t, 0)),
        pl.BlockSpec((tm, LANES), lambda b, t: (t, 0)),
        _full(w["gmix"].shape), _full(w["win"].shape), _full(w["gql"].shape), _full(w["wuq"].shape),
        _full(w["gkv"].shape), _full(w["wuk"].shape), _full(w2.shape),
        _full(w["gq_s" if sample else "gq_p"].shape), _full(w["gk"].shape),
    ]
    return pl.pallas_call(
        functools.partial(_proj_kernel, sample=sample),
        grid=(nb, nt),
        in_specs=in_specs,
        out_specs=out_specs,
        out_shape=out_shape,
        compiler_params=_cparams("parallel", "parallel"),
        name="proj_s" if sample else "proj_p",
    )(x2, ct, st, w["gmix"], w["win"], w["gql"], w["wuq"], w["gkv"], w["wuk"], w2,
      w["gq_s" if sample else "gq_p"], w["gk"])


S5_HALF = SSM_S // 2
S5_QUARTER = SSM_S // 4


def _s5_kernel(u_ref, h0_ref, are_ref, aim_ref, bw_ref, cw_ref, dsk_ref, wglu_ref, bglu_ref, gout_ref,
               o_ref, sfin_ref, xs_ref, st_ref, *, steps):
    @pl.when(pl.program_id(1) == 0)
    def _():
        st_ref[...] = h0_ref[...]

    u = u_ref[...]
    ub = u.astype(BF16)
    half_w = SSM_W // 2
    for k in range(2):
        uk = ub[:, k * half_w:(k + 1) * half_w]
        xs_ref[:, k * S5_HALF:(k + 1) * S5_HALF] = _dot(uk, bw_ref[2 * k])
        xs_ref[:, SSM_S + k * S5_HALF:SSM_S + (k + 1) * S5_HALF] = _dot(uk, bw_ref[2 * k + 1])

    for q in range(4):
        lr = slice(q * S5_QUARTER, (q + 1) * S5_QUARTER)
        li = slice(SSM_S + q * S5_QUARTER, SSM_S + (q + 1) * S5_QUARTER)
        ar = are_ref[:, lr]
        ai = aim_ref[:, lr]

        def step(t, carry):
            sr, si = carry
            r0 = pl.multiple_of(t * SUBLANES, SUBLANES)
            nr = ar * sr - ai * si + xs_ref[pl.ds(r0, SUBLANES), lr]
            ni = ar * si + ai * sr + xs_ref[pl.ds(r0, SUBLANES), li]
            xs_ref[pl.ds(r0, SUBLANES), lr] = nr
            xs_ref[pl.ds(r0, SUBLANES), li] = ni
            return nr, ni

        sr, si = lax.fori_loop(0, steps, step, (st_ref[:, lr], st_ref[:, li]), unroll=4)
        st_ref[:, lr] = sr
        st_ref[:, li] = si
    sfin_ref[...] = st_ref[...]

    ys = []
    for k in range(2):
        sre = xs_ref[:, k * S5_HALF:(k + 1) * S5_HALF].astype(BF16)
        sim = xs_ref[:, SSM_S + k * S5_HALF:SSM_S + (k + 1) * S5_HALF].astype(BF16)
        ys.append(_dot(sre, cw_ref[2 * k]) + _dot(sim, cw_ref[2 * k + 1]))
    y = jnp.concatenate(ys, axis=-1) + dsk_ref[...] * u
    g = jax.nn.gelu(y)
    out = g * jax.nn.sigmoid(_dot(g.astype(BF16), wglu_ref[...]) + bglu_ref[...])
    o_ref[...] = _rms(out, gout_ref[...]).astype(BF16)


def _s5(u2, h0, w, *, ngroups, nchunks, steps):
    rows = steps * SUBLANES
    blk = lambda g, t: (g * nchunks + t, 0)
    return pl.pallas_call(
        functools.partial(_s5_kernel, steps=steps),
        grid=(ngroups, nchunks),
        in_specs=[
            pl.BlockSpec((rows, SSM_W), blk),
            pl.BlockSpec((SUBLANES, 2 * SSM_S), lambda g, t: (g, 0)),
            _full(w["s5_are"].shape), _full(w["s5_aim"].shape), _full(w["s5_bw"].shape),
            _full(w["s5_cw"].shape), _full(w["s5_d"].shape), _full(w["wglu"].shape),
            _full(w["bglu"].shape), _full(w["gout_ssm"].shape),
        ],
        out_specs=[
            pl.BlockSpec((rows, SSM_W), blk),
            pl.BlockSpec((SUBLANES, 2 * SSM_S), lambda g, t: (g, 0)),
        ],
        out_shape=[
            jax.ShapeDtypeStruct(u2.shape, BF16),
            jax.ShapeDtypeStruct(h0.shape, F32),
        ],
        scratch_shapes=[pltpu.VMEM((rows, 2 * SSM_S), F32), pltpu.VMEM((SUBLANES, 2 * SSM_S), F32)],
        compiler_params=_cparams("parallel", "arbitrary"),
        name="s5",
    )(u2, h0, w["s5_are"], w["s5_aim"], w["s5_bw"], w["s5_cw"], w["s5_d"], w["wglu"], w["bglu"],
      w["gout_ssm"])


def _attn_p_kernel(q_ref, k_ref, v_ref, o_ref, *, tq, tk):
    i = pl.program_id(2)
    nd = tq // tk
    row = lax.broadcasted_iota(jnp.int32, (tk, tk), 0)
    col = lax.broadcasted_iota(jnp.int32, (tk, tk), 1)

    def block(j, carry, diag):
        k0 = pl.multiple_of(j * tk, tk)
        k2 = k_ref[pl.ds(k0, tk), :]
        v2 = v_ref[pl.ds(k0, tk), :]
        r0 = 0 if diag is None else diag * tk
        out = []
        for hh in range(2):
            m, acc = carry[hh]
            s = _dot_nt(q_ref[r0:, hh * LANES:(hh + 1) * LANES], k2[:, hh * LANES:(hh + 1) * LANES])
            if diag is not None:
                tri = jnp.where(col <= row, s[:tk], NEG)
                s = tri if r0 + tk == tq else jnp.concatenate([tri, s[tk:]], axis=0)
            m_new = jnp.maximum(m[r0:], jnp.max(s, axis=-1, keepdims=True))
            p = jnp.exp2(s - m_new).astype(BF16)
            acc_new = jnp.exp2(m[r0:] - m_new) * acc[r0:] + _dot(p, v2[:, hh * LANES:(hh + 1) * LANES])
            if r0:
                m_new = jnp.concatenate([m[:r0], m_new], axis=0)
                acc_new = jnp.concatenate([acc[:r0], acc_new], axis=0)
            out.append((m_new, acc_new))
        return tuple(out)

    init = tuple((jnp.full((tq, 1), NEG, F32), jnp.zeros((tq, LANES), F32)) for _ in range(2))
    carry = lax.fori_loop(0, i * nd, lambda j, c: block(j, c, None), init)
    for d in range(nd):
        carry = block(i * nd + d, carry, d)
    o_even, o_odd = (acc / pltpu.roll(acc, V_DIM, 1) for _, acc in carry)
    lane = lax.broadcasted_iota(jnp.int32, (tq, LANES), 1)
    o_ref[...] = jnp.where(lane < V_DIM, o_even, pltpu.roll(o_odd, V_DIM, 1))


def _attn_p(q, k, v, *, nb, t, tq, tk):
    nq = t // tq
    return pl.pallas_call(
        functools.partial(_attn_p_kernel, tq=tq, tk=tk),
        grid=(nb, N_HEADS // 2, nq),
        in_specs=[
            pl.BlockSpec((tq, 2 * LANES), lambda b, h, i: (b * nq + i, h)),
            pl.BlockSpec((t, 2 * LANES), lambda b, h, i: (b, h)),
            pl.BlockSpec((t, 2 * LANES), lambda b, h, i: (b, h)),
        ],
        out_specs=pl.BlockSpec((tq, 2 * V_DIM), lambda b, h, i: (b * nq + i, h)),
        out_shape=jax.ShapeDtypeStruct((nb * t, N_HEADS * V_DIM), F32),
        compiler_params=_cparams("parallel", "parallel", "arbitrary"),
        name="attn_p",
    )(q, k, v)


ATTN_S_PAGES = 32
NEW_KEYS = 16


def _attn_s_kernel(pt_ref, ql_ref, qr_ref, cn_ref, krn_ref, rnt_ref, cc_hbm, ckr_hbm, cr_hbm,
                   o_ref, cbuf, krbuf, rbuf, cb_ref, sc_ref, sems, *, n_pages, dec_seq):
    b = pl.program_id(0)
    nb = pl.num_programs(0)
    bp = min(ATTN_S_PAGES, n_pages)
    nblk = n_pages // bp
    keys = bp * PAGE_SIZE
    nrow = dec_seq * N_HEADS
    scale = 1.0 / math.sqrt(QK_DIM)
    slot = b % 2

    def start_block(bb, i, dst):
        for p in range(bp):
            pg = i * bp + p
            page = pt_ref[bb * n_pages + pg]
            lanes = pl.ds(pl.multiple_of(pg * PAGE_SIZE, PAGE_SIZE), PAGE_SIZE)
            pltpu.make_async_copy(cc_hbm.at[page], cbuf.at[dst, pg], sems.at[dst, 0]).start()
            pltpu.make_async_copy(ckr_hbm.at[page], krbuf.at[dst, :, lanes], sems.at[dst, 1]).start()
            pltpu.make_async_copy(cr_hbm.at[page], rbuf.at[dst, :, lanes], sems.at[dst, 2]).start()

    @pl.when(b == 0)
    def _():
        lax.fori_loop(0, nblk, lambda i, c: (start_block(0, i, 0), c)[1], 0)

    pltpu.make_async_copy(cbuf.at[slot], cbuf.at[slot], sems.at[slot, 0]).wait()
    pltpu.make_async_copy(krbuf.at[slot], krbuf.at[slot], sems.at[slot, 1]).wait()
    pltpu.make_async_copy(rbuf.at[slot], rbuf.at[slot], sems.at[slot, 2]).wait()

    ql = ql_ref[0]
    qr = qr_ref[0]

    def tile_heads(r_t):
        return jnp.concatenate([r_t] * dec_seq, axis=0)

    cnb = cn_ref[0].astype(BF16)
    s_n = (_dot_nt(ql, cnb) + _dot(qr, krn_ref[0].astype(BF16))) * tile_heads(rnt_ref[0]) * scale
    qs = lax.broadcasted_iota(jnp.int32, (nrow, NEW_KEYS), 0) // N_HEADS
    kj = lax.broadcasted_iota(jnp.int32, (nrow, NEW_KEYS), 1)
    s_n = jnp.where(kj <= qs, s_n, NEG)

    def score_block(i, m):
        @pl.when(b + 1 < nb)
        def _():
            start_block(b + 1, i, 1 - slot)

        k0 = pl.multiple_of(i * keys, keys)
        cb = cbuf[slot, pl.ds(i * bp, bp)].reshape(keys, KV_LORA).astype(BF16)
        cb_ref[pl.ds(k0, keys), :] = cb
        raw = _dot_nt(ql, cb) + _dot(qr, krbuf[slot, :, pl.ds(k0, keys)].astype(BF16))
        s = raw * tile_heads(rbuf[slot, :, pl.ds(k0, keys)]) * scale
        sc_ref[:, pl.ds(k0, keys)] = s
        return jnp.maximum(m, jnp.max(s, axis=-1, keepdims=True))

    m = lax.fori_loop(0, nblk, score_block, jnp.max(s_n, axis=-1, keepdims=True))
    p_n = jnp.exp(s_n - m)

    def value_block(i, carry):
        l, acc = carry
        k0 = pl.multiple_of(i * keys, keys)
        p = jnp.exp(sc_ref[:, pl.ds(k0, keys)] - m)
        return (l + jnp.sum(p, axis=-1, keepdims=True),
                acc + _dot(p.astype(BF16), cb_ref[pl.ds(k0, keys), :]))

    l, acc = lax.fori_loop(0, nblk, value_block,
                           (jnp.sum(p_n, axis=-1, keepdims=True), _dot(p_n.astype(BF16), cnb)))
    o_ref[0] = acc / l


def _attn_s(page_table, ql, qr, cn, krn, rnt, cache_c, cache_kr, cache_r, *, dec_seq):
    nb, n_pages = page_table.shape
    nrow = dec_seq * N_HEADS
    b3 = lambda b, pt: (b, 0, 0)
    return pl.pallas_call(
        functools.partial(_attn_s_kernel, n_pages=n_pages, dec_seq=dec_seq),
        grid_spec=pltpu.PrefetchScalarGridSpec(
            num_scalar_prefetch=1,
            grid=(nb,),
            in_specs=[
                pl.BlockSpec((1, nrow, KV_LORA), b3),
                pl.BlockSpec((1, nrow, ROPE_DIM), b3),
                pl.BlockSpec((1, NEW_KEYS, KV_LORA), b3),
                pl.BlockSpec((1, ROPE_DIM, NEW_KEYS), b3),
                pl.BlockSpec((1, N_HEADS, NEW_KEYS), b3),
                pl.BlockSpec(memory_space=pl.ANY),
                pl.BlockSpec(memory_space=pl.ANY),
                pl.BlockSpec(memory_space=pl.ANY),
            ],
            out_specs=pl.BlockSpec((1, nrow, KV_LORA), b3),
            scratch_shapes=[
                pltpu.VMEM((2, n_pages, PAGE_SIZE, KV_LORA), F32),
                pltpu.VMEM((2, ROPE_DIM, n_pages * PAGE_SIZE), F32),
                pltpu.VMEM((2, N_HEADS, n_pages * PAGE_SIZE), F32),
                pltpu.VMEM((n_pages * PAGE_SIZE, KV_LORA), BF16),
                pltpu.VMEM((nrow, n_pages * PAGE_SIZE), F32),
                pltpu.SemaphoreType.DMA((2, 3)),
            ],
        ),
        out_shape=jax.ShapeDtypeStruct((nb, nrow, KV_LORA), F32),
        compiler_params=_cparams("arbitrary"),
        name="attn_s",
    )(page_table.reshape(-1), ql, qr, cn, krn, rnt, cache_c, cache_kr, cache_r)


def _latent_out_kernel(ol_ref, wuv_ref, o_ref):
    for h in range(N_HEADS):
        o_ref[:, h * V_DIM:(h + 1) * V_DIM] = _dot(
            ol_ref[:, h * KV_LORA:(h + 1) * KV_LORA].astype(BF16), wuv_ref[h])


def _latent_out(ol, wuv_h):
    n = ol.shape[0]
    return pl.pallas_call(
        _latent_out_kernel,
        grid=(1,),
        in_specs=[_full(ol.shape), _full(wuv_h.shape)],
        out_specs=_full((n, N_HEADS * V_DIM)),
        out_shape=jax.ShapeDtypeStruct((n, N_HEADS * V_DIM), F32),
        compiler_params=_cparams("arbitrary"),
        name="latent_out",
    )(ol, wuv_h)


ROUTE_W = 16


def _mix_kernel(h_ref, os_ref, om_ref, gom_ref, wout_ref, gffn_ref, wr_ref, br_ref,
                h1_ref, xn_ref, route_ref, cnt_ref):
    @pl.when((pl.program_id(0) == 0) & (pl.program_id(1) == 0))
    def _():
        cnt_ref[...] = jnp.zeros_like(cnt_ref)

    tm = h_ref.shape[0]
    omn = _rms(om_ref[...], gom_ref[...]).astype(BF16)
    h1 = h_ref[...] + _dot(os_ref[...], wout_ref[:SSM_W, :]) + _dot(omn, wout_ref[SSM_W:, :])
    h1_ref[...] = h1
    xn = _rms(h1, gffn_ref[...])
    xn_ref[...] = xn
    work = _dot(xn.astype(BF16), wr_ref[...]) + br_ref[...]

    lane = lax.broadcasted_iota(jnp.int32, work.shape, 1).astype(F32)
    route = jnp.zeros(work.shape, F32)
    sels, vals = [], []
    for k in range(TOP_K):
        mk = jnp.max(work, axis=-1, keepdims=True)
        ik = jnp.min(jnp.where(work == mk, lane, float(LANES)), axis=-1, keepdims=True)
        sel = lane == ik
        work = jnp.where(sel, -jnp.inf, work)
        route = jnp.where(lane == float(k), ik, route)
        sels.append(sel)
        vals.append(mk)
    es = [jnp.exp(v - vals[0]) for v in vals]
    den = es[0] + es[1] + es[2] + es[3]
    onehot = jnp.zeros(work.shape, F32)
    for k in range(TOP_K):
        route = jnp.where(lane == float(TOP_K + k), es[k] / den, route)
        onehot = jnp.where(sels[k], 1.0, onehot)

    r_i = lax.broadcasted_iota(jnp.int32, (tm, tm), 0)
    c_i = lax.broadcasted_iota(jnp.int32, (tm, tm), 1)
    tri = jnp.where(c_i < r_i, 1.0, 0.0).astype(BF16)
    cum = _dot(tri, onehot.astype(BF16)) + cnt_ref[...]
    for k in range(TOP_K):
        rank = jnp.sum(jnp.where(sels[k], cum, 0.0), axis=-1, keepdims=True)
        route = jnp.where(lane == float(2 * TOP_K + k), rank, route)
    route_ref[...] = route[:, :ROUTE_W]
    cnt_ref[...] = cnt_ref[...] + jnp.sum(onehot, axis=0, keepdims=True)


def _mix(h2, o_ssm, o_mla, w, *, nb, nt, tm):
    n = h2.shape[0]
    row = lambda b, t: (b * nt + t, 0)
    return pl.pallas_call(
        _mix_kernel,
        grid=(nb, nt),
        in_specs=[
            pl.BlockSpec((tm, D_MODEL), row),
            pl.BlockSpec((tm, SSM_W), lambda b, t: (t, b)),
            pl.BlockSpec((tm, N_HEADS * V_DIM), row),
            _full(w["gout_mla"].shape), _full(w["wout"].shape), _full(w["gffn"].shape),
            _full(w["wr"].shape), _full(w["br"].shape),
        ],
        out_specs=[
            pl.BlockSpec((tm, D_MODEL), row),
            pl.BlockSpec((tm, D_MODEL), row),
            pl.BlockSpec((tm, ROUTE_W), row),
            _full((1, LANES)),
        ],
        out_shape=[
            jax.ShapeDtypeStruct((n, D_MODEL), F32),
            jax.ShapeDtypeStruct((n, D_MODEL), F32),
            jax.ShapeDtypeStruct((n, ROUTE_W), F32),
            jax.ShapeDtypeStruct((1, LANES), F32),
        ],
        compiler_params=_cparams("arbitrary", "arbitrary"),
        name="mix",
    )(h2, o_ssm, o_mla, w["gout_mla"], w["wout"], w["gffn"], w["wr"], w["br"])


def _experts_kernel(tile_ref, e_ref, first_ref, valid_ref, lo_ref, hi_ref, newe_ref,
                    x_ref, wgu_ref, bgu_ref, wd_ref, bd_ref, o_ref, wgu_bf, wd_bf):
    i = pl.program_id(0)

    @pl.when(newe_ref[i] == 1)
    def _():
        wgu_bf[...] = wgu_ref[0].astype(BF16)
        wd_bf[...] = wd_ref[0].astype(BF16)

    @pl.when(valid_ref[i] == 1)
    def _():
        bm = x_ref.shape[0]
        hdn = _dot(x_ref[...].astype(BF16), wgu_bf[...]) + bgu_ref[0]
        x_glu = jnp.minimum(hdn[:, :D_FF], SWIGLU_LIMIT)
        x_lin = jnp.clip(hdn[:, D_FF:], -SWIGLU_LIMIT, SWIGLU_LIMIT)
        act = x_glu * jax.nn.sigmoid(SWIGLU_ALPHA * x_glu) * (x_lin + 1.0)
        y = _dot(act.astype(BF16), wd_bf[...]) + bd_ref[0]
        rows = tile_ref[i] * bm + lax.broadcasted_iota(jnp.int32, (bm, 1), 0)
        mine = (rows >= lo_ref[i]) & (rows < hi_ref[i])

        @pl.when(first_ref[i] == 1)
        def _():
            o_ref[...] = jnp.where(mine, y, 0.0)

        @pl.when(first_ref[i] == 0)
        def _():
            o_ref[...] = jnp.where(mine, y, o_ref[...])


def _experts(xs, items, w, *, bm):
    r = xs.shape[0]
    n_items = items[0].shape[0]
    xmap = lambda i, tile, *_: (tile[i], 0)
    emap = lambda i, tile, e, *_: (e[i], 0, 0)
    return pl.pallas_call(
        _experts_kernel,
        grid_spec=pltpu.PrefetchScalarGridSpec(
            num_scalar_prefetch=len(items),
            grid=(n_items,),
            in_specs=[
                pl.BlockSpec((bm, D_MODEL), xmap),
                pl.BlockSpec((1, D_MODEL, 2 * D_FF), emap),
                pl.BlockSpec((1, 1, 2 * D_FF), emap),
                pl.BlockSpec((1, D_FF, D_MODEL), emap),
                pl.BlockSpec((1, 1, D_MODEL), emap),
            ],
            out_specs=pl.BlockSpec((bm, D_MODEL), xmap),
            scratch_shapes=[pltpu.VMEM((D_MODEL, 2 * D_FF), BF16), pltpu.VMEM((D_FF, D_MODEL), BF16)],
        ),
        out_shape=jax.ShapeDtypeStruct((r, D_MODEL), F32),
        compiler_params=_cparams("arbitrary"),
        name="experts",
    )(*items, xs, w["wgu"], w["bgu"], w["wd"], w["bd"])


def _expert_items(counts, n_rows, bm):
    n_tiles = n_rows // bm
    n_items = n_tiles + N_EXPERTS - 1
    ends = jnp.cumsum(counts)
    starts = ends - counts
    first_tile = starts // bm
    per_e = jnp.where(counts > 0, (ends - 1) // bm - first_tile + 1, 0)
    item_end = jnp.cumsum(per_e)
    i = jnp.arange(n_items, dtype=jnp.int32)
    valid = i < item_end[-1]
    e = jnp.minimum(jnp.sum(i[:, None] >= item_end[None, :], axis=1), N_EXPERTS - 1).astype(jnp.int32)
    tile = first_tile[e] + i - (item_end[e] - per_e[e])
    tile = jnp.where(valid, tile, n_tiles - 1).astype(jnp.int32)
    e = jnp.where(valid, e, e[item_end[-1] - 1])
    one = jnp.ones((1,), jnp.int32)
    first = jnp.concatenate([one, (tile[1:] != tile[:-1]).astype(jnp.int32)])
    new_e = jnp.concatenate([one, (e[1:] != e[:-1]).astype(jnp.int32)])
    return (tile, e, first, valid.astype(jnp.int32),
            starts[e].astype(jnp.int32), ends[e].astype(jnp.int32), new_e)


def _ple_kernel(h1_ref, yg_ref, route_ref, p_ref, gple_ref, wg_ref, wp_ref, o_ref):
    h2 = h1_ref[...]
    route = route_ref[...]
    w = SC_GATHER_TOKENS
    for k in range(TOP_K):
        yk = yg_ref[:, k * w:(k + 1) * w, :].reshape(h2.shape)
        h2 = h2 + yk * route[:, TOP_K + k:TOP_K + k + 1]
    gate = jax.nn.sigmoid(_dot(_rms(h2, gple_ref[...]).astype(BF16), wg_ref[...]))
    o_ref[...] = h2 + _dot(p_ref[...].astype(BF16), wp_ref[...]) * gate


def _ple(h1, yg, route, p, w, *, tm):
    n = h1.shape[0]
    row = lambda t: (t, 0)
    return pl.pallas_call(
        _ple_kernel,
        grid=(n // tm,),
        in_specs=[
            pl.BlockSpec((tm, D_MODEL), row),
            pl.BlockSpec((tm // SC_GATHER_TOKENS, TOP_K * SC_GATHER_TOKENS, D_MODEL), lambda t: (t, 0, 0)),
            pl.BlockSpec((tm, ROUTE_W), row),
            pl.BlockSpec((tm, PLE_DIM), row),
            _full(w["gple"].shape), _full(w["wpg"].shape), _full(w["wp"].shape),
        ],
        out_specs=pl.BlockSpec((tm, D_MODEL), row),
        out_shape=jax.ShapeDtypeStruct((n, D_MODEL), F32),
        compiler_params=_cparams("parallel"),
        name="ple",
    )(h1, yg, route, p, w["gple"], w["wpg"], w["wp"])


SC_INDEX_ROW = 128
SC_SCATTER_TOKENS = SC_INDEX_ROW // TOP_K
SC_GATHER_TOKENS = 8


def _sc_mesh():
    return plsc.VectorSubcoreMesh(core_axis_name="c", subcore_axis_name="s")


def _window_index(dest, w):
    n = dest.shape[0]
    idx = dest.reshape(n // w, w, TOP_K).transpose(0, 2, 1).reshape(n // w, TOP_K * w)
    if TOP_K * w < SC_INDEX_ROW:
        idx = jnp.concatenate([idx, jnp.zeros((n // w, SC_INDEX_ROW - TOP_K * w), jnp.int32)], axis=-1)
    return idx


def _sc_dispatch(xn, idx):
    n = xn.shape[0]
    w = SC_SCATTER_TOKENS

    @pl.kernel(out_type=jax.ShapeDtypeStruct((n * TOP_K, D_MODEL), xn.dtype),
               mesh=_sc_mesh(), scratch_types=[pltpu.SemaphoreType.DMA])
    def scatter(x_hbm, i_hbm, o_hbm, sem):
        def body(x_vmem, i_vmem):
            copies = [pltpu.make_async_copy(x_vmem, o_hbm.at[i_vmem.at[0, pl.ds(k * w, w)]], sem)
                      for k in range(TOP_K)]
            for cp in copies:
                cp.start()
            for cp in copies:
                cp.wait()

        pltpu.emit_pipeline(
            body,
            grid=(n // w,),
            in_specs=[pl.BlockSpec((w, D_MODEL), lambda i: (i, 0)),
                      pl.BlockSpec((1, SC_INDEX_ROW), lambda i: (i, 0))],
            out_specs=[],
            core_axis_name=("c", "s"),
            dimension_semantics=(pltpu.PARALLEL,),
        )(x_hbm, i_hbm)

    return scatter(xn, idx)


def _sc_combine(ys, idx):
    w = SC_GATHER_TOKENS
    n = idx.shape[0] * w
    rows = TOP_K * w

    @pl.kernel(out_type=jax.ShapeDtypeStruct((n // w, rows, D_MODEL), ys.dtype),
               mesh=_sc_mesh(), scratch_types=[])
    def gather(y_hbm, i_hbm, o_hbm):
        def body(i_vmem, o_vmem):
            pltpu.sync_copy(y_hbm.at[i_vmem.at[0, pl.ds(0, rows)]], o_vmem.at[0])

        pltpu.emit_pipeline(
            body,
            grid=(n // w,),
            in_specs=[pl.BlockSpec((1, SC_INDEX_ROW), lambda i: (i, 0))],
            out_specs=[pl.BlockSpec((1, rows, D_MODEL), lambda i: (i, 0, 0))],
            core_axis_name=("c", "s"),
            dimension_semantics=(pltpu.PARALLEL,),
        )(i_hbm, o_hbm)

    return gather(ys, idx)


def _rope_tables(pos):
    inv = ROPE_THETA ** (-jnp.arange(0, ROPE_DIM, 2, dtype=F32) / ROPE_DIM)
    ang = pos.astype(F32)[:, None] * inv[None, :]
    cos, sin = jnp.cos(ang), jnp.sin(ang)
    n = pos.shape[0]
    pad = jnp.zeros((n, LANES - QK_DIM), F32)
    ct = jnp.concatenate([jnp.ones((n, NOPE_DIM), F32), cos, cos, pad], axis=-1)
    st = jnp.concatenate([jnp.zeros((n, NOPE_DIM), F32), sin, sin, pad], axis=-1)
    return ct, st


def _rope_partner(w_rope):
    return jnp.concatenate([-w_rope[..., ROPE_HALF:], w_rope[..., :ROPE_HALF]], axis=-1)


def _head_tile(nope, rope):
    pad = jnp.zeros(rope.shape[:-1] + (LANES - QK_DIM,), rope.dtype)
    return jnp.concatenate([nope, rope, pad], axis=-1)


def _block_diag(m):
    g, a, b = m.shape
    hg = g // 2
    eye = jnp.eye(hg, dtype=m.dtype)
    mh = m.reshape(2, hg, a, b)
    return jnp.einsum("kgab,gh->kgahb", mh, eye).reshape(2, hg * a, hg * b)


def _prep_weights(g_mix, w_in, s5_a_re, s5_a_im, s5_log_dt, s5_b_re, s5_b_im, s5_c_re, s5_c_im, s5_d,
                  w_glu, b_glu, g_q_lora, w_uq, g_kv_lora, w_uk, w_uv, g_qk_q, g_qk_k, g_out_ssm,
                  g_out_mla, w_out, g_ffn, w_router, b_router, w_gate_up, b_gate_up, w_down, b_down,
                  g_ple, w_ple_gate, w_ple):
    w = {}
    row = lambda v: v.reshape(1, -1).astype(F32)
    o = SSM_W + Q_LORA + KV_LORA
    w_kr = w_in[:, o:o + ROPE_DIM]
    zeros_n = jnp.zeros((D_MODEL, NOPE_DIM), F32)
    w["win"] = jnp.concatenate(
        [w_in[:, :o], _head_tile(zeros_n, w_kr), _head_tile(zeros_n, _rope_partner(w_kr))], axis=-1).astype(BF16)
    w["gmix"] = row(g_mix)
    w["gql"] = row(g_q_lora)
    w["gkv"] = row(g_kv_lora)

    wq = w_uq.reshape(Q_LORA, N_HEADS, QK_DIM)
    q1 = _head_tile(wq[..., :NOPE_DIM], wq[..., NOPE_DIM:])
    q2 = _head_tile(jnp.zeros_like(wq[..., :NOPE_DIM]), _rope_partner(wq[..., NOPE_DIM:]))
    w["wuq"] = jnp.concatenate([q1.reshape(Q_LORA, HP), q2.reshape(Q_LORA, HP)], axis=-1).astype(BF16)
    w["wuk"] = jnp.concatenate(
        [w_uk, jnp.zeros((KV_LORA, N_HEADS, LANES - NOPE_DIM), F32)], axis=-1).reshape(KV_LORA, HP).astype(BF16)
    w["wuv"] = jnp.concatenate(
        [w_uv, jnp.zeros((KV_LORA, N_HEADS, LANES - V_DIM), F32)], axis=-1).reshape(KV_LORA, HP).astype(BF16)
    w["wuv_h"] = w_uv.transpose(1, 0, 2).astype(BF16)
    wukT = w_uk.transpose(1, 2, 0)
    w["wukT"] = jnp.concatenate(
        [wukT, jnp.zeros((N_HEADS, LANES - NOPE_DIM, KV_LORA), F32)], axis=1).astype(BF16)
    gpad = jnp.zeros((LANES - QK_DIM,), F32)
    w["gq_p"] = row(jnp.concatenate([g_qk_q * (math.log2(math.e) / math.sqrt(QK_DIM)), gpad]))
    w["gq_s"] = row(jnp.concatenate([g_qk_q, gpad]))
    w["gk"] = row(jnp.concatenate([g_qk_k, gpad]))

    dt = jnp.exp(s5_log_dt.astype(F32))[:, None]
    ar, ai = s5_a_re.astype(F32), s5_a_im.astype(F32)
    mag = jnp.exp(dt * ar)
    abar_re, abar_im = mag * jnp.cos(dt * ai), mag * jnp.sin(dt * ai)
    den = ar * ar + ai * ai
    nr, ni = abar_re - 1.0, abar_im
    coef_re = (nr * ar + ni * ai) / den
    coef_im = (ni * ar - nr * ai) / den
    br, bi = s5_b_re.astype(F32), s5_b_im.astype(F32)
    bbar_re = coef_re[..., None] * br - coef_im[..., None] * bi
    bbar_im = coef_re[..., None] * bi + coef_im[..., None] * br
    bre = _block_diag(bbar_re.transpose(0, 2, 1))
    bim = _block_diag(bbar_im.transpose(0, 2, 1))
    w["s5_bw"] = jnp.stack([bre[0], bim[0], bre[1], bim[1]]).astype(BF16)
    cre = _block_diag(s5_c_re.astype(F32).transpose(0, 2, 1))
    cim = _block_diag(-s5_c_im.astype(F32).transpose(0, 2, 1))
    w["s5_cw"] = jnp.stack([cre[0], cim[0], cre[1], cim[1]]).astype(BF16)
    w["s5_are"] = jnp.broadcast_to(abar_re.reshape(1, SSM_S), (SUBLANES, SSM_S))
    w["s5_aim"] = jnp.broadcast_to(abar_im.reshape(1, SSM_S), (SUBLANES, SSM_S))
    w["s5_d"] = row(s5_d)
    w["wglu"] = w_glu.astype(BF16)
    w["bglu"] = row(b_glu)
    w["gout_ssm"] = row(g_out_ssm)
    w["gout_mla"] = row(g_out_mla)
    w["wout"] = w_out.astype(BF16)
    w["gffn"] = row(g_ffn)
    w["wr"] = jnp.concatenate([w_router, jnp.zeros((D_MODEL, LANES - N_EXPERTS), F32)], axis=-1).astype(BF16)
    w["br"] = row(jnp.concatenate([b_router.astype(F32), jnp.full((LANES - N_EXPERTS,), NEG, F32)]))
    w["wgu"] = w_gate_up
    w["bgu"] = b_gate_up.reshape(N_EXPERTS, 1, 2 * D_FF).astype(F32)
    w["wd"] = w_down
    w["bd"] = b_down.reshape(N_EXPERTS, 1, D_MODEL).astype(F32)
    w["gple"] = row(g_ple)
    w["wpg"] = w_ple_gate.astype(BF16)
    w["wp"] = w_ple.astype(BF16)
    return w


EXPERT_ROWS = 512


def _moe_experts(mixed, w, during_dispatch=()):
    h1, xn, route, cnt = mixed
    n = h1.shape[0]
    top_i = route[:, :TOP_K].astype(jnp.int32)
    rank = route[:, 2 * TOP_K:3 * TOP_K].astype(jnp.int32)
    counts = cnt[0, :N_EXPERTS].astype(jnp.int32)
    starts = jnp.cumsum(counts) - counts
    dest = starts[top_i] + rank
    gather_idx = _window_index(dest, SC_GATHER_TOKENS)
    xs = _after(_sc_dispatch(xn, _window_index(dest, SC_SCATTER_TOKENS)), (gather_idx, during_dispatch))
    bm = min(EXPERT_ROWS, max(LANES, n * TOP_K // N_EXPERTS))
    return _experts(xs, _expert_items(counts, n * TOP_K, bm), w, bm=bm), gather_idx


def _moe_combine(mixed, ys, gather_idx, p2, w, *, tm):
    h1, _, route, _ = mixed
    return _ple(h1, _sc_combine(ys, gather_idx), route, p2, w, tm=tm)


def _after(x, anchor):
    return lax.optimization_barrier((x, anchor))[0]


def kernel(x_prompt, x_sample, cache_kv_latent, cache_k_rope, cache_k_rstd, state_ssm_re, state_ssm_im,
           page_table, p_prompt, p_sample, g_mix, w_in, s5_a_re, s5_a_im, s5_log_dt, s5_b_re, s5_b_im,
           s5_c_re, s5_c_im, s5_d, w_glu, b_glu, g_q_lora, w_uq, g_kv_lora, w_uk, w_uv, g_qk_q, g_qk_k,
           g_out_ssm, g_out_mla, w_out, g_ffn, w_router, b_router, w_gate_up, b_gate_up, w_down, b_down,
           g_ple, w_ple_gate, w_ple):
    assert g_mix.shape[0] == 1, "single-layer step"
    nb, t, _ = x_prompt.shape
    db, ds, _ = x_sample.shape
    assert nb == SUBLANES and db % SUBLANES == 0
    w = _prep_weights(g_mix[0], w_in[0], s5_a_re[0], s5_a_im[0], s5_log_dt[0], s5_b_re[0], s5_b_im[0],
                      s5_c_re[0], s5_c_im[0], s5_d[0], w_glu[0], b_glu[0], g_q_lora[0], w_uq[0],
                      g_kv_lora[0], w_uk[0], w_uv[0], g_qk_q[0], g_qk_k[0], g_out_ssm[0], g_out_mla[0],
                      w_out[0], g_ffn[0], w_router[0], b_router[0], w_gate_up[0], b_gate_up[0],
                      w_down[0], b_down[0], g_ple[0], w_ple_gate[0], w_ple[0])

    tm = min(512, t)
    nt = t // tm
    xp = x_prompt.reshape(nb * t, D_MODEL)
    ct, st = _rope_tables(jnp.arange(t))
    u, q, k, v, c_p, kr_p, rstd_p = _proj(xp, ct, st, w, nb=nb, nt=nt, tm=tm, sample=False)
    steps = min(128, t)
    o_ssm, s_fin = _s5(u.reshape(t * nb, SSM_W), jnp.zeros((nb, 2 * SSM_S), F32), w,
                       ngroups=1, nchunks=t // steps, steps=steps)
    o_mla = _attn_p(q, k, v, nb=nb, t=t, tq=min(1024, t), tk=min(512, t))
    mixed_p = _mix(xp, o_ssm.reshape(t, nb * SSM_W), o_mla, w, nb=nb, nt=nt, tm=tm)
    sr_p = s_fin[:, :SSM_S].reshape(1, nb, SSM_G, SSM_P)
    si_p = s_fin[:, SSM_S:].reshape(1, nb, SSM_G, SSM_P)

    ns = db * ds
    ng = db // SUBLANES
    xs_ = _after(x_sample.reshape(ns, D_MODEL), mixed_p[2])
    ct_s, st_s = _rope_tables(PAST_LEN + jnp.tile(jnp.arange(ds), db))
    u_s, qf, qlat, c_s, kr_s, rstd_s = _proj(xs_, ct_s, st_s, w, nb=1, nt=1, tm=ns, sample=True)
    to_scan = lambda a: a.reshape(ng, SUBLANES, ds, -1).transpose(0, 2, 1, 3).reshape(ns, -1)
    from_scan = lambda a: a.reshape(ng, ds, SUBLANES, -1).transpose(0, 2, 1, 3).reshape(ns, -1)
    h0 = jnp.concatenate([state_ssm_re[0].reshape(db, SSM_S), state_ssm_im[0].reshape(db, SSM_S)],
                         axis=-1).astype(F32)
    o_ssm_s, s_fin_s = _s5(to_scan(u_s), h0, w, ngroups=ng, nchunks=1, steps=ds)
    pad_keys = lambda a: jnp.concatenate(
        [a, jnp.zeros((db, NEW_KEYS - ds, a.shape[-1]), a.dtype)], axis=1)
    key_major = lambda a: a.transpose(0, 2, 1)
    attn_in = (page_table,
               qlat.reshape(db, ds * N_HEADS, KV_LORA),
               qf.reshape(db, ds * N_HEADS, LANES)[:, :, NOPE_DIM:QK_DIM],
               pad_keys(c_s.reshape(db, ds, KV_LORA)),
               key_major(pad_keys(kr_s.reshape(db, ds, ROPE_DIM))),
               key_major(pad_keys(rstd_s.reshape(db, ds, N_HEADS))))
    caches = (cache_kv_latent[0], key_major(cache_k_rope[0]), key_major(cache_k_rstd[0]))

    def attn_rows(lo, hi, anchor=None):
        pt, *rest = (a[lo:hi] for a in attn_in)
        if anchor is not None:
            pt = _after(pt, anchor)
        return _attn_s(pt, *rest, *caches, dec_seq=ds)

    o_lat_a = attn_rows(0, db // 2)
    ys_p, dest_p = _moe_experts(mixed_p, w, during_dispatch=o_lat_a)
    o_lat_b = attn_rows(db // 2, db, anchor=ys_p)
    o_lat = jnp.concatenate([o_lat_a, o_lat_b], axis=0)
    o_mla_s = _latent_out(o_lat.reshape(ns, N_HEADS * KV_LORA), w["wuv_h"])

    y_p = _moe_combine(mixed_p, ys_p, dest_p, p_prompt[0].reshape(nb * t, PLE_DIM), w, tm=tm)
    mixed_s = _mix(xs_, from_scan(o_ssm_s), _after(o_mla_s, y_p), w, nb=1, nt=1, tm=ns)
    ys_s, dest_s = _moe_experts(mixed_s, w)
    y_s = _moe_combine(mixed_s, ys_s, dest_s, p_sample[0].reshape(ns, PLE_DIM), w, tm=ns)

    return (y_p.reshape(nb, t, D_MODEL), y_s.reshape(db, ds, D_MODEL),
            c_p.reshape(1, nb, t, KV_LORA), kr_p.reshape(1, nb, t, ROPE_DIM), rstd_p.reshape(1, nb, t, N_HEADS),
            sr_p, si_p,
            c_s.reshape(1, db, ds, KV_LORA), kr_s.reshape(1, db, ds, ROPE_DIM), rstd_s.reshape(1, db, ds, N_HEADS),
            s_fin_s[:, :SSM_S].reshape(1, db, SSM_G, SSM_P), s_fin_s[:, SSM_S:].reshape(1, db, SSM_G, SSM_P))
```

```python
import functools
import math

import jax
import jax.numpy as jnp
from jax import lax
from jax.experimental import pallas as pl
from jax.experimental.pallas import tpu as pltpu
from jax.experimental.pallas import tpu_sc as plsc

D_MODEL = 1024
SSM_W = 512
SSM_GC = 16
SSM_G = SSM_W // SSM_GC
SSM_P = 64
SSM_S = SSM_G * SSM_P
N_HEADS = 8
NOPE_DIM = 64
ROPE_DIM = 32
ROPE_HALF = ROPE_DIM // 2
QK_DIM = NOPE_DIM + ROPE_DIM
V_DIM = 64
Q_LORA = 384
KV_LORA = 256
ROPE_THETA = 10000.0
N_EXPERTS = 32
TOP_K = 4
D_FF = D_MODEL
SWIGLU_ALPHA = 1.702
SWIGLU_LIMIT = 7.0
PLE_DIM = 256
PAST_LEN = 16384
PAGE_SIZE = 128
EPS = 1e-6
NEG = -1e30

LANES = 128
SUBLANES = 8
VMEM_LIMIT = 56 * 1024 * 1024

HP = N_HEADS * LANES
BF16 = jnp.bfloat16
F32 = jnp.float32


def _cparams(*sem):
    return pltpu.CompilerParams(dimension_semantics=sem, vmem_limit_bytes=VMEM_LIMIT)


def _rms(x, g):
    r = lax.rsqrt(jnp.mean(x * x, axis=-1, keepdims=True) + EPS)
    return x * r * g


def _dot(a, b):
    return jnp.dot(a, b, preferred_element_type=F32)


def _dot_nt(a, b):
    return lax.dot_general(a, b, (((1,), (1,)), ((), ())), preferred_element_type=F32)


def _full(shape):
    nd = len(shape)
    return pl.BlockSpec(shape, lambda *_: (0,) * nd)


def _proj_kernel(x_ref, ct_ref, st_ref, gmix_ref, win_ref, gql_ref, wuq_ref, gkv_ref, wuk_ref,
                 w2_ref, gq_ref, gk_ref, u_ref, q_ref, k2_ref, *rest, sample):
    if sample:
        c_ref, kr_ref, rstd_ref = rest
    else:
        v2_ref, c_ref, kr_ref, rstd_ref = rest
    xn = _rms(x_ref[...], gmix_ref[...]).astype(BF16)
    z = _dot(xn, win_ref[...])
    u_ref[...] = z[:, :SSM_W]
    o = SSM_W
    cq = z[:, o:o + Q_LORA]
    o += Q_LORA
    ckv = z[:, o:o + KV_LORA]
    o += KV_LORA
    ct = ct_ref[...]
    st = st_ref[...]
    kblock = z[:, o:o + LANES] * ct + z[:, o + LANES:o + 2 * LANES] * st
    kr_ref[...] = kblock[:, NOPE_DIM:QK_DIM]

    qq = _dot(_rms(cq, gql_ref[...]).astype(BF16), wuq_ref[...])
    c = _rms(ckv, gkv_ref[...])
    c_ref[...] = c
    cb = c.astype(BF16)
    kn = _dot(cb, wuk_ref[...])
    if not sample:
        hl = lax.broadcasted_iota(jnp.int32, (1, HP), 1) & (LANES - 1)
        v2_ref[...] = (_dot(cb, w2_ref[...]) + jnp.where(hl >= V_DIM, 1.0, 0.0)).astype(BF16)

    lane = lax.broadcasted_iota(jnp.int32, ct.shape, 1)
    rstd_all = jnp.zeros(ct.shape, F32)
    gq = gq_ref[...]
    gk = gk_ref[...]
    for h in range(N_HEADS):
        sl = slice(h * LANES, (h + 1) * LANES)
        qr = qq[:, sl] * ct + qq[:, HP + h * LANES:HP + (h + 1) * LANES] * st
        qn = qr * lax.rsqrt(jnp.sum(qr * qr, axis=-1, keepdims=True) * (1.0 / QK_DIM) + EPS) * gq
        kh = kn[:, sl] + kblock
        rs = lax.rsqrt(jnp.sum(kh * kh, axis=-1, keepdims=True) * (1.0 / QK_DIM) + EPS)
        rstd_all = jnp.where(lane == h, rs, rstd_all)
        if sample:
            qf = (qn * gk).astype(BF16)
            q_ref[:, sl] = qf
            k2_ref[:, h * KV_LORA:(h + 1) * KV_LORA] = _dot(qf, w2_ref[h]).astype(BF16)
        else:
            q_ref[:, sl] = qn.astype(BF16)
            k2_ref[:, sl] = (kh * rs * gk).astype(BF16)
    rstd_ref[...] = rstd_all[:, :N_HEADS]


def _proj(x2, ct, st, w, *, nb, nt, tm, sample):
    n = x2.shape[0]
    row = lambda b, t: (b * nt + t, 0)
    rows = lambda width: pl.BlockSpec((tm, width), row)
    k2_w = N_HEADS * KV_LORA if sample else HP
    w2 = w["wukT"] if sample else w["wuv"]
    out_shape = [
        jax.ShapeDtypeStruct((n, SSM_W), F32),
        jax.ShapeDtypeStruct((n, HP), BF16),
        jax.ShapeDtypeStruct((n, k2_w), BF16),
        *([] if sample else [jax.ShapeDtypeStruct((n, HP), BF16)]),
        jax.ShapeDtypeStruct((n, KV_LORA), F32),
        jax.ShapeDtypeStruct((n, ROPE_DIM), F32),
        jax.ShapeDtypeStruct((n, N_HEADS), F32),
    ]
    out_specs = [
        rows(SSM_W), rows(HP), rows(k2_w), *([] if sample else [rows(HP)]),
        rows(KV_LORA), rows(ROPE_DIM), rows(N_HEADS),
    ]
    in_specs = [
        rows(D_MODEL),
        pl.BlockSpec((tm, LANES), lambda b, t: (t, 0)),
        pl.BlockSpec((tm, LANES), lambda b, t: (t, 0)),
        _full(w["gmix"].shape), _full(w["win"].shape), _full(w["gql"].shape), _full(w["wuq"].shape),
        _full(w["gkv"].shape), _full(w["wuk"].shape), _full(w2.shape),
        _full(w["gq_s" if sample else "gq_p"].shape), _full(w["gk"].shape),
    ]
    return pl.pallas_call(
        functools.partial(_proj_kernel, sample=sample),
        grid=(nb, nt),
        in_specs=in_specs,
        out_specs=out_specs,
        out_shape=out_shape,
        compiler_params=_cparams("parallel", "parallel"),
        name="proj_s" if sample else "proj_p",
    )(x2, ct, st, w["gmix"], w["win"], w["gql"], w["wuq"], w["gkv"], w["wuk"], w2,
      w["gq_s" if sample else "gq_p"], w["gk"])


S5_HALF = SSM_S // 2
S5_QUARTER = SSM_S // 4


def _s5_kernel(u_ref, h0_ref, are_ref, aim_ref, bw_ref, cw_ref, dsk_ref, wglu_ref, bglu_ref, gout_ref,
               o_ref, sfin_ref, xs_ref, st_ref, *il_ref, steps):
    @pl.when(pl.program_id(1) == 0)
    def _():
        st_ref[...] = h0_ref[...]

    il = il_ref[0] if il_ref else None
    lane_tiles = SSM_W // LANES
    if il is not None:
        for b in range(SUBLANES):
            for c in range(lane_tiles):
                il[c, pl.ds(b, steps, stride=SUBLANES), :] = u_ref[b, :, c * LANES:(c + 1) * LANES]
        u = jnp.concatenate([il[c] for c in range(lane_tiles)], axis=-1)
    else:
        u = u_ref[...]
    ub = u.astype(BF16)
    half_w = SSM_W // 2
    for k in range(2):
        uk = ub[:, k * half_w:(k + 1) * half_w]
        xs_ref[:, k * S5_HALF:(k + 1) * S5_HALF] = _dot(uk, bw_ref[2 * k])
        xs_ref[:, SSM_S + k * S5_HALF:SSM_S + (k + 1) * S5_HALF] = _dot(uk, bw_ref[2 * k + 1])

    for q in range(4):
        lr = slice(q * S5_QUARTER, (q + 1) * S5_QUARTER)
        li = slice(SSM_S + q * S5_QUARTER, SSM_S + (q + 1) * S5_QUARTER)
        ar = are_ref[:, lr]
        ai = aim_ref[:, lr]

        def step(t, carry):
            sr, si = carry
            r0 = pl.multiple_of(t * SUBLANES, SUBLANES)
            nr = ar * sr - ai * si + xs_ref[pl.ds(r0, SUBLANES), lr]
            ni = ar * si + ai * sr + xs_ref[pl.ds(r0, SUBLANES), li]
            xs_ref[pl.ds(r0, SUBLANES), lr] = nr
            xs_ref[pl.ds(r0, SUBLANES), li] = ni
            return nr, ni

        sr, si = lax.fori_loop(0, steps, step, (st_ref[:, lr], st_ref[:, li]), unroll=4)
        st_ref[:, lr] = sr
        st_ref[:, li] = si
    sfin_ref[...] = st_ref[...]

    ys = []
    for k in range(2):
        sre = xs_ref[:, k * S5_HALF:(k + 1) * S5_HALF].astype(BF16)
        sim = xs_ref[:, SSM_S + k * S5_HALF:SSM_S + (k + 1) * S5_HALF].astype(BF16)
        ys.append(_dot(sre, cw_ref[2 * k]) + _dot(sim, cw_ref[2 * k + 1]))
    y = jnp.concatenate(ys, axis=-1) + dsk_ref[...] * u
    g = jax.nn.gelu(y)
    out = g * jax.nn.sigmoid(_dot(g.astype(BF16), wglu_ref[...]) + bglu_ref[...])
    on = _rms(out, gout_ref[...])
    if il is not None:
        for c in range(lane_tiles):
            il[c] = on[:, c * LANES:(c + 1) * LANES]
        for b in range(SUBLANES):
            for c in range(lane_tiles):
                o_ref[b, :, c * LANES:(c + 1) * LANES] = il[c, pl.ds(b, steps, stride=SUBLANES), :].astype(BF16)
    else:
        o_ref[...] = on.astype(BF16)


def _s5(u, h0, w, *, ngroups, nchunks, steps):
    rows = steps * SUBLANES
    batch_major = u.ndim == 3
    if batch_major:
        assert ngroups == 1 and u.shape[0] == SUBLANES
        ublock = pl.BlockSpec((SUBLANES, steps, SSM_W), lambda g, t: (0, t, 0))
    else:
        ublock = pl.BlockSpec((rows, SSM_W), lambda g, t: (g * nchunks + t, 0))
    return pl.pallas_call(
        functools.partial(_s5_kernel, steps=steps),
        grid=(ngroups, nchunks),
        in_specs=[
            ublock,
            pl.BlockSpec((SUBLANES, 2 * SSM_S), lambda g, t: (g, 0)),
            _full(w["s5_are"].shape), _full(w["s5_aim"].shape), _full(w["s5_bw"].shape),
            _full(w["s5_cw"].shape), _full(w["s5_d"].shape), _full(w["wglu"].shape),
            _full(w["bglu"].shape), _full(w["gout_ssm"].shape),
        ],
        out_specs=[
            ublock,
            pl.BlockSpec((SUBLANES, 2 * SSM_S), lambda g, t: (g, 0)),
        ],
        out_shape=[
            jax.ShapeDtypeStruct(u.shape, BF16),
            jax.ShapeDtypeStruct(h0.shape, F32),
        ],
        scratch_shapes=[pltpu.VMEM((rows, 2 * SSM_S), F32), pltpu.VMEM((SUBLANES, 2 * SSM_S), F32)]
        + ([pltpu.VMEM((SSM_W // LANES, rows, LANES), F32)] if batch_major else []),
        compiler_params=_cparams("parallel", "arbitrary"),
        name="s5",
    )(u, h0, w["s5_are"], w["s5_aim"], w["s5_bw"], w["s5_cw"], w["s5_d"], w["wglu"], w["bglu"],
      w["gout_ssm"])


ATTN_P_HEADS = 4


def _attn_p_kernel(q_ref, k_ref, v_ref, o_ref, *, tq, tk):
    i = pl.program_id(2)
    nd = tq // tk
    row = lax.broadcasted_iota(jnp.int32, (tk, tk), 0)
    col = lax.broadcasted_iota(jnp.int32, (tk, tk), 1)

    def block(j, carry, diag):
        k0 = pl.multiple_of(j * tk, tk)
        k2 = k_ref[pl.ds(k0, tk), :]
        v2 = v_ref[pl.ds(k0, tk), :]
        r0 = 0 if diag is None else diag * tk
        out = []
        for hh in range(ATTN_P_HEADS):
            m, acc = carry[hh]
            s = _dot_nt(q_ref[r0:, hh * LANES:(hh + 1) * LANES], k2[:, hh * LANES:(hh + 1) * LANES])
            if diag is not None:
                tri = jnp.where(col <= row, s[:tk], NEG)
                s = tri if r0 + tk == tq else jnp.concatenate([tri, s[tk:]], axis=0)
            m_new = jnp.maximum(m[r0:], jnp.max(s, axis=-1, keepdims=True))
            p = jnp.exp2(s - m_new).astype(BF16)
            acc_new = jnp.exp2(m[r0:] - m_new) * acc[r0:] + _dot(p, v2[:, hh * LANES:(hh + 1) * LANES])
            if r0:
                m_new = jnp.concatenate([m[:r0], m_new], axis=0)
                acc_new = jnp.concatenate([acc[:r0], acc_new], axis=0)
            out.append((m_new, acc_new))
        return tuple(out)

    init = tuple((jnp.full((tq, 1), NEG, F32), jnp.zeros((tq, LANES), F32)) for _ in range(ATTN_P_HEADS))
    carry = lax.fori_loop(0, i * nd, lambda j, c: block(j, c, None), init)
    for d in range(nd):
        carry = block(i * nd + d, carry, d)
    o = [acc / pltpu.roll(acc, V_DIM, 1) for _, acc in carry]
    lane = lax.broadcasted_iota(jnp.int32, (tq, LANES), 1)
    for pr in range(ATTN_P_HEADS // 2):
        o_ref[:, pr * LANES:(pr + 1) * LANES] = jnp.where(
            lane < V_DIM, o[2 * pr], pltpu.roll(o[2 * pr + 1], V_DIM, 1))


def _attn_p(q, k, v, *, nb, t, tq, tk):
    nq = t // tq
    hw = ATTN_P_HEADS * LANES
    return pl.pallas_call(
        functools.partial(_attn_p_kernel, tq=tq, tk=tk),
        grid=(nb, N_HEADS // ATTN_P_HEADS, nq),
        in_specs=[
            pl.BlockSpec((tq, hw), lambda b, h, i: (b * nq + i, h)),
            pl.BlockSpec((t, hw), lambda b, h, i: (b, h)),
            pl.BlockSpec((t, hw), lambda b, h, i: (b, h)),
        ],
        out_specs=pl.BlockSpec((tq, ATTN_P_HEADS * V_DIM), lambda b, h, i: (b * nq + i, h)),
        out_shape=jax.ShapeDtypeStruct((nb * t, N_HEADS * V_DIM), F32),
        compiler_params=_cparams("parallel", "parallel", "arbitrary"),
        name="attn_p",
    )(q, k, v)


ATTN_S_PAGES = 32
NEW_KEYS = 16


def _attn_s_kernel(pt_ref, ql_ref, qr_ref, cn_ref, krn_ref, rnt_ref, cc_hbm, ckr_hbm, cr_hbm,
                   o_ref, cbuf, krbuf, rbuf, cb_ref, sc_ref, sems, *, n_pages, dec_seq):
    b = pl.program_id(0)
    nb = pl.num_programs(0)
    bp = min(ATTN_S_PAGES, n_pages)
    nblk = n_pages // bp
    keys = bp * PAGE_SIZE
    nrow = dec_seq * N_HEADS
    scale = 1.0 / math.sqrt(QK_DIM)
    slot = b % 2

    def start_block(bb, i, dst):
        for p in range(bp):
            pg = i * bp + p
            page = pt_ref[bb * n_pages + pg]
            lanes = pl.ds(pl.multiple_of(pg * PAGE_SIZE, PAGE_SIZE), PAGE_SIZE)
            pltpu.make_async_copy(cc_hbm.at[page], cbuf.at[dst, pg], sems.at[dst, 0]).start()
            pltpu.make_async_copy(ckr_hbm.at[page], krbuf.at[dst, :, lanes], sems.at[dst, 1]).start()
            pltpu.make_async_copy(cr_hbm.at[page], rbuf.at[dst, :, lanes], sems.at[dst, 2]).start()

    @pl.when(b == 0)
    def _():
        lax.fori_loop(0, nblk, lambda i, c: (start_block(0, i, 0), c)[1], 0)

    pltpu.make_async_copy(cbuf.at[slot], cbuf.at[slot], sems.at[slot, 0]).wait()
    pltpu.make_async_copy(krbuf.at[slot], krbuf.at[slot], sems.at[slot, 1]).wait()
    pltpu.make_async_copy(rbuf.at[slot], rbuf.at[slot], sems.at[slot, 2]).wait()

    ql = ql_ref[0]
    qr = qr_ref[0]

    def tile_heads(r_t):
        return jnp.concatenate([r_t] * dec_seq, axis=0)

    cnb = cn_ref[0].astype(BF16)
    s_n = (_dot_nt(ql, cnb) + _dot(qr, krn_ref[0].astype(BF16))) * tile_heads(rnt_ref[0]) * scale
    qs = lax.broadcasted_iota(jnp.int32, (nrow, NEW_KEYS), 0) // N_HEADS
    kj = lax.broadcasted_iota(jnp.int32, (nrow, NEW_KEYS), 1)
    s_n = jnp.where(kj <= qs, s_n, NEG)

    def score_block(i, m):
        @pl.when(b + 1 < nb)
        def _():
            start_block(b + 1, i, 1 - slot)

        k0 = pl.multiple_of(i * keys, keys)
        cb = cbuf[slot, pl.ds(i * bp, bp)].reshape(keys, KV_LORA).astype(BF16)
        cb_ref[pl.ds(k0, keys), :] = cb
        raw = _dot_nt(ql, cb) + _dot(qr, krbuf[slot, :, pl.ds(k0, keys)].astype(BF16))
        s = raw * tile_heads(rbuf[slot, :, pl.ds(k0, keys)]) * scale
        sc_ref[:, pl.ds(k0, keys)] = s
        return jnp.maximum(m, jnp.max(s, axis=-1, keepdims=True))

    m = lax.fori_loop(0, nblk, score_block, jnp.max(s_n, axis=-1, keepdims=True))
    p_n = jnp.exp(s_n - m)

    def value_block(i, carry):
        l, acc = carry
        k0 = pl.multiple_of(i * keys, keys)
        p = jnp.exp(sc_ref[:, pl.ds(k0, keys)] - m)
        return (l + jnp.sum(p, axis=-1, keepdims=True),
                acc + _dot(p.astype(BF16), cb_ref[pl.ds(k0, keys), :]))

    l, acc = lax.fori_loop(0, nblk, value_block,
                           (jnp.sum(p_n, axis=-1, keepdims=True), _dot(p_n.astype(BF16), cnb)))
    o_ref[0] = acc / l


def _attn_s(page_table, ql, qr, cn, krn, rnt, cache_c, cache_kr, cache_r, *, dec_seq):
    nb, n_pages = page_table.shape
    nrow = dec_seq * N_HEADS
    b3 = lambda b, pt: (b, 0, 0)
    return pl.pallas_call(
        functools.partial(_attn_s_kernel, n_pages=n_pages, dec_seq=dec_seq),
        grid_spec=pltpu.PrefetchScalarGridSpec(
            num_scalar_prefetch=1,
            grid=(nb,),
            in_specs=[
                pl.BlockSpec((1, nrow, KV_LORA), b3),
                pl.BlockSpec((1, nrow, ROPE_DIM), b3),
                pl.BlockSpec((1, NEW_KEYS, KV_LORA), b3),
                pl.BlockSpec((1, ROPE_DIM, NEW_KEYS), b3),
                pl.BlockSpec((1, N_HEADS, NEW_KEYS), b3),
                pl.BlockSpec(memory_space=pl.ANY),
                pl.BlockSpec(memory_space=pl.ANY),
                pl.BlockSpec(memory_space=pl.ANY),
            ],
            out_specs=pl.BlockSpec((1, nrow, KV_LORA), b3),
            scratch_shapes=[
                pltpu.VMEM((2, n_pages, PAGE_SIZE, KV_LORA), F32),
                pltpu.VMEM((2, ROPE_DIM, n_pages * PAGE_SIZE), F32),
                pltpu.VMEM((2, N_HEADS, n_pages * PAGE_SIZE), F32),
                pltpu.VMEM((n_pages * PAGE_SIZE, KV_LORA), BF16),
                pltpu.VMEM((nrow, n_pages * PAGE_SIZE), F32),
                pltpu.SemaphoreType.DMA((2, 3)),
            ],
        ),
        out_shape=jax.ShapeDtypeStruct((nb, nrow, KV_LORA), F32),
        compiler_params=_cparams("arbitrary"),
        name="attn_s",
    )(page_table.reshape(-1), ql, qr, cn, krn, rnt, cache_c, cache_kr, cache_r)


def _latent_out_kernel(ol_ref, wuv_ref, o_ref):
    for h in range(N_HEADS):
        o_ref[:, h * V_DIM:(h + 1) * V_DIM] = _dot(
            ol_ref[:, h * KV_LORA:(h + 1) * KV_LORA].astype(BF16), wuv_ref[h])


def _latent_out(ol, wuv_h):
    n = ol.shape[0]
    return pl.pallas_call(
        _latent_out_kernel,
        grid=(1,),
        in_specs=[_full(ol.shape), _full(wuv_h.shape)],
        out_specs=_full((n, N_HEADS * V_DIM)),
        out_shape=jax.ShapeDtypeStruct((n, N_HEADS * V_DIM), F32),
        compiler_params=_cparams("arbitrary"),
        name="latent_out",
    )(ol, wuv_h)


ROUTE_W = 16


def _mix_kernel(h_ref, os_ref, om_ref, gom_ref, wout_ref, gffn_ref, wr_ref, br_ref,
                h1_ref, xn_ref, route_ref, cnt_ref):
    @pl.when((pl.program_id(0) == 0) & (pl.program_id(1) == 0))
    def _():
        cnt_ref[...] = jnp.zeros_like(cnt_ref)

    tm = h_ref.shape[0]
    omn = _rms(om_ref[...], gom_ref[...]).astype(BF16)
    h1 = h_ref[...] + _dot(os_ref[...], wout_ref[:SSM_W, :]) + _dot(omn, wout_ref[SSM_W:, :])
    h1_ref[...] = h1
    xn = _rms(h1, gffn_ref[...])
    xn_ref[...] = xn
    work = _dot(xn.astype(BF16), wr_ref[...]) + br_ref[...]

    lane = lax.broadcasted_iota(jnp.int32, work.shape, 1).astype(F32)
    route = jnp.zeros(work.shape, F32)
    sels, vals = [], []
    for k in range(TOP_K):
        mk = jnp.max(work, axis=-1, keepdims=True)
        ik = jnp.min(jnp.where(work == mk, lane, float(LANES)), axis=-1, keepdims=True)
        sel = lane == ik
        work = jnp.where(sel, -jnp.inf, work)
        route = jnp.where(lane == float(k), ik, route)
        sels.append(sel)
        vals.append(mk)
    es = [jnp.exp(v - vals[0]) for v in vals]
    den = es[0] + es[1] + es[2] + es[3]
    onehot = jnp.zeros(work.shape, F32)
    for k in range(TOP_K):
        route = jnp.where(lane == float(TOP_K + k), es[k] / den, route)
        onehot = jnp.where(sels[k], 1.0, onehot)

    r_i = lax.broadcasted_iota(jnp.int32, (tm, tm), 0)
    c_i = lax.broadcasted_iota(jnp.int32, (tm, tm), 1)
    tri = jnp.where(c_i < r_i, 1.0, 0.0).astype(BF16)
    cum = _dot(tri, onehot.astype(BF16)) + cnt_ref[...]
    for k in range(TOP_K):
        rank = jnp.sum(jnp.where(sels[k], cum, 0.0), axis=-1, keepdims=True)
        route = jnp.where(lane == float(2 * TOP_K + k), rank, route)
    route_ref[...] = route[:, :ROUTE_W]
    cnt_ref[...] = cnt_ref[...] + jnp.sum(onehot, axis=0, keepdims=True)


def _mix(h2, o_ssm, o_mla, w, *, nb, nt, tm):
    n = h2.shape[0]
    row = lambda b, t: (b * nt + t, 0)
    return pl.pallas_call(
        _mix_kernel,
        grid=(nb, nt),
        in_specs=[
            pl.BlockSpec((tm, D_MODEL), row),
            pl.BlockSpec((tm, SSM_W), row),
            pl.BlockSpec((tm, N_HEADS * V_DIM), row),
            _full(w["gout_mla"].shape), _full(w["wout"].shape), _full(w["gffn"].shape),
            _full(w["wr"].shape), _full(w["br"].shape),
        ],
        out_specs=[
            pl.BlockSpec((tm, D_MODEL), row),
            pl.BlockSpec((tm, D_MODEL), row),
            pl.BlockSpec((tm, ROUTE_W), row),
            _full((1, LANES)),
        ],
        out_shape=[
            jax.ShapeDtypeStruct((n, D_MODEL), F32),
            jax.ShapeDtypeStruct((n, D_MODEL), F32),
            jax.ShapeDtypeStruct((n, ROUTE_W), F32),
            jax.ShapeDtypeStruct((1, LANES), F32),
        ],
        compiler_params=_cparams("arbitrary", "arbitrary"),
        name="mix",
    )(h2, o_ssm, o_mla, w["gout_mla"], w["wout"], w["gffn"], w["wr"], w["br"])


def _experts_kernel(tile_ref, e_ref, first_ref, valid_ref, lo_ref, hi_ref, newe_ref,
                    x_ref, wgu_ref, bgu_ref, wd_ref, bd_ref, o_ref, wgu_bf, wd_bf):
    i = pl.program_id(0)

    @pl.when(newe_ref[i] == 1)
    def _():
        wgu_bf[...] = wgu_ref[0].astype(BF16)
        wd_bf[...] = wd_ref[0].astype(BF16)

    @pl.when(valid_ref[i] == 1)
    def _():
        bm = x_ref.shape[0]
        hdn = _dot(x_ref[...].astype(BF16), wgu_bf[...]) + bgu_ref[0]
        x_glu = jnp.minimum(hdn[:, :D_FF], SWIGLU_LIMIT)
        x_lin = jnp.clip(hdn[:, D_FF:], -SWIGLU_LIMIT, SWIGLU_LIMIT)
        act = x_glu * jax.nn.sigmoid(SWIGLU_ALPHA * x_glu) * (x_lin + 1.0)
        y = _dot(act.astype(BF16), wd_bf[...]) + bd_ref[0]
        rows = tile_ref[i] * bm + lax.broadcasted_iota(jnp.int32, (bm, 1), 0)
        mine = (rows >= lo_ref[i]) & (rows < hi_ref[i])

        @pl.when(first_ref[i] == 1)
        def _():
            o_ref[...] = jnp.where(mine, y, 0.0)

        @pl.when(first_ref[i] == 0)
        def _():
            o_ref[...] = jnp.where(mine, y, o_ref[...])


def _experts(xs, items, w, *, bm):
    r = xs.shape[0]
    n_items = items[0].shape[0]
    xmap = lambda i, tile, *_: (tile[i], 0)
    emap = lambda i, tile, e, *_: (e[i], 0, 0)
    return pl.pallas_call(
        _experts_kernel,
        grid_spec=pltpu.PrefetchScalarGridSpec(
            num_scalar_prefetch=len(items),
            grid=(n_items,),
            in_specs=[
                pl.BlockSpec((bm, D_MODEL), xmap),
                pl.BlockSpec((1, D_MODEL, 2 * D_FF), emap),
                pl.BlockSpec((1, 1, 2 * D_FF), emap),
                pl.BlockSpec((1, D_FF, D_MODEL), emap),
                pl.BlockSpec((1, 1, D_MODEL), emap),
            ],
            out_specs=pl.BlockSpec((bm, D_MODEL), xmap),
            scratch_shapes=[pltpu.VMEM((D_MODEL, 2 * D_FF), BF16), pltpu.VMEM((D_FF, D_MODEL), BF16)],
        ),
        out_shape=jax.ShapeDtypeStruct((r, D_MODEL), F32),
        compiler_params=_cparams("arbitrary"),
        name="experts",
    )(*items, xs, w["wgu"], w["bgu"], w["wd"], w["bd"])


def _expert_items(counts, n_rows, bm):
    n_tiles = n_rows // bm
    n_items = n_tiles + N_EXPERTS - 1
    ends = jnp.cumsum(counts)
    starts = ends - counts
    first_tile = starts // bm
    per_e = jnp.where(counts > 0, (ends - 1) // bm - first_tile + 1, 0)
    item_end = jnp.cumsum(per_e)
    i = jnp.arange(n_items, dtype=jnp.int32)
    valid = i < item_end[-1]
    e = jnp.sum(jnp.minimum(i, item_end[-1] - 1)[:, None] >= item_end[None, :], axis=1).astype(jnp.int32)
    of_e = e[:, None] == jnp.arange(N_EXPERTS, dtype=jnp.int32)[None, :]
    pick = lambda table: jnp.sum(jnp.where(of_e, table[None, :], 0), axis=1).astype(jnp.int32)
    tile = jnp.where(valid, pick(first_tile) + i - pick(item_end - per_e), n_tiles - 1).astype(jnp.int32)
    one = jnp.ones((1,), jnp.int32)
    first = jnp.concatenate([one, (tile[1:] != tile[:-1]).astype(jnp.int32)])
    new_e = jnp.concatenate([one, (e[1:] != e[:-1]).astype(jnp.int32)])
    return (tile, e, first, valid.astype(jnp.int32), pick(starts), pick(ends), new_e)


def _ple_kernel(h1_ref, yg_ref, route_ref, p_ref, gple_ref, wg_ref, wp_ref, o_ref):
    h2 = h1_ref[...]
    route = route_ref[...]
    w = SC_GATHER_TOKENS
    for k in range(TOP_K):
        yk = yg_ref[:, k * w:(k + 1) * w, :].reshape(h2.shape)
        h2 = h2 + yk * route[:, TOP_K + k:TOP_K + k + 1]
    gate = jax.nn.sigmoid(_dot(_rms(h2, gple_ref[...]).astype(BF16), wg_ref[...]))
    o_ref[...] = h2 + _dot(p_ref[...].astype(BF16), wp_ref[...]) * gate


def _ple(h1, yg, route, p, w, *, tm):
    n = h1.shape[0]
    row = lambda t: (t, 0)
    return pl.pallas_call(
        _ple_kernel,
        grid=(n // tm,),
        in_specs=[
            pl.BlockSpec((tm, D_MODEL), row),
            pl.BlockSpec((tm // SC_GATHER_TOKENS, TOP_K * SC_GATHER_TOKENS, D_MODEL), lambda t: (t, 0, 0)),
            pl.BlockSpec((tm, ROUTE_W), row),
            pl.BlockSpec((tm, PLE_DIM), row),
            _full(w["gple"].shape), _full(w["wpg"].shape), _full(w["wp"].shape),
        ],
        out_specs=pl.BlockSpec((tm, D_MODEL), row),
        out_shape=jax.ShapeDtypeStruct((n, D_MODEL), F32),
        compiler_params=_cparams("parallel"),
        name="ple",
    )(h1, yg, route, p, w["gple"], w["wpg"], w["wp"])


SC_INDEX_ROW = 128
SC_SCATTER_TOKENS = SC_INDEX_ROW // TOP_K
SC_GATHER_TOKENS = 8


def _sc_mesh():
    return plsc.VectorSubcoreMesh(core_axis_name="c", subcore_axis_name="s")


def _window_index(dest, w):
    n = dest.shape[0]
    idx = dest.reshape(n // w, w, TOP_K).transpose(0, 2, 1).reshape(n // w, TOP_K * w)
    if TOP_K * w < SC_INDEX_ROW:
        idx = jnp.concatenate([idx, jnp.zeros((n // w, SC_INDEX_ROW - TOP_K * w), jnp.int32)], axis=-1)
    return idx


def _sc_dispatch(xn, idx):
    n = xn.shape[0]
    w = SC_SCATTER_TOKENS

    @pl.kernel(out_type=jax.ShapeDtypeStruct((n * TOP_K, D_MODEL), xn.dtype),
               mesh=_sc_mesh(), scratch_types=[pltpu.SemaphoreType.DMA])
    def scatter(x_hbm, i_hbm, o_hbm, sem):
        def body(x_vmem, i_vmem):
            copies = [pltpu.make_async_copy(x_vmem, o_hbm.at[i_vmem.at[0, pl.ds(k * w, w)]], sem)
                      for k in range(TOP_K)]
            for cp in copies:
                cp.start()
            for cp in copies:
                cp.wait()

        pltpu.emit_pipeline(
            body,
            grid=(n // w,),
            in_specs=[pl.BlockSpec((w, D_MODEL), lambda i: (i, 0)),
                      pl.BlockSpec((1, SC_INDEX_ROW), lambda i: (i, 0))],
            out_specs=[],
            core_axis_name=("c", "s"),
            dimension_semantics=(pltpu.PARALLEL,),
        )(x_hbm, i_hbm)

    return scatter(xn, idx)


def _sc_combine(ys, idx):
    w = SC_GATHER_TOKENS
    n = idx.shape[0] * w
    rows = TOP_K * w

    @pl.kernel(out_type=jax.ShapeDtypeStruct((n // w, rows, D_MODEL), ys.dtype),
               mesh=_sc_mesh(), scratch_types=[])
    def gather(y_hbm, i_hbm, o_hbm):
        def body(i_vmem, o_vmem):
            pltpu.sync_copy(y_hbm.at[i_vmem.at[0, pl.ds(0, rows)]], o_vmem.at[0])

        pltpu.emit_pipeline(
            body,
            grid=(n // w,),
            in_specs=[pl.BlockSpec((1, SC_INDEX_ROW), lambda i: (i, 0))],
            out_specs=[pl.BlockSpec((1, rows, D_MODEL), lambda i: (i, 0, 0))],
            core_axis_name=("c", "s"),
            dimension_semantics=(pltpu.PARALLEL,),
        )(i_hbm, o_hbm)

    return gather(ys, idx)


def _rope_tables(pos):
    inv = ROPE_THETA ** (-jnp.arange(0, ROPE_DIM, 2, dtype=F32) / ROPE_DIM)
    ang = pos.astype(F32)[:, None] * inv[None, :]
    cos, sin = jnp.cos(ang), jnp.sin(ang)
    n = pos.shape[0]
    pad = jnp.zeros((n, LANES - QK_DIM), F32)
    ct = jnp.concatenate([jnp.ones((n, NOPE_DIM), F32), cos, cos, pad], axis=-1)
    st = jnp.concatenate([jnp.zeros((n, NOPE_DIM), F32), sin, sin, pad], axis=-1)
    return ct, st


def _rope_partner(w_rope):
    return jnp.concatenate([-w_rope[..., ROPE_HALF:], w_rope[..., :ROPE_HALF]], axis=-1)


def _head_tile(nope, rope):
    pad = jnp.zeros(rope.shape[:-1] + (LANES - QK_DIM,), rope.dtype)
    return jnp.concatenate([nope, rope, pad], axis=-1)


def _block_diag(m):
    g, a, b = m.shape
    hg = g // 2
    eye = jnp.eye(hg, dtype=m.dtype)
    mh = m.reshape(2, hg, a, b)
    return jnp.einsum("kgab,gh->kgahb", mh, eye).reshape(2, hg * a, hg * b)


def _prep_weights(g_mix, w_in, s5_a_re, s5_a_im, s5_log_dt, s5_b_re, s5_b_im, s5_c_re, s5_c_im, s5_d,
                  w_glu, b_glu, g_q_lora, w_uq, g_kv_lora, w_uk, w_uv, g_qk_q, g_qk_k, g_out_ssm,
                  g_out_mla, w_out, g_ffn, w_router, b_router, w_gate_up, b_gate_up, w_down, b_down,
                  g_ple, w_ple_gate, w_ple):
    w = {}
    row = lambda v: v.reshape(1, -1).astype(F32)
    o = SSM_W + Q_LORA + KV_LORA
    w_kr = w_in[:, o:o + ROPE_DIM]
    zeros_n = jnp.zeros((D_MODEL, NOPE_DIM), F32)
    w["win"] = jnp.concatenate(
        [w_in[:, :o], _head_tile(zeros_n, w_kr), _head_tile(zeros_n, _rope_partner(w_kr))], axis=-1).astype(BF16)
    w["gmix"] = row(g_mix)
    w["gql"] = row(g_q_lora)
    w["gkv"] = row(g_kv_lora)

    wq = w_uq.reshape(Q_LORA, N_HEADS, QK_DIM)
    q1 = _head_tile(wq[..., :NOPE_DIM], wq[..., NOPE_DIM:])
    q2 = _head_tile(jnp.zeros_like(wq[..., :NOPE_DIM]), _rope_partner(wq[..., NOPE_DIM:]))
    w["wuq"] = jnp.concatenate([q1.reshape(Q_LORA, HP), q2.reshape(Q_LORA, HP)], axis=-1).astype(BF16)
    w["wuk"] = jnp.concatenate(
        [w_uk, jnp.zeros((KV_LORA, N_HEADS, LANES - NOPE_DIM), F32)], axis=-1).reshape(KV_LORA, HP).astype(BF16)
    w["wuv"] = jnp.concatenate(
        [w_uv, jnp.zeros((KV_LORA, N_HEADS, LANES - V_DIM), F32)], axis=-1).reshape(KV_LORA, HP).astype(BF16)
    w["wuv_h"] = w_uv.transpose(1, 0, 2).astype(BF16)
    wukT = w_uk.transpose(1, 2, 0)
    w["wukT"] = jnp.concatenate(
        [wukT, jnp.zeros((N_HEADS, LANES - NOPE_DIM, KV_LORA), F32)], axis=1).astype(BF16)
    gpad = jnp.zeros((LANES - QK_DIM,), F32)
    w["gq_p"] = row(jnp.concatenate([g_qk_q * (math.log2(math.e) / math.sqrt(QK_DIM)), gpad]))
    w["gq_s"] = row(jnp.concatenate([g_qk_q, gpad]))
    w["gk"] = row(jnp.concatenate([g_qk_k, gpad]))

    dt = jnp.exp(s5_log_dt.astype(F32))[:, None]
    ar, ai = s5_a_re.astype(F32), s5_a_im.astype(F32)
    mag = jnp.exp(dt * ar)
    abar_re, abar_im = mag * jnp.cos(dt * ai), mag * jnp.sin(dt * ai)
    den = ar * ar + ai * ai
    nr, ni = abar_re - 1.0, abar_im
    coef_re = (nr * ar + ni * ai) / den
    coef_im = (ni * ar - nr * ai) / den
    br, bi = s5_b_re.astype(F32), s5_b_im.astype(F32)
    bbar_re = coef_re[..., None] * br - coef_im[..., None] * bi
    bbar_im = coef_re[..., None] * bi + coef_im[..., None] * br
    bre = _block_diag(bbar_re.transpose(0, 2, 1))
    bim = _block_diag(bbar_im.transpose(0, 2, 1))
    w["s5_bw"] = jnp.stack([bre[0], bim[0], bre[1], bim[1]]).astype(BF16)
    cre = _block_diag(s5_c_re.astype(F32).transpose(0, 2, 1))
    cim = _block_diag(-s5_c_im.astype(F32).transpose(0, 2, 1))
    w["s5_cw"] = jnp.stack([cre[0], cim[0], cre[1], cim[1]]).astype(BF16)
    w["s5_are"] = jnp.broadcast_to(abar_re.reshape(1, SSM_S), (SUBLANES, SSM_S))
    w["s5_aim"] = jnp.broadcast_to(abar_im.reshape(1, SSM_S), (SUBLANES, SSM_S))
    w["s5_d"] = row(s5_d)
    w["wglu"] = w_glu.astype(BF16)
    w["bglu"] = row(b_glu)
    w["gout_ssm"] = row(g_out_ssm)
    w["gout_mla"] = row(g_out_mla)
    w["wout"] = w_out.astype(BF16)
    w["gffn"] = row(g_ffn)
    w["wr"] = jnp.concatenate([w_router, jnp.zeros((D_MODEL, LANES - N_EXPERTS), F32)], axis=-1).astype(BF16)
    w["br"] = row(jnp.concatenate([b_router.astype(F32), jnp.full((LANES - N_EXPERTS,), NEG, F32)]))
    w["wgu"] = w_gate_up
    w["bgu"] = b_gate_up.reshape(N_EXPERTS, 1, 2 * D_FF).astype(F32)
    w["wd"] = w_down
    w["bd"] = b_down.reshape(N_EXPERTS, 1, D_MODEL).astype(F32)
    w["gple"] = row(g_ple)
    w["wpg"] = w_ple_gate.astype(BF16)
    w["wp"] = w_ple.astype(BF16)
    return w


EXPERT_ROWS = 512


def _moe_experts(mixed, w, during_dispatch=()):
    h1, xn, route, cnt = mixed
    n = h1.shape[0]
    top_i = route[:, :TOP_K].astype(jnp.int32)
    rank = route[:, 2 * TOP_K:3 * TOP_K].astype(jnp.int32)
    counts = cnt[0, :N_EXPERTS].astype(jnp.int32)
    starts = jnp.cumsum(counts) - counts
    dest = starts[top_i] + rank
    gather_idx = _window_index(dest, SC_GATHER_TOKENS)
    xs = _after(_sc_dispatch(xn, _window_index(dest, SC_SCATTER_TOKENS)), (gather_idx, during_dispatch))
    bm = min(EXPERT_ROWS, max(LANES, n * TOP_K // N_EXPERTS))
    return _experts(xs, _expert_items(counts, n * TOP_K, bm), w, bm=bm), gather_idx


def _moe_combine(mixed, ys, gather_idx, p2, w, *, tm):
    h1, _, route, _ = mixed
    return _ple(h1, _sc_combine(ys, gather_idx), route, p2, w, tm=tm)


def _after(x, anchor):
    return lax.optimization_barrier((x, anchor))[0]


def kernel(x_prompt, x_sample, cache_kv_latent, cache_k_rope, cache_k_rstd, state_ssm_re, state_ssm_im,
           page_table, p_prompt, p_sample, g_mix, w_in, s5_a_re, s5_a_im, s5_log_dt, s5_b_re, s5_b_im,
           s5_c_re, s5_c_im, s5_d, w_glu, b_glu, g_q_lora, w_uq, g_kv_lora, w_uk, w_uv, g_qk_q, g_qk_k,
           g_out_ssm, g_out_mla, w_out, g_ffn, w_router, b_router, w_gate_up, b_gate_up, w_down, b_down,
           g_ple, w_ple_gate, w_ple):
    assert g_mix.shape[0] == 1, "single-layer step"
    nb, t, _ = x_prompt.shape
    db, ds, _ = x_sample.shape
    assert nb == SUBLANES and db % SUBLANES == 0
    w = _prep_weights(g_mix[0], w_in[0], s5_a_re[0], s5_a_im[0], s5_log_dt[0], s5_b_re[0], s5_b_im[0],
                      s5_c_re[0], s5_c_im[0], s5_d[0], w_glu[0], b_glu[0], g_q_lora[0], w_uq[0],
                      g_kv_lora[0], w_uk[0], w_uv[0], g_qk_q[0], g_qk_k[0], g_out_ssm[0], g_out_mla[0],
                      w_out[0], g_ffn[0], w_router[0], b_router[0], w_gate_up[0], b_gate_up[0],
                      w_down[0], b_down[0], g_ple[0], w_ple_gate[0], w_ple[0])

    tm = min(512, t)
    nt = t // tm
    xp = x_prompt.reshape(nb * t, D_MODEL)
    ct, st = _rope_tables(jnp.arange(t))
    u, q, k, v, c_p, kr_p, rstd_p = _proj(xp, ct, st, w, nb=nb, nt=nt, tm=tm, sample=False)
    steps = min(128, t)
    o_ssm, s_fin = _s5(u.reshape(nb, t, SSM_W), jnp.zeros((nb, 2 * SSM_S), F32), w,
                       ngroups=1, nchunks=t // steps, steps=steps)
    o_mla = _attn_p(q, k, v, nb=nb, t=t, tq=min(1024, t), tk=min(512, t))
    mixed_p = _mix(xp, o_ssm.reshape(nb * t, SSM_W), o_mla, w, nb=nb, nt=nt, tm=tm)
    sr_p = s_fin[:, :SSM_S].reshape(1, nb, SSM_G, SSM_P)
    si_p = s_fin[:, SSM_S:].reshape(1, nb, SSM_G, SSM_P)

    ns = db * ds
    ng = db // SUBLANES
    xs_ = _after(x_sample.reshape(ns, D_MODEL), mixed_p[2])
    ct_s, st_s = _rope_tables(PAST_LEN + jnp.tile(jnp.arange(ds), db))
    u_s, qf, qlat, c_s, kr_s, rstd_s = _proj(xs_, ct_s, st_s, w, nb=1, nt=1, tm=ns, sample=True)
    to_scan = lambda a: a.reshape(ng, SUBLANES, ds, -1).transpose(0, 2, 1, 3).reshape(ns, -1)
    from_scan = lambda a: a.reshape(ng, ds, SUBLANES, -1).transpose(0, 2, 1, 3).reshape(ns, -1)
    h0 = jnp.concatenate([state_ssm_re[0].reshape(db, SSM_S), state_ssm_im[0].reshape(db, SSM_S)],
                         axis=-1).astype(F32)
    o_ssm_s, s_fin_s = _s5(to_scan(u_s), h0, w, ngroups=ng, nchunks=1, steps=ds)
    pad_keys = lambda a: jnp.concatenate(
        [a, jnp.zeros((db, NEW_KEYS - ds, a.shape[-1]), a.dtype)], axis=1)
    key_major = lambda a: a.transpose(0, 2, 1)
    attn_in = (page_table,
               qlat.reshape(db, ds * N_HEADS, KV_LORA),
               qf.reshape(db, ds * N_HEADS, LANES)[:, :, NOPE_DIM:QK_DIM],
               pad_keys(c_s.reshape(db, ds, KV_LORA)),
               key_major(pad_keys(kr_s.reshape(db, ds, ROPE_DIM))),
               key_major(pad_keys(rstd_s.reshape(db, ds, N_HEADS))))
    caches = (cache_kv_latent[0], key_major(cache_k_rope[0]), key_major(cache_k_rstd[0]))

    def attn_rows(lo, hi, anchor=None):
        pt, *rest = (a[lo:hi] for a in attn_in)
        if anchor is not None:
            pt = _after(pt, anchor)
        return _attn_s(pt, *rest, *caches, dec_seq=ds)

    o_lat_a = attn_rows(0, db // 2)
    ys_p, dest_p = _moe_experts(mixed_p, w, during_dispatch=o_lat_a)
    o_lat_b = attn_rows(db // 2, db, anchor=ys_p)
    o_lat = jnp.concatenate([o_lat_a, o_lat_b], axis=0)
    o_mla_s = _latent_out(o_lat.reshape(ns, N_HEADS * KV_LORA), w["wuv_h"])

    y_p = _moe_combine(mixed_p, ys_p, dest_p, p_prompt[0].reshape(nb * t, PLE_DIM), w, tm=tm)
    mixed_s = _mix(xs_, from_scan(o_ssm_s), _after(o_mla_s, y_p), w, nb=1, nt=1, tm=ns)
    ys_s, dest_s = _moe_experts(mixed_s, w)
    y_s = _moe_combine(mixed_s, ys_s, dest_s, p_sample[0].reshape(ns, PLE_DIM), w, tm=ns)

    return (y_p.reshape(nb, t, D_MODEL), y_s.reshape(db, ds, D_MODEL),
            c_p.reshape(1, nb, t, KV_LORA), kr_p.reshape(1, nb, t, ROPE_DIM), rstd_p.reshape(1, nb, t, N_HEADS),
            sr_p, si_p,
            c_s.reshape(1, db, ds, KV_LORA), kr_s.reshape(1, db, ds, ROPE_DIM), rstd_s.reshape(1, db, ds, N_HEADS),
            s_fin_s[:, :SSM_S].reshape(1, db, SSM_G, SSM_P), s_fin_s[:, SSM_S:].reshape(1, db, SSM_G, SSM_P))
```

```python
import functools
import math

import jax
import jax.numpy as jnp
from jax import lax
from jax.experimental import pallas as pl
from jax.experimental.pallas import tpu as pltpu
from jax.experimental.pallas import tpu_sc as plsc

D_MODEL = 1024
SSM_W = 512
SSM_GC = 16
SSM_G = SSM_W // SSM_GC
SSM_P = 64
SSM_S = SSM_G * SSM_P
N_HEADS = 8
NOPE_DIM = 64
ROPE_DIM = 32
ROPE_HALF = ROPE_DIM // 2
QK_DIM = NOPE_DIM + ROPE_DIM
V_DIM = 64
Q_LORA = 384
KV_LORA = 256
ROPE_THETA = 10000.0
N_EXPERTS = 32
TOP_K = 4
D_FF = D_MODEL
SWIGLU_ALPHA = 1.702
SWIGLU_LIMIT = 7.0
PLE_DIM = 256
PAST_LEN = 16384
PAGE_SIZE = 128
EPS = 1e-6
NEG = -1e30

LANES = 128
SUBLANES = 8
VMEM_LIMIT = 56 * 1024 * 1024

HP = N_HEADS * LANES
BF16 = jnp.bfloat16
F32 = jnp.float32


def _cparams(*sem):
    return pltpu.CompilerParams(dimension_semantics=sem, vmem_limit_bytes=VMEM_LIMIT)


def _rms(x, g):
    r = lax.rsqrt(jnp.mean(x * x, axis=-1, keepdims=True) + EPS)
    return x * r * g


def _dot(a, b):
    return jnp.dot(a, b, preferred_element_type=F32)


def _dot_nt(a, b):
    return lax.dot_general(a, b, (((1,), (1,)), ((), ())), preferred_element_type=F32)


def _full(shape):
    nd = len(shape)
    return pl.BlockSpec(shape, lambda *_: (0,) * nd)


def _proj_kernel(x_ref, ct_ref, st_ref, gmix_ref, win_ref, gql_ref, wuq_ref, gkv_ref, wuk_ref,
                 w2_ref, gq_ref, gk_ref, u_ref, q_ref, k2_ref, *rest, sample):
    if sample:
        c_ref, kr_ref, rstd_ref = rest
    else:
        v2_ref, c_ref, kr_ref, rstd_ref = rest
    xn = _rms(x_ref[...], gmix_ref[...]).astype(BF16)
    z = _dot(xn, win_ref[...])
    u_ref[...] = z[:, :SSM_W]
    o = SSM_W
    cq = z[:, o:o + Q_LORA]
    o += Q_LORA
    ckv = z[:, o:o + KV_LORA]
    o += KV_LORA
    ct = ct_ref[...]
    st = st_ref[...]
    kblock = z[:, o:o + LANES] * ct + z[:, o + LANES:o + 2 * LANES] * st
    kr_ref[...] = kblock[:, NOPE_DIM:QK_DIM]

    qq = _dot(_rms(cq, gql_ref[...]).astype(BF16), wuq_ref[...])
    c = _rms(ckv, gkv_ref[...])
    c_ref[...] = c
    cb = c.astype(BF16)
    kn = _dot(cb, wuk_ref[...])
    if not sample:
        hl = lax.broadcasted_iota(jnp.int32, (1, HP), 1) & (LANES - 1)
        v2_ref[...] = (_dot(cb, w2_ref[...]) + jnp.where(hl >= V_DIM, 1.0, 0.0)).astype(BF16)

    lane = lax.broadcasted_iota(jnp.int32, ct.shape, 1)
    rstd_all = jnp.zeros(ct.shape, F32)
    gq = gq_ref[...]
    gk = gk_ref[...]
    for h in range(N_HEADS):
        sl = slice(h * LANES, (h + 1) * LANES)
        qr = qq[:, sl] * ct + qq[:, HP + h * LANES:HP + (h + 1) * LANES] * st
        qn = qr * lax.rsqrt(jnp.sum(qr * qr, axis=-1, keepdims=True) * (1.0 / QK_DIM) + EPS) * gq
        kh = kn[:, sl] + kblock
        rs = lax.rsqrt(jnp.sum(kh * kh, axis=-1, keepdims=True) * (1.0 / QK_DIM) + EPS)
        rstd_all = jnp.where(lane == h, rs, rstd_all)
        if sample:
            qf = (qn * gk).astype(BF16)
            q_ref[:, sl] = qf
            k2_ref[:, h * KV_LORA:(h + 1) * KV_LORA] = _dot(qf, w2_ref[h]).astype(BF16)
        else:
            q_ref[:, sl] = qn.astype(BF16)
            k2_ref[:, sl] = (kh * rs * gk).astype(BF16)
    rstd_ref[...] = rstd_all[:, :N_HEADS]


def _proj(x2, ct, st, w, *, nb, nt, tm, sample):
    n = x2.shape[0]
    row = lambda b, t: (b * nt + t, 0)
    rows = lambda width: pl.BlockSpec((tm, width), row)
    k2_w = N_HEADS * KV_LORA if sample else HP
    w2 = w["wukT"] if sample else w["wuv"]
    out_shape = [
        jax.ShapeDtypeStruct((n, SSM_W), F32),
        jax.ShapeDtypeStruct((n, HP), BF16),
        jax.ShapeDtypeStruct((n, k2_w), BF16),
        *([] if sample else [jax.ShapeDtypeStruct((n, HP), BF16)]),
        jax.ShapeDtypeStruct((n, KV_LORA), F32),
        jax.ShapeDtypeStruct((n, ROPE_DIM), F32),
        jax.ShapeDtypeStruct((n, N_HEADS), F32),
    ]
    out_specs = [
        rows(SSM_W), rows(HP), rows(k2_w), *([] if sample else [rows(HP)]),
        rows(KV_LORA), rows(ROPE_DIM), rows(N_HEADS),
    ]
    in_specs = [
        rows(D_MODEL),
        pl.BlockSpec((tm, LANES), lambda b, t: (t, 0)),
        pl.BlockSpec((tm, LANES), lambda b, t: (t, 0)),
        _full(w["gmix"].shape), _full(w["win"].shape), _full(w["gql"].shape), _full(w["wuq"].shape),
        _full(w["gkv"].shape), _full(w["wuk"].shape), _full(w2.shape),
        _full(w["gq_s" if sample else "gq_p"].shape), _full(w["gk"].shape),
    ]
    return pl.pallas_call(
        functools.partial(_proj_kernel, sample=sample),
        grid=(nb, nt),
        in_specs=in_specs,
        out_specs=out_specs,
        out_shape=out_shape,
        compiler_params=_cparams("parallel", "parallel"),
        name="proj_s" if sample else "proj_p",
    )(x2, ct, st, w["gmix"], w["win"], w["gql"], w["wuq"], w["gkv"], w["wuk"], w2,
      w["gq_s" if sample else "gq_p"], w["gk"])


S5_HALF = SSM_S // 2
S5_QUARTER = SSM_S // 4


def _s5_kernel(u_ref, h0_ref, are_ref, aim_ref, bw_ref, cw_ref, dsk_ref, wglu_ref, bglu_ref, gout_ref,
               o_ref, sfin_ref, xs_ref, st_ref, *il_ref, steps):
    @pl.when(pl.program_id(1) == 0)
    def _():
        st_ref[...] = h0_ref[...]

    il = il_ref[0] if il_ref else None
    lane_tiles = SSM_W // LANES
    if il is not None:
        for b in range(SUBLANES):
            for c in range(lane_tiles):
                il[c, pl.ds(b, steps, stride=SUBLANES), :] = u_ref[b, :, c * LANES:(c + 1) * LANES]
        u = jnp.concatenate([il[c] for c in range(lane_tiles)], axis=-1)
    else:
        u = u_ref[...]
    ub = u.astype(BF16)
    half_w = SSM_W // 2
    for k in range(2):
        uk = ub[:, k * half_w:(k + 1) * half_w]
        xs_ref[:, k * S5_HALF:(k + 1) * S5_HALF] = _dot(uk, bw_ref[2 * k])
        xs_ref[:, SSM_S + k * S5_HALF:SSM_S + (k + 1) * S5_HALF] = _dot(uk, bw_ref[2 * k + 1])

    for q in range(4):
        lr = slice(q * S5_QUARTER, (q + 1) * S5_QUARTER)
        li = slice(SSM_S + q * S5_QUARTER, SSM_S + (q + 1) * S5_QUARTER)
        ar = are_ref[:, lr]
        ai = aim_ref[:, lr]

        def step(t, carry):
            sr, si = carry
            r0 = pl.multiple_of(t * SUBLANES, SUBLANES)
            nr = ar * sr - ai * si + xs_ref[pl.ds(r0, SUBLANES), lr]
            ni = ar * si + ai * sr + xs_ref[pl.ds(r0, SUBLANES), li]
            xs_ref[pl.ds(r0, SUBLANES), lr] = nr
            xs_ref[pl.ds(r0, SUBLANES), li] = ni
            return nr, ni

        sr, si = lax.fori_loop(0, steps, step, (st_ref[:, lr], st_ref[:, li]), unroll=4)
        st_ref[:, lr] = sr
        st_ref[:, li] = si
    sfin_ref[...] = st_ref[...]

    ys = []
    for k in range(2):
        sre = xs_ref[:, k * S5_HALF:(k + 1) * S5_HALF].astype(BF16)
        sim = xs_ref[:, SSM_S + k * S5_HALF:SSM_S + (k + 1) * S5_HALF].astype(BF16)
        ys.append(_dot(sre, cw_ref[2 * k]) + _dot(sim, cw_ref[2 * k + 1]))
    y = jnp.concatenate(ys, axis=-1) + dsk_ref[...] * u
    g = jax.nn.gelu(y)
    out = g * jax.nn.sigmoid(_dot(g.astype(BF16), wglu_ref[...]) + bglu_ref[...])
    on = _rms(out, gout_ref[...])
    if il is not None:
        for c in range(lane_tiles):
            il[c] = on[:, c * LANES:(c + 1) * LANES]
        for b in range(SUBLANES):
            for c in range(lane_tiles):
                o_ref[b, :, c * LANES:(c + 1) * LANES] = il[c, pl.ds(b, steps, stride=SUBLANES), :].astype(BF16)
    else:
        o_ref[...] = on.astype(BF16)


def _s5(u, h0, w, *, ngroups, nchunks, steps):
    rows = steps * SUBLANES
    batch_major = u.ndim == 3
    if batch_major:
        assert ngroups == 1 and u.shape[0] == SUBLANES
        ublock = pl.BlockSpec((SUBLANES, steps, SSM_W), lambda g, t: (0, t, 0))
    else:
        ublock = pl.BlockSpec((rows, SSM_W), lambda g, t: (g * nchunks + t, 0))
    return pl.pallas_call(
        functools.partial(_s5_kernel, steps=steps),
        grid=(ngroups, nchunks),
        in_specs=[
            ublock,
            pl.BlockSpec((SUBLANES, 2 * SSM_S), lambda g, t: (g, 0)),
            _full(w["s5_are"].shape), _full(w["s5_aim"].shape), _full(w["s5_bw"].shape),
            _full(w["s5_cw"].shape), _full(w["s5_d"].shape), _full(w["wglu"].shape),
            _full(w["bglu"].shape), _full(w["gout_ssm"].shape),
        ],
        out_specs=[
            ublock,
            pl.BlockSpec((SUBLANES, 2 * SSM_S), lambda g, t: (g, 0)),
        ],
        out_shape=[
            jax.ShapeDtypeStruct(u.shape, BF16),
            jax.ShapeDtypeStruct(h0.shape, F32),
        ],
        scratch_shapes=[pltpu.VMEM((rows, 2 * SSM_S), F32), pltpu.VMEM((SUBLANES, 2 * SSM_S), F32)]
        + ([pltpu.VMEM((SSM_W // LANES, rows, LANES), F32)] if batch_major else []),
        compiler_params=_cparams("parallel", "arbitrary"),
        name="s5",
    )(u, h0, w["s5_are"], w["s5_aim"], w["s5_bw"], w["s5_cw"], w["s5_d"], w["wglu"], w["bglu"],
      w["gout_ssm"])


ATTN_P_HEADS = 4


def _attn_p_kernel(q_ref, k_ref, v_ref, o_ref, *, tq, tk):
    i = pl.program_id(2)
    nd = tq // tk
    row = lax.broadcasted_iota(jnp.int32, (tk, tk), 0)
    col = lax.broadcasted_iota(jnp.int32, (tk, tk), 1)

    def block(j, carry, diag):
        k0 = pl.multiple_of(j * tk, tk)
        k2 = k_ref[pl.ds(k0, tk), :]
        v2 = v_ref[pl.ds(k0, tk), :]
        r0 = 0 if diag is None else diag * tk
        out = []
        for hh in range(ATTN_P_HEADS):
            m, acc = carry[hh]
            s = _dot_nt(q_ref[r0:, hh * LANES:(hh + 1) * LANES], k2[:, hh * LANES:(hh + 1) * LANES])
            if diag is not None:
                tri = jnp.where(col <= row, s[:tk], NEG)
                s = tri if r0 + tk == tq else jnp.concatenate([tri, s[tk:]], axis=0)
            m_new = jnp.maximum(m[r0:], jnp.max(s, axis=-1, keepdims=True))
            p = jnp.exp2(s - m_new).astype(BF16)
            acc_new = jnp.exp2(m[r0:] - m_new) * acc[r0:] + _dot(p, v2[:, hh * LANES:(hh + 1) * LANES])
            if r0:
                m_new = jnp.concatenate([m[:r0], m_new], axis=0)
                acc_new = jnp.concatenate([acc[:r0], acc_new], axis=0)
            out.append((m_new, acc_new))
        return tuple(out)

    init = tuple((jnp.full((tq, 1), NEG, F32), jnp.zeros((tq, LANES), F32)) for _ in range(ATTN_P_HEADS))
    carry = lax.fori_loop(0, i * nd, lambda j, c: block(j, c, None), init)
    for d in range(nd):
        carry = block(i * nd + d, carry, d)
    o = [acc / pltpu.roll(acc, V_DIM, 1) for _, acc in carry]
    lane = lax.broadcasted_iota(jnp.int32, (tq, LANES), 1)
    for pr in range(ATTN_P_HEADS // 2):
        o_ref[:, pr * LANES:(pr + 1) * LANES] = jnp.where(
            lane < V_DIM, o[2 * pr], pltpu.roll(o[2 * pr + 1], V_DIM, 1))


def _attn_p(q, k, v, *, nb, t, tq, tk):
    nq = t // tq
    hw = ATTN_P_HEADS * LANES
    return pl.pallas_call(
        functools.partial(_attn_p_kernel, tq=tq, tk=tk),
        grid=(nb, N_HEADS // ATTN_P_HEADS, nq),
        in_specs=[
            pl.BlockSpec((tq, hw), lambda b, h, i: (b * nq + i, h)),
            pl.BlockSpec((t, hw), lambda b, h, i: (b, h)),
            pl.BlockSpec((t, hw), lambda b, h, i: (b, h)),
        ],
        out_specs=pl.BlockSpec((tq, ATTN_P_HEADS * V_DIM), lambda b, h, i: (b * nq + i, h)),
        out_shape=jax.ShapeDtypeStruct((nb * t, N_HEADS * V_DIM), F32),
        compiler_params=_cparams("parallel", "parallel", "arbitrary"),
        name="attn_p",
    )(q, k, v)


ATTN_S_PAGES = 32
NEW_KEYS = 16


def _attn_s_kernel(pt_ref, ql_ref, qr_ref, cn_ref, krn_ref, rnt_ref, cc_hbm, ckr_hbm, cr_hbm,
                   o_ref, cbuf, krbuf, rbuf, cb_ref, sc_ref, sems, *, n_pages, dec_seq):
    b = pl.program_id(0)
    nb = pl.num_programs(0)
    bp = min(ATTN_S_PAGES, n_pages)
    nblk = n_pages // bp
    keys = bp * PAGE_SIZE
    nrow = dec_seq * N_HEADS
    scale = 1.0 / math.sqrt(QK_DIM)
    slot = b % 2

    def start_block(bb, i, dst):
        for p in range(bp):
            pg = i * bp + p
            page = pt_ref[bb * n_pages + pg]
            lanes = pl.ds(pl.multiple_of(pg * PAGE_SIZE, PAGE_SIZE), PAGE_SIZE)
            pltpu.make_async_copy(cc_hbm.at[page], cbuf.at[dst, pg], sems.at[dst, 0]).start()
            pltpu.make_async_copy(ckr_hbm.at[page], krbuf.at[dst, :, lanes], sems.at[dst, 1]).start()
            pltpu.make_async_copy(cr_hbm.at[page], rbuf.at[dst, :, lanes], sems.at[dst, 2]).start()

    @pl.when(b == 0)
    def _():
        lax.fori_loop(0, nblk, lambda i, c: (start_block(0, i, 0), c)[1], 0)

    pltpu.make_async_copy(cbuf.at[slot], cbuf.at[slot], sems.at[slot, 0]).wait()
    pltpu.make_async_copy(krbuf.at[slot], krbuf.at[slot], sems.at[slot, 1]).wait()
    pltpu.make_async_copy(rbuf.at[slot], rbuf.at[slot], sems.at[slot, 2]).wait()

    ql = ql_ref[0]
    qr = qr_ref[0]

    def tile_heads(r_t):
        return jnp.concatenate([r_t] * dec_seq, axis=0)

    cnb = cn_ref[0].astype(BF16)
    s_n = (_dot_nt(ql, cnb) + _dot(qr, krn_ref[0].astype(BF16))) * tile_heads(rnt_ref[0]) * scale
    qs = lax.broadcasted_iota(jnp.int32, (nrow, NEW_KEYS), 0) // N_HEADS
    kj = lax.broadcasted_iota(jnp.int32, (nrow, NEW_KEYS), 1)
    s_n = jnp.where(kj <= qs, s_n, NEG)

    def score_block(i, m):
        @pl.when(b + 1 < nb)
        def _():
            start_block(b + 1, i, 1 - slot)

        k0 = pl.multiple_of(i * keys, keys)
        cb = cbuf[slot, pl.ds(i * bp, bp)].reshape(keys, KV_LORA).astype(BF16)
        cb_ref[pl.ds(k0, keys), :] = cb
        raw = _dot_nt(ql, cb) + _dot(qr, krbuf[slot, :, pl.ds(k0, keys)].astype(BF16))
        s = raw * tile_heads(rbuf[slot, :, pl.ds(k0, keys)]) * scale
        sc_ref[:, pl.ds(k0, keys)] = s
        return jnp.maximum(m, jnp.max(s, axis=-1, keepdims=True))

    m = lax.fori_loop(0, nblk, score_block, jnp.max(s_n, axis=-1, keepdims=True))
    p_n = jnp.exp(s_n - m)

    def value_block(i, carry):
        l, acc = carry
        k0 = pl.multiple_of(i * keys, keys)
        p = jnp.exp(sc_ref[:, pl.ds(k0, keys)] - m)
        return (l + jnp.sum(p, axis=-1, keepdims=True),
                acc + _dot(p.astype(BF16), cb_ref[pl.ds(k0, keys), :]))

    l, acc = lax.fori_loop(0, nblk, value_block,
                           (jnp.sum(p_n, axis=-1, keepdims=True), _dot(p_n.astype(BF16), cnb)))
    o_ref[0] = acc / l


def _attn_s(page_table, ql, qr, cn, krn, rnt, cache_c, cache_kr, cache_r, *, dec_seq):
    nb, n_pages = page_table.shape
    nrow = dec_seq * N_HEADS
    b3 = lambda b, pt: (b, 0, 0)
    return pl.pallas_call(
        functools.partial(_attn_s_kernel, n_pages=n_pages, dec_seq=dec_seq),
        grid_spec=pltpu.PrefetchScalarGridSpec(
            num_scalar_prefetch=1,
            grid=(nb,),
            in_specs=[
                pl.BlockSpec((1, nrow, KV_LORA), b3),
                pl.BlockSpec((1, nrow, ROPE_DIM), b3),
                pl.BlockSpec((1, NEW_KEYS, KV_LORA), b3),
                pl.BlockSpec((1, ROPE_DIM, NEW_KEYS), b3),
                pl.BlockSpec((1, N_HEADS, NEW_KEYS), b3),
                pl.BlockSpec(memory_space=pl.ANY),
                pl.BlockSpec(memory_space=pl.ANY),
                pl.BlockSpec(memory_space=pl.ANY),
            ],
            out_specs=pl.BlockSpec((1, nrow, KV_LORA), b3),
            scratch_shapes=[
                pltpu.VMEM((2, n_pages, PAGE_SIZE, KV_LORA), F32),
                pltpu.VMEM((2, ROPE_DIM, n_pages * PAGE_SIZE), F32),
                pltpu.VMEM((2, N_HEADS, n_pages * PAGE_SIZE), F32),
                pltpu.VMEM((n_pages * PAGE_SIZE, KV_LORA), BF16),
                pltpu.VMEM((nrow, n_pages * PAGE_SIZE), F32),
                pltpu.SemaphoreType.DMA((2, 3)),
            ],
        ),
        out_shape=jax.ShapeDtypeStruct((nb, nrow, KV_LORA), F32),
        compiler_params=_cparams("arbitrary"),
        name="attn_s",
    )(page_table.reshape(-1), ql, qr, cn, krn, rnt, cache_c, cache_kr, cache_r)


def _latent_out_kernel(ol_ref, wuv_ref, o_ref):
    for h in range(N_HEADS):
        o_ref[:, h * V_DIM:(h + 1) * V_DIM] = _dot(
            ol_ref[:, h * KV_LORA:(h + 1) * KV_LORA].astype(BF16), wuv_ref[h])


def _latent_out(ol, wuv_h):
    n = ol.shape[0]
    return pl.pallas_call(
        _latent_out_kernel,
        grid=(1,),
        in_specs=[_full(ol.shape), _full(wuv_h.shape)],
        out_specs=_full((n, N_HEADS * V_DIM)),
        out_shape=jax.ShapeDtypeStruct((n, N_HEADS * V_DIM), F32),
        compiler_params=_cparams("arbitrary"),
        name="latent_out",
    )(ol, wuv_h)


ROUTE_W = 16


def _mix_kernel(h_ref, os_ref, om_ref, gom_ref, wout_ref, gffn_ref, wrt_ref, brt_ref,
                h1_ref, xn_ref, route_ref, routet_ref, cnt_ref):
    @pl.when((pl.program_id(0) == 0) & (pl.program_id(1) == 0))
    def _():
        cnt_ref[...] = jnp.zeros_like(cnt_ref)

    tm = h_ref.shape[0]
    omn = _rms(om_ref[...], gom_ref[...]).astype(BF16)
    h1 = h_ref[...] + _dot(os_ref[...], wout_ref[:SSM_W, :]) + _dot(omn, wout_ref[SSM_W:, :])
    h1_ref[...] = h1
    xn = _rms(h1, gffn_ref[...])
    xn_ref[...] = xn
    work = (_dot_nt(wrt_ref[...], xn.astype(BF16)) + brt_ref[...])[:N_EXPERTS]

    eidx = lax.broadcasted_iota(jnp.int32, work.shape, 0).astype(F32)
    sels, vals, rows = [], [], []
    for k in range(TOP_K):
        mk = jnp.max(work, axis=0, keepdims=True)
        ik = jnp.min(jnp.where(work == mk, eidx, float(N_EXPERTS)), axis=0, keepdims=True)
        sel = eidx == ik
        work = jnp.where(sel, -jnp.inf, work)
        sels.append(sel)
        vals.append(mk)
        rows.append(ik)
    es = [jnp.exp(v - vals[0]) for v in vals]
    den = es[0] + es[1] + es[2] + es[3]
    rows += [e / den for e in es]
    onehot = jnp.zeros(eidx.shape, F32)
    for sel in sels:
        onehot = jnp.where(sel, 1.0, onehot)

    t_j = lax.broadcasted_iota(jnp.int32, (tm, tm), 0)
    t_i = lax.broadcasted_iota(jnp.int32, (tm, tm), 1)
    earlier = jnp.where(t_j < t_i, 1.0, 0.0).astype(BF16)
    cnt = cnt_ref[...]
    cum = _dot(onehot.astype(BF16), earlier) + jnp.concatenate([cnt] * pl.cdiv(tm, LANES), axis=1)[:, :tm]
    rows += [jnp.sum(jnp.where(sel, cum, 0.0), axis=0, keepdims=True) for sel in sels]
    cnt_ref[...] = cnt + jnp.sum(onehot, axis=1, keepdims=True)

    sub = lax.broadcasted_iota(jnp.int32, (ROUTE_W, tm), 0)
    rt = jnp.zeros((ROUTE_W, tm), F32)
    for r, v in enumerate(rows):
        rt = jnp.where(sub == r, v, rt)
    routet_ref[...] = rt
    route_ref[...] = jnp.concatenate([rt, jnp.zeros((LANES - ROUTE_W, tm), F32)], axis=0).T[:, :ROUTE_W]


def _mix(h2, o_ssm, o_mla, w, *, nb, nt, tm):
    n = h2.shape[0]
    row = lambda b, t: (b * nt + t, 0)
    return pl.pallas_call(
        _mix_kernel,
        grid=(nb, nt),
        in_specs=[
            pl.BlockSpec((tm, D_MODEL), row),
            pl.BlockSpec((tm, SSM_W), row),
            pl.BlockSpec((tm, N_HEADS * V_DIM), row),
            _full(w["gout_mla"].shape), _full(w["wout"].shape), _full(w["gffn"].shape),
            _full(w["wrt"].shape), _full(w["brt"].shape),
        ],
        out_specs=[
            pl.BlockSpec((tm, D_MODEL), row),
            pl.BlockSpec((tm, D_MODEL), row),
            pl.BlockSpec((tm, ROUTE_W), row),
            pl.BlockSpec((ROUTE_W, tm), lambda b, t: (0, b * nt + t)),
            _full((N_EXPERTS, LANES)),
        ],
        out_shape=[
            jax.ShapeDtypeStruct((n, D_MODEL), F32),
            jax.ShapeDtypeStruct((n, D_MODEL), F32),
            jax.ShapeDtypeStruct((n, ROUTE_W), F32),
            jax.ShapeDtypeStruct((ROUTE_W, n), F32),
            jax.ShapeDtypeStruct((N_EXPERTS, LANES), F32),
        ],
        compiler_params=_cparams("arbitrary", "arbitrary"),
        name="mix",
    )(h2, o_ssm, o_mla, w["gout_mla"], w["wout"], w["gffn"], w["wrt"], w["brt"])


def _experts_kernel(tile_ref, e_ref, first_ref, valid_ref, lo_ref, hi_ref, newe_ref,
                    x_ref, wgu_ref, bgu_ref, wd_ref, bd_ref, o_ref, wgu_bf, wd_bf):
    i = pl.program_id(0)

    @pl.when(newe_ref[i] == 1)
    def _():
        wgu_bf[...] = wgu_ref[0].astype(BF16)
        wd_bf[...] = wd_ref[0].astype(BF16)

    @pl.when(valid_ref[i] == 1)
    def _():
        bm = x_ref.shape[0]
        hdn = _dot(x_ref[...].astype(BF16), wgu_bf[...]) + bgu_ref[0]
        x_glu = jnp.minimum(hdn[:, :D_FF], SWIGLU_LIMIT)
        x_lin = jnp.clip(hdn[:, D_FF:], -SWIGLU_LIMIT, SWIGLU_LIMIT)
        act = x_glu * jax.nn.sigmoid(SWIGLU_ALPHA * x_glu) * (x_lin + 1.0)
        y = _dot(act.astype(BF16), wd_bf[...]) + bd_ref[0]
        rows = tile_ref[i] * bm + lax.broadcasted_iota(jnp.int32, (bm, 1), 0)
        mine = (rows >= lo_ref[i]) & (rows < hi_ref[i])

        @pl.when(first_ref[i] == 1)
        def _():
            o_ref[...] = jnp.where(mine, y, 0.0)

        @pl.when(first_ref[i] == 0)
        def _():
            o_ref[...] = jnp.where(mine, y, o_ref[...])


def _experts(xs, items, w, *, bm):
    r = xs.shape[0]
    n_items = items[0].shape[0]
    xmap = lambda i, tile, *_: (tile[i], 0)
    emap = lambda i, tile, e, *_: (e[i], 0, 0)
    return pl.pallas_call(
        _experts_kernel,
        grid_spec=pltpu.PrefetchScalarGridSpec(
            num_scalar_prefetch=len(items),
            grid=(n_items,),
            in_specs=[
                pl.BlockSpec((bm, D_MODEL), xmap),
                pl.BlockSpec((1, D_MODEL, 2 * D_FF), emap),
                pl.BlockSpec((1, 1, 2 * D_FF), emap),
                pl.BlockSpec((1, D_FF, D_MODEL), emap),
                pl.BlockSpec((1, 1, D_MODEL), emap),
            ],
            out_specs=pl.BlockSpec((bm, D_MODEL), xmap),
            scratch_shapes=[pltpu.VMEM((D_MODEL, 2 * D_FF), BF16), pltpu.VMEM((D_FF, D_MODEL), BF16)],
        ),
        out_shape=jax.ShapeDtypeStruct((r, D_MODEL), F32),
        compiler_params=_cparams("arbitrary"),
        name="experts",
    )(*items, xs, w["wgu"], w["bgu"], w["wd"], w["bd"])


def _expert_items(counts, n_rows, bm):
    n_tiles = n_rows // bm
    n_items = n_tiles + N_EXPERTS - 1
    ends = jnp.cumsum(counts)
    starts = ends - counts
    first_tile = starts // bm
    per_e = jnp.where(counts > 0, (ends - 1) // bm - first_tile + 1, 0)
    item_end = jnp.cumsum(per_e)
    i = jnp.arange(n_items, dtype=jnp.int32)
    valid = i < item_end[-1]
    e = jnp.sum(jnp.minimum(i, item_end[-1] - 1)[:, None] >= item_end[None, :], axis=1).astype(jnp.int32)
    of_e = e[:, None] == jnp.arange(N_EXPERTS, dtype=jnp.int32)[None, :]
    pick = lambda table: jnp.sum(jnp.where(of_e, table[None, :], 0), axis=1).astype(jnp.int32)
    tile = jnp.where(valid, pick(first_tile) + i - pick(item_end - per_e), n_tiles - 1).astype(jnp.int32)
    one = jnp.ones((1,), jnp.int32)
    first = jnp.concatenate([one, (tile[1:] != tile[:-1]).astype(jnp.int32)])
    new_e = jnp.concatenate([one, (e[1:] != e[:-1]).astype(jnp.int32)])
    return (tile, e, first, valid.astype(jnp.int32), pick(starts), pick(ends), new_e)


def _ple_kernel(h1_ref, yg_ref, route_ref, p_ref, gple_ref, wg_ref, wp_ref, o_ref):
    h2 = h1_ref[...]
    route = route_ref[...]
    w = SC_GATHER_TOKENS
    for k in range(TOP_K):
        yk = yg_ref[:, k * w:(k + 1) * w, :].reshape(h2.shape)
        h2 = h2 + yk * route[:, TOP_K + k:TOP_K + k + 1]
    gate = jax.nn.sigmoid(_dot(_rms(h2, gple_ref[...]).astype(BF16), wg_ref[...]))
    o_ref[...] = h2 + _dot(p_ref[...].astype(BF16), wp_ref[...]) * gate


def _ple(h1, yg, route, p, w, *, tm):
    n = h1.shape[0]
    row = lambda t: (t, 0)
    return pl.pallas_call(
        _ple_kernel,
        grid=(n // tm,),
        in_specs=[
            pl.BlockSpec((tm, D_MODEL), row),
            pl.BlockSpec((tm // SC_GATHER_TOKENS, TOP_K * SC_GATHER_TOKENS, D_MODEL), lambda t: (t, 0, 0)),
            pl.BlockSpec((tm, ROUTE_W), row),
            pl.BlockSpec((tm, PLE_DIM), row),
            _full(w["gple"].shape), _full(w["wpg"].shape), _full(w["wp"].shape),
        ],
        out_specs=pl.BlockSpec((tm, D_MODEL), row),
        out_shape=jax.ShapeDtypeStruct((n, D_MODEL), F32),
        compiler_params=_cparams("parallel"),
        name="ple",
    )(h1, yg, route, p, w["gple"], w["wpg"], w["wp"])


SC_INDEX_ROW = 128
SC_SCATTER_TOKENS = SC_INDEX_ROW // TOP_K
SC_GATHER_TOKENS = 8


def _sc_mesh():
    return plsc.VectorSubcoreMesh(core_axis_name="c", subcore_axis_name="s")


def _window_index(dest_t, w):
    n = dest_t.shape[1]
    idx = dest_t.reshape(TOP_K, n // w, w).transpose(1, 0, 2).reshape(n // w, TOP_K * w)
    if TOP_K * w < SC_INDEX_ROW:
        idx = jnp.concatenate([idx, jnp.zeros((n // w, SC_INDEX_ROW - TOP_K * w), jnp.int32)], axis=-1)
    return idx


def _sc_dispatch(xn, idx):
    n = xn.shape[0]
    w = SC_SCATTER_TOKENS

    @pl.kernel(out_type=jax.ShapeDtypeStruct((n * TOP_K, D_MODEL), xn.dtype),
               mesh=_sc_mesh(), scratch_types=[pltpu.SemaphoreType.DMA])
    def scatter(x_hbm, i_hbm, o_hbm, sem):
        def body(x_vmem, i_vmem):
            copies = [pltpu.make_async_copy(x_vmem, o_hbm.at[i_vmem.at[0, pl.ds(k * w, w)]], sem)
                      for k in range(TOP_K)]
            for cp in copies:
                cp.start()
            for cp in copies:
                cp.wait()

        pltpu.emit_pipeline(
            body,
            grid=(n // w,),
            in_specs=[pl.BlockSpec((w, D_MODEL), lambda i: (i, 0)),
                      pl.BlockSpec((1, SC_INDEX_ROW), lambda i: (i, 0))],
            out_specs=[],
            core_axis_name=("c", "s"),
            dimension_semantics=(pltpu.PARALLEL,),
        )(x_hbm, i_hbm)

    return scatter(xn, idx)


def _sc_combine(ys, idx):
    w = SC_GATHER_TOKENS
    n = idx.shape[0] * w
    rows = TOP_K * w

    @pl.kernel(out_type=jax.ShapeDtypeStruct((n // w, rows, D_MODEL), ys.dtype),
               mesh=_sc_mesh(), scratch_types=[])
    def gather(y_hbm, i_hbm, o_hbm):
        def body(i_vmem, o_vmem):
            pltpu.sync_copy(y_hbm.at[i_vmem.at[0, pl.ds(0, rows)]], o_vmem.at[0])

        pltpu.emit_pipeline(
            body,
            grid=(n // w,),
            in_specs=[pl.BlockSpec((1, SC_INDEX_ROW), lambda i: (i, 0))],
            out_specs=[pl.BlockSpec((1, rows, D_MODEL), lambda i: (i, 0, 0))],
            core_axis_name=("c", "s"),
            dimension_semantics=(pltpu.PARALLEL,),
        )(i_hbm, o_hbm)

    return gather(ys, idx)


def _rope_tables(pos):
    inv = ROPE_THETA ** (-jnp.arange(0, ROPE_DIM, 2, dtype=F32) / ROPE_DIM)
    ang = pos.astype(F32)[:, None] * inv[None, :]
    cos, sin = jnp.cos(ang), jnp.sin(ang)
    n = pos.shape[0]
    pad = jnp.zeros((n, LANES - QK_DIM), F32)
    ct = jnp.concatenate([jnp.ones((n, NOPE_DIM), F32), cos, cos, pad], axis=-1)
    st = jnp.concatenate([jnp.zeros((n, NOPE_DIM), F32), sin, sin, pad], axis=-1)
    return ct, st


def _rope_partner(w_rope):
    return jnp.concatenate([-w_rope[..., ROPE_HALF:], w_rope[..., :ROPE_HALF]], axis=-1)


def _head_tile(nope, rope):
    pad = jnp.zeros(rope.shape[:-1] + (LANES - QK_DIM,), rope.dtype)
    return jnp.concatenate([nope, rope, pad], axis=-1)


def _block_diag(m):
    g, a, b = m.shape
    hg = g // 2
    eye = jnp.eye(hg, dtype=m.dtype)
    mh = m.reshape(2, hg, a, b)
    return jnp.einsum("kgab,gh->kgahb", mh, eye).reshape(2, hg * a, hg * b)


def _prep_weights(g_mix, w_in, s5_a_re, s5_a_im, s5_log_dt, s5_b_re, s5_b_im, s5_c_re, s5_c_im, s5_d,
                  w_glu, b_glu, g_q_lora, w_uq, g_kv_lora, w_uk, w_uv, g_qk_q, g_qk_k, g_out_ssm,
                  g_out_mla, w_out, g_ffn, w_router, b_router, w_gate_up, b_gate_up, w_down, b_down,
                  g_ple, w_ple_gate, w_ple):
    w = {}
    row = lambda v: v.reshape(1, -1).astype(F32)
    o = SSM_W + Q_LORA + KV_LORA
    w_kr = w_in[:, o:o + ROPE_DIM]
    zeros_n = jnp.zeros((D_MODEL, NOPE_DIM), F32)
    w["win"] = jnp.concatenate(
        [w_in[:, :o], _head_tile(zeros_n, w_kr), _head_tile(zeros_n, _rope_partner(w_kr))], axis=-1).astype(BF16)
    w["gmix"] = row(g_mix)
    w["gql"] = row(g_q_lora)
    w["gkv"] = row(g_kv_lora)

    wq = w_uq.reshape(Q_LORA, N_HEADS, QK_DIM)
    q1 = _head_tile(wq[..., :NOPE_DIM], wq[..., NOPE_DIM:])
    q2 = _head_tile(jnp.zeros_like(wq[..., :NOPE_DIM]), _rope_partner(wq[..., NOPE_DIM:]))
    w["wuq"] = jnp.concatenate([q1.reshape(Q_LORA, HP), q2.reshape(Q_LORA, HP)], axis=-1).astype(BF16)
    w["wuk"] = jnp.concatenate(
        [w_uk, jnp.zeros((KV_LORA, N_HEADS, LANES - NOPE_DIM), F32)], axis=-1).reshape(KV_LORA, HP).astype(BF16)
    w["wuv"] = jnp.concatenate(
        [w_uv, jnp.zeros((KV_LORA, N_HEADS, LANES - V_DIM), F32)], axis=-1).reshape(KV_LORA, HP).astype(BF16)
    w["wuv_h"] = w_uv.transpose(1, 0, 2).astype(BF16)
    wukT = w_uk.transpose(1, 2, 0)
    w["wukT"] = jnp.concatenate(
        [wukT, jnp.zeros((N_HEADS, LANES - NOPE_DIM, KV_LORA), F32)], axis=1).astype(BF16)
    gpad = jnp.zeros((LANES - QK_DIM,), F32)
    w["gq_p"] = row(jnp.concatenate([g_qk_q * (math.log2(math.e) / math.sqrt(QK_DIM)), gpad]))
    w["gq_s"] = row(jnp.concatenate([g_qk_q, gpad]))
    w["gk"] = row(jnp.concatenate([g_qk_k, gpad]))

    dt = jnp.exp(s5_log_dt.astype(F32))[:, None]
    ar, ai = s5_a_re.astype(F32), s5_a_im.astype(F32)
    mag = jnp.exp(dt * ar)
    abar_re, abar_im = mag * jnp.cos(dt * ai), mag * jnp.sin(dt * ai)
    den = ar * ar + ai * ai
    nr, ni = abar_re - 1.0, abar_im
    coef_re = (nr * ar + ni * ai) / den
    coef_im = (ni * ar - nr * ai) / den
    br, bi = s5_b_re.astype(F32), s5_b_im.astype(F32)
    bbar_re = coef_re[..., None] * br - coef_im[..., None] * bi
    bbar_im = coef_re[..., None] * bi + coef_im[..., None] * br
    bre = _block_diag(bbar_re.transpose(0, 2, 1))
    bim = _block_diag(bbar_im.transpose(0, 2, 1))
    w["s5_bw"] = jnp.stack([bre[0], bim[0], bre[1], bim[1]]).astype(BF16)
    cre = _block_diag(s5_c_re.astype(F32).transpose(0, 2, 1))
    cim = _block_diag(-s5_c_im.astype(F32).transpose(0, 2, 1))
    w["s5_cw"] = jnp.stack([cre[0], cim[0], cre[1], cim[1]]).astype(BF16)
    w["s5_are"] = jnp.broadcast_to(abar_re.reshape(1, SSM_S), (SUBLANES, SSM_S))
    w["s5_aim"] = jnp.broadcast_to(abar_im.reshape(1, SSM_S), (SUBLANES, SSM_S))
    w["s5_d"] = row(s5_d)
    w["wglu"] = w_glu.astype(BF16)
    w["bglu"] = row(b_glu)
    w["gout_ssm"] = row(g_out_ssm)
    w["gout_mla"] = row(g_out_mla)
    w["wout"] = w_out.astype(BF16)
    w["gffn"] = row(g_ffn)
    w["wrt"] = jnp.concatenate([w_router.T, jnp.zeros((LANES - N_EXPERTS, D_MODEL), F32)], axis=0).astype(BF16)
    w["brt"] = jnp.concatenate([b_router.astype(F32), jnp.zeros((LANES - N_EXPERTS,), F32)]).reshape(LANES, 1)
    w["wgu"] = w_gate_up
    w["bgu"] = b_gate_up.reshape(N_EXPERTS, 1, 2 * D_FF).astype(F32)
    w["wd"] = w_down
    w["bd"] = b_down.reshape(N_EXPERTS, 1, D_MODEL).astype(F32)
    w["gple"] = row(g_ple)
    w["wpg"] = w_ple_gate.astype(BF16)
    w["wp"] = w_ple.astype(BF16)
    return w


EXPERT_ROWS = 512


def _moe_experts(mixed, w, during_dispatch=()):
    h1, xn, _, route_t, cnt = mixed
    n = h1.shape[0]
    top_i = route_t[:TOP_K].astype(jnp.int32)
    rank = route_t[2 * TOP_K:3 * TOP_K].astype(jnp.int32)
    counts = cnt[:, 0].astype(jnp.int32)
    starts = jnp.cumsum(counts) - counts
    dest = starts[top_i] + rank
    gather_idx = _window_index(dest, SC_GATHER_TOKENS)
    xs = _after(_sc_dispatch(xn, _window_index(dest, SC_SCATTER_TOKENS)), (gather_idx, during_dispatch))
    bm = min(EXPERT_ROWS, max(LANES, n * TOP_K // N_EXPERTS))
    return _experts(xs, _expert_items(counts, n * TOP_K, bm), w, bm=bm), gather_idx


def _moe_combine(mixed, ys, gather_idx, p2, w, *, tm):
    h1, _, route, _, _ = mixed
    return _ple(h1, _sc_combine(ys, gather_idx), route, p2, w, tm=tm)


def _after(x, anchor):
    return lax.optimization_barrier((x, anchor))[0]


def kernel(x_prompt, x_sample, cache_kv_latent, cache_k_rope, cache_k_rstd, state_ssm_re, state_ssm_im,
           page_table, p_prompt, p_sample, g_mix, w_in, s5_a_re, s5_a_im, s5_log_dt, s5_b_re, s5_b_im,
           s5_c_re, s5_c_im, s5_d, w_glu, b_glu, g_q_lora, w_uq, g_kv_lora, w_uk, w_uv, g_qk_q, g_qk_k,
           g_out_ssm, g_out_mla, w_out, g_ffn, w_router, b_router, w_gate_up, b_gate_up, w_down, b_down,
           g_ple, w_ple_gate, w_ple):
    assert g_mix.shape[0] == 1, "single-layer step"
    nb, t, _ = x_prompt.shape
    db, ds, _ = x_sample.shape
    assert nb == SUBLANES and db % SUBLANES == 0
    w = _prep_weights(g_mix[0], w_in[0], s5_a_re[0], s5_a_im[0], s5_log_dt[0], s5_b_re[0], s5_b_im[0],
                      s5_c_re[0], s5_c_im[0], s5_d[0], w_glu[0], b_glu[0], g_q_lora[0], w_uq[0],
                      g_kv_lora[0], w_uk[0], w_uv[0], g_qk_q[0], g_qk_k[0], g_out_ssm[0], g_out_mla[0],
                      w_out[0], g_ffn[0], w_router[0], b_router[0], w_gate_up[0], b_gate_up[0],
                      w_down[0], b_down[0], g_ple[0], w_ple_gate[0], w_ple[0])

    tm = min(512, t)
    nt = t // tm
    xp = x_prompt.reshape(nb * t, D_MODEL)
    ct, st = _rope_tables(jnp.arange(t))
    u, q, k, v, c_p, kr_p, rstd_p = _proj(xp, ct, st, w, nb=nb, nt=nt, tm=tm, sample=False)
    steps = min(128, t)
    o_ssm, s_fin = _s5(u.reshape(nb, t, SSM_W), jnp.zeros((nb, 2 * SSM_S), F32), w,
                       ngroups=1, nchunks=t // steps, steps=steps)
    o_mla = _attn_p(q, k, v, nb=nb, t=t, tq=min(1024, t), tk=min(512, t))
    mixed_p = _mix(xp, o_ssm.reshape(nb * t, SSM_W), o_mla, w, nb=nb, nt=nt, tm=tm)
    sr_p = s_fin[:, :SSM_S].reshape(1, nb, SSM_G, SSM_P)
    si_p = s_fin[:, SSM_S:].reshape(1, nb, SSM_G, SSM_P)

    ns = db * ds
    ng = db // SUBLANES
    xs_ = _after(x_sample.reshape(ns, D_MODEL), mixed_p[2])
    ct_s, st_s = _rope_tables(PAST_LEN + jnp.tile(jnp.arange(ds), db))
    u_s, qf, qlat, c_s, kr_s, rstd_s = _proj(xs_, ct_s, st_s, w, nb=1, nt=1, tm=ns, sample=True)
    to_scan = lambda a: a.reshape(ng, SUBLANES, ds, -1).transpose(0, 2, 1, 3).reshape(ns, -1)
    from_scan = lambda a: a.reshape(ng, ds, SUBLANES, -1).transpose(0, 2, 1, 3).reshape(ns, -1)
    h0 = jnp.concatenate([state_ssm_re[0].reshape(db, SSM_S), state_ssm_im[0].reshape(db, SSM_S)],
                         axis=-1).astype(F32)
    o_ssm_s, s_fin_s = _s5(to_scan(u_s), h0, w, ngroups=ng, nchunks=1, steps=ds)
    pad_keys = lambda a: jnp.concatenate(
        [a, jnp.zeros((db, NEW_KEYS - ds, a.shape[-1]), a.dtype)], axis=1)
    key_major = lambda a: a.transpose(0, 2, 1)
    attn_in = (page_table,
               qlat.reshape(db, ds * N_HEADS, KV_LORA),
               qf.reshape(db, ds * N_HEADS, LANES)[:, :, NOPE_DIM:QK_DIM],
               pad_keys(c_s.reshape(db, ds, KV_LORA)),
               key_major(pad_keys(kr_s.reshape(db, ds, ROPE_DIM))),
               key_major(pad_keys(rstd_s.reshape(db, ds, N_HEADS))))
    caches = (cache_kv_latent[0], key_major(cache_k_rope[0]), key_major(cache_k_rstd[0]))

    def attn_rows(lo, hi, anchor=None):
        pt, *rest = (a[lo:hi] for a in attn_in)
        if anchor is not None:
            pt = _after(pt, anchor)
        return _attn_s(pt, *rest, *caches, dec_seq=ds)

    o_lat_a = attn_rows(0, db // 2)
    ys_p, dest_p = _moe_experts(mixed_p, w, during_dispatch=o_lat_a)
    o_lat_b = attn_rows(db // 2, db, anchor=ys_p)
    o_lat = jnp.concatenate([o_lat_a, o_lat_b], axis=0)
    o_mla_s = _latent_out(o_lat.reshape(ns, N_HEADS * KV_LORA), w["wuv_h"])

    y_p = _moe_combine(mixed_p, ys_p, dest_p, p_prompt[0].reshape(nb * t, PLE_DIM), w, tm=tm)
    mixed_s = _mix(xs_, from_scan(o_ssm_s), _after(o_mla_s, y_p), w, nb=1, nt=1, tm=ns)
    ys_s, dest_s = _moe_experts(mixed_s, w)
    y_s = _moe_combine(mixed_s, ys_s, dest_s, p_sample[0].reshape(ns, PLE_DIM), w, tm=ns)

    return (y_p.reshape(nb, t, D_MODEL), y_s.reshape(db, ds, D_MODEL),
            c_p.reshape(1, nb, t, KV_LORA), kr_p.reshape(1, nb, t, ROPE_DIM), rstd_p.reshape(1, nb, t, N_HEADS),
            sr_p, si_p,
            c_s.reshape(1, db, ds, KV_LORA), kr_s.reshape(1, db, ds, ROPE_DIM), rstd_s.reshape(1, db, ds, N_HEADS),
            s_fin_s[:, :SSM_S].reshape(1, db, SSM_G, SSM_P), s_fin_s[:, SSM_S:].reshape(1, db, SSM_G, SSM_P))
```

```python
import functools
import math

import jax
import jax.numpy as jnp
from jax import lax
from jax.experimental import pallas as pl
from jax.experimental.pallas import tpu as pltpu
from jax.experimental.pallas import tpu_sc as plsc

D_MODEL = 1024
SSM_W = 512
SSM_GC = 16
SSM_G = SSM_W // SSM_GC
SSM_P = 64
SSM_S = SSM_G * SSM_P
N_HEADS = 8
NOPE_DIM = 64
ROPE_DIM = 32
ROPE_HALF = ROPE_DIM // 2
QK_DIM = NOPE_DIM + ROPE_DIM
V_DIM = 64
Q_LORA = 384
KV_LORA = 256
ROPE_THETA = 10000.0
N_EXPERTS = 32
TOP_K = 4
D_FF = D_MODEL
SWIGLU_ALPHA = 1.702
SWIGLU_LIMIT = 7.0
PLE_DIM = 256
PAST_LEN = 16384
PAGE_SIZE = 128
EPS = 1e-6
NEG = -1e30

LANES = 128
SUBLANES = 8
VMEM_LIMIT = 56 * 1024 * 1024

HP = N_HEADS * LANES
BF16 = jnp.bfloat16
F32 = jnp.float32


def _cparams(*sem):
    return pltpu.CompilerParams(dimension_semantics=sem, vmem_limit_bytes=VMEM_LIMIT)


def _rms(x, g):
    r = lax.rsqrt(jnp.mean(x * x, axis=-1, keepdims=True) + EPS)
    return x * r * g


def _dot(a, b):
    return jnp.dot(a, b, preferred_element_type=F32)


def _dot_nt(a, b):
    return lax.dot_general(a, b, (((1,), (1,)), ((), ())), preferred_element_type=F32)


def _full(shape):
    nd = len(shape)
    return pl.BlockSpec(shape, lambda *_: (0,) * nd)


def _proj_kernel(x_ref, ct_ref, st_ref, gmix_ref, win_ref, gql_ref, wuq_ref, gkv_ref, wuk_ref,
                 w2_ref, gq_ref, gk_ref, u_ref, q_ref, k2_ref, *rest, sample):
    if sample:
        c_ref, kr_ref, rstd_ref = rest
    else:
        v2_ref, c_ref, kr_ref, rstd_ref = rest
    xn = _rms(x_ref[...], gmix_ref[...]).astype(BF16)
    z = _dot(xn, win_ref[...])
    u_ref[...] = z[:, :SSM_W]
    o = SSM_W
    cq = z[:, o:o + Q_LORA]
    o += Q_LORA
    ckv = z[:, o:o + KV_LORA]
    o += KV_LORA
    ct = ct_ref[...]
    st = st_ref[...]
    kblock = z[:, o:o + LANES] * ct + z[:, o + LANES:o + 2 * LANES] * st
    kr_ref[...] = kblock[:, NOPE_DIM:QK_DIM]

    qq = _dot(_rms(cq, gql_ref[...]).astype(BF16), wuq_ref[...])
    c = _rms(ckv, gkv_ref[...])
    c_ref[...] = c
    cb = c.astype(BF16)
    kn = _dot(cb, wuk_ref[...])
    if not sample:
        hl = lax.broadcasted_iota(jnp.int32, (1, HP), 1) & (LANES - 1)
        v2_ref[...] = (_dot(cb, w2_ref[...]) + jnp.where(hl >= V_DIM, 1.0, 0.0)).astype(BF16)

    lane = lax.broadcasted_iota(jnp.int32, ct.shape, 1)
    rstd_all = jnp.zeros(ct.shape, F32)
    gq = gq_ref[...]
    gk = gk_ref[...]
    for h in range(N_HEADS):
        sl = slice(h * LANES, (h + 1) * LANES)
        qr = qq[:, sl] * ct + qq[:, HP + h * LANES:HP + (h + 1) * LANES] * st
        qn = qr * lax.rsqrt(jnp.sum(qr * qr, axis=-1, keepdims=True) * (1.0 / QK_DIM) + EPS) * gq
        kh = kn[:, sl] + kblock
        rs = lax.rsqrt(jnp.sum(kh * kh, axis=-1, keepdims=True) * (1.0 / QK_DIM) + EPS)
        rstd_all = jnp.where(lane == h, rs, rstd_all)
        if sample:
            qf = (qn * gk).astype(BF16)
            q_ref[:, sl] = qf
            k2_ref[:, h * KV_LORA:(h + 1) * KV_LORA] = _dot(qf, w2_ref[h]).astype(BF16)
        else:
            q_ref[:, sl] = qn.astype(BF16)
            k2_ref[:, sl] = (kh * rs * gk).astype(BF16)
    rstd_ref[...] = rstd_all[:, :N_HEADS]


def _proj(x2, ct, st, w, *, nb, nt, tm, sample):
    n = x2.shape[0]
    row = lambda b, t: (b * nt + t, 0)
    rows = lambda width: pl.BlockSpec((tm, width), row)
    k2_w = N_HEADS * KV_LORA if sample else HP
    w2 = w["wukT"] if sample else w["wuv"]
    out_shape = [
        jax.ShapeDtypeStruct((n, SSM_W), F32),
        jax.ShapeDtypeStruct((n, HP), BF16),
        jax.ShapeDtypeStruct((n, k2_w), BF16),
        *([] if sample else [jax.ShapeDtypeStruct((n, HP), BF16)]),
        jax.ShapeDtypeStruct((n, KV_LORA), F32),
        jax.ShapeDtypeStruct((n, ROPE_DIM), F32),
        jax.ShapeDtypeStruct((n, N_HEADS), F32),
    ]
    out_specs = [
        rows(SSM_W), rows(HP), rows(k2_w), *([] if sample else [rows(HP)]),
        rows(KV_LORA), rows(ROPE_DIM), rows(N_HEADS),
    ]
    in_specs = [
        rows(D_MODEL),
        pl.BlockSpec((tm, LANES), lambda b, t: (t, 0)),
        pl.BlockSpec((tm, LANES), lambda b, t: (t, 0)),
        _full(w["gmix"].shape), _full(w["win"].shape), _full(w["gql"].shape), _full(w["wuq"].shape),
        _full(w["gkv"].shape), _full(w["wuk"].shape), _full(w2.shape),
        _full(w["gq_s" if sample else "gq_p"].shape), _full(w["gk"].shape),
    ]
    return pl.pallas_call(
        functools.partial(_proj_kernel, sample=sample),
        grid=(nb, nt),
        in_specs=in_specs,
        out_specs=out_specs,
        out_shape=out_shape,
        compiler_params=_cparams("parallel", "parallel"),
        name="proj_s" if sample else "proj_p",
    )(x2, ct, st, w["gmix"], w["win"], w["gql"], w["wuq"], w["gkv"], w["wuk"], w2,
      w["gq_s" if sample else "gq_p"], w["gk"])


S5_HALF = SSM_S // 2
S5_QUARTER = SSM_S // 4


def _s5_kernel(u_ref, h0_ref, are_ref, aim_ref, bw_ref, cw_ref, dsk_ref, wglu_ref, bglu_ref, gout_ref,
               o_ref, sfin_ref, xs_ref, st_ref, *il_ref, steps):
    @pl.when(pl.program_id(1) == 0)
    def _():
        st_ref[...] = h0_ref[...]

    il = il_ref[0] if il_ref else None
    lane_tiles = SSM_W // LANES
    if il is not None:
        for b in range(SUBLANES):
            for c in range(lane_tiles):
                il[c, pl.ds(b, steps, stride=SUBLANES), :] = u_ref[b, :, c * LANES:(c + 1) * LANES]
        u = jnp.concatenate([il[c] for c in range(lane_tiles)], axis=-1)
    else:
        u = u_ref[...]
    ub = u.astype(BF16)
    half_w = SSM_W // 2
    for k in range(2):
        uk = ub[:, k * half_w:(k + 1) * half_w]
        xs_ref[:, k * S5_HALF:(k + 1) * S5_HALF] = _dot(uk, bw_ref[2 * k])
        xs_ref[:, SSM_S + k * S5_HALF:SSM_S + (k + 1) * S5_HALF] = _dot(uk, bw_ref[2 * k + 1])

    for q in range(4):
        lr = slice(q * S5_QUARTER, (q + 1) * S5_QUARTER)
        li = slice(SSM_S + q * S5_QUARTER, SSM_S + (q + 1) * S5_QUARTER)
        ar = are_ref[:, lr]
        ai = aim_ref[:, lr]

        def step(t, carry):
            sr, si = carry
            r0 = pl.multiple_of(t * SUBLANES, SUBLANES)
            nr = ar * sr - ai * si + xs_ref[pl.ds(r0, SUBLANES), lr]
            ni = ar * si + ai * sr + xs_ref[pl.ds(r0, SUBLANES), li]
            xs_ref[pl.ds(r0, SUBLANES), lr] = nr
            xs_ref[pl.ds(r0, SUBLANES), li] = ni
            return nr, ni

        sr, si = lax.fori_loop(0, steps, step, (st_ref[:, lr], st_ref[:, li]), unroll=4)
        st_ref[:, lr] = sr
        st_ref[:, li] = si
    sfin_ref[...] = st_ref[...]

    ys = []
    for k in range(2):
        sre = xs_ref[:, k * S5_HALF:(k + 1) * S5_HALF].astype(BF16)
        sim = xs_ref[:, SSM_S + k * S5_HALF:SSM_S + (k + 1) * S5_HALF].astype(BF16)
        ys.append(_dot(sre, cw_ref[2 * k]) + _dot(sim, cw_ref[2 * k + 1]))
    y = jnp.concatenate(ys, axis=-1) + dsk_ref[...] * u
    g = jax.nn.gelu(y)
    out = g * jax.nn.sigmoid(_dot(g.astype(BF16), wglu_ref[...]) + bglu_ref[...])
    on = _rms(out, gout_ref[...])
    if il is not None:
        for c in range(lane_tiles):
            il[c] = on[:, c * LANES:(c + 1) * LANES]
        for b in range(SUBLANES):
            for c in range(lane_tiles):
                o_ref[b, :, c * LANES:(c + 1) * LANES] = il[c, pl.ds(b, steps, stride=SUBLANES), :].astype(BF16)
    else:
        o_ref[...] = on.astype(BF16)


def _s5(u, h0, w, *, ngroups, nchunks, steps):
    rows = steps * SUBLANES
    batch_major = u.ndim == 3
    if batch_major:
        assert ngroups == 1 and u.shape[0] == SUBLANES
        ublock = pl.BlockSpec((SUBLANES, steps, SSM_W), lambda g, t: (0, t, 0))
    else:
        ublock = pl.BlockSpec((rows, SSM_W), lambda g, t: (g * nchunks + t, 0))
    return pl.pallas_call(
        functools.partial(_s5_kernel, steps=steps),
        grid=(ngroups, nchunks),
        in_specs=[
            ublock,
            pl.BlockSpec((SUBLANES, 2 * SSM_S), lambda g, t: (g, 0)),
            _full(w["s5_are"].shape), _full(w["s5_aim"].shape), _full(w["s5_bw"].shape),
            _full(w["s5_cw"].shape), _full(w["s5_d"].shape), _full(w["wglu"].shape),
            _full(w["bglu"].shape), _full(w["gout_ssm"].shape),
        ],
        out_specs=[
            ublock,
            pl.BlockSpec((SUBLANES, 2 * SSM_S), lambda g, t: (g, 0)),
        ],
        out_shape=[
            jax.ShapeDtypeStruct(u.shape, BF16),
            jax.ShapeDtypeStruct(h0.shape, F32),
        ],
        scratch_shapes=[pltpu.VMEM((rows, 2 * SSM_S), F32), pltpu.VMEM((SUBLANES, 2 * SSM_S), F32)]
        + ([pltpu.VMEM((SSM_W // LANES, rows, LANES), F32)] if batch_major else []),
        compiler_params=_cparams("parallel", "arbitrary"),
        name="s5",
    )(u, h0, w["s5_are"], w["s5_aim"], w["s5_bw"], w["s5_cw"], w["s5_d"], w["wglu"], w["bglu"],
      w["gout_ssm"])


ATTN_P_HEADS = 4


def _attn_p_kernel(q_ref, k_ref, v_ref, o_ref, *, tq, tk):
    i = pl.program_id(2)
    nd = tq // tk
    row = lax.broadcasted_iota(jnp.int32, (tk, tk), 0)
    col = lax.broadcasted_iota(jnp.int32, (tk, tk), 1)

    def block(j, carry, diag):
        k0 = pl.multiple_of(j * tk, tk)
        k2 = k_ref[pl.ds(k0, tk), :]
        v2 = v_ref[pl.ds(k0, tk), :]
        r0 = 0 if diag is None else diag * tk
        out = []
        for hh in range(ATTN_P_HEADS):
            m, acc = carry[hh]
            s = _dot_nt(q_ref[r0:, hh * LANES:(hh + 1) * LANES], k2[:, hh * LANES:(hh + 1) * LANES])
            if diag is not None:
                tri = jnp.where(col <= row, s[:tk], NEG)
                s = tri if r0 + tk == tq else jnp.concatenate([tri, s[tk:]], axis=0)
            m_new = jnp.maximum(m[r0:], jnp.max(s, axis=-1, keepdims=True))
            p = jnp.exp2(s - m_new).astype(BF16)
            acc_new = jnp.exp2(m[r0:] - m_new) * acc[r0:] + _dot(p, v2[:, hh * LANES:(hh + 1) * LANES])
            if r0:
                m_new = jnp.concatenate([m[:r0], m_new], axis=0)
                acc_new = jnp.concatenate([acc[:r0], acc_new], axis=0)
            out.append((m_new, acc_new))
        return tuple(out)

    init = tuple((jnp.full((tq, 1), NEG, F32), jnp.zeros((tq, LANES), F32)) for _ in range(ATTN_P_HEADS))
    carry = lax.fori_loop(0, i * nd, lambda j, c: block(j, c, None), init)
    for d in range(nd):
        carry = block(i * nd + d, carry, d)
    o = [acc / pltpu.roll(acc, V_DIM, 1) for _, acc in carry]
    lane = lax.broadcasted_iota(jnp.int32, (tq, LANES), 1)
    for pr in range(ATTN_P_HEADS // 2):
        o_ref[:, pr * LANES:(pr + 1) * LANES] = jnp.where(
            lane < V_DIM, o[2 * pr], pltpu.roll(o[2 * pr + 1], V_DIM, 1))


def _attn_p(q, k, v, *, nb, t, tq, tk):
    nq = t // tq
    hw = ATTN_P_HEADS * LANES
    return pl.pallas_call(
        functools.partial(_attn_p_kernel, tq=tq, tk=tk),
        grid=(nb, N_HEADS // ATTN_P_HEADS, nq),
        in_specs=[
            pl.BlockSpec((tq, hw), lambda b, h, i: (b * nq + i, h)),
            pl.BlockSpec((t, hw), lambda b, h, i: (b, h)),
            pl.BlockSpec((t, hw), lambda b, h, i: (b, h)),
        ],
        out_specs=pl.BlockSpec((tq, ATTN_P_HEADS * V_DIM), lambda b, h, i: (b * nq + i, h)),
        out_shape=jax.ShapeDtypeStruct((nb * t, N_HEADS * V_DIM), F32),
        compiler_params=_cparams("parallel", "parallel", "arbitrary"),
        name="attn_p",
    )(q, k, v)


ATTN_S_PAGES = 32
NEW_KEYS = 16


def _attn_s_kernel(pt_ref, ql_ref, qr_ref, cn_ref, krn_ref, rnt_ref, cc_hbm, ckr_hbm, cr_hbm,
                   o_ref, cbuf, krbuf, rbuf, cb_ref, sc_ref, sems, *, n_pages, dec_seq):
    b = pl.program_id(0)
    nb = pl.num_programs(0)
    bp = min(ATTN_S_PAGES, n_pages)
    nblk = n_pages // bp
    keys = bp * PAGE_SIZE
    nrow = dec_seq * N_HEADS
    scale = 1.0 / math.sqrt(QK_DIM)
    slot = b % 2

    def start_block(bb, i, dst):
        for p in range(bp):
            pg = i * bp + p
            page = pt_ref[bb * n_pages + pg]
            lanes = pl.ds(pl.multiple_of(pg * PAGE_SIZE, PAGE_SIZE), PAGE_SIZE)
            pltpu.make_async_copy(cc_hbm.at[page], cbuf.at[dst, pg], sems.at[dst, 0]).start()
            pltpu.make_async_copy(ckr_hbm.at[page], krbuf.at[dst, :, lanes], sems.at[dst, 1]).start()
            pltpu.make_async_copy(cr_hbm.at[page], rbuf.at[dst, :, lanes], sems.at[dst, 2]).start()

    @pl.when(b == 0)
    def _():
        lax.fori_loop(0, nblk, lambda i, c: (start_block(0, i, 0), c)[1], 0)

    pltpu.make_async_copy(cbuf.at[slot], cbuf.at[slot], sems.at[slot, 0]).wait()
    pltpu.make_async_copy(krbuf.at[slot], krbuf.at[slot], sems.at[slot, 1]).wait()
    pltpu.make_async_copy(rbuf.at[slot], rbuf.at[slot], sems.at[slot, 2]).wait()

    ql = ql_ref[0]
    qr = qr_ref[0]

    def tile_heads(r_t):
        return jnp.concatenate([r_t] * dec_seq, axis=0)

    cnb = cn_ref[0].astype(BF16)
    s_n = (_dot_nt(ql, cnb) + _dot(qr, krn_ref[0].astype(BF16))) * tile_heads(rnt_ref[0]) * scale
    qs = lax.broadcasted_iota(jnp.int32, (nrow, NEW_KEYS), 0) // N_HEADS
    kj = lax.broadcasted_iota(jnp.int32, (nrow, NEW_KEYS), 1)
    s_n = jnp.where(kj <= qs, s_n, NEG)

    def score_block(i, m):
        @pl.when(b + 1 < nb)
        def _():
            start_block(b + 1, i, 1 - slot)

        k0 = pl.multiple_of(i * keys, keys)
        cb = cbuf[slot, pl.ds(i * bp, bp)].reshape(keys, KV_LORA).astype(BF16)
        cb_ref[pl.ds(k0, keys), :] = cb
        raw = _dot_nt(ql, cb) + _dot(qr, krbuf[slot, :, pl.ds(k0, keys)].astype(BF16))
        s = raw * tile_heads(rbuf[slot, :, pl.ds(k0, keys)]) * scale
        sc_ref[:, pl.ds(k0, keys)] = s
        return jnp.maximum(m, jnp.max(s, axis=-1, keepdims=True))

    m = lax.fori_loop(0, nblk, score_block, jnp.max(s_n, axis=-1, keepdims=True))
    p_n = jnp.exp(s_n - m)

    def value_block(i, carry):
        l, acc = carry
        k0 = pl.multiple_of(i * keys, keys)
        p = jnp.exp(sc_ref[:, pl.ds(k0, keys)] - m)
        return (l + jnp.sum(p, axis=-1, keepdims=True),
                acc + _dot(p.astype(BF16), cb_ref[pl.ds(k0, keys), :]))

    l, acc = lax.fori_loop(0, nblk, value_block,
                           (jnp.sum(p_n, axis=-1, keepdims=True), _dot(p_n.astype(BF16), cnb)))
    o_ref[0] = acc / l


def _attn_s(page_table, ql, qr, cn, krn, rnt, cache_c, cache_kr, cache_r, *, dec_seq):
    nb, n_pages = page_table.shape
    nrow = dec_seq * N_HEADS
    b3 = lambda b, pt: (b, 0, 0)
    return pl.pallas_call(
        functools.partial(_attn_s_kernel, n_pages=n_pages, dec_seq=dec_seq),
        grid_spec=pltpu.PrefetchScalarGridSpec(
            num_scalar_prefetch=1,
            grid=(nb,),
            in_specs=[
                pl.BlockSpec((1, nrow, KV_LORA), b3),
                pl.BlockSpec((1, nrow, ROPE_DIM), b3),
                pl.BlockSpec((1, NEW_KEYS, KV_LORA), b3),
                pl.BlockSpec((1, ROPE_DIM, NEW_KEYS), b3),
                pl.BlockSpec((1, N_HEADS, NEW_KEYS), b3),
                pl.BlockSpec(memory_space=pl.ANY),
                pl.BlockSpec(memory_space=pl.ANY),
                pl.BlockSpec(memory_space=pl.ANY),
            ],
            out_specs=pl.BlockSpec((1, nrow, KV_LORA), b3),
            scratch_shapes=[
                pltpu.VMEM((2, n_pages, PAGE_SIZE, KV_LORA), F32),
                pltpu.VMEM((2, ROPE_DIM, n_pages * PAGE_SIZE), F32),
                pltpu.VMEM((2, N_HEADS, n_pages * PAGE_SIZE), F32),
                pltpu.VMEM((n_pages * PAGE_SIZE, KV_LORA), BF16),
                pltpu.VMEM((nrow, n_pages * PAGE_SIZE), F32),
                pltpu.SemaphoreType.DMA((2, 3)),
            ],
        ),
        out_shape=jax.ShapeDtypeStruct((nb, nrow, KV_LORA), F32),
        compiler_params=_cparams("arbitrary"),
        name="attn_s",
    )(page_table.reshape(-1), ql, qr, cn, krn, rnt, cache_c, cache_kr, cache_r)


def _latent_out_kernel(ol_ref, wuv_ref, o_ref):
    for h in range(N_HEADS):
        o_ref[:, h * V_DIM:(h + 1) * V_DIM] = _dot(
            ol_ref[:, h * KV_LORA:(h + 1) * KV_LORA].astype(BF16), wuv_ref[h])


def _latent_out(ol, wuv_h):
    n = ol.shape[0]
    return pl.pallas_call(
        _latent_out_kernel,
        grid=(1,),
        in_specs=[_full(ol.shape), _full(wuv_h.shape)],
        out_specs=_full((n, N_HEADS * V_DIM)),
        out_shape=jax.ShapeDtypeStruct((n, N_HEADS * V_DIM), F32),
        compiler_params=_cparams("arbitrary"),
        name="latent_out",
    )(ol, wuv_h)


ROUTE_W = 16


def _mix_kernel(h_ref, os_ref, om_ref, gom_ref, wout_ref, gffn_ref, wrt_ref, brt_ref,
                h1_ref, xn_ref, route_ref, routet_ref, cnt_ref):
    @pl.when((pl.program_id(0) == 0) & (pl.program_id(1) == 0))
    def _():
        cnt_ref[...] = jnp.zeros_like(cnt_ref)

    tm = h_ref.shape[0]
    omn = _rms(om_ref[...], gom_ref[...]).astype(BF16)
    h1 = h_ref[...] + _dot(os_ref[...], wout_ref[:SSM_W, :]) + _dot(omn, wout_ref[SSM_W:, :])
    h1_ref[...] = h1
    xn = _rms(h1, gffn_ref[...])
    xn_ref[...] = xn
    work = (_dot_nt(wrt_ref[...], xn.astype(BF16)) + brt_ref[...])[:N_EXPERTS]

    eidx = lax.broadcasted_iota(jnp.int32, work.shape, 0).astype(F32)
    sels, vals, rows = [], [], []
    for k in range(TOP_K):
        mk = jnp.max(work, axis=0, keepdims=True)
        ik = jnp.min(jnp.where(work == mk, eidx, float(N_EXPERTS)), axis=0, keepdims=True)
        sel = eidx == ik
        work = jnp.where(sel, -jnp.inf, work)
        sels.append(sel)
        vals.append(mk)
        rows.append(ik)
    es = [jnp.exp(v - vals[0]) for v in vals]
    den = es[0] + es[1] + es[2] + es[3]
    rows += [e / den for e in es]
    onehot = jnp.zeros(eidx.shape, F32)
    for sel in sels:
        onehot = jnp.where(sel, 1.0, onehot)

    t_j = lax.broadcasted_iota(jnp.int32, (tm, tm), 0)
    t_i = lax.broadcasted_iota(jnp.int32, (tm, tm), 1)
    earlier = jnp.where(t_j < t_i, 1.0, 0.0).astype(BF16)
    cnt = cnt_ref[...]
    cum = _dot(onehot.astype(BF16), earlier) + jnp.concatenate([cnt] * pl.cdiv(tm, LANES), axis=1)[:, :tm]
    rows += [jnp.sum(jnp.where(sel, cum, 0.0), axis=0, keepdims=True) for sel in sels]
    cnt_ref[...] = cnt + jnp.sum(onehot, axis=1, keepdims=True)

    sub = lax.broadcasted_iota(jnp.int32, (ROUTE_W, tm), 0)
    rt = jnp.zeros((ROUTE_W, tm), F32)
    for r, v in enumerate(rows):
        rt = jnp.where(sub == r, v, rt)
    routet_ref[...] = rt
    route_ref[...] = jnp.concatenate([rt, jnp.zeros((LANES - ROUTE_W, tm), F32)], axis=0).T[:, :ROUTE_W]


def _mix(h2, o_ssm, o_mla, w, *, nb, nt, tm):
    n = h2.shape[0]
    row = lambda b, t: (b * nt + t, 0)
    return pl.pallas_call(
        _mix_kernel,
        grid=(nb, nt),
        in_specs=[
            pl.BlockSpec((tm, D_MODEL), row),
            pl.BlockSpec((tm, SSM_W), row),
            pl.BlockSpec((tm, N_HEADS * V_DIM), row),
            _full(w["gout_mla"].shape), _full(w["wout"].shape), _full(w["gffn"].shape),
            _full(w["wrt"].shape), _full(w["brt"].shape),
        ],
        out_specs=[
            pl.BlockSpec((tm, D_MODEL), row),
            pl.BlockSpec((tm, D_MODEL), row),
            pl.BlockSpec((tm, ROUTE_W), row),
            pl.BlockSpec((ROUTE_W, tm), lambda b, t: (0, b * nt + t)),
            _full((N_EXPERTS, LANES)),
        ],
        out_shape=[
            jax.ShapeDtypeStruct((n, D_MODEL), F32),
            jax.ShapeDtypeStruct((n, D_MODEL), F32),
            jax.ShapeDtypeStruct((n, ROUTE_W), F32),
            jax.ShapeDtypeStruct((ROUTE_W, n), F32),
            jax.ShapeDtypeStruct((N_EXPERTS, LANES), F32),
        ],
        compiler_params=_cparams("arbitrary", "arbitrary"),
        name="mix",
    )(h2, o_ssm, o_mla, w["gout_mla"], w["wout"], w["gffn"], w["wrt"], w["brt"])


def _experts_kernel(tile_ref, e_ref, first_ref, valid_ref, lo_ref, hi_ref, newe_ref,
                    x_ref, wgu_ref, bgu_ref, wd_ref, bd_ref, o_ref, wgu_bf, wd_bf):
    i = pl.program_id(0)

    @pl.when(newe_ref[i] == 1)
    def _():
        wgu_bf[...] = wgu_ref[0].astype(BF16)
        wd_bf[...] = wd_ref[0].astype(BF16)

    @pl.when(valid_ref[i] == 1)
    def _():
        bm = x_ref.shape[0]
        hdn = _dot(x_ref[...].astype(BF16), wgu_bf[...]) + bgu_ref[0]
        x_glu = jnp.minimum(hdn[:, :D_FF], SWIGLU_LIMIT)
        x_lin = jnp.clip(hdn[:, D_FF:], -SWIGLU_LIMIT, SWIGLU_LIMIT)
        act = x_glu * jax.nn.sigmoid(SWIGLU_ALPHA * x_glu) * (x_lin + 1.0)
        y = _dot(act.astype(BF16), wd_bf[...]) + bd_ref[0]
        rows = tile_ref[i] * bm + lax.broadcasted_iota(jnp.int32, (bm, 1), 0)
        mine = (rows >= lo_ref[i]) & (rows < hi_ref[i])

        @pl.when(first_ref[i] == 1)
        def _():
            o_ref[...] = jnp.where(mine, y, 0.0)

        @pl.when(first_ref[i] == 0)
        def _():
            o_ref[...] = jnp.where(mine, y, o_ref[...])


def _experts(xs, items, w, *, bm):
    r = xs.shape[0]
    n_items = items[0].shape[0]
    xmap = lambda i, tile, *_: (tile[i], 0)
    emap = lambda i, tile, e, *_: (e[i], 0, 0)
    return pl.pallas_call(
        _experts_kernel,
        grid_spec=pltpu.PrefetchScalarGridSpec(
            num_scalar_prefetch=len(items),
            grid=(n_items,),
            in_specs=[
                pl.BlockSpec((bm, D_MODEL), xmap),
                pl.BlockSpec((1, D_MODEL, 2 * D_FF), emap),
                pl.BlockSpec((1, 1, 2 * D_FF), emap),
                pl.BlockSpec((1, D_FF, D_MODEL), emap),
                pl.BlockSpec((1, 1, D_MODEL), emap),
            ],
            out_specs=pl.BlockSpec((bm, D_MODEL), xmap),
            scratch_shapes=[pltpu.VMEM((D_MODEL, 2 * D_FF), BF16), pltpu.VMEM((D_FF, D_MODEL), BF16)],
        ),
        out_shape=jax.ShapeDtypeStruct((r, D_MODEL), F32),
        compiler_params=_cparams("arbitrary"),
        name="experts",
    )(*items, xs, w["wgu"], w["bgu"], w["wd"], w["bd"])


def _expert_items(counts, n_rows, bm):
    n_tiles = n_rows // bm
    n_items = n_tiles + N_EXPERTS - 1
    ends = jnp.cumsum(counts)
    starts = ends - counts
    first_tile = starts // bm
    per_e = jnp.where(counts > 0, (ends - 1) // bm - first_tile + 1, 0)
    item_end = jnp.cumsum(per_e)
    i = jnp.arange(n_items, dtype=jnp.int32)
    valid = i < item_end[-1]
    e = jnp.sum(jnp.minimum(i, item_end[-1] - 1)[:, None] >= item_end[None, :], axis=1).astype(jnp.int32)
    of_e = e[:, None] == jnp.arange(N_EXPERTS, dtype=jnp.int32)[None, :]
    pick = lambda table: jnp.sum(jnp.where(of_e, table[None, :], 0), axis=1).astype(jnp.int32)
    tile = jnp.where(valid, pick(first_tile) + i - pick(item_end - per_e), n_tiles - 1).astype(jnp.int32)
    one = jnp.ones((1,), jnp.int32)
    first = jnp.concatenate([one, (tile[1:] != tile[:-1]).astype(jnp.int32)])
    new_e = jnp.concatenate([one, (e[1:] != e[:-1]).astype(jnp.int32)])
    return (tile, e, first, valid.astype(jnp.int32), pick(starts), pick(ends), new_e)


def _ple_kernel(h1_ref, yg_ref, route_ref, p_ref, gple_ref, wg_ref, wp_ref, o_ref):
    h2 = h1_ref[...]
    route = route_ref[...]
    w = SC_GATHER_TOKENS
    for k in range(TOP_K):
        yk = yg_ref[:, k * w:(k + 1) * w, :].reshape(h2.shape)
        h2 = h2 + yk * route[:, TOP_K + k:TOP_K + k + 1]
    gate = jax.nn.sigmoid(_dot(_rms(h2, gple_ref[...]).astype(BF16), wg_ref[...]))
    o_ref[...] = h2 + _dot(p_ref[...].astype(BF16), wp_ref[...]) * gate


def _ple(h1, yg, route, p, w, *, tm):
    n = h1.shape[0]
    row = lambda t: (t, 0)
    return pl.pallas_call(
        _ple_kernel,
        grid=(n // tm,),
        in_specs=[
            pl.BlockSpec((tm, D_MODEL), row),
            pl.BlockSpec((tm // SC_GATHER_TOKENS, TOP_K * SC_GATHER_TOKENS, D_MODEL), lambda t: (t, 0, 0)),
            pl.BlockSpec((tm, ROUTE_W), row),
            pl.BlockSpec((tm, PLE_DIM), row),
            _full(w["gple"].shape), _full(w["wpg"].shape), _full(w["wp"].shape),
        ],
        out_specs=pl.BlockSpec((tm, D_MODEL), row),
        out_shape=jax.ShapeDtypeStruct((n, D_MODEL), F32),
        compiler_params=_cparams("parallel"),
        name="ple",
    )(h1, yg, route, p, w["gple"], w["wpg"], w["wp"])


SC_INDEX_ROW = 128
SC_SCATTER_TOKENS = SC_INDEX_ROW // TOP_K
SC_GATHER_TOKENS = 8


def _sc_mesh():
    return plsc.VectorSubcoreMesh(core_axis_name="c", subcore_axis_name="s")


def _window_index(dest_t, w):
    n = dest_t.shape[1]
    idx = dest_t.reshape(TOP_K, n // w, w).transpose(1, 0, 2).reshape(n // w, TOP_K * w)
    if TOP_K * w < SC_INDEX_ROW:
        idx = jnp.concatenate([idx, jnp.zeros((n // w, SC_INDEX_ROW - TOP_K * w), jnp.int32)], axis=-1)
    return idx


def _sc_dispatch(xn, idx):
    n = xn.shape[0]
    w = SC_SCATTER_TOKENS

    @pl.kernel(out_type=jax.ShapeDtypeStruct((n * TOP_K, D_MODEL), xn.dtype),
               mesh=_sc_mesh(), scratch_types=[pltpu.SemaphoreType.DMA])
    def scatter(x_hbm, i_hbm, o_hbm, sem):
        def body(x_vmem, i_vmem):
            copies = [pltpu.make_async_copy(x_vmem, o_hbm.at[i_vmem.at[0, pl.ds(k * w, w)]], sem)
                      for k in range(TOP_K)]
            for cp in copies:
                cp.start()
            for cp in copies:
                cp.wait()

        pltpu.emit_pipeline(
            body,
            grid=(n // w,),
            in_specs=[pl.BlockSpec((w, D_MODEL), lambda i: (i, 0)),
                      pl.BlockSpec((1, SC_INDEX_ROW), lambda i: (i, 0))],
            out_specs=[],
            core_axis_name=("c", "s"),
            dimension_semantics=(pltpu.PARALLEL,),
        )(x_hbm, i_hbm)

    return scatter(xn, idx)


def _sc_combine(ys, idx):
    w = SC_GATHER_TOKENS
    n = idx.shape[0] * w
    rows = TOP_K * w

    @pl.kernel(out_type=jax.ShapeDtypeStruct((n // w, rows, D_MODEL), ys.dtype),
               mesh=_sc_mesh(), scratch_types=[])
    def gather(y_hbm, i_hbm, o_hbm):
        def body(i_vmem, o_vmem):
            pltpu.sync_copy(y_hbm.at[i_vmem.at[0, pl.ds(0, rows)]], o_vmem.at[0])

        pltpu.emit_pipeline(
            body,
            grid=(n // w,),
            in_specs=[pl.BlockSpec((1, SC_INDEX_ROW), lambda i: (i, 0))],
            out_specs=[pl.BlockSpec((1, rows, D_MODEL), lambda i: (i, 0, 0))],
            core_axis_name=("c", "s"),
            dimension_semantics=(pltpu.PARALLEL,),
        )(i_hbm, o_hbm)

    return gather(ys, idx)


def _rope_tables(pos):
    inv = ROPE_THETA ** (-jnp.arange(0, ROPE_DIM, 2, dtype=F32) / ROPE_DIM)
    ang = pos.astype(F32)[:, None] * inv[None, :]
    cos, sin = jnp.cos(ang), jnp.sin(ang)
    n = pos.shape[0]
    pad = jnp.zeros((n, LANES - QK_DIM), F32)
    ct = jnp.concatenate([jnp.ones((n, NOPE_DIM), F32), cos, cos, pad], axis=-1)
    st = jnp.concatenate([jnp.zeros((n, NOPE_DIM), F32), sin, sin, pad], axis=-1)
    return ct, st


def _rope_partner(w_rope):
    return jnp.concatenate([-w_rope[..., ROPE_HALF:], w_rope[..., :ROPE_HALF]], axis=-1)


def _head_tile(nope, rope):
    pad = jnp.zeros(rope.shape[:-1] + (LANES - QK_DIM,), rope.dtype)
    return jnp.concatenate([nope, rope, pad], axis=-1)


def _block_diag(m):
    g, a, b = m.shape
    hg = g // 2
    eye = jnp.eye(hg, dtype=m.dtype)
    mh = m.reshape(2, hg, a, b)
    return jnp.einsum("kgab,gh->kgahb", mh, eye).reshape(2, hg * a, hg * b)


def _prep_weights(g_mix, w_in, s5_a_re, s5_a_im, s5_log_dt, s5_b_re, s5_b_im, s5_c_re, s5_c_im, s5_d,
                  w_glu, b_glu, g_q_lora, w_uq, g_kv_lora, w_uk, w_uv, g_qk_q, g_qk_k, g_out_ssm,
                  g_out_mla, w_out, g_ffn, w_router, b_router, w_gate_up, b_gate_up, w_down, b_down,
                  g_ple, w_ple_gate, w_ple):
    w = {}
    row = lambda v: v.reshape(1, -1).astype(F32)
    o = SSM_W + Q_LORA + KV_LORA
    w_kr = w_in[:, o:o + ROPE_DIM]
    zeros_n = jnp.zeros((D_MODEL, NOPE_DIM), F32)
    w["win"] = jnp.concatenate(
        [w_in[:, :o], _head_tile(zeros_n, w_kr), _head_tile(zeros_n, _rope_partner(w_kr))], axis=-1).astype(BF16)
    w["gmix"] = row(g_mix)
    w["gql"] = row(g_q_lora)
    w["gkv"] = row(g_kv_lora)

    wq = w_uq.reshape(Q_LORA, N_HEADS, QK_DIM)
    q1 = _head_tile(wq[..., :NOPE_DIM], wq[..., NOPE_DIM:])
    q2 = _head_tile(jnp.zeros_like(wq[..., :NOPE_DIM]), _rope_partner(wq[..., NOPE_DIM:]))
    w["wuq"] = jnp.concatenate([q1.reshape(Q_LORA, HP), q2.reshape(Q_LORA, HP)], axis=-1).astype(BF16)
    w["wuk"] = jnp.concatenate(
        [w_uk, jnp.zeros((KV_LORA, N_HEADS, LANES - NOPE_DIM), F32)], axis=-1).reshape(KV_LORA, HP).astype(BF16)
    w["wuv"] = jnp.concatenate(
        [w_uv, jnp.zeros((KV_LORA, N_HEADS, LANES - V_DIM), F32)], axis=-1).reshape(KV_LORA, HP).astype(BF16)
    w["wuv_h"] = w_uv.transpose(1, 0, 2).astype(BF16)
    wukT = w_uk.transpose(1, 2, 0)
    w["wukT"] = jnp.concatenate(
        [wukT, jnp.zeros((N_HEADS, LANES - NOPE_DIM, KV_LORA), F32)], axis=1).astype(BF16)
    gpad = jnp.zeros((LANES - QK_DIM,), F32)
    w["gq_p"] = row(jnp.concatenate([g_qk_q * (math.log2(math.e) / math.sqrt(QK_DIM)), gpad]))
    w["gq_s"] = row(jnp.concatenate([g_qk_q, gpad]))
    w["gk"] = row(jnp.concatenate([g_qk_k, gpad]))

    dt = jnp.exp(s5_log_dt.astype(F32))[:, None]
    ar, ai = s5_a_re.astype(F32), s5_a_im.astype(F32)
    mag = jnp.exp(dt * ar)
    abar_re, abar_im = mag * jnp.cos(dt * ai), mag * jnp.sin(dt * ai)
    den = ar * ar + ai * ai
    nr, ni = abar_re - 1.0, abar_im
    coef_re = (nr * ar + ni * ai) / den
    coef_im = (ni * ar - nr * ai) / den
    br, bi = s5_b_re.astype(F32), s5_b_im.astype(F32)
    bbar_re = coef_re[..., None] * br - coef_im[..., None] * bi
    bbar_im = coef_re[..., None] * bi + coef_im[..., None] * br
    bre = _block_diag(bbar_re.transpose(0, 2, 1))
    bim = _block_diag(bbar_im.transpose(0, 2, 1))
    w["s5_bw"] = jnp.stack([bre[0], bim[0], bre[1], bim[1]]).astype(BF16)
    cre = _block_diag(s5_c_re.astype(F32).transpose(0, 2, 1))
    cim = _block_diag(-s5_c_im.astype(F32).transpose(0, 2, 1))
    w["s5_cw"] = jnp.stack([cre[0], cim[0], cre[1], cim[1]]).astype(BF16)
    w["s5_are"] = jnp.broadcast_to(abar_re.reshape(1, SSM_S), (SUBLANES, SSM_S))
    w["s5_aim"] = jnp.broadcast_to(abar_im.reshape(1, SSM_S), (SUBLANES, SSM_S))
    w["s5_d"] = row(s5_d)
    w["wglu"] = w_glu.astype(BF16)
    w["bglu"] = row(b_glu)
    w["gout_ssm"] = row(g_out_ssm)
    w["gout_mla"] = row(g_out_mla)
    w["wout"] = w_out.astype(BF16)
    w["gffn"] = row(g_ffn)
    w["wrt"] = jnp.concatenate([w_router.T, jnp.zeros((LANES - N_EXPERTS, D_MODEL), F32)], axis=0).astype(BF16)
    w["brt"] = jnp.concatenate([b_router.astype(F32), jnp.zeros((LANES - N_EXPERTS,), F32)]).reshape(LANES, 1)
    w["wgu"] = w_gate_up
    w["bgu"] = b_gate_up.reshape(N_EXPERTS, 1, 2 * D_FF).astype(F32)
    w["wd"] = w_down
    w["bd"] = b_down.reshape(N_EXPERTS, 1, D_MODEL).astype(F32)
    w["gple"] = row(g_ple)
    w["wpg"] = w_ple_gate.astype(BF16)
    w["wp"] = w_ple.astype(BF16)
    return w


EXPERT_ROWS = 512


def _moe_experts(mixed, w, during_dispatch=()):
    h1, xn, _, route_t, cnt = mixed
    n = h1.shape[0]
    top_i = route_t[:TOP_K].astype(jnp.int32)
    rank = route_t[2 * TOP_K:3 * TOP_K].astype(jnp.int32)
    counts = cnt[:, 0].astype(jnp.int32)
    starts = jnp.cumsum(counts) - counts
    dest = rank
    for e in range(N_EXPERTS):
        dest = dest + jnp.where(top_i == e, starts[e], 0)
    gather_idx = _window_index(dest, SC_GATHER_TOKENS)
    xs = _after(_sc_dispatch(xn, _window_index(dest, SC_SCATTER_TOKENS)), (gather_idx, during_dispatch))
    bm = min(EXPERT_ROWS, max(LANES, n * TOP_K // N_EXPERTS))
    return _experts(xs, _expert_items(counts, n * TOP_K, bm), w, bm=bm), gather_idx


def _moe_combine(mixed, ys, gather_idx, p2, w, *, tm):
    h1, _, route, _, _ = mixed
    return _ple(h1, _sc_combine(ys, gather_idx), route, p2, w, tm=tm)


def _after(x, anchor):
    return lax.optimization_barrier((x, anchor))[0]


def kernel(x_prompt, x_sample, cache_kv_latent, cache_k_rope, cache_k_rstd, state_ssm_re, state_ssm_im,
           page_table, p_prompt, p_sample, g_mix, w_in, s5_a_re, s5_a_im, s5_log_dt, s5_b_re, s5_b_im,
           s5_c_re, s5_c_im, s5_d, w_glu, b_glu, g_q_lora, w_uq, g_kv_lora, w_uk, w_uv, g_qk_q, g_qk_k,
           g_out_ssm, g_out_mla, w_out, g_ffn, w_router, b_router, w_gate_up, b_gate_up, w_down, b_down,
           g_ple, w_ple_gate, w_ple):
    assert g_mix.shape[0] == 1, "single-layer step"
    nb, t, _ = x_prompt.shape
    db, ds, _ = x_sample.shape
    assert nb == SUBLANES and db % SUBLANES == 0
    w = _prep_weights(g_mix[0], w_in[0], s5_a_re[0], s5_a_im[0], s5_log_dt[0], s5_b_re[0], s5_b_im[0],
                      s5_c_re[0], s5_c_im[0], s5_d[0], w_glu[0], b_glu[0], g_q_lora[0], w_uq[0],
                      g_kv_lora[0], w_uk[0], w_uv[0], g_qk_q[0], g_qk_k[0], g_out_ssm[0], g_out_mla[0],
                      w_out[0], g_ffn[0], w_router[0], b_router[0], w_gate_up[0], b_gate_up[0],
                      w_down[0], b_down[0], g_ple[0], w_ple_gate[0], w_ple[0])

    tm = min(512, t)
    nt = t // tm
    xp = x_prompt.reshape(nb * t, D_MODEL)
    ct, st = _rope_tables(jnp.arange(t))
    u, q, k, v, c_p, kr_p, rstd_p = _proj(xp, ct, st, w, nb=nb, nt=nt, tm=tm, sample=False)
    steps = min(128, t)
    o_ssm, s_fin = _s5(u.reshape(nb, t, SSM_W), jnp.zeros((nb, 2 * SSM_S), F32), w,
                       ngroups=1, nchunks=t // steps, steps=steps)
    o_mla = _attn_p(q, k, v, nb=nb, t=t, tq=min(1024, t), tk=min(512, t))
    mixed_p = _mix(xp, o_ssm.reshape(nb * t, SSM_W), o_mla, w, nb=nb, nt=nt, tm=tm)
    sr_p = s_fin[:, :SSM_S].reshape(1, nb, SSM_G, SSM_P)
    si_p = s_fin[:, SSM_S:].reshape(1, nb, SSM_G, SSM_P)

    ns = db * ds
    ng = db // SUBLANES
    xs_ = _after(x_sample.reshape(ns, D_MODEL), mixed_p[2])
    ct_s, st_s = _rope_tables(PAST_LEN + jnp.tile(jnp.arange(ds), db))
    u_s, qf, qlat, c_s, kr_s, rstd_s = _proj(xs_, ct_s, st_s, w, nb=1, nt=1, tm=ns, sample=True)
    to_scan = lambda a: a.reshape(ng, SUBLANES, ds, -1).transpose(0, 2, 1, 3).reshape(ns, -1)
    from_scan = lambda a: a.reshape(ng, ds, SUBLANES, -1).transpose(0, 2, 1, 3).reshape(ns, -1)
    h0 = jnp.concatenate([state_ssm_re[0].reshape(db, SSM_S), state_ssm_im[0].reshape(db, SSM_S)],
                         axis=-1).astype(F32)
    o_ssm_s, s_fin_s = _s5(to_scan(u_s), h0, w, ngroups=ng, nchunks=1, steps=ds)
    pad_keys = lambda a: jnp.concatenate(
        [a, jnp.zeros((db, NEW_KEYS - ds, a.shape[-1]), a.dtype)], axis=1)
    key_major = lambda a: a.transpose(0, 2, 1)
    attn_in = (page_table,
               qlat.reshape(db, ds * N_HEADS, KV_LORA),
               qf.reshape(db, ds * N_HEADS, LANES)[:, :, NOPE_DIM:QK_DIM],
               pad_keys(c_s.reshape(db, ds, KV_LORA)),
               key_major(pad_keys(kr_s.reshape(db, ds, ROPE_DIM))),
               key_major(pad_keys(rstd_s.reshape(db, ds, N_HEADS))))
    caches = (cache_kv_latent[0], key_major(cache_k_rope[0]), key_major(cache_k_rstd[0]))

    def attn_rows(lo, hi, anchor=None):
        pt, *rest = (a[lo:hi] for a in attn_in)
        if anchor is not None:
            pt = _after(pt, anchor)
        return _attn_s(pt, *rest, *caches, dec_seq=ds)

    o_lat_a = attn_rows(0, db // 2)
    ys_p, dest_p = _moe_experts(mixed_p, w, during_dispatch=o_lat_a)
    o_lat_b = attn_rows(db // 2, db, anchor=ys_p)
    o_lat = jnp.concatenate([o_lat_a, o_lat_b], axis=0)
    o_mla_s = _latent_out(o_lat.reshape(ns, N_HEADS * KV_LORA), w["wuv_h"])

    y_p = _moe_combine(mixed_p, ys_p, dest_p, p_prompt[0].reshape(nb * t, PLE_DIM), w, tm=tm)
    mixed_s = _mix(xs_, from_scan(o_ssm_s), _after(o_mla_s, y_p), w, nb=1, nt=1, tm=ns)
    ys_s, dest_s = _moe_experts(mixed_s, w)
    y_s = _moe_combine(mixed_s, ys_s, dest_s, p_sample[0].reshape(ns, PLE_DIM), w, tm=ns)

    return (y_p.reshape(nb, t, D_MODEL), y_s.reshape(db, ds, D_MODEL),
            c_p.reshape(1, nb, t, KV_LORA), kr_p.reshape(1, nb, t, ROPE_DIM), rstd_p.reshape(1, nb, t, N_HEADS),
            sr_p, si_p,
            c_s.reshape(1, db, ds, KV_LORA), kr_s.reshape(1, db, ds, ROPE_DIM), rstd_s.reshape(1, db, ds, N_HEADS),
            s_fin_s[:, :SSM_S].reshape(1, db, SSM_G, SSM_P), s_fin_s[:, SSM_S:].reshape(1, db, SSM_G, SSM_P))
```

```python
import functools
import math

import jax
import jax.numpy as jnp
from jax import lax
from jax.experimental import pallas as pl
from jax.experimental.pallas import tpu as pltpu
from jax.experimental.pallas import tpu_sc as plsc

D_MODEL = 1024
SSM_W = 512
SSM_GC = 16
SSM_G = SSM_W // SSM_GC
SSM_P = 64
SSM_S = SSM_G * SSM_P
N_HEADS = 8
NOPE_DIM = 64
ROPE_DIM = 32
ROPE_HALF = ROPE_DIM // 2
QK_DIM = NOPE_DIM + ROPE_DIM
V_DIM = 64
Q_LORA = 384
KV_LORA = 256
ROPE_THETA = 10000.0
N_EXPERTS = 32
TOP_K = 4
D_FF = D_MODEL
SWIGLU_ALPHA = 1.702
SWIGLU_LIMIT = 7.0
PLE_DIM = 256
PAST_LEN = 16384
PAGE_SIZE = 128
EPS = 1e-6
NEG = -1e30

LANES = 128
SUBLANES = 8
VMEM_LIMIT = 56 * 1024 * 1024

HP = N_HEADS * LANES
BF16 = jnp.bfloat16
F32 = jnp.float32


def _cparams(*sem):
    return pltpu.CompilerParams(dimension_semantics=sem, vmem_limit_bytes=VMEM_LIMIT)


def _rms(x, g):
    r = lax.rsqrt(jnp.mean(x * x, axis=-1, keepdims=True) + EPS)
    return x * r * g


def _dot(a, b):
    return jnp.dot(a, b, preferred_element_type=F32)


def _dot_nt(a, b):
    return lax.dot_general(a, b, (((1,), (1,)), ((), ())), preferred_element_type=F32)


def _full(shape):
    nd = len(shape)
    return pl.BlockSpec(shape, lambda *_: (0,) * nd)


def _proj_kernel(x_ref, ct_ref, st_ref, gmix_ref, win_ref, gql_ref, wuq_ref, gkv_ref, wuk_ref,
                 w2_ref, gq_ref, gk_ref, u_ref, q_ref, k2_ref, *rest, sample):
    if sample:
        c_ref, kr_ref, rstd_ref = rest
    else:
        v2_ref, c_ref, kr_ref, rstd_ref = rest
    xn = _rms(x_ref[...], gmix_ref[...]).astype(BF16)
    z = _dot(xn, win_ref[...])
    u_ref[...] = z[:, :SSM_W]
    o = SSM_W
    cq = z[:, o:o + Q_LORA]
    o += Q_LORA
    ckv = z[:, o:o + KV_LORA]
    o += KV_LORA
    ct = ct_ref[...]
    st = st_ref[...]
    kblock = z[:, o:o + LANES] * ct + z[:, o + LANES:o + 2 * LANES] * st
    kr_ref[...] = kblock[:, NOPE_DIM:QK_DIM]

    qq = _dot(_rms(cq, gql_ref[...]).astype(BF16), wuq_ref[...])
    c = _rms(ckv, gkv_ref[...])
    c_ref[...] = c
    cb = c.astype(BF16)
    kn = _dot(cb, wuk_ref[...])
    if not sample:
        hl = lax.broadcasted_iota(jnp.int32, (1, HP), 1) & (LANES - 1)
        v2_ref[...] = (_dot(cb, w2_ref[...]) + jnp.where(hl >= V_DIM, 1.0, 0.0)).astype(BF16)

    lane = lax.broadcasted_iota(jnp.int32, ct.shape, 1)
    rstd_all = jnp.zeros(ct.shape, F32)
    gq = gq_ref[...]
    gk = gk_ref[...]
    for h in range(N_HEADS):
        sl = slice(h * LANES, (h + 1) * LANES)
        qr = qq[:, sl] * ct + qq[:, HP + h * LANES:HP + (h + 1) * LANES] * st
        qn = qr * lax.rsqrt(jnp.sum(qr * qr, axis=-1, keepdims=True) * (1.0 / QK_DIM) + EPS) * gq
        kh = kn[:, sl] + kblock
        rs = lax.rsqrt(jnp.sum(kh * kh, axis=-1, keepdims=True) * (1.0 / QK_DIM) + EPS)
        rstd_all = jnp.where(lane == h, rs, rstd_all)
        if sample:
            qf = (qn * gk).astype(BF16)
            q_ref[:, sl] = qf
            k2_ref[:, h * KV_LORA:(h + 1) * KV_LORA] = _dot(qf, w2_ref[h]).astype(BF16)
        else:
            q_ref[:, sl] = qn.astype(BF16)
            k2_ref[:, sl] = (kh * rs * gk).astype(BF16)
    rstd_ref[...] = rstd_all[:, :N_HEADS]


def _proj(x2, ct, st, w, *, nb, nt, tm, sample):
    n = x2.shape[0]
    row = lambda b, t: (b * nt + t, 0)
    rows = lambda width: pl.BlockSpec((tm, width), row)
    k2_w = N_HEADS * KV_LORA if sample else HP
    w2 = w["wukT"] if sample else w["wuv"]
    out_shape = [
        jax.ShapeDtypeStruct((n, SSM_W), F32),
        jax.ShapeDtypeStruct((n, HP), BF16),
        jax.ShapeDtypeStruct((n, k2_w), BF16),
        *([] if sample else [jax.ShapeDtypeStruct((n, HP), BF16)]),
        jax.ShapeDtypeStruct((n, KV_LORA), F32),
        jax.ShapeDtypeStruct((n, ROPE_DIM), F32),
        jax.ShapeDtypeStruct((n, N_HEADS), F32),
    ]
    out_specs = [
        rows(SSM_W), rows(HP), rows(k2_w), *([] if sample else [rows(HP)]),
        rows(KV_LORA), rows(ROPE_DIM), rows(N_HEADS),
    ]
    in_specs = [
        rows(D_MODEL),
        pl.BlockSpec((tm, LANES), lambda b, t: (t, 0)),
        pl.BlockSpec((tm, LANES), lambda b, t: (t, 0)),
        _full(w["gmix"].shape), _full(w["win"].shape), _full(w["gql"].shape), _full(w["wuq"].shape),
        _full(w["gkv"].shape), _full(w["wuk"].shape), _full(w2.shape),
        _full(w["gq_s" if sample else "gq_p"].shape), _full(w["gk"].shape),
    ]
    return pl.pallas_call(
        functools.partial(_proj_kernel, sample=sample),
        grid=(nb, nt),
        in_specs=in_specs,
        out_specs=out_specs,
        out_shape=out_shape,
        compiler_params=_cparams("parallel", "parallel"),
        name="proj_s" if sample else "proj_p",
    )(x2, ct, st, w["gmix"], w["win"], w["gql"], w["wuq"], w["gkv"], w["wuk"], w2,
      w["gq_s" if sample else "gq_p"], w["gk"])


S5_HALF = SSM_S // 2
S5_QUARTER = SSM_S // 4


def _s5_kernel(u_ref, h0_ref, are_ref, aim_ref, bw_ref, cw_ref, dsk_ref, wglu_ref, bglu_ref, gout_ref,
               o_ref, sfin_ref, xs_ref, st_ref, *il_ref, steps):
    @pl.when(pl.program_id(1) == 0)
    def _():
        st_ref[...] = h0_ref[...]

    il = il_ref[0] if il_ref else None
    lane_tiles = SSM_W // LANES
    if il is not None:
        for b in range(SUBLANES):
            for c in range(lane_tiles):
                il[c, pl.ds(b, steps, stride=SUBLANES), :] = u_ref[b, :, c * LANES:(c + 1) * LANES]
        u = jnp.concatenate([il[c] for c in range(lane_tiles)], axis=-1)
    else:
        u = u_ref[...]
    ub = u.astype(BF16)
    half_w = SSM_W // 2
    for k in range(2):
        uk = ub[:, k * half_w:(k + 1) * half_w]
        xs_ref[:, k * S5_HALF:(k + 1) * S5_HALF] = _dot(uk, bw_ref[2 * k])
        xs_ref[:, SSM_S + k * S5_HALF:SSM_S + (k + 1) * S5_HALF] = _dot(uk, bw_ref[2 * k + 1])

    for q in range(4):
        lr = slice(q * S5_QUARTER, (q + 1) * S5_QUARTER)
        li = slice(SSM_S + q * S5_QUARTER, SSM_S + (q + 1) * S5_QUARTER)
        ar = are_ref[:, lr]
        ai = aim_ref[:, lr]

        def step(t, carry):
            sr, si = carry
            r0 = pl.multiple_of(t * SUBLANES, SUBLANES)
            nr = ar * sr - ai * si + xs_ref[pl.ds(r0, SUBLANES), lr]
            ni = ar * si + ai * sr + xs_ref[pl.ds(r0, SUBLANES), li]
            xs_ref[pl.ds(r0, SUBLANES), lr] = nr
            xs_ref[pl.ds(r0, SUBLANES), li] = ni
            return nr, ni

        sr, si = lax.fori_loop(0, steps, step, (st_ref[:, lr], st_ref[:, li]), unroll=4)
        st_ref[:, lr] = sr
        st_ref[:, li] = si
    sfin_ref[...] = st_ref[...]

    ys = []
    for k in range(2):
        sre = xs_ref[:, k * S5_HALF:(k + 1) * S5_HALF].astype(BF16)
        sim = xs_ref[:, SSM_S + k * S5_HALF:SSM_S + (k + 1) * S5_HALF].astype(BF16)
        ys.append(_dot(sre, cw_ref[2 * k]) + _dot(sim, cw_ref[2 * k + 1]))
    y = jnp.concatenate(ys, axis=-1) + dsk_ref[...] * u
    g = jax.nn.gelu(y)
    out = g * jax.nn.sigmoid(_dot(g.astype(BF16), wglu_ref[...]) + bglu_ref[...])
    on = _rms(out, gout_ref[...])
    if il is not None:
        for c in range(lane_tiles):
            il[c] = on[:, c * LANES:(c + 1) * LANES]
        for b in range(SUBLANES):
            for c in range(lane_tiles):
                o_ref[b, :, c * LANES:(c + 1) * LANES] = il[c, pl.ds(b, steps, stride=SUBLANES), :].astype(BF16)
    else:
        o_ref[...] = on.astype(BF16)


def _s5(u, h0, w, *, ngroups, nchunks, steps):
    rows = steps * SUBLANES
    batch_major = u.ndim == 3
    if batch_major:
        assert ngroups == 1 and u.shape[0] == SUBLANES
        ublock = pl.BlockSpec((SUBLANES, steps, SSM_W), lambda g, t: (0, t, 0))
    else:
        ublock = pl.BlockSpec((rows, SSM_W), lambda g, t: (g * nchunks + t, 0))
    return pl.pallas_call(
        functools.partial(_s5_kernel, steps=steps),
        grid=(ngroups, nchunks),
        in_specs=[
            ublock,
            pl.BlockSpec((SUBLANES, 2 * SSM_S), lambda g, t: (g, 0)),
            _full(w["s5_are"].shape), _full(w["s5_aim"].shape), _full(w["s5_bw"].shape),
            _full(w["s5_cw"].shape), _full(w["s5_d"].shape), _full(w["wglu"].shape),
            _full(w["bglu"].shape), _full(w["gout_ssm"].shape),
        ],
        out_specs=[
            ublock,
            pl.BlockSpec((SUBLANES, 2 * SSM_S), lambda g, t: (g, 0)),
        ],
        out_shape=[
            jax.ShapeDtypeStruct(u.shape, BF16),
            jax.ShapeDtypeStruct(h0.shape, F32),
        ],
        scratch_shapes=[pltpu.VMEM((rows, 2 * SSM_S), F32), pltpu.VMEM((SUBLANES, 2 * SSM_S), F32)]
        + ([pltpu.VMEM((SSM_W // LANES, rows, LANES), F32)] if batch_major else []),
        compiler_params=_cparams("parallel", "arbitrary"),
        name="s5",
    )(u, h0, w["s5_are"], w["s5_aim"], w["s5_bw"], w["s5_cw"], w["s5_d"], w["wglu"], w["bglu"],
      w["gout_ssm"])


ATTN_P_HEADS = 4


def _attn_p_kernel(q_ref, k_ref, v_ref, o_ref, *, tq, tk):
    i = pl.program_id(2)
    nd = tq // tk
    row = lax.broadcasted_iota(jnp.int32, (tk, tk), 0)
    col = lax.broadcasted_iota(jnp.int32, (tk, tk), 1)

    def block(j, carry, diag):
        k0 = pl.multiple_of(j * tk, tk)
        k2 = k_ref[pl.ds(k0, tk), :]
        v2 = v_ref[pl.ds(k0, tk), :]
        r0 = 0 if diag is None else diag * tk
        out = []
        for hh in range(ATTN_P_HEADS):
            m, acc = carry[hh]
            s = _dot_nt(q_ref[r0:, hh * LANES:(hh + 1) * LANES], k2[:, hh * LANES:(hh + 1) * LANES])
            if diag is not None:
                tri = jnp.where(col <= row, s[:tk], NEG)
                s = tri if r0 + tk == tq else jnp.concatenate([tri, s[tk:]], axis=0)
            m_new = jnp.maximum(m[r0:], jnp.max(s, axis=-1, keepdims=True))
            p = jnp.exp2(s - m_new).astype(BF16)
            acc_new = jnp.exp2(m[r0:] - m_new) * acc[r0:] + _dot(p, v2[:, hh * LANES:(hh + 1) * LANES])
            if r0:
                m_new = jnp.concatenate([m[:r0], m_new], axis=0)
                acc_new = jnp.concatenate([acc[:r0], acc_new], axis=0)
            out.append((m_new, acc_new))
        return tuple(out)

    init = tuple((jnp.full((tq, 1), NEG, F32), jnp.zeros((tq, LANES), F32)) for _ in range(ATTN_P_HEADS))
    carry = lax.fori_loop(0, i * nd, lambda j, c: block(j, c, None), init)
    for d in range(nd):
        carry = block(i * nd + d, carry, d)
    o = [acc / pltpu.roll(acc, V_DIM, 1) for _, acc in carry]
    lane = lax.broadcasted_iota(jnp.int32, (tq, LANES), 1)
    for pr in range(ATTN_P_HEADS // 2):
        o_ref[:, pr * LANES:(pr + 1) * LANES] = jnp.where(
            lane < V_DIM, o[2 * pr], pltpu.roll(o[2 * pr + 1], V_DIM, 1))


def _attn_p(q, k, v, *, nb, t, tq, tk):
    nq = t // tq
    hw = ATTN_P_HEADS * LANES
    return pl.pallas_call(
        functools.partial(_attn_p_kernel, tq=tq, tk=tk),
        grid=(nb, N_HEADS // ATTN_P_HEADS, nq),
        in_specs=[
            pl.BlockSpec((tq, hw), lambda b, h, i: (b * nq + i, h)),
            pl.BlockSpec((t, hw), lambda b, h, i: (b, h)),
            pl.BlockSpec((t, hw), lambda b, h, i: (b, h)),
        ],
        out_specs=pl.BlockSpec((tq, ATTN_P_HEADS * V_DIM), lambda b, h, i: (b * nq + i, h)),
        out_shape=jax.ShapeDtypeStruct((nb * t, N_HEADS * V_DIM), F32),
        compiler_params=_cparams("parallel", "parallel", "arbitrary"),
        name="attn_p",
    )(q, k, v)


ATTN_S_PAGES = 32
NEW_KEYS = 16


def _attn_s_kernel(pt_ref, ql_ref, qr_ref, cn_ref, krn_ref, rnt_ref, cc_hbm, ckr_hbm, cr_hbm,
                   o_ref, cbuf, krbuf, rbuf, cb_ref, sc_ref, sems, *, n_pages, dec_seq):
    b = pl.program_id(0)
    nb = pl.num_programs(0)
    bp = min(ATTN_S_PAGES, n_pages)
    nblk = n_pages // bp
    keys = bp * PAGE_SIZE
    nrow = dec_seq * N_HEADS
    scale = 1.0 / math.sqrt(QK_DIM)
    slot = b % 2

    def start_block(bb, i, dst):
        for p in range(bp):
            pg = i * bp + p
            page = pt_ref[bb * n_pages + pg]
            lanes = pl.ds(pl.multiple_of(pg * PAGE_SIZE, PAGE_SIZE), PAGE_SIZE)
            pltpu.make_async_copy(cc_hbm.at[page], cbuf.at[dst, pg], sems.at[dst, 0]).start()
            pltpu.make_async_copy(ckr_hbm.at[page], krbuf.at[dst, :, lanes], sems.at[dst, 1]).start()
            pltpu.make_async_copy(cr_hbm.at[page], rbuf.at[dst, :, lanes], sems.at[dst, 2]).start()

    @pl.when(b == 0)
    def _():
        lax.fori_loop(0, nblk, lambda i, c: (start_block(0, i, 0), c)[1], 0)

    pltpu.make_async_copy(cbuf.at[slot], cbuf.at[slot], sems.at[slot, 0]).wait()
    pltpu.make_async_copy(krbuf.at[slot], krbuf.at[slot], sems.at[slot, 1]).wait()
    pltpu.make_async_copy(rbuf.at[slot], rbuf.at[slot], sems.at[slot, 2]).wait()

    ql = ql_ref[0]
    qr = qr_ref[0]

    def tile_heads(r_t):
        return jnp.concatenate([r_t] * dec_seq, axis=0)

    cnb = cn_ref[0].astype(BF16)
    s_n = (_dot_nt(ql, cnb) + _dot(qr, krn_ref[0].astype(BF16))) * tile_heads(rnt_ref[0]) * scale
    qs = lax.broadcasted_iota(jnp.int32, (nrow, NEW_KEYS), 0) // N_HEADS
    kj = lax.broadcasted_iota(jnp.int32, (nrow, NEW_KEYS), 1)
    s_n = jnp.where(kj <= qs, s_n, NEG)

    def score_block(i, m):
        @pl.when(b + 1 < nb)
        def _():
            start_block(b + 1, i, 1 - slot)

        k0 = pl.multiple_of(i * keys, keys)
        cb = cbuf[slot, pl.ds(i * bp, bp)].reshape(keys, KV_LORA).astype(BF16)
        cb_ref[pl.ds(k0, keys), :] = cb
        raw = _dot_nt(ql, cb) + _dot(qr, krbuf[slot, :, pl.ds(k0, keys)].astype(BF16))
        s = raw * tile_heads(rbuf[slot, :, pl.ds(k0, keys)]) * scale
        sc_ref[:, pl.ds(k0, keys)] = s
        return jnp.maximum(m, jnp.max(s, axis=-1, keepdims=True))

    m = lax.fori_loop(0, nblk, score_block, jnp.max(s_n, axis=-1, keepdims=True))
    p_n = jnp.exp(s_n - m)

    def value_block(i, carry):
        l, acc = carry
        k0 = pl.multiple_of(i * keys, keys)
        p = jnp.exp(sc_ref[:, pl.ds(k0, keys)] - m)
        return (l + jnp.sum(p, axis=-1, keepdims=True),
                acc + _dot(p.astype(BF16), cb_ref[pl.ds(k0, keys), :]))

    l, acc = lax.fori_loop(0, nblk, value_block,
                           (jnp.sum(p_n, axis=-1, keepdims=True), _dot(p_n.astype(BF16), cnb)))
    o_ref[0] = acc / l


def _attn_s(page_table, ql, qr, cn, krn, rnt, cache_c, cache_kr, cache_r, *, dec_seq):
    nb, n_pages = page_table.shape
    nrow = dec_seq * N_HEADS
    b3 = lambda b, pt: (b, 0, 0)
    return pl.pallas_call(
        functools.partial(_attn_s_kernel, n_pages=n_pages, dec_seq=dec_seq),
        grid_spec=pltpu.PrefetchScalarGridSpec(
            num_scalar_prefetch=1,
            grid=(nb,),
            in_specs=[
                pl.BlockSpec((1, nrow, KV_LORA), b3),
                pl.BlockSpec((1, nrow, ROPE_DIM), b3),
                pl.BlockSpec((1, NEW_KEYS, KV_LORA), b3),
                pl.BlockSpec((1, ROPE_DIM, NEW_KEYS), b3),
                pl.BlockSpec((1, N_HEADS, NEW_KEYS), b3),
                pl.BlockSpec(memory_space=pl.ANY),
                pl.BlockSpec(memory_space=pl.ANY),
                pl.BlockSpec(memory_space=pl.ANY),
            ],
            out_specs=pl.BlockSpec((1, nrow, KV_LORA), b3),
            scratch_shapes=[
                pltpu.VMEM((2, n_pages, PAGE_SIZE, KV_LORA), F32),
                pltpu.VMEM((2, ROPE_DIM, n_pages * PAGE_SIZE), F32),
                pltpu.VMEM((2, N_HEADS, n_pages * PAGE_SIZE), F32),
                pltpu.VMEM((n_pages * PAGE_SIZE, KV_LORA), BF16),
                pltpu.VMEM((nrow, n_pages * PAGE_SIZE), F32),
                pltpu.SemaphoreType.DMA((2, 3)),
            ],
        ),
        out_shape=jax.ShapeDtypeStruct((nb, nrow, KV_LORA), F32),
        compiler_params=_cparams("arbitrary"),
        name="attn_s",
    )(page_table.reshape(-1), ql, qr, cn, krn, rnt, cache_c, cache_kr, cache_r)


def _latent_out_kernel(ol_ref, wuv_ref, o_ref):
    for h in range(N_HEADS):
        o_ref[:, h * V_DIM:(h + 1) * V_DIM] = _dot(
            ol_ref[:, h * KV_LORA:(h + 1) * KV_LORA].astype(BF16), wuv_ref[h])


def _latent_out(ol, wuv_h):
    n = ol.shape[0]
    return pl.pallas_call(
        _latent_out_kernel,
        grid=(1,),
        in_specs=[_full(ol.shape), _full(wuv_h.shape)],
        out_specs=_full((n, N_HEADS * V_DIM)),
        out_shape=jax.ShapeDtypeStruct((n, N_HEADS * V_DIM), F32),
        compiler_params=_cparams("arbitrary"),
        name="latent_out",
    )(ol, wuv_h)


ROUTE_W = 16
PACKED_W = D_MODEL // 2


def _pack_bf16_halves(xb):
    bits = lambda v: pltpu.bitcast(v.astype(F32), jnp.uint32)
    return (bits(xb[:, :PACKED_W]) >> 16) | bits(xb[:, PACKED_W:])


def _unpack_bf16_halves(words):
    lo = pltpu.bitcast(words << 16, F32)
    hi = pltpu.bitcast(words & jnp.uint32(0xFFFF0000), F32)
    return jnp.concatenate([lo, hi], axis=-1).astype(BF16)


def _mix_kernel(h_ref, os_ref, om_ref, gom_ref, wout_ref, gffn_ref, wrt_ref, brt_ref,
                h1_ref, xn_ref, route_ref, routet_ref, cnt_ref):
    @pl.when((pl.program_id(0) == 0) & (pl.program_id(1) == 0))
    def _():
        cnt_ref[...] = jnp.zeros_like(cnt_ref)

    tm = h_ref.shape[0]
    omn = _rms(om_ref[...], gom_ref[...]).astype(BF16)
    h1 = h_ref[...] + _dot(os_ref[...], wout_ref[:SSM_W, :]) + _dot(omn, wout_ref[SSM_W:, :])
    h1_ref[...] = h1
    xb = _rms(h1, gffn_ref[...]).astype(BF16)
    xn_ref[...] = _pack_bf16_halves(xb)
    work = (_dot_nt(wrt_ref[...], xb) + brt_ref[...])[:N_EXPERTS]

    eidx = lax.broadcasted_iota(jnp.int32, work.shape, 0).astype(F32)
    sels, vals, rows = [], [], []
    for k in range(TOP_K):
        mk = jnp.max(work, axis=0, keepdims=True)
        ik = jnp.min(jnp.where(work == mk, eidx, float(N_EXPERTS)), axis=0, keepdims=True)
        sel = eidx == ik
        work = jnp.where(sel, -jnp.inf, work)
        sels.append(sel)
        vals.append(mk)
        rows.append(ik)
    es = [jnp.exp(v - vals[0]) for v in vals]
    den = es[0] + es[1] + es[2] + es[3]
    rows += [e / den for e in es]
    onehot = jnp.zeros(eidx.shape, F32)
    for sel in sels:
        onehot = jnp.where(sel, 1.0, onehot)

    t_j = lax.broadcasted_iota(jnp.int32, (tm, tm), 0)
    t_i = lax.broadcasted_iota(jnp.int32, (tm, tm), 1)
    earlier = jnp.where(t_j < t_i, 1.0, 0.0).astype(BF16)
    cnt = cnt_ref[...]
    cum = _dot(onehot.astype(BF16), earlier) + jnp.concatenate([cnt] * pl.cdiv(tm, LANES), axis=1)[:, :tm]
    rows += [jnp.sum(jnp.where(sel, cum, 0.0), axis=0, keepdims=True) for sel in sels]
    cnt_ref[...] = cnt + jnp.sum(onehot, axis=1, keepdims=True)

    sub = lax.broadcasted_iota(jnp.int32, (ROUTE_W, tm), 0)
    rt = jnp.zeros((ROUTE_W, tm), F32)
    for r, v in enumerate(rows):
        rt = jnp.where(sub == r, v, rt)
    routet_ref[...] = rt
    route_ref[...] = jnp.concatenate([rt, jnp.zeros((LANES - ROUTE_W, tm), F32)], axis=0).T[:, :ROUTE_W]


def _mix(h2, o_ssm, o_mla, w, *, nb, nt, tm):
    n = h2.shape[0]
    row = lambda b, t: (b * nt + t, 0)
    return pl.pallas_call(
        _mix_kernel,
        grid=(nb, nt),
        in_specs=[
            pl.BlockSpec((tm, D_MODEL), row),
            pl.BlockSpec((tm, SSM_W), row),
            pl.BlockSpec((tm, N_HEADS * V_DIM), row),
            _full(w["gout_mla"].shape), _full(w["wout"].shape), _full(w["gffn"].shape),
            _full(w["wrt"].shape), _full(w["brt"].shape),
        ],
        out_specs=[
            pl.BlockSpec((tm, D_MODEL), row),
            pl.BlockSpec((tm, PACKED_W), row),
            pl.BlockSpec((tm, ROUTE_W), row),
            pl.BlockSpec((ROUTE_W, tm), lambda b, t: (0, b * nt + t)),
            _full((N_EXPERTS, LANES)),
        ],
        out_shape=[
            jax.ShapeDtypeStruct((n, D_MODEL), F32),
            jax.ShapeDtypeStruct((n, PACKED_W), jnp.uint32),
            jax.ShapeDtypeStruct((n, ROUTE_W), F32),
            jax.ShapeDtypeStruct((ROUTE_W, n), F32),
            jax.ShapeDtypeStruct((N_EXPERTS, LANES), F32),
        ],
        compiler_params=_cparams("arbitrary", "arbitrary"),
        name="mix",
    )(h2, o_ssm, o_mla, w["gout_mla"], w["wout"], w["gffn"], w["wrt"], w["brt"])


def _experts_kernel(tile_ref, e_ref, first_ref, valid_ref, lo_ref, hi_ref, newe_ref,
                    x_ref, wgu_ref, bgu_ref, wd_ref, bd_ref, o_ref, wgu_bf, wd_bf):
    i = pl.program_id(0)

    @pl.when(newe_ref[i] == 1)
    def _():
        wgu_bf[...] = wgu_ref[0].astype(BF16)
        wd_bf[...] = wd_ref[0].astype(BF16)

    @pl.when(valid_ref[i] == 1)
    def _():
        bm = x_ref.shape[0]
        hdn = _dot(_unpack_bf16_halves(x_ref[...]), wgu_bf[...]) + bgu_ref[0]
        x_glu = jnp.minimum(hdn[:, :D_FF], SWIGLU_LIMIT)
        x_lin = jnp.clip(hdn[:, D_FF:], -SWIGLU_LIMIT, SWIGLU_LIMIT)
        act = x_glu * jax.nn.sigmoid(SWIGLU_ALPHA * x_glu) * (x_lin + 1.0)
        y = _dot(act.astype(BF16), wd_bf[...]) + bd_ref[0]
        rows = tile_ref[i] * bm + lax.broadcasted_iota(jnp.int32, (bm, 1), 0)
        mine = (rows >= lo_ref[i]) & (rows < hi_ref[i])

        @pl.when(first_ref[i] == 1)
        def _():
            o_ref[...] = jnp.where(mine, y, 0.0)

        @pl.when(first_ref[i] == 0)
        def _():
            o_ref[...] = jnp.where(mine, y, o_ref[...])


def _experts(xs, items, w, *, bm):
    r = xs.shape[0]
    n_items = items[0].shape[0]
    xmap = lambda i, tile, *_: (tile[i], 0)
    emap = lambda i, tile, e, *_: (e[i], 0, 0)
    return pl.pallas_call(
        _experts_kernel,
        grid_spec=pltpu.PrefetchScalarGridSpec(
            num_scalar_prefetch=len(items),
            grid=(n_items,),
            in_specs=[
                pl.BlockSpec((bm, PACKED_W), xmap),
                pl.BlockSpec((1, D_MODEL, 2 * D_FF), emap),
                pl.BlockSpec((1, 1, 2 * D_FF), emap),
                pl.BlockSpec((1, D_FF, D_MODEL), emap),
                pl.BlockSpec((1, 1, D_MODEL), emap),
            ],
            out_specs=pl.BlockSpec((bm, D_MODEL), xmap),
            scratch_shapes=[pltpu.VMEM((D_MODEL, 2 * D_FF), BF16), pltpu.VMEM((D_FF, D_MODEL), BF16)],
        ),
        out_shape=jax.ShapeDtypeStruct((r, D_MODEL), F32),
        compiler_params=_cparams("arbitrary"),
        name="experts",
    )(*items, xs, w["wgu"], w["bgu"], w["wd"], w["bd"])


def _expert_items(counts, n_rows, bm):
    n_tiles = n_rows // bm
    n_items = n_tiles + N_EXPERTS - 1
    ends = jnp.cumsum(counts)
    starts = ends - counts
    first_tile = starts // bm
    per_e = jnp.where(counts > 0, (ends - 1) // bm - first_tile + 1, 0)
    item_end = jnp.cumsum(per_e)
    i = jnp.arange(n_items, dtype=jnp.int32)
    valid = i < item_end[-1]
    e = jnp.sum(jnp.minimum(i, item_end[-1] - 1)[:, None] >= item_end[None, :], axis=1).astype(jnp.int32)
    of_e = e[:, None] == jnp.arange(N_EXPERTS, dtype=jnp.int32)[None, :]
    pick = lambda table: jnp.sum(jnp.where(of_e, table[None, :], 0), axis=1).astype(jnp.int32)
    tile = jnp.where(valid, pick(first_tile) + i - pick(item_end - per_e), n_tiles - 1).astype(jnp.int32)
    one = jnp.ones((1,), jnp.int32)
    first = jnp.concatenate([one, (tile[1:] != tile[:-1]).astype(jnp.int32)])
    new_e = jnp.concatenate([one, (e[1:] != e[:-1]).astype(jnp.int32)])
    return (tile, e, first, valid.astype(jnp.int32), pick(starts), pick(ends), new_e)


def _ple_kernel(h1_ref, yg_ref, route_ref, p_ref, gple_ref, wg_ref, wp_ref, o_ref):
    h2 = h1_ref[...]
    route = route_ref[...]
    w = SC_GATHER_TOKENS
    for k in range(TOP_K):
        yk = yg_ref[:, k * w:(k + 1) * w, :].reshape(h2.shape)
        h2 = h2 + yk * route[:, TOP_K + k:TOP_K + k + 1]
    gate = jax.nn.sigmoid(_dot(_rms(h2, gple_ref[...]).astype(BF16), wg_ref[...]))
    o_ref[...] = h2 + _dot(p_ref[...].astype(BF16), wp_ref[...]) * gate


def _ple(h1, yg, route, p, w, *, tm):
    n = h1.shape[0]
    row = lambda t: (t, 0)
    return pl.pallas_call(
        _ple_kernel,
        grid=(n // tm,),
        in_specs=[
            pl.BlockSpec((tm, D_MODEL), row),
            pl.BlockSpec((tm // SC_GATHER_TOKENS, TOP_K * SC_GATHER_TOKENS, D_MODEL), lambda t: (t, 0, 0)),
            pl.BlockSpec((tm, ROUTE_W), row),
            pl.BlockSpec((tm, PLE_DIM), row),
            _full(w["gple"].shape), _full(w["wpg"].shape), _full(w["wp"].shape),
        ],
        out_specs=pl.BlockSpec((tm, D_MODEL), row),
        out_shape=jax.ShapeDtypeStruct((n, D_MODEL), F32),
        compiler_params=_cparams("parallel"),
        name="ple",
    )(h1, yg, route, p, w["gple"], w["wpg"], w["wp"])


SC_INDEX_ROW = 128
SC_SCATTER_TOKENS = SC_INDEX_ROW // TOP_K
SC_GATHER_TOKENS = 8


def _sc_mesh():
    return plsc.VectorSubcoreMesh(core_axis_name="c", subcore_axis_name="s")


def _window_index(dest_t, w):
    n = dest_t.shape[1]
    idx = dest_t.reshape(TOP_K, n // w, w).transpose(1, 0, 2).reshape(n // w, TOP_K * w)
    if TOP_K * w < SC_INDEX_ROW:
        idx = jnp.concatenate([idx, jnp.zeros((n // w, SC_INDEX_ROW - TOP_K * w), jnp.int32)], axis=-1)
    return idx


def _sc_dispatch(xn, idx):
    n = xn.shape[0]
    w = SC_SCATTER_TOKENS

    @pl.kernel(out_type=jax.ShapeDtypeStruct((n * TOP_K, xn.shape[1]), xn.dtype),
               mesh=_sc_mesh(), scratch_types=[pltpu.SemaphoreType.DMA])
    def scatter(x_hbm, i_hbm, o_hbm, sem):
        def body(x_vmem, i_vmem):
            copies = [pltpu.make_async_copy(x_vmem, o_hbm.at[i_vmem.at[0, pl.ds(k * w, w)]], sem)
                      for k in range(TOP_K)]
            for cp in copies:
                cp.start()
            for cp in copies:
                cp.wait()

        pltpu.emit_pipeline(
            body,
            grid=(n // w,),
            in_specs=[pl.BlockSpec((w, xn.shape[1]), lambda i: (i, 0)),
                      pl.BlockSpec((1, SC_INDEX_ROW), lambda i: (i, 0))],
            out_specs=[],
            core_axis_name=("c", "s"),
            dimension_semantics=(pltpu.PARALLEL,),
        )(x_hbm, i_hbm)

    return scatter(xn, idx)


def _sc_combine(ys, idx):
    w = SC_GATHER_TOKENS
    n = idx.shape[0] * w
    rows = TOP_K * w

    @pl.kernel(out_type=jax.ShapeDtypeStruct((n // w, rows, D_MODEL), ys.dtype),
               mesh=_sc_mesh(), scratch_types=[])
    def gather(y_hbm, i_hbm, o_hbm):
        def body(i_vmem, o_vmem):
            pltpu.sync_copy(y_hbm.at[i_vmem.at[0, pl.ds(0, rows)]], o_vmem.at[0])

        pltpu.emit_pipeline(
            body,
            grid=(n // w,),
            in_specs=[pl.BlockSpec((1, SC_INDEX_ROW), lambda i: (i, 0))],
            out_specs=[pl.BlockSpec((1, rows, D_MODEL), lambda i: (i, 0, 0))],
            core_axis_name=("c", "s"),
            dimension_semantics=(pltpu.PARALLEL,),
        )(i_hbm, o_hbm)

    return gather(ys, idx)


def _rope_tables(pos):
    inv = ROPE_THETA ** (-jnp.arange(0, ROPE_DIM, 2, dtype=F32) / ROPE_DIM)
    ang = pos.astype(F32)[:, None] * inv[None, :]
    cos, sin = jnp.cos(ang), jnp.sin(ang)
    n = pos.shape[0]
    pad = jnp.zeros((n, LANES - QK_DIM), F32)
    ct = jnp.concatenate([jnp.ones((n, NOPE_DIM), F32), cos, cos, pad], axis=-1)
    st = jnp.concatenate([jnp.zeros((n, NOPE_DIM), F32), sin, sin, pad], axis=-1)
    return ct, st


def _rope_partner(w_rope):
    return jnp.concatenate([-w_rope[..., ROPE_HALF:], w_rope[..., :ROPE_HALF]], axis=-1)


def _head_tile(nope, rope):
    pad = jnp.zeros(rope.shape[:-1] + (LANES - QK_DIM,), rope.dtype)
    return jnp.concatenate([nope, rope, pad], axis=-1)


def _block_diag(m):
    g, a, b = m.shape
    hg = g // 2
    eye = jnp.eye(hg, dtype=m.dtype)
    mh = m.reshape(2, hg, a, b)
    return jnp.einsum("kgab,gh->kgahb", mh, eye).reshape(2, hg * a, hg * b)


def _prep_weights(g_mix, w_in, s5_a_re, s5_a_im, s5_log_dt, s5_b_re, s5_b_im, s5_c_re, s5_c_im, s5_d,
                  w_glu, b_glu, g_q_lora, w_uq, g_kv_lora, w_uk, w_uv, g_qk_q, g_qk_k, g_out_ssm,
                  g_out_mla, w_out, g_ffn, w_router, b_router, w_gate_up, b_gate_up, w_down, b_down,
                  g_ple, w_ple_gate, w_ple):
    w = {}
    row = lambda v: v.reshape(1, -1).astype(F32)
    o = SSM_W + Q_LORA + KV_LORA
    w_kr = w_in[:, o:o + ROPE_DIM]
    zeros_n = jnp.zeros((D_MODEL, NOPE_DIM), F32)
    w["win"] = jnp.concatenate(
        [w_in[:, :o], _head_tile(zeros_n, w_kr), _head_tile(zeros_n, _rope_partner(w_kr))], axis=-1).astype(BF16)
    w["gmix"] = row(g_mix)
    w["gql"] = row(g_q_lora)
    w["gkv"] = row(g_kv_lora)

    wq = w_uq.reshape(Q_LORA, N_HEADS, QK_DIM)
    q1 = _head_tile(wq[..., :NOPE_DIM], wq[..., NOPE_DIM:])
    q2 = _head_tile(jnp.zeros_like(wq[..., :NOPE_DIM]), _rope_partner(wq[..., NOPE_DIM:]))
    w["wuq"] = jnp.concatenate([q1.reshape(Q_LORA, HP), q2.reshape(Q_LORA, HP)], axis=-1).astype(BF16)
    w["wuk"] = jnp.concatenate(
        [w_uk, jnp.zeros((KV_LORA, N_HEADS, LANES - NOPE_DIM), F32)], axis=-1).reshape(KV_LORA, HP).astype(BF16)
    w["wuv"] = jnp.concatenate(
        [w_uv, jnp.zeros((KV_LORA, N_HEADS, LANES - V_DIM), F32)], axis=-1).reshape(KV_LORA, HP).astype(BF16)
    w["wuv_h"] = w_uv.transpose(1, 0, 2).astype(BF16)
    wukT = w_uk.transpose(1, 2, 0)
    w["wukT"] = jnp.concatenate(
        [wukT, jnp.zeros((N_HEADS, LANES - NOPE_DIM, KV_LORA), F32)], axis=1).astype(BF16)
    gpad = jnp.zeros((LANES - QK_DIM,), F32)
    w["gq_p"] = row(jnp.concatenate([g_qk_q * (math.log2(math.e) / math.sqrt(QK_DIM)), gpad]))
    w["gq_s"] = row(jnp.concatenate([g_qk_q, gpad]))
    w["gk"] = row(jnp.concatenate([g_qk_k, gpad]))

    dt = jnp.exp(s5_log_dt.astype(F32))[:, None]
    ar, ai = s5_a_re.astype(F32), s5_a_im.astype(F32)
    mag = jnp.exp(dt * ar)
    abar_re, abar_im = mag * jnp.cos(dt * ai), mag * jnp.sin(dt * ai)
    den = ar * ar + ai * ai
    nr, ni = abar_re - 1.0, abar_im
    coef_re = (nr * ar + ni * ai) / den
    coef_im = (ni * ar - nr * ai) / den
    br, bi = s5_b_re.astype(F32), s5_b_im.astype(F32)
    bbar_re = coef_re[..., None] * br - coef_im[..., None] * bi
    bbar_im = coef_re[..., None] * bi + coef_im[..., None] * br
    bre = _block_diag(bbar_re.transpose(0, 2, 1))
    bim = _block_diag(bbar_im.transpose(0, 2, 1))
    w["s5_bw"] = jnp.stack([bre[0], bim[0], bre[1], bim[1]]).astype(BF16)
    cre = _block_diag(s5_c_re.astype(F32).transpose(0, 2, 1))
    cim = _block_diag(-s5_c_im.astype(F32).transpose(0, 2, 1))
    w["s5_cw"] = jnp.stack([cre[0], cim[0], cre[1], cim[1]]).astype(BF16)
    w["s5_are"] = jnp.broadcast_to(abar_re.reshape(1, SSM_S), (SUBLANES, SSM_S))
    w["s5_aim"] = jnp.broadcast_to(abar_im.reshape(1, SSM_S), (SUBLANES, SSM_S))
    w["s5_d"] = row(s5_d)
    w["wglu"] = w_glu.astype(BF16)
    w["bglu"] = row(b_glu)
    w["gout_ssm"] = row(g_out_ssm)
    w["gout_mla"] = row(g_out_mla)
    w["wout"] = w_out.astype(BF16)
    w["gffn"] = row(g_ffn)
    w["wrt"] = jnp.concatenate([w_router.T, jnp.zeros((LANES - N_EXPERTS, D_MODEL), F32)], axis=0).astype(BF16)
    w["brt"] = jnp.concatenate([b_router.astype(F32), jnp.zeros((LANES - N_EXPERTS,), F32)]).reshape(LANES, 1)
    w["wgu"] = w_gate_up
    w["bgu"] = b_gate_up.reshape(N_EXPERTS, 1, 2 * D_FF).astype(F32)
    w["wd"] = w_down
    w["bd"] = b_down.reshape(N_EXPERTS, 1, D_MODEL).astype(F32)
    w["gple"] = row(g_ple)
    w["wpg"] = w_ple_gate.astype(BF16)
    w["wp"] = w_ple.astype(BF16)
    return w


EXPERT_ROWS = 512


def _moe_experts(mixed, w, during_dispatch=()):
    h1, xn, _, route_t, cnt = mixed
    n = h1.shape[0]
    top_i = route_t[:TOP_K].astype(jnp.int32)
    rank = route_t[2 * TOP_K:3 * TOP_K].astype(jnp.int32)
    counts = cnt[:, 0].astype(jnp.int32)
    starts = jnp.cumsum(counts) - counts
    dest = rank
    for e in range(N_EXPERTS):
        dest = dest + jnp.where(top_i == e, starts[e], 0)
    gather_idx = _window_index(dest, SC_GATHER_TOKENS)
    xs = _after(_sc_dispatch(xn, _window_index(dest, SC_SCATTER_TOKENS)), (gather_idx, during_dispatch))
    bm = min(EXPERT_ROWS, max(LANES, n * TOP_K // N_EXPERTS))
    return _experts(xs, _expert_items(counts, n * TOP_K, bm), w, bm=bm), gather_idx


def _moe_combine(mixed, ys, gather_idx, p2, w, *, tm):
    h1, _, route, _, _ = mixed
    return _ple(h1, _sc_combine(ys, gather_idx), route, p2, w, tm=tm)


def _after(x, anchor):
    return lax.optimization_barrier((x, anchor))[0]


def kernel(x_prompt, x_sample, cache_kv_latent, cache_k_rope, cache_k_rstd, state_ssm_re, state_ssm_im,
           page_table, p_prompt, p_sample, g_mix, w_in, s5_a_re, s5_a_im, s5_log_dt, s5_b_re, s5_b_im,
           s5_c_re, s5_c_im, s5_d, w_glu, b_glu, g_q_lora, w_uq, g_kv_lora, w_uk, w_uv, g_qk_q, g_qk_k,
           g_out_ssm, g_out_mla, w_out, g_ffn, w_router, b_router, w_gate_up, b_gate_up, w_down, b_down,
           g_ple, w_ple_gate, w_ple):
    assert g_mix.shape[0] == 1, "single-layer step"
    nb, t, _ = x_prompt.shape
    db, ds, _ = x_sample.shape
    assert nb == SUBLANES and db % SUBLANES == 0
    w = _prep_weights(g_mix[0], w_in[0], s5_a_re[0], s5_a_im[0], s5_log_dt[0], s5_b_re[0], s5_b_im[0],
                      s5_c_re[0], s5_c_im[0], s5_d[0], w_glu[0], b_glu[0], g_q_lora[0], w_uq[0],
                      g_kv_lora[0], w_uk[0], w_uv[0], g_qk_q[0], g_qk_k[0], g_out_ssm[0], g_out_mla[0],
                      w_out[0], g_ffn[0], w_router[0], b_router[0], w_gate_up[0], b_gate_up[0],
                      w_down[0], b_down[0], g_ple[0], w_ple_gate[0], w_ple[0])

    tm = min(512, t)
    nt = t // tm
    xp = x_prompt.reshape(nb * t, D_MODEL)
    ct, st = _rope_tables(jnp.arange(t))
    u, q, k, v, c_p, kr_p, rstd_p = _proj(xp, ct, st, w, nb=nb, nt=nt, tm=tm, sample=False)
    steps = min(128, t)
    o_ssm, s_fin = _s5(u.reshape(nb, t, SSM_W), jnp.zeros((nb, 2 * SSM_S), F32), w,
                       ngroups=1, nchunks=t // steps, steps=steps)
    o_mla = _attn_p(q, k, v, nb=nb, t=t, tq=min(1024, t), tk=min(512, t))
    mixed_p = _mix(xp, o_ssm.reshape(nb * t, SSM_W), o_mla, w, nb=nb, nt=nt, tm=tm)
    sr_p = s_fin[:, :SSM_S].reshape(1, nb, SSM_G, SSM_P)
    si_p = s_fin[:, SSM_S:].reshape(1, nb, SSM_G, SSM_P)

    ns = db * ds
    ng = db // SUBLANES
    xs_ = _after(x_sample.reshape(ns, D_MODEL), mixed_p[2])
    ct_s, st_s = _rope_tables(PAST_LEN + jnp.tile(jnp.arange(ds), db))
    u_s, qf, qlat, c_s, kr_s, rstd_s = _proj(xs_, ct_s, st_s, w, nb=1, nt=1, tm=ns, sample=True)
    to_scan = lambda a: a.reshape(ng, SUBLANES, ds, -1).transpose(0, 2, 1, 3).reshape(ns, -1)
    from_scan = lambda a: a.reshape(ng, ds, SUBLANES, -1).transpose(0, 2, 1, 3).reshape(ns, -1)
    h0 = jnp.concatenate([state_ssm_re[0].reshape(db, SSM_S), state_ssm_im[0].reshape(db, SSM_S)],
                         axis=-1).astype(F32)
    o_ssm_s, s_fin_s = _s5(to_scan(u_s), h0, w, ngroups=ng, nchunks=1, steps=ds)
    pad_keys = lambda a: jnp.concatenate(
        [a, jnp.zeros((db, NEW_KEYS - ds, a.shape[-1]), a.dtype)], axis=1)
    key_major = lambda a: a.transpose(0, 2, 1)
    attn_in = (page_table,
               qlat.reshape(db, ds * N_HEADS, KV_LORA),
               qf.reshape(db, ds * N_HEADS, LANES)[:, :, NOPE_DIM:QK_DIM],
               pad_keys(c_s.reshape(db, ds, KV_LORA)),
               key_major(pad_keys(kr_s.reshape(db, ds, ROPE_DIM))),
               key_major(pad_keys(rstd_s.reshape(db, ds, N_HEADS))))
    caches = (cache_kv_latent[0], key_major(cache_k_rope[0]), key_major(cache_k_rstd[0]))

    def attn_rows(lo, hi, anchor=None):
        pt, *rest = (a[lo:hi] for a in attn_in)
        if anchor is not None:
            pt = _after(pt, anchor)
        return _attn_s(pt, *rest, *caches, dec_seq=ds)

    o_lat_a = attn_rows(0, db // 2)
    ys_p, dest_p = _moe_experts(mixed_p, w, during_dispatch=o_lat_a)
    o_lat_b = attn_rows(db // 2, db, anchor=ys_p)
    o_lat = jnp.concatenate([o_lat_a, o_lat_b], axis=0)
    o_mla_s = _latent_out(o_lat.reshape(ns, N_HEADS * KV_LORA), w["wuv_h"])

    y_p = _moe_combine(mixed_p, ys_p, dest_p, p_prompt[0].reshape(nb * t, PLE_DIM), w, tm=tm)
    mixed_s = _mix(xs_, from_scan(o_ssm_s), _after(o_mla_s, y_p), w, nb=1, nt=1, tm=ns)
    ys_s, dest_s = _moe_experts(mixed_s, w)
    y_s = _moe_combine(mixed_s, ys_s, dest_s, p_sample[0].reshape(ns, PLE_DIM), w, tm=ns)

    return (y_p.reshape(nb, t, D_MODEL), y_s.reshape(db, ds, D_MODEL),
            c_p.reshape(1, nb, t, KV_LORA), kr_p.reshape(1, nb, t, ROPE_DIM), rstd_p.reshape(1, nb, t, N_HEADS),
            sr_p, si_p,
            c_s.reshape(1, db, ds, KV_LORA), kr_s.reshape(1, db, ds, ROPE_DIM), rstd_s.reshape(1, db, ds, N_HEADS),
            s_fin_s[:, :SSM_S].reshape(1, db, SSM_G, SSM_P), s_fin_s[:, SSM_S:].reshape(1, db, SSM_G, SSM_P))
```

```python
import functools
import math

import jax
import jax.numpy as jnp
from jax import lax
from jax.experimental import pallas as pl
from jax.experimental.pallas import tpu as pltpu
from jax.experimental.pallas import tpu_sc as plsc

D_MODEL = 1024
SSM_W = 512
SSM_GC = 16
SSM_G = SSM_W // SSM_GC
SSM_P = 64
SSM_S = SSM_G * SSM_P
N_HEADS = 8
NOPE_DIM = 64
ROPE_DIM = 32
ROPE_HALF = ROPE_DIM // 2
QK_DIM = NOPE_DIM + ROPE_DIM
V_DIM = 64
Q_LORA = 384
KV_LORA = 256
ROPE_THETA = 10000.0
N_EXPERTS = 32
TOP_K = 4
D_FF = D_MODEL
SWIGLU_ALPHA = 1.702
SWIGLU_LIMIT = 7.0
PLE_DIM = 256
PAST_LEN = 16384
PAGE_SIZE = 128
EPS = 1e-6
NEG = -1e30

LANES = 128
SUBLANES = 8
VMEM_LIMIT = 56 * 1024 * 1024

HP = N_HEADS * LANES
BF16 = jnp.bfloat16
F32 = jnp.float32

TOKEN_ROWS = 512
S5_STEPS = 128
ATTN_P_HEADS = 4
ATTN_P_TQ = 1024
ATTN_P_TK = 512
ATTN_S_PAGES = 32
NEW_KEYS = 16
EXPERT_ROWS = 512
SC_INDEX_ROW = 128
SC_SCATTER_TOKENS = SC_INDEX_ROW // TOP_K
SC_GATHER_TOKENS = 8


def _cparams(*sem):
    return pltpu.CompilerParams(dimension_semantics=sem, vmem_limit_bytes=VMEM_LIMIT)


def _rms(x, g):
    r = lax.rsqrt(jnp.mean(x * x, axis=-1, keepdims=True) + EPS)
    return x * r * g


def _dot(a, b):
    return jnp.dot(a, b, preferred_element_type=F32)


def _dot_nt(a, b):
    return lax.dot_general(a, b, (((1,), (1,)), ((), ())), preferred_element_type=F32)


def _full(shape):
    nd = len(shape)
    return pl.BlockSpec(shape, lambda *_: (0,) * nd)


def _proj_kernel(x_ref, ct_ref, st_ref, gmix_ref, win_ref, gql_ref, wuq_ref, gkv_ref, wuk_ref,
                 w2_ref, gq_ref, gk_ref, u_ref, q_ref, k2_ref, *rest, sample):
    if sample:
        c_ref, kr_ref, rstd_ref = rest
    else:
        v2_ref, c_ref, kr_ref, rstd_ref = rest
    xn = _rms(x_ref[...], gmix_ref[...]).astype(BF16)
    z = _dot(xn, win_ref[...])
    u_ref[...] = z[:, :SSM_W]
    o = SSM_W
    cq = z[:, o:o + Q_LORA]
    o += Q_LORA
    ckv = z[:, o:o + KV_LORA]
    o += KV_LORA
    ct = ct_ref[...]
    st = st_ref[...]
    kblock = z[:, o:o + LANES] * ct + z[:, o + LANES:o + 2 * LANES] * st
    kr_ref[...] = kblock[:, NOPE_DIM:QK_DIM]

    qq = _dot(_rms(cq, gql_ref[...]).astype(BF16), wuq_ref[...])
    c = _rms(ckv, gkv_ref[...])
    c_ref[...] = c
    cb = c.astype(BF16)
    kn = _dot(cb, wuk_ref[...])
    if not sample:
        hl = lax.broadcasted_iota(jnp.int32, (1, HP), 1) & (LANES - 1)
        v2_ref[...] = (_dot(cb, w2_ref[...]) + jnp.where(hl >= V_DIM, 1.0, 0.0)).astype(BF16)

    lane = lax.broadcasted_iota(jnp.int32, ct.shape, 1)
    rstd_all = jnp.zeros(ct.shape, F32)
    gq = gq_ref[...]
    gk = gk_ref[...]
    for h in range(N_HEADS):
        sl = slice(h * LANES, (h + 1) * LANES)
        qr = qq[:, sl] * ct + qq[:, HP + h * LANES:HP + (h + 1) * LANES] * st
        qn = qr * lax.rsqrt(jnp.sum(qr * qr, axis=-1, keepdims=True) * (1.0 / QK_DIM) + EPS) * gq
        kh = kn[:, sl] + kblock
        rs = lax.rsqrt(jnp.sum(kh * kh, axis=-1, keepdims=True) * (1.0 / QK_DIM) + EPS)
        rstd_all = jnp.where(lane == h, rs, rstd_all)
        if sample:
            qf = (qn * gk).astype(BF16)
            q_ref[:, sl] = qf
            k2_ref[:, h * KV_LORA:(h + 1) * KV_LORA] = _dot(qf, w2_ref[h]).astype(BF16)
        else:
            q_ref[:, sl] = qn.astype(BF16)
            k2_ref[:, sl] = (kh * rs * gk).astype(BF16)
    rstd_ref[...] = rstd_all[:, :N_HEADS]


def _proj(x2, ct, st, w, *, nb, nt, tm, sample):
    n = x2.shape[0]
    row = lambda b, t: (b * nt + t, 0)
    rows = lambda width: pl.BlockSpec((tm, width), row)
    k2_w = N_HEADS * KV_LORA if sample else HP
    w2 = w["wukT"] if sample else w["wuv"]
    out_shape = [
        jax.ShapeDtypeStruct((n, SSM_W), F32),
        jax.ShapeDtypeStruct((n, HP), BF16),
        jax.ShapeDtypeStruct((n, k2_w), BF16),
        *([] if sample else [jax.ShapeDtypeStruct((n, HP), BF16)]),
        jax.ShapeDtypeStruct((n, KV_LORA), F32),
        jax.ShapeDtypeStruct((n, ROPE_DIM), F32),
        jax.ShapeDtypeStruct((n, N_HEADS), F32),
    ]
    out_specs = [
        rows(SSM_W), rows(HP), rows(k2_w), *([] if sample else [rows(HP)]),
        rows(KV_LORA), rows(ROPE_DIM), rows(N_HEADS),
    ]
    in_specs = [
        rows(D_MODEL),
        pl.BlockSpec((tm, LANES), lambda b, t: (t, 0)),
        pl.BlockSpec((tm, LANES), lambda b, t: (t, 0)),
        _full(w["gmix"].shape), _full(w["win"].shape), _full(w["gql"].shape), _full(w["wuq"].shape),
        _full(w["gkv"].shape), _full(w["wuk"].shape), _full(w2.shape),
        _full(w["gq_s" if sample else "gq_p"].shape), _full(w["gk"].shape),
    ]
    return pl.pallas_call(
        functools.partial(_proj_kernel, sample=sample),
        grid=(nb, nt),
        in_specs=in_specs,
        out_specs=out_specs,
        out_shape=out_shape,
        compiler_params=_cparams("parallel", "parallel"),
        name="proj_s" if sample else "proj_p",
    )(x2, ct, st, w["gmix"], w["win"], w["gql"], w["wuq"], w["gkv"], w["wuk"], w2,
      w["gq_s" if sample else "gq_p"], w["gk"])


S5_HALF = SSM_S // 2
S5_QUARTER = SSM_S // 4


def _s5_kernel(u_ref, h0_ref, are_ref, aim_ref, bw_ref, cw_ref, dsk_ref, wglu_ref, bglu_ref, gout_ref,
               o_ref, sfin_ref, xs_ref, st_ref, *il_ref, steps):
    @pl.when(pl.program_id(1) == 0)
    def _():
        st_ref[...] = h0_ref[...]

    il = il_ref[0] if il_ref else None
    lane_tiles = SSM_W // LANES
    if il is not None:
        for b in range(SUBLANES):
            for c in range(lane_tiles):
                il[c, pl.ds(b, steps, stride=SUBLANES), :] = u_ref[b, :, c * LANES:(c + 1) * LANES]
        u = jnp.concatenate([il[c] for c in range(lane_tiles)], axis=-1)
    else:
        u = u_ref[...]
    ub = u.astype(BF16)
    half_w = SSM_W // 2
    for k in range(2):
        uk = ub[:, k * half_w:(k + 1) * half_w]
        xs_ref[:, k * S5_HALF:(k + 1) * S5_HALF] = _dot(uk, bw_ref[2 * k])
        xs_ref[:, SSM_S + k * S5_HALF:SSM_S + (k + 1) * S5_HALF] = _dot(uk, bw_ref[2 * k + 1])

    for q in range(4):
        lr = slice(q * S5_QUARTER, (q + 1) * S5_QUARTER)
        li = slice(SSM_S + q * S5_QUARTER, SSM_S + (q + 1) * S5_QUARTER)
        ar = are_ref[:, lr]
        ai = aim_ref[:, lr]

        def step(t, carry):
            sr, si = carry
            r0 = pl.multiple_of(t * SUBLANES, SUBLANES)
            nr = ar * sr - ai * si + xs_ref[pl.ds(r0, SUBLANES), lr]
            ni = ar * si + ai * sr + xs_ref[pl.ds(r0, SUBLANES), li]
            xs_ref[pl.ds(r0, SUBLANES), lr] = nr
            xs_ref[pl.ds(r0, SUBLANES), li] = ni
            return nr, ni

        sr, si = lax.fori_loop(0, steps, step, (st_ref[:, lr], st_ref[:, li]), unroll=4)
        st_ref[:, lr] = sr
        st_ref[:, li] = si
    sfin_ref[...] = st_ref[...]

    ys = []
    for k in range(2):
        sre = xs_ref[:, k * S5_HALF:(k + 1) * S5_HALF].astype(BF16)
        sim = xs_ref[:, SSM_S + k * S5_HALF:SSM_S + (k + 1) * S5_HALF].astype(BF16)
        ys.append(_dot(sre, cw_ref[2 * k]) + _dot(sim, cw_ref[2 * k + 1]))
    y = jnp.concatenate(ys, axis=-1) + dsk_ref[...] * u
    g = jax.nn.gelu(y)
    out = g * jax.nn.sigmoid(_dot(g.astype(BF16), wglu_ref[...]) + bglu_ref[...])
    on = _rms(out, gout_ref[...])
    if il is not None:
        for c in range(lane_tiles):
            il[c] = on[:, c * LANES:(c + 1) * LANES]
        for b in range(SUBLANES):
            for c in range(lane_tiles):
                o_ref[b, :, c * LANES:(c + 1) * LANES] = il[c, pl.ds(b, steps, stride=SUBLANES), :].astype(BF16)
    else:
        o_ref[...] = on.astype(BF16)


def _s5(u, h0, w, *, ngroups, nchunks, steps):
    rows = steps * SUBLANES
    batch_major = u.ndim == 3
    if batch_major:
        assert ngroups == 1 and u.shape[0] == SUBLANES
        ublock = pl.BlockSpec((SUBLANES, steps, SSM_W), lambda g, t: (0, t, 0))
    else:
        ublock = pl.BlockSpec((rows, SSM_W), lambda g, t: (g * nchunks + t, 0))
    return pl.pallas_call(
        functools.partial(_s5_kernel, steps=steps),
        grid=(ngroups, nchunks),
        in_specs=[
            ublock,
            pl.BlockSpec((SUBLANES, 2 * SSM_S), lambda g, t: (g, 0)),
            _full(w["s5_are"].shape), _full(w["s5_aim"].shape), _full(w["s5_bw"].shape),
            _full(w["s5_cw"].shape), _full(w["s5_d"].shape), _full(w["wglu"].shape),
            _full(w["bglu"].shape), _full(w["gout_ssm"].shape),
        ],
        out_specs=[
            ublock,
            pl.BlockSpec((SUBLANES, 2 * SSM_S), lambda g, t: (g, 0)),
        ],
        out_shape=[
            jax.ShapeDtypeStruct(u.shape, BF16),
            jax.ShapeDtypeStruct(h0.shape, F32),
        ],
        scratch_shapes=[pltpu.VMEM((rows, 2 * SSM_S), F32), pltpu.VMEM((SUBLANES, 2 * SSM_S), F32)]
        + ([pltpu.VMEM((SSM_W // LANES, rows, LANES), F32)] if batch_major else []),
        compiler_params=_cparams("parallel", "arbitrary"),
        name="s5",
    )(u, h0, w["s5_are"], w["s5_aim"], w["s5_bw"], w["s5_cw"], w["s5_d"], w["wglu"], w["bglu"],
      w["gout_ssm"])


def _attn_p_kernel(q_ref, k_ref, v_ref, o_ref, *, tq, tk):
    i = pl.program_id(2)
    nd = tq // tk
    row = lax.broadcasted_iota(jnp.int32, (tk, tk), 0)
    col = lax.broadcasted_iota(jnp.int32, (tk, tk), 1)

    def block(j, carry, diag):
        k0 = pl.multiple_of(j * tk, tk)
        k2 = k_ref[pl.ds(k0, tk), :]
        v2 = v_ref[pl.ds(k0, tk), :]
        r0 = 0 if diag is None else diag * tk
        out = []
        for hh in range(ATTN_P_HEADS):
            m, acc = carry[hh]
            s = _dot_nt(q_ref[r0:, hh * LANES:(hh + 1) * LANES], k2[:, hh * LANES:(hh + 1) * LANES])
            if diag is not None:
                tri = jnp.where(col <= row, s[:tk], NEG)
                s = tri if r0 + tk == tq else jnp.concatenate([tri, s[tk:]], axis=0)
            m_new = jnp.maximum(m[r0:], jnp.max(s, axis=-1, keepdims=True))
            p = jnp.exp2(s - m_new).astype(BF16)
            acc_new = jnp.exp2(m[r0:] - m_new) * acc[r0:] + _dot(p, v2[:, hh * LANES:(hh + 1) * LANES])
            if r0:
                m_new = jnp.concatenate([m[:r0], m_new], axis=0)
                acc_new = jnp.concatenate([acc[:r0], acc_new], axis=0)
            out.append((m_new, acc_new))
        return tuple(out)

    init = tuple((jnp.full((tq, 1), NEG, F32), jnp.zeros((tq, LANES), F32)) for _ in range(ATTN_P_HEADS))
    carry = lax.fori_loop(0, i * nd, lambda j, c: block(j, c, None), init)
    for d in range(nd):
        carry = block(i * nd + d, carry, d)
    o = [acc / pltpu.roll(acc, V_DIM, 1) for _, acc in carry]
    lane = lax.broadcasted_iota(jnp.int32, (tq, LANES), 1)
    for pr in range(ATTN_P_HEADS // 2):
        o_ref[:, pr * LANES:(pr + 1) * LANES] = jnp.where(
            lane < V_DIM, o[2 * pr], pltpu.roll(o[2 * pr + 1], V_DIM, 1))


def _attn_p(q, k, v, *, nb, t, tq, tk):
    nq = t // tq
    hw = ATTN_P_HEADS * LANES
    return pl.pallas_call(
        functools.partial(_attn_p_kernel, tq=tq, tk=tk),
        grid=(nb, N_HEADS // ATTN_P_HEADS, nq),
        in_specs=[
            pl.BlockSpec((tq, hw), lambda b, h, i: (b * nq + i, h)),
            pl.BlockSpec((t, hw), lambda b, h, i: (b, h)),
            pl.BlockSpec((t, hw), lambda b, h, i: (b, h)),
        ],
        out_specs=pl.BlockSpec((tq, ATTN_P_HEADS * V_DIM), lambda b, h, i: (b * nq + i, h)),
        out_shape=jax.ShapeDtypeStruct((nb * t, N_HEADS * V_DIM), F32),
        compiler_params=_cparams("parallel", "parallel", "arbitrary"),
        name="attn_p",
    )(q, k, v)


def _attn_s_kernel(pt_ref, ql_ref, qr_ref, cn_ref, krn_ref, rnt_ref, cc_hbm, ckr_hbm, cr_hbm,
                   o_ref, cbuf, krbuf, rbuf, cb_ref, sc_ref, sems, *, n_pages, dec_seq):
    b = pl.program_id(0)
    nb = pl.num_programs(0)
    bp = min(ATTN_S_PAGES, n_pages)
    nblk = n_pages // bp
    keys = bp * PAGE_SIZE
    nrow = dec_seq * N_HEADS
    scale = 1.0 / math.sqrt(QK_DIM)
    slot = b % 2

    def start_block(bb, i, dst):
        for p in range(bp):
            pg = i * bp + p
            page = pt_ref[bb * n_pages + pg]
            lanes = pl.ds(pl.multiple_of(pg * PAGE_SIZE, PAGE_SIZE), PAGE_SIZE)
            pltpu.make_async_copy(cc_hbm.at[page], cbuf.at[dst, pg], sems.at[dst, 0]).start()
            pltpu.make_async_copy(ckr_hbm.at[page], krbuf.at[dst, :, lanes], sems.at[dst, 1]).start()
            pltpu.make_async_copy(cr_hbm.at[page], rbuf.at[dst, :, lanes], sems.at[dst, 2]).start()

    @pl.when(b == 0)
    def _():
        lax.fori_loop(0, nblk, lambda i, c: (start_block(0, i, 0), c)[1], 0)

    def wait_row(s):
        pltpu.make_async_copy(cbuf.at[s], cbuf.at[s], sems.at[s, 0]).wait()
        pltpu.make_async_copy(krbuf.at[s], krbuf.at[s], sems.at[s, 1]).wait()
        pltpu.make_async_copy(rbuf.at[s], rbuf.at[s], sems.at[s, 2]).wait()

    wait_row(slot)

    ql = ql_ref[0]
    qr = qr_ref[0]

    def tile_heads(r_t):
        return jnp.concatenate([r_t] * dec_seq, axis=0)

    cnb = cn_ref[0].astype(BF16)
    s_n = (_dot_nt(ql, cnb) + _dot(qr, krn_ref[0].astype(BF16))) * tile_heads(rnt_ref[0]) * scale
    qs = lax.broadcasted_iota(jnp.int32, (nrow, NEW_KEYS), 0) // N_HEADS
    kj = lax.broadcasted_iota(jnp.int32, (nrow, NEW_KEYS), 1)
    s_n = jnp.where(kj <= qs, s_n, NEG)

    def score_block(i, m):
        @pl.when(b + 1 < nb)
        def _():
            start_block(b + 1, i, 1 - slot)

        k0 = pl.multiple_of(i * keys, keys)
        cb = cbuf[slot, pl.ds(i * bp, bp)].reshape(keys, KV_LORA).astype(BF16)
        cb_ref[pl.ds(k0, keys), :] = cb
        raw = _dot_nt(ql, cb) + _dot(qr, krbuf[slot, :, pl.ds(k0, keys)].astype(BF16))
        s = raw * tile_heads(rbuf[slot, :, pl.ds(k0, keys)]) * scale
        sc_ref[:, pl.ds(k0, keys)] = s
        return jnp.maximum(m, jnp.max(s, axis=-1, keepdims=True))

    m = lax.fori_loop(0, nblk, score_block, jnp.max(s_n, axis=-1, keepdims=True))
    p_n = jnp.exp(s_n - m)

    def value_block(i, carry):
        l, acc = carry
        k0 = pl.multiple_of(i * keys, keys)
        p = jnp.exp(sc_ref[:, pl.ds(k0, keys)] - m)
        return (l + jnp.sum(p, axis=-1, keepdims=True),
                acc + _dot(p.astype(BF16), cb_ref[pl.ds(k0, keys), :]))

    l, acc = lax.fori_loop(0, nblk, value_block,
                           (jnp.sum(p_n, axis=-1, keepdims=True), _dot(p_n.astype(BF16), cnb)))
    o_ref[0] = acc / l


def _attn_s(page_table, ql, qr, cn, krn, rnt, cache_c, cache_kr, cache_r, *, dec_seq):
    nb, n_pages = page_table.shape
    nrow = dec_seq * N_HEADS
    b3 = lambda b, pt: (b, 0, 0)
    return pl.pallas_call(
        functools.partial(_attn_s_kernel, n_pages=n_pages, dec_seq=dec_seq),
        grid_spec=pltpu.PrefetchScalarGridSpec(
            num_scalar_prefetch=1,
            grid=(nb,),
            in_specs=[
                pl.BlockSpec((1, nrow, KV_LORA), b3),
                pl.BlockSpec((1, nrow, ROPE_DIM), b3),
                pl.BlockSpec((1, NEW_KEYS, KV_LORA), b3),
                pl.BlockSpec((1, ROPE_DIM, NEW_KEYS), b3),
                pl.BlockSpec((1, N_HEADS, NEW_KEYS), b3),
                pl.BlockSpec(memory_space=pl.ANY),
                pl.BlockSpec(memory_space=pl.ANY),
                pl.BlockSpec(memory_space=pl.ANY),
            ],
            out_specs=pl.BlockSpec((1, nrow, KV_LORA), b3),
            scratch_shapes=[
                pltpu.VMEM((2, n_pages, PAGE_SIZE, KV_LORA), F32),
                pltpu.VMEM((2, ROPE_DIM, n_pages * PAGE_SIZE), F32),
                pltpu.VMEM((2, N_HEADS, n_pages * PAGE_SIZE), F32),
                pltpu.VMEM((n_pages * PAGE_SIZE, KV_LORA), BF16),
                pltpu.VMEM((nrow, n_pages * PAGE_SIZE), F32),
                pltpu.SemaphoreType.DMA((2, 3)),
            ],
        ),
        out_shape=jax.ShapeDtypeStruct((nb, nrow, KV_LORA), F32),
        compiler_params=_cparams("arbitrary"),
        name="attn_s",
    )(page_table.reshape(-1), ql, qr, cn, krn, rnt, cache_c, cache_kr, cache_r)


def _latent_out_kernel(ol_ref, wuv_ref, o_ref):
    for h in range(N_HEADS):
        o_ref[:, h * V_DIM:(h + 1) * V_DIM] = _dot(
            ol_ref[:, h * KV_LORA:(h + 1) * KV_LORA].astype(BF16), wuv_ref[h])


def _latent_out(ol, wuv_h):
    n = ol.shape[0]
    return pl.pallas_call(
        _latent_out_kernel,
        grid=(1,),
        in_specs=[_full(ol.shape), _full(wuv_h.shape)],
        out_specs=_full((n, N_HEADS * V_DIM)),
        out_shape=jax.ShapeDtypeStruct((n, N_HEADS * V_DIM), F32),
        compiler_params=_cparams("arbitrary"),
        name="latent_out",
    )(ol, wuv_h)


ROUTE_W = 16
PACKED_W = D_MODEL // 2


def _pack_bf16_halves(xb):
    bits = lambda v: pltpu.bitcast(v.astype(F32), jnp.uint32)
    return (bits(xb[:, :PACKED_W]) >> 16) | bits(xb[:, PACKED_W:])


def _unpack_bf16_halves(words):
    lo = pltpu.bitcast(words << 16, F32)
    hi = pltpu.bitcast(words & jnp.uint32(0xFFFF0000), F32)
    return jnp.concatenate([lo, hi], axis=-1).astype(BF16)


def _mix_kernel(h_ref, os_ref, om_ref, gom_ref, wout_ref, gffn_ref, wrt_ref, brt_ref,
                h1_ref, xn_ref, route_ref, routet_ref, cnt_ref):
    @pl.when((pl.program_id(0) == 0) & (pl.program_id(1) == 0))
    def _():
        cnt_ref[...] = jnp.zeros_like(cnt_ref)

    tm = h_ref.shape[0]
    omn = _rms(om_ref[...], gom_ref[...]).astype(BF16)
    h1 = h_ref[...] + _dot(os_ref[...], wout_ref[:SSM_W, :]) + _dot(omn, wout_ref[SSM_W:, :])
    h1_ref[...] = h1
    xb = _rms(h1, gffn_ref[...]).astype(BF16)
    xn_ref[...] = _pack_bf16_halves(xb)
    work = (_dot_nt(wrt_ref[...], xb) + brt_ref[...])[:N_EXPERTS]

    eidx = lax.broadcasted_iota(jnp.int32, work.shape, 0).astype(F32)
    sels, vals, rows = [], [], []
    for k in range(TOP_K):
        mk = jnp.max(work, axis=0, keepdims=True)
        ik = jnp.min(jnp.where(work == mk, eidx, float(N_EXPERTS)), axis=0, keepdims=True)
        sel = eidx == ik
        work = jnp.where(sel, -jnp.inf, work)
        sels.append(sel)
        vals.append(mk)
        rows.append(ik)
    es = [jnp.exp(v - vals[0]) for v in vals]
    den = es[0] + es[1] + es[2] + es[3]
    rows += [e / den for e in es]
    onehot = jnp.zeros(eidx.shape, F32)
    for sel in sels:
        onehot = jnp.where(sel, 1.0, onehot)

    t_j = lax.broadcasted_iota(jnp.int32, (tm, tm), 0)
    t_i = lax.broadcasted_iota(jnp.int32, (tm, tm), 1)
    earlier = jnp.where(t_j < t_i, 1.0, 0.0).astype(BF16)
    cnt = cnt_ref[...]
    cum = _dot(onehot.astype(BF16), earlier) + jnp.concatenate([cnt] * pl.cdiv(tm, LANES), axis=1)[:, :tm]
    rows += [jnp.sum(jnp.where(sel, cum, 0.0), axis=0, keepdims=True) for sel in sels]
    cnt_ref[...] = cnt + jnp.sum(onehot, axis=1, keepdims=True)

    sub = lax.broadcasted_iota(jnp.int32, (ROUTE_W, tm), 0)
    rt = jnp.zeros((ROUTE_W, tm), F32)
    for r, v in enumerate(rows):
        rt = jnp.where(sub == r, v, rt)
    routet_ref[...] = rt
    route_ref[...] = jnp.concatenate([rt, jnp.zeros((LANES - ROUTE_W, tm), F32)], axis=0).T[:, :ROUTE_W]


def _mix(h2, o_ssm, o_mla, w, *, nb, nt, tm):
    n = h2.shape[0]
    row = lambda b, t: (b * nt + t, 0)
    return pl.pallas_call(
        _mix_kernel,
        grid=(nb, nt),
        in_specs=[
            pl.BlockSpec((tm, D_MODEL), row),
            pl.BlockSpec((tm, SSM_W), row),
            pl.BlockSpec((tm, N_HEADS * V_DIM), row),
            _full(w["gout_mla"].shape), _full(w["wout"].shape), _full(w["gffn"].shape),
            _full(w["wrt"].shape), _full(w["brt"].shape),
        ],
        out_specs=[
            pl.BlockSpec((tm, D_MODEL), row),
            pl.BlockSpec((tm, PACKED_W), row),
            pl.BlockSpec((tm, ROUTE_W), row),
            pl.BlockSpec((ROUTE_W, tm), lambda b, t: (0, b * nt + t)),
            _full((N_EXPERTS, LANES)),
        ],
        out_shape=[
            jax.ShapeDtypeStruct((n, D_MODEL), F32),
            jax.ShapeDtypeStruct((n, PACKED_W), jnp.uint32),
            jax.ShapeDtypeStruct((n, ROUTE_W), F32),
            jax.ShapeDtypeStruct((ROUTE_W, n), F32),
            jax.ShapeDtypeStruct((N_EXPERTS, LANES), F32),
        ],
        compiler_params=_cparams("arbitrary", "arbitrary"),
        name="mix",
    )(h2, o_ssm, o_mla, w["gout_mla"], w["wout"], w["gffn"], w["wrt"], w["brt"])


def _experts_kernel(tile_ref, e_ref, first_ref, valid_ref, lo_ref, hi_ref, newe_ref,
                    x_ref, wgu_ref, bgu_ref, wd_ref, bd_ref, o_ref, wgu_bf, wd_bf):
    i = pl.program_id(0)

    @pl.when(newe_ref[i] == 1)
    def _():
        wgu_bf[...] = wgu_ref[0].astype(BF16)
        wd_bf[...] = wd_ref[0].astype(BF16)

    @pl.when(valid_ref[i] == 1)
    def _():
        bm = x_ref.shape[0]
        hdn = _dot(_unpack_bf16_halves(x_ref[...]), wgu_bf[...]) + bgu_ref[0]
        x_glu = jnp.minimum(hdn[:, :D_FF], SWIGLU_LIMIT)
        x_lin = jnp.clip(hdn[:, D_FF:], -SWIGLU_LIMIT, SWIGLU_LIMIT)
        act = x_glu * jax.nn.sigmoid(SWIGLU_ALPHA * x_glu) * (x_lin + 1.0)
        y = _dot(act.astype(BF16), wd_bf[...]) + bd_ref[0]
        rows = tile_ref[i] * bm + lax.broadcasted_iota(jnp.int32, (bm, 1), 0)
        mine = (rows >= lo_ref[i]) & (rows < hi_ref[i])

        @pl.when(first_ref[i] == 1)
        def _():
            o_ref[...] = jnp.where(mine, y, 0.0)

        @pl.when(first_ref[i] == 0)
        def _():
            o_ref[...] = jnp.where(mine, y, o_ref[...])


def _experts(xs, items, w, *, bm):
    r = xs.shape[0]
    n_items = items[0].shape[0]
    xmap = lambda i, tile, *_: (tile[i], 0)
    emap = lambda i, tile, e, *_: (e[i], 0, 0)
    return pl.pallas_call(
        _experts_kernel,
        grid_spec=pltpu.PrefetchScalarGridSpec(
            num_scalar_prefetch=len(items),
            grid=(n_items,),
            in_specs=[
                pl.BlockSpec((bm, PACKED_W), xmap),
                pl.BlockSpec((1, D_MODEL, 2 * D_FF), emap),
                pl.BlockSpec((1, 1, 2 * D_FF), emap),
                pl.BlockSpec((1, D_FF, D_MODEL), emap),
                pl.BlockSpec((1, 1, D_MODEL), emap),
            ],
            out_specs=pl.BlockSpec((bm, D_MODEL), xmap),
            scratch_shapes=[pltpu.VMEM((D_MODEL, 2 * D_FF), BF16), pltpu.VMEM((D_FF, D_MODEL), BF16)],
        ),
        out_shape=jax.ShapeDtypeStruct((r, D_MODEL), F32),
        compiler_params=_cparams("arbitrary"),
        name="experts",
    )(*items, xs, w["wgu"], w["bgu"], w["wd"], w["bd"])


def _expert_items(counts, n_rows, bm):
    n_tiles = n_rows // bm
    n_items = n_tiles + N_EXPERTS - 1
    ends = jnp.cumsum(counts)
    starts = ends - counts
    first_tile = starts // bm
    per_e = jnp.where(counts > 0, (ends - 1) // bm - first_tile + 1, 0)
    item_end = jnp.cumsum(per_e)
    i = jnp.arange(n_items, dtype=jnp.int32)
    valid = i < item_end[-1]
    e = jnp.sum(jnp.minimum(i, item_end[-1] - 1)[:, None] >= item_end[None, :], axis=1).astype(jnp.int32)
    of_e = e[:, None] == jnp.arange(N_EXPERTS, dtype=jnp.int32)[None, :]
    pick = lambda table: jnp.sum(jnp.where(of_e, table[None, :], 0), axis=1).astype(jnp.int32)
    tile = jnp.where(valid, pick(first_tile) + i - pick(item_end - per_e), n_tiles - 1).astype(jnp.int32)
    one = jnp.ones((1,), jnp.int32)
    first = jnp.concatenate([one, (tile[1:] != tile[:-1]).astype(jnp.int32)])
    new_e = jnp.concatenate([one, (e[1:] != e[:-1]).astype(jnp.int32)])
    return (tile, e, first, valid.astype(jnp.int32), pick(starts), pick(ends), new_e)


def _ple_kernel(h1_ref, yg_ref, route_ref, p_ref, gple_ref, wg_ref, wp_ref, o_ref):
    h2 = h1_ref[...]
    route = route_ref[...]
    w = SC_GATHER_TOKENS
    for k in range(TOP_K):
        yk = yg_ref[:, k * w:(k + 1) * w, :].reshape(h2.shape)
        h2 = h2 + yk * route[:, TOP_K + k:TOP_K + k + 1]
    gate = jax.nn.sigmoid(_dot(_rms(h2, gple_ref[...]).astype(BF16), wg_ref[...]))
    o_ref[...] = h2 + _dot(p_ref[...].astype(BF16), wp_ref[...]) * gate


def _ple(h1, yg, route, p, w, *, tm):
    n = h1.shape[0]
    row = lambda t: (t, 0)
    return pl.pallas_call(
        _ple_kernel,
        grid=(n // tm,),
        in_specs=[
            pl.BlockSpec((tm, D_MODEL), row),
            pl.BlockSpec((tm // SC_GATHER_TOKENS, TOP_K * SC_GATHER_TOKENS, D_MODEL), lambda t: (t, 0, 0)),
            pl.BlockSpec((tm, ROUTE_W), row),
            pl.BlockSpec((tm, PLE_DIM), row),
            _full(w["gple"].shape), _full(w["wpg"].shape), _full(w["wp"].shape),
        ],
        out_specs=pl.BlockSpec((tm, D_MODEL), row),
        out_shape=jax.ShapeDtypeStruct((n, D_MODEL), F32),
        compiler_params=_cparams("parallel"),
        name="ple",
    )(h1, yg, route, p, w["gple"], w["wpg"], w["wp"])


def _sc_mesh():
    return plsc.VectorSubcoreMesh(core_axis_name="c", subcore_axis_name="s")


def _window_index(dest_t, w):
    n = dest_t.shape[1]
    idx = dest_t.reshape(TOP_K, n // w, w).transpose(1, 0, 2).reshape(n // w, TOP_K * w)
    if TOP_K * w < SC_INDEX_ROW:
        idx = jnp.concatenate([idx, jnp.zeros((n // w, SC_INDEX_ROW - TOP_K * w), jnp.int32)], axis=-1)
    return idx


def _sc_dispatch(xn, idx):
    n = xn.shape[0]
    w = SC_SCATTER_TOKENS

    @pl.kernel(out_type=jax.ShapeDtypeStruct((n * TOP_K, xn.shape[1]), xn.dtype),
               mesh=_sc_mesh(), scratch_types=[pltpu.SemaphoreType.DMA])
    def scatter(x_hbm, i_hbm, o_hbm, sem):
        def body(x_vmem, i_vmem):
            copies = [pltpu.make_async_copy(x_vmem, o_hbm.at[i_vmem.at[0, pl.ds(k * w, w)]], sem)
                      for k in range(TOP_K)]
            for cp in copies:
                cp.start()
            for cp in copies:
                cp.wait()

        pltpu.emit_pipeline(
            body,
            grid=(n // w,),
            in_specs=[pl.BlockSpec((w, xn.shape[1]), lambda i: (i, 0)),
                      pl.BlockSpec((1, SC_INDEX_ROW), lambda i: (i, 0))],
            out_specs=[],
            core_axis_name=("c", "s"),
            dimension_semantics=(pltpu.PARALLEL,),
        )(x_hbm, i_hbm)

    return scatter(xn, idx)


def _sc_combine(ys, idx):
    w = SC_GATHER_TOKENS
    n = idx.shape[0] * w
    rows = TOP_K * w

    @pl.kernel(out_type=jax.ShapeDtypeStruct((n // w, rows, D_MODEL), ys.dtype),
               mesh=_sc_mesh(), scratch_types=[])
    def gather(y_hbm, i_hbm, o_hbm):
        def body(i_vmem, o_vmem):
            pltpu.sync_copy(y_hbm.at[i_vmem.at[0, pl.ds(0, rows)]], o_vmem.at[0])

        pltpu.emit_pipeline(
            body,
            grid=(n // w,),
            in_specs=[pl.BlockSpec((1, SC_INDEX_ROW), lambda i: (i, 0))],
            out_specs=[pl.BlockSpec((1, rows, D_MODEL), lambda i: (i, 0, 0))],
            core_axis_name=("c", "s"),
            dimension_semantics=(pltpu.PARALLEL,),
        )(i_hbm, o_hbm)

    return gather(ys, idx)


def _rope_tables(pos):
    inv = ROPE_THETA ** (-jnp.arange(0, ROPE_DIM, 2, dtype=F32) / ROPE_DIM)
    ang = pos.astype(F32)[:, None] * inv[None, :]
    cos, sin = jnp.cos(ang), jnp.sin(ang)
    n = pos.shape[0]
    pad = jnp.zeros((n, LANES - QK_DIM), F32)
    ct = jnp.concatenate([jnp.ones((n, NOPE_DIM), F32), cos, cos, pad], axis=-1)
    st = jnp.concatenate([jnp.zeros((n, NOPE_DIM), F32), sin, sin, pad], axis=-1)
    return ct, st


def _rope_partner(w_rope):
    return jnp.concatenate([-w_rope[..., ROPE_HALF:], w_rope[..., :ROPE_HALF]], axis=-1)


def _head_tile(nope, rope):
    pad = jnp.zeros(rope.shape[:-1] + (LANES - QK_DIM,), rope.dtype)
    return jnp.concatenate([nope, rope, pad], axis=-1)


def _block_diag(m):
    g, a, b = m.shape
    hg = g // 2
    eye = jnp.eye(hg, dtype=m.dtype)
    mh = m.reshape(2, hg, a, b)
    return jnp.einsum("kgab,gh->kgahb", mh, eye).reshape(2, hg * a, hg * b)


def _prep_weights(g_mix, w_in, s5_a_re, s5_a_im, s5_log_dt, s5_b_re, s5_b_im, s5_c_re, s5_c_im, s5_d,
                  w_glu, b_glu, g_q_lora, w_uq, g_kv_lora, w_uk, w_uv, g_qk_q, g_qk_k, g_out_ssm,
                  g_out_mla, w_out, g_ffn, w_router, b_router, w_gate_up, b_gate_up, w_down, b_down,
                  g_ple, w_ple_gate, w_ple):
    w = {}
    row = lambda v: v.reshape(1, -1).astype(F32)
    o = SSM_W + Q_LORA + KV_LORA
    w_kr = w_in[:, o:o + ROPE_DIM]
    zeros_n = jnp.zeros((D_MODEL, NOPE_DIM), F32)
    w["win"] = jnp.concatenate(
        [w_in[:, :o], _head_tile(zeros_n, w_kr), _head_tile(zeros_n, _rope_partner(w_kr))], axis=-1).astype(BF16)
    w["gmix"] = row(g_mix)
    w["gql"] = row(g_q_lora)
    w["gkv"] = row(g_kv_lora)

    wq = w_uq.reshape(Q_LORA, N_HEADS, QK_DIM)
    q1 = _head_tile(wq[..., :NOPE_DIM], wq[..., NOPE_DIM:])
    q2 = _head_tile(jnp.zeros_like(wq[..., :NOPE_DIM]), _rope_partner(wq[..., NOPE_DIM:]))
    w["wuq"] = jnp.concatenate([q1.reshape(Q_LORA, HP), q2.reshape(Q_LORA, HP)], axis=-1).astype(BF16)
    w["wuk"] = jnp.concatenate(
        [w_uk, jnp.zeros((KV_LORA, N_HEADS, LANES - NOPE_DIM), F32)], axis=-1).reshape(KV_LORA, HP).astype(BF16)
    w["wuv"] = jnp.concatenate(
        [w_uv, jnp.zeros((KV_LORA, N_HEADS, LANES - V_DIM), F32)], axis=-1).reshape(KV_LORA, HP).astype(BF16)
    w["wuv_h"] = w_uv.transpose(1, 0, 2).astype(BF16)
    wukT = w_uk.transpose(1, 2, 0)
    w["wukT"] = jnp.concatenate(
        [wukT, jnp.zeros((N_HEADS, LANES - NOPE_DIM, KV_LORA), F32)], axis=1).astype(BF16)
    gpad = jnp.zeros((LANES - QK_DIM,), F32)
    w["gq_p"] = row(jnp.concatenate([g_qk_q * (math.log2(math.e) / math.sqrt(QK_DIM)), gpad]))
    w["gq_s"] = row(jnp.concatenate([g_qk_q, gpad]))
    w["gk"] = row(jnp.concatenate([g_qk_k, gpad]))

    dt = jnp.exp(s5_log_dt.astype(F32))[:, None]
    ar, ai = s5_a_re.astype(F32), s5_a_im.astype(F32)
    mag = jnp.exp(dt * ar)
    abar_re, abar_im = mag * jnp.cos(dt * ai), mag * jnp.sin(dt * ai)
    den = ar * ar + ai * ai
    nr, ni = abar_re - 1.0, abar_im
    coef_re = (nr * ar + ni * ai) / den
    coef_im = (ni * ar - nr * ai) / den
    br, bi = s5_b_re.astype(F32), s5_b_im.astype(F32)
    bbar_re = coef_re[..., None] * br - coef_im[..., None] * bi
    bbar_im = coef_re[..., None] * bi + coef_im[..., None] * br
    bre = _block_diag(bbar_re.transpose(0, 2, 1))
    bim = _block_diag(bbar_im.transpose(0, 2, 1))
    w["s5_bw"] = jnp.stack([bre[0], bim[0], bre[1], bim[1]]).astype(BF16)
    cre = _block_diag(s5_c_re.astype(F32).transpose(0, 2, 1))
    cim = _block_diag(-s5_c_im.astype(F32).transpose(0, 2, 1))
    w["s5_cw"] = jnp.stack([cre[0], cim[0], cre[1], cim[1]]).astype(BF16)
    w["s5_are"] = jnp.broadcast_to(abar_re.reshape(1, SSM_S), (SUBLANES, SSM_S))
    w["s5_aim"] = jnp.broadcast_to(abar_im.reshape(1, SSM_S), (SUBLANES, SSM_S))
    w["s5_d"] = row(s5_d)
    w["wglu"] = w_glu.astype(BF16)
    w["bglu"] = row(b_glu)
    w["gout_ssm"] = row(g_out_ssm)
    w["gout_mla"] = row(g_out_mla)
    w["wout"] = w_out.astype(BF16)
    w["gffn"] = row(g_ffn)
    w["wrt"] = jnp.concatenate([w_router.T, jnp.zeros((LANES - N_EXPERTS, D_MODEL), F32)], axis=0).astype(BF16)
    w["brt"] = jnp.concatenate([b_router.astype(F32), jnp.zeros((LANES - N_EXPERTS,), F32)]).reshape(LANES, 1)
    w["wgu"] = w_gate_up
    w["bgu"] = b_gate_up.reshape(N_EXPERTS, 1, 2 * D_FF).astype(F32)
    w["wd"] = w_down
    w["bd"] = b_down.reshape(N_EXPERTS, 1, D_MODEL).astype(F32)
    w["gple"] = row(g_ple)
    w["wpg"] = w_ple_gate.astype(BF16)
    w["wp"] = w_ple.astype(BF16)
    return w


def _moe_experts(mixed, w, during_dispatch=()):
    h1, xn, _, route_t, cnt = mixed
    n = h1.shape[0]
    top_i = route_t[:TOP_K].astype(jnp.int32)
    rank = route_t[2 * TOP_K:3 * TOP_K].astype(jnp.int32)
    counts = cnt[:, 0].astype(jnp.int32)
    starts = jnp.cumsum(counts) - counts
    dest = rank
    for e in range(N_EXPERTS):
        dest = dest + jnp.where(top_i == e, starts[e], 0)
    gather_idx = _window_index(dest, SC_GATHER_TOKENS)
    xs = _after(_sc_dispatch(xn, _window_index(dest, SC_SCATTER_TOKENS)), (gather_idx, during_dispatch))
    bm = min(EXPERT_ROWS, max(LANES, n * TOP_K // N_EXPERTS))
    return _experts(xs, _expert_items(counts, n * TOP_K, bm), w, bm=bm), gather_idx


def _moe_combine(mixed, ys, gather_idx, p2, w, *, tm):
    h1, _, route, _, _ = mixed
    return _ple(h1, _sc_combine(ys, gather_idx), route, p2, w, tm=tm)


def _after(x, anchor):
    return lax.optimization_barrier((x, anchor))[0]


def kernel(x_prompt, x_sample, cache_kv_latent, cache_k_rope, cache_k_rstd, state_ssm_re, state_ssm_im,
           page_table, p_prompt, p_sample, g_mix, w_in, s5_a_re, s5_a_im, s5_log_dt, s5_b_re, s5_b_im,
           s5_c_re, s5_c_im, s5_d, w_glu, b_glu, g_q_lora, w_uq, g_kv_lora, w_uk, w_uv, g_qk_q, g_qk_k,
           g_out_ssm, g_out_mla, w_out, g_ffn, w_router, b_router, w_gate_up, b_gate_up, w_down, b_down,
           g_ple, w_ple_gate, w_ple):
    assert g_mix.shape[0] == 1, "single-layer step"
    nb, t, _ = x_prompt.shape
    db, ds, _ = x_sample.shape
    assert nb == SUBLANES and db % (2 * SUBLANES) == 0, "S5 scans 8 batch rows per vreg; attn_s splits rows in two"
    assert t % min(ATTN_P_TQ, t) == 0 and t % min(TOKEN_ROWS, t) == 0 and t % min(S5_STEPS, t) == 0
    assert (db * ds) % SC_SCATTER_TOKENS == 0 and page_table.shape[1] % min(ATTN_S_PAGES, page_table.shape[1]) == 0
    w = _prep_weights(g_mix[0], w_in[0], s5_a_re[0], s5_a_im[0], s5_log_dt[0], s5_b_re[0], s5_b_im[0],
                      s5_c_re[0], s5_c_im[0], s5_d[0], w_glu[0], b_glu[0], g_q_lora[0], w_uq[0],
                      g_kv_lora[0], w_uk[0], w_uv[0], g_qk_q[0], g_qk_k[0], g_out_ssm[0], g_out_mla[0],
                      w_out[0], g_ffn[0], w_router[0], b_router[0], w_gate_up[0], b_gate_up[0],
                      w_down[0], b_down[0], g_ple[0], w_ple_gate[0], w_ple[0])

    tm = min(TOKEN_ROWS, t)
    nt = t // tm
    xp = x_prompt.reshape(nb * t, D_MODEL)
    ct, st = _rope_tables(jnp.arange(t))
    u, q, k, v, c_p, kr_p, rstd_p = _proj(xp, ct, st, w, nb=nb, nt=nt, tm=tm, sample=False)
    steps = min(S5_STEPS, t)
    o_ssm, s_fin = _s5(u.reshape(nb, t, SSM_W), jnp.zeros((nb, 2 * SSM_S), F32), w,
                       ngroups=1, nchunks=t // steps, steps=steps)
    o_mla = _attn_p(q, k, v, nb=nb, t=t, tq=min(ATTN_P_TQ, t), tk=min(ATTN_P_TK, t))
    mixed_p = _mix(xp, o_ssm.reshape(nb * t, SSM_W), o_mla, w, nb=nb, nt=nt, tm=tm)
    sr_p = s_fin[:, :SSM_S].reshape(1, nb, SSM_G, SSM_P)
    si_p = s_fin[:, SSM_S:].reshape(1, nb, SSM_G, SSM_P)

    ns = db * ds
    ng = db // SUBLANES
    xs_ = _after(x_sample.reshape(ns, D_MODEL), mixed_p[2])
    ct_s, st_s = _rope_tables(PAST_LEN + jnp.tile(jnp.arange(ds), db))
    u_s, qf, qlat, c_s, kr_s, rstd_s = _proj(xs_, ct_s, st_s, w, nb=1, nt=1, tm=ns, sample=True)
    to_scan = lambda a: a.reshape(ng, SUBLANES, ds, -1).transpose(0, 2, 1, 3).reshape(ns, -1)
    from_scan = lambda a: a.reshape(ng, ds, SUBLANES, -1).transpose(0, 2, 1, 3).reshape(ns, -1)
    h0 = jnp.concatenate([state_ssm_re[0].reshape(db, SSM_S), state_ssm_im[0].reshape(db, SSM_S)],
                         axis=-1).astype(F32)
    o_ssm_s, s_fin_s = _s5(to_scan(u_s), h0, w, ngroups=ng, nchunks=1, steps=ds)
    pad_keys = lambda a: jnp.concatenate(
        [a, jnp.zeros((db, NEW_KEYS - ds, a.shape[-1]), a.dtype)], axis=1)
    key_major = lambda a: a.transpose(0, 2, 1)
    attn_in = (page_table,
               qlat.reshape(db, ds * N_HEADS, KV_LORA),
               qf.reshape(db, ds * N_HEADS, LANES)[:, :, NOPE_DIM:QK_DIM],
               pad_keys(c_s.reshape(db, ds, KV_LORA)),
               key_major(pad_keys(kr_s.reshape(db, ds, ROPE_DIM))),
               key_major(pad_keys(rstd_s.reshape(db, ds, N_HEADS))))
    caches = (cache_kv_latent[0], key_major(cache_k_rope[0]), key_major(cache_k_rstd[0]))

    def attn_rows(lo, hi, anchor=None):
        pt, *rest = (a[lo:hi] for a in attn_in)
        if anchor is not None:
            pt = _after(pt, anchor)
        return _attn_s(pt, *rest, *caches, dec_seq=ds)

    o_lat_a = attn_rows(0, db // 2)
    ys_p, dest_p = _moe_experts(mixed_p, w, during_dispatch=o_lat_a)
    o_lat_b = attn_rows(db // 2, db, anchor=ys_p)
    o_lat = jnp.concatenate([o_lat_a, o_lat_b], axis=0)
    o_mla_s = _latent_out(o_lat.reshape(ns, N_HEADS * KV_LORA), w["wuv_h"])

    y_p = _moe_combine(mixed_p, ys_p, dest_p, p_prompt[0].reshape(nb * t, PLE_DIM), w, tm=tm)
    mixed_s = _mix(xs_, from_scan(o_ssm_s), _after(o_mla_s, y_p), w, nb=1, nt=1, tm=ns)
    ys_s, dest_s = _moe_experts(mixed_s, w)
    y_s = _moe_combine(mixed_s, ys_s, dest_s, p_sample[0].reshape(ns, PLE_DIM), w, tm=ns)

    return (y_p.reshape(nb, t, D_MODEL), y_s.reshape(db, ds, D_MODEL),
            c_p.reshape(1, nb, t, KV_LORA), kr_p.reshape(1, nb, t, ROPE_DIM), rstd_p.reshape(1, nb, t, N_HEADS),
            sr_p, si_p,
            c_s.reshape(1, db, ds, KV_LORA), kr_s.reshape(1, db, ds, ROPE_DIM), rstd_s.reshape(1, db, ds, N_HEADS),
            s_fin_s[:, :SSM_S].reshape(1, db, SSM_G, SSM_P), s_fin_s[:, SSM_S:].reshape(1, db, SSM_G, SSM_P))
```

```python
import functools
import math

import jax
import jax.numpy as jnp
from jax import lax
from jax.experimental import pallas as pl
from jax.experimental.pallas import tpu as pltpu
from jax.experimental.pallas import tpu_sc as plsc

D_MODEL = 1024
SSM_W = 512
SSM_GC = 16
SSM_G = SSM_W // SSM_GC
SSM_P = 64
SSM_S = SSM_G * SSM_P
N_HEADS = 8
NOPE_DIM = 64
ROPE_DIM = 32
ROPE_HALF = ROPE_DIM // 2
QK_DIM = NOPE_DIM + ROPE_DIM
V_DIM = 64
Q_LORA = 384
KV_LORA = 256
ROPE_THETA = 10000.0
N_EXPERTS = 32
TOP_K = 4
D_FF = D_MODEL
SWIGLU_ALPHA = 1.702
SWIGLU_LIMIT = 7.0
PLE_DIM = 256
PAST_LEN = 16384
PAGE_SIZE = 128
EPS = 1e-6
NEG = -1e30

LANES = 128
SUBLANES = 8
VMEM_LIMIT = 56 * 1024 * 1024

HP = N_HEADS * LANES
BF16 = jnp.bfloat16
F32 = jnp.float32

TOKEN_ROWS = 512
S5_STEPS = 128
ATTN_P_HEADS = 4
ATTN_P_TQ = 1024
ATTN_P_TK = 512
ATTN_S_PAGES = 64
NEW_KEYS = 16
EXPERT_ROWS = 512
SC_INDEX_ROW = 128
SC_SCATTER_TOKENS = SC_INDEX_ROW // TOP_K
SC_GATHER_TOKENS = 8


def _cparams(*sem):
    return pltpu.CompilerParams(dimension_semantics=sem, vmem_limit_bytes=VMEM_LIMIT)


def _rms(x, g):
    r = lax.rsqrt(jnp.mean(x * x, axis=-1, keepdims=True) + EPS)
    return x * r * g


def _dot(a, b):
    return jnp.dot(a, b, preferred_element_type=F32)


def _dot_nt(a, b):
    return lax.dot_general(a, b, (((1,), (1,)), ((), ())), preferred_element_type=F32)


def _full(shape):
    nd = len(shape)
    return pl.BlockSpec(shape, lambda *_: (0,) * nd)


def _proj_kernel(x_ref, ct_ref, st_ref, gmix_ref, win_ref, gql_ref, wuq_ref, gkv_ref, wuk_ref,
                 w2_ref, gq_ref, gk_ref, u_ref, q_ref, k2_ref, *rest, sample):
    if sample:
        c_ref, kr_ref, rstd_ref = rest
    else:
        v2_ref, c_ref, kr_ref, rstd_ref = rest
    xn = _rms(x_ref[...], gmix_ref[...]).astype(BF16)
    z = _dot(xn, win_ref[...])
    u_ref[...] = z[:, :SSM_W]
    o = SSM_W
    cq = z[:, o:o + Q_LORA]
    o += Q_LORA
    ckv = z[:, o:o + KV_LORA]
    o += KV_LORA
    ct = ct_ref[...]
    st = st_ref[...]
    kblock = z[:, o:o + LANES] * ct + z[:, o + LANES:o + 2 * LANES] * st
    kr_ref[...] = kblock[:, NOPE_DIM:QK_DIM]

    qq = _dot(_rms(cq, gql_ref[...]).astype(BF16), wuq_ref[...])
    c = _rms(ckv, gkv_ref[...])
    c_ref[...] = c
    cb = c.astype(BF16)
    kn = _dot(cb, wuk_ref[...])
    if not sample:
        hl = lax.broadcasted_iota(jnp.int32, (1, HP), 1) & (LANES - 1)
        v2_ref[...] = (_dot(cb, w2_ref[...]) + jnp.where(hl >= V_DIM, 1.0, 0.0)).astype(BF16)

    lane = lax.broadcasted_iota(jnp.int32, ct.shape, 1)
    rstd_all = jnp.zeros(ct.shape, F32)
    gq = gq_ref[...]
    gk = gk_ref[...]
    for h in range(N_HEADS):
        sl = slice(h * LANES, (h + 1) * LANES)
        qr = qq[:, sl] * ct + qq[:, HP + h * LANES:HP + (h + 1) * LANES] * st
        qn = qr * lax.rsqrt(jnp.sum(qr * qr, axis=-1, keepdims=True) * (1.0 / QK_DIM) + EPS) * gq
        kh = kn[:, sl] + kblock
        rs = lax.rsqrt(jnp.sum(kh * kh, axis=-1, keepdims=True) * (1.0 / QK_DIM) + EPS)
        rstd_all = jnp.where(lane == h, rs, rstd_all)
        if sample:
            qf = (qn * gk).astype(BF16)
            q_ref[:, sl] = qf
            k2_ref[:, h * KV_LORA:(h + 1) * KV_LORA] = _dot(qf, w2_ref[h]).astype(BF16)
        else:
            q_ref[:, sl] = qn.astype(BF16)
            k2_ref[:, sl] = (kh * rs * gk).astype(BF16)
    rstd_ref[...] = rstd_all[:, :N_HEADS]


def _proj(x2, ct, st, w, *, nb, nt, tm, sample):
    n = x2.shape[0]
    row = lambda b, t: (b * nt + t, 0)
    rows = lambda width: pl.BlockSpec((tm, width), row)
    k2_w = N_HEADS * KV_LORA if sample else HP
    w2 = w["wukT"] if sample else w["wuv"]
    out_shape = [
        jax.ShapeDtypeStruct((n, SSM_W), F32),
        jax.ShapeDtypeStruct((n, HP), BF16),
        jax.ShapeDtypeStruct((n, k2_w), BF16),
        *([] if sample else [jax.ShapeDtypeStruct((n, HP), BF16)]),
        jax.ShapeDtypeStruct((n, KV_LORA), F32),
        jax.ShapeDtypeStruct((n, ROPE_DIM), F32),
        jax.ShapeDtypeStruct((n, N_HEADS), F32),
    ]
    out_specs = [
        rows(SSM_W), rows(HP), rows(k2_w), *([] if sample else [rows(HP)]),
        rows(KV_LORA), rows(ROPE_DIM), rows(N_HEADS),
    ]
    in_specs = [
        rows(D_MODEL),
        pl.BlockSpec((tm, LANES), lambda b, t: (t, 0)),
        pl.BlockSpec((tm, LANES), lambda b, t: (t, 0)),
        _full(w["gmix"].shape), _full(w["win"].shape), _full(w["gql"].shape), _full(w["wuq"].shape),
        _full(w["gkv"].shape), _full(w["wuk"].shape), _full(w2.shape),
        _full(w["gq_s" if sample else "gq_p"].shape), _full(w["gk"].shape),
    ]
    return pl.pallas_call(
        functools.partial(_proj_kernel, sample=sample),
        grid=(nb, nt),
        in_specs=in_specs,
        out_specs=out_specs,
        out_shape=out_shape,
        compiler_params=_cparams("parallel", "parallel"),
        name="proj_s" if sample else "proj_p",
    )(x2, ct, st, w["gmix"], w["win"], w["gql"], w["wuq"], w["gkv"], w["wuk"], w2,
      w["gq_s" if sample else "gq_p"], w["gk"])


S5_HALF = SSM_S // 2
S5_QUARTER = SSM_S // 4


def _s5_kernel(u_ref, h0_ref, are_ref, aim_ref, bw_ref, cw_ref, dsk_ref, wglu_ref, bglu_ref, gout_ref,
               o_ref, sfin_ref, xs_ref, st_ref, *il_ref, steps):
    @pl.when(pl.program_id(1) == 0)
    def _():
        st_ref[...] = h0_ref[...]

    il = il_ref[0] if il_ref else None
    lane_tiles = SSM_W // LANES
    if il is not None:
        for b in range(SUBLANES):
            for c in range(lane_tiles):
                il[c, pl.ds(b, steps, stride=SUBLANES), :] = u_ref[b, :, c * LANES:(c + 1) * LANES]
        u = jnp.concatenate([il[c] for c in range(lane_tiles)], axis=-1)
    else:
        u = u_ref[...]
    ub = u.astype(BF16)
    half_w = SSM_W // 2
    for k in range(2):
        uk = ub[:, k * half_w:(k + 1) * half_w]
        xs_ref[:, k * S5_HALF:(k + 1) * S5_HALF] = _dot(uk, bw_ref[2 * k])
        xs_ref[:, SSM_S + k * S5_HALF:SSM_S + (k + 1) * S5_HALF] = _dot(uk, bw_ref[2 * k + 1])

    for q in range(4):
        lr = slice(q * S5_QUARTER, (q + 1) * S5_QUARTER)
        li = slice(SSM_S + q * S5_QUARTER, SSM_S + (q + 1) * S5_QUARTER)
        ar = are_ref[:, lr]
        ai = aim_ref[:, lr]

        def step(t, carry):
            sr, si = carry
            r0 = pl.multiple_of(t * SUBLANES, SUBLANES)
            nr = ar * sr - ai * si + xs_ref[pl.ds(r0, SUBLANES), lr]
            ni = ar * si + ai * sr + xs_ref[pl.ds(r0, SUBLANES), li]
            xs_ref[pl.ds(r0, SUBLANES), lr] = nr
            xs_ref[pl.ds(r0, SUBLANES), li] = ni
            return nr, ni

        sr, si = lax.fori_loop(0, steps, step, (st_ref[:, lr], st_ref[:, li]), unroll=4)
        st_ref[:, lr] = sr
        st_ref[:, li] = si
    sfin_ref[...] = st_ref[...]

    ys = []
    for k in range(2):
        sre = xs_ref[:, k * S5_HALF:(k + 1) * S5_HALF].astype(BF16)
        sim = xs_ref[:, SSM_S + k * S5_HALF:SSM_S + (k + 1) * S5_HALF].astype(BF16)
        ys.append(_dot(sre, cw_ref[2 * k]) + _dot(sim, cw_ref[2 * k + 1]))
    y = jnp.concatenate(ys, axis=-1) + dsk_ref[...] * u
    g = jax.nn.gelu(y)
    out = g * jax.nn.sigmoid(_dot(g.astype(BF16), wglu_ref[...]) + bglu_ref[...])
    on = _rms(out, gout_ref[...])
    if il is not None:
        for c in range(lane_tiles):
            il[c] = on[:, c * LANES:(c + 1) * LANES]
        for b in range(SUBLANES):
            for c in range(lane_tiles):
                o_ref[b, :, c * LANES:(c + 1) * LANES] = il[c, pl.ds(b, steps, stride=SUBLANES), :].astype(BF16)
    else:
        o_ref[...] = on.astype(BF16)


def _s5(u, h0, w, *, ngroups, nchunks, steps):
    rows = steps * SUBLANES
    batch_major = u.ndim == 3
    if batch_major:
        assert ngroups == 1 and u.shape[0] == SUBLANES
        ublock = pl.BlockSpec((SUBLANES, steps, SSM_W), lambda g, t: (0, t, 0))
    else:
        ublock = pl.BlockSpec((rows, SSM_W), lambda g, t: (g * nchunks + t, 0))
    return pl.pallas_call(
        functools.partial(_s5_kernel, steps=steps),
        grid=(ngroups, nchunks),
        in_specs=[
            ublock,
            pl.BlockSpec((SUBLANES, 2 * SSM_S), lambda g, t: (g, 0)),
            _full(w["s5_are"].shape), _full(w["s5_aim"].shape), _full(w["s5_bw"].shape),
            _full(w["s5_cw"].shape), _full(w["s5_d"].shape), _full(w["wglu"].shape),
            _full(w["bglu"].shape), _full(w["gout_ssm"].shape),
        ],
        out_specs=[
            ublock,
            pl.BlockSpec((SUBLANES, 2 * SSM_S), lambda g, t: (g, 0)),
        ],
        out_shape=[
            jax.ShapeDtypeStruct(u.shape, BF16),
            jax.ShapeDtypeStruct(h0.shape, F32),
        ],
        scratch_shapes=[pltpu.VMEM((rows, 2 * SSM_S), F32), pltpu.VMEM((SUBLANES, 2 * SSM_S), F32)]
        + ([pltpu.VMEM((SSM_W // LANES, rows, LANES), F32)] if batch_major else []),
        compiler_params=_cparams("parallel", "arbitrary"),
        name="s5",
    )(u, h0, w["s5_are"], w["s5_aim"], w["s5_bw"], w["s5_cw"], w["s5_d"], w["wglu"], w["bglu"],
      w["gout_ssm"])


def _attn_p_kernel(q_ref, k_ref, v_ref, o_ref, *, tq, tk):
    i = pl.program_id(2)
    nd = tq // tk
    row = lax.broadcasted_iota(jnp.int32, (tk, tk), 0)
    col = lax.broadcasted_iota(jnp.int32, (tk, tk), 1)

    def block(j, carry, diag):
        k0 = pl.multiple_of(j * tk, tk)
        k2 = k_ref[pl.ds(k0, tk), :]
        v2 = v_ref[pl.ds(k0, tk), :]
        r0 = 0 if diag is None else diag * tk
        out = []
        for hh in range(ATTN_P_HEADS):
            m, acc = carry[hh]
            s = _dot_nt(q_ref[r0:, hh * LANES:(hh + 1) * LANES], k2[:, hh * LANES:(hh + 1) * LANES])
            if diag is not None:
                tri = jnp.where(col <= row, s[:tk], NEG)
                s = tri if r0 + tk == tq else jnp.concatenate([tri, s[tk:]], axis=0)
            m_new = jnp.maximum(m[r0:], jnp.max(s, axis=-1, keepdims=True))
            p = jnp.exp2(s - m_new).astype(BF16)
            acc_new = jnp.exp2(m[r0:] - m_new) * acc[r0:] + _dot(p, v2[:, hh * LANES:(hh + 1) * LANES])
            if r0:
                m_new = jnp.concatenate([m[:r0], m_new], axis=0)
                acc_new = jnp.concatenate([acc[:r0], acc_new], axis=0)
            out.append((m_new, acc_new))
        return tuple(out)

    init = tuple((jnp.full((tq, 1), NEG, F32), jnp.zeros((tq, LANES), F32)) for _ in range(ATTN_P_HEADS))
    carry = lax.fori_loop(0, i * nd, lambda j, c: block(j, c, None), init)
    for d in range(nd):
        carry = block(i * nd + d, carry, d)
    o = [acc / pltpu.roll(acc, V_DIM, 1) for _, acc in carry]
    lane = lax.broadcasted_iota(jnp.int32, (tq, LANES), 1)
    for pr in range(ATTN_P_HEADS // 2):
        o_ref[:, pr * LANES:(pr + 1) * LANES] = jnp.where(
            lane < V_DIM, o[2 * pr], pltpu.roll(o[2 * pr + 1], V_DIM, 1))


def _attn_p(q, k, v, *, nb, t, tq, tk):
    nq = t // tq
    hw = ATTN_P_HEADS * LANES
    return pl.pallas_call(
        functools.partial(_attn_p_kernel, tq=tq, tk=tk),
        grid=(nb, N_HEADS // ATTN_P_HEADS, nq),
        in_specs=[
            pl.BlockSpec((tq, hw), lambda b, h, i: (b * nq + i, h)),
            pl.BlockSpec((t, hw), lambda b, h, i: (b, h)),
            pl.BlockSpec((t, hw), lambda b, h, i: (b, h)),
        ],
        out_specs=pl.BlockSpec((tq, ATTN_P_HEADS * V_DIM), lambda b, h, i: (b * nq + i, h)),
        out_shape=jax.ShapeDtypeStruct((nb * t, N_HEADS * V_DIM), F32),
        compiler_params=_cparams("parallel", "parallel", "arbitrary"),
        name="attn_p",
    )(q, k, v)


def _attn_s_kernel(pt_ref, ql_ref, qr_ref, cn_ref, krn_ref, rnt_ref, cc_hbm, ckr_hbm, cr_hbm,
                   o_ref, cbuf, krbuf, rbuf, cb_ref, sc_ref, sems, *, n_pages, dec_seq):
    b = pl.program_id(0)
    nb = pl.num_programs(0)
    bp = min(ATTN_S_PAGES, n_pages)
    nblk = n_pages // bp
    keys = bp * PAGE_SIZE
    nrow = dec_seq * N_HEADS
    scale = 1.0 / math.sqrt(QK_DIM)
    slot = b % 2

    def start_block(bb, i, dst):
        for p in range(bp):
            pg = i * bp + p
            page = pt_ref[bb * n_pages + pg]
            lanes = pl.ds(pl.multiple_of(pg * PAGE_SIZE, PAGE_SIZE), PAGE_SIZE)
            pltpu.make_async_copy(cc_hbm.at[page], cbuf.at[dst, pg], sems.at[dst, 0]).start()
            pltpu.make_async_copy(ckr_hbm.at[page], krbuf.at[dst, :, lanes], sems.at[dst, 1]).start()
            pltpu.make_async_copy(cr_hbm.at[page], rbuf.at[dst, :, lanes], sems.at[dst, 2]).start()

    @pl.when(b == 0)
    def _():
        lax.fori_loop(0, nblk, lambda i, c: (start_block(0, i, 0), c)[1], 0)

    def wait_row(s):
        pltpu.make_async_copy(cbuf.at[s], cbuf.at[s], sems.at[s, 0]).wait()
        pltpu.make_async_copy(krbuf.at[s], krbuf.at[s], sems.at[s, 1]).wait()
        pltpu.make_async_copy(rbuf.at[s], rbuf.at[s], sems.at[s, 2]).wait()

    wait_row(slot)

    ql = ql_ref[0]
    qr = qr_ref[0]

    def tile_heads(r_t):
        return jnp.concatenate([r_t] * dec_seq, axis=0)

    cnb = cn_ref[0].astype(BF16)
    s_n = (_dot_nt(ql, cnb) + _dot(qr, krn_ref[0].astype(BF16))) * tile_heads(rnt_ref[0]) * scale
    qs = lax.broadcasted_iota(jnp.int32, (nrow, NEW_KEYS), 0) // N_HEADS
    kj = lax.broadcasted_iota(jnp.int32, (nrow, NEW_KEYS), 1)
    s_n = jnp.where(kj <= qs, s_n, NEG)

    def score_block(i, m):
        @pl.when(b + 1 < nb)
        def _():
            start_block(b + 1, i, 1 - slot)

        k0 = pl.multiple_of(i * keys, keys)
        cb = cbuf[slot, pl.ds(i * bp, bp)].reshape(keys, KV_LORA).astype(BF16)
        cb_ref[pl.ds(k0, keys), :] = cb
        raw = _dot_nt(ql, cb) + _dot(qr, krbuf[slot, :, pl.ds(k0, keys)].astype(BF16))
        s = raw * tile_heads(rbuf[slot, :, pl.ds(k0, keys)]) * scale
        sc_ref[:, pl.ds(k0, keys)] = s
        return jnp.maximum(m, jnp.max(s, axis=-1, keepdims=True))

    m = lax.fori_loop(0, nblk, score_block, jnp.max(s_n, axis=-1, keepdims=True))
    p_n = jnp.exp(s_n - m)

    def value_block(i, carry):
        l, acc = carry
        k0 = pl.multiple_of(i * keys, keys)
        p = jnp.exp(sc_ref[:, pl.ds(k0, keys)] - m)
        return (l + jnp.sum(p, axis=-1, keepdims=True),
                acc + _dot(p.astype(BF16), cb_ref[pl.ds(k0, keys), :]))

    l, acc = lax.fori_loop(0, nblk, value_block,
                           (jnp.sum(p_n, axis=-1, keepdims=True), _dot(p_n.astype(BF16), cnb)))
    o_ref[0] = acc / l


def _attn_s(page_table, ql, qr, cn, krn, rnt, cache_c, cache_kr, cache_r, *, dec_seq):
    nb, n_pages = page_table.shape
    nrow = dec_seq * N_HEADS
    b3 = lambda b, pt: (b, 0, 0)
    return pl.pallas_call(
        functools.partial(_attn_s_kernel, n_pages=n_pages, dec_seq=dec_seq),
        grid_spec=pltpu.PrefetchScalarGridSpec(
            num_scalar_prefetch=1,
            grid=(nb,),
            in_specs=[
                pl.BlockSpec((1, nrow, KV_LORA), b3),
                pl.BlockSpec((1, nrow, ROPE_DIM), b3),
                pl.BlockSpec((1, NEW_KEYS, KV_LORA), b3),
                pl.BlockSpec((1, ROPE_DIM, NEW_KEYS), b3),
                pl.BlockSpec((1, N_HEADS, NEW_KEYS), b3),
                pl.BlockSpec(memory_space=pl.ANY),
                pl.BlockSpec(memory_space=pl.ANY),
                pl.BlockSpec(memory_space=pl.ANY),
            ],
            out_specs=pl.BlockSpec((1, nrow, KV_LORA), b3),
            scratch_shapes=[
                pltpu.VMEM((2, n_pages, PAGE_SIZE, KV_LORA), F32),
                pltpu.VMEM((2, ROPE_DIM, n_pages * PAGE_SIZE), F32),
                pltpu.VMEM((2, N_HEADS, n_pages * PAGE_SIZE), F32),
                pltpu.VMEM((n_pages * PAGE_SIZE, KV_LORA), BF16),
                pltpu.VMEM((nrow, n_pages * PAGE_SIZE), F32),
                pltpu.SemaphoreType.DMA((2, 3)),
            ],
        ),
        out_shape=jax.ShapeDtypeStruct((nb, nrow, KV_LORA), F32),
        compiler_params=_cparams("arbitrary"),
        name="attn_s",
    )(page_table.reshape(-1), ql, qr, cn, krn, rnt, cache_c, cache_kr, cache_r)


def _latent_out_kernel(ol_ref, wuv_ref, o_ref):
    for h in range(N_HEADS):
        o_ref[:, h * V_DIM:(h + 1) * V_DIM] = _dot(
            ol_ref[:, h * KV_LORA:(h + 1) * KV_LORA].astype(BF16), wuv_ref[h])


def _latent_out(ol, wuv_h):
    n = ol.shape[0]
    return pl.pallas_call(
        _latent_out_kernel,
        grid=(1,),
        in_specs=[_full(ol.shape), _full(wuv_h.shape)],
        out_specs=_full((n, N_HEADS * V_DIM)),
        out_shape=jax.ShapeDtypeStruct((n, N_HEADS * V_DIM), F32),
        compiler_params=_cparams("arbitrary"),
        name="latent_out",
    )(ol, wuv_h)


ROUTE_W = 16
PACKED_W = D_MODEL // 2


def _pack_bf16_halves(xb):
    bits = lambda v: pltpu.bitcast(v.astype(F32), jnp.uint32)
    return (bits(xb[:, :PACKED_W]) >> 16) | bits(xb[:, PACKED_W:])


def _unpack_bf16_halves(words):
    lo = pltpu.bitcast(words << 16, F32)
    hi = pltpu.bitcast(words & jnp.uint32(0xFFFF0000), F32)
    return jnp.concatenate([lo, hi], axis=-1).astype(BF16)


def _mix_kernel(h_ref, os_ref, om_ref, gom_ref, wout_ref, gffn_ref, wrt_ref, brt_ref,
                h1_ref, xn_ref, route_ref, routet_ref, cnt_ref):
    @pl.when((pl.program_id(0) == 0) & (pl.program_id(1) == 0))
    def _():
        cnt_ref[...] = jnp.zeros_like(cnt_ref)

    tm = h_ref.shape[0]
    omn = _rms(om_ref[...], gom_ref[...]).astype(BF16)
    h1 = h_ref[...] + _dot(os_ref[...], wout_ref[:SSM_W, :]) + _dot(omn, wout_ref[SSM_W:, :])
    h1_ref[...] = h1
    xb = _rms(h1, gffn_ref[...]).astype(BF16)
    xn_ref[...] = _pack_bf16_halves(xb)
    work = (_dot_nt(wrt_ref[...], xb) + brt_ref[...])[:N_EXPERTS]

    eidx = lax.broadcasted_iota(jnp.int32, work.shape, 0).astype(F32)
    sels, vals, rows = [], [], []
    for k in range(TOP_K):
        mk = jnp.max(work, axis=0, keepdims=True)
        ik = jnp.min(jnp.where(work == mk, eidx, float(N_EXPERTS)), axis=0, keepdims=True)
        sel = eidx == ik
        work = jnp.where(sel, -jnp.inf, work)
        sels.append(sel)
        vals.append(mk)
        rows.append(ik)
    es = [jnp.exp(v - vals[0]) for v in vals]
    den = es[0] + es[1] + es[2] + es[3]
    rows += [e / den for e in es]
    onehot = jnp.zeros(eidx.shape, F32)
    for sel in sels:
        onehot = jnp.where(sel, 1.0, onehot)

    t_j = lax.broadcasted_iota(jnp.int32, (tm, tm), 0)
    t_i = lax.broadcasted_iota(jnp.int32, (tm, tm), 1)
    earlier = jnp.where(t_j < t_i, 1.0, 0.0).astype(BF16)
    cnt = cnt_ref[...]
    cum = _dot(onehot.astype(BF16), earlier) + jnp.concatenate([cnt] * pl.cdiv(tm, LANES), axis=1)[:, :tm]
    rows += [jnp.sum(jnp.where(sel, cum, 0.0), axis=0, keepdims=True) for sel in sels]
    cnt_ref[...] = cnt + jnp.sum(onehot, axis=1, keepdims=True)

    sub = lax.broadcasted_iota(jnp.int32, (ROUTE_W, tm), 0)
    rt = jnp.zeros((ROUTE_W, tm), F32)
    for r, v in enumerate(rows):
        rt = jnp.where(sub == r, v, rt)
    routet_ref[...] = rt
    route_ref[...] = jnp.concatenate([rt, jnp.zeros((LANES - ROUTE_W, tm), F32)], axis=0).T[:, :ROUTE_W]


def _mix(h2, o_ssm, o_mla, w, *, nb, nt, tm):
    n = h2.shape[0]
    row = lambda b, t: (b * nt + t, 0)
    return pl.pallas_call(
        _mix_kernel,
        grid=(nb, nt),
        in_specs=[
            pl.BlockSpec((tm, D_MODEL), row),
            pl.BlockSpec((tm, SSM_W), row),
            pl.BlockSpec((tm, N_HEADS * V_DIM), row),
            _full(w["gout_mla"].shape), _full(w["wout"].shape), _full(w["gffn"].shape),
            _full(w["wrt"].shape), _full(w["brt"].shape),
        ],
        out_specs=[
            pl.BlockSpec((tm, D_MODEL), row),
            pl.BlockSpec((tm, PACKED_W), row),
            pl.BlockSpec((tm, ROUTE_W), row),
            pl.BlockSpec((ROUTE_W, tm), lambda b, t: (0, b * nt + t)),
            _full((N_EXPERTS, LANES)),
        ],
        out_shape=[
            jax.ShapeDtypeStruct((n, D_MODEL), F32),
            jax.ShapeDtypeStruct((n, PACKED_W), jnp.uint32),
            jax.ShapeDtypeStruct((n, ROUTE_W), F32),
            jax.ShapeDtypeStruct((ROUTE_W, n), F32),
            jax.ShapeDtypeStruct((N_EXPERTS, LANES), F32),
        ],
        compiler_params=_cparams("arbitrary", "arbitrary"),
        name="mix",
    )(h2, o_ssm, o_mla, w["gout_mla"], w["wout"], w["gffn"], w["wrt"], w["brt"])


def _experts_kernel(tile_ref, e_ref, first_ref, valid_ref, lo_ref, hi_ref, newe_ref,
                    x_ref, wgu_ref, bgu_ref, wd_ref, bd_ref, o_ref, wgu_bf, wd_bf):
    i = pl.program_id(0)

    @pl.when(newe_ref[i] == 1)
    def _():
        wgu_bf[...] = wgu_ref[0].astype(BF16)
        wd_bf[...] = wd_ref[0].astype(BF16)

    @pl.when(valid_ref[i] == 1)
    def _():
        bm = x_ref.shape[0]
        hdn = _dot(_unpack_bf16_halves(x_ref[...]), wgu_bf[...]) + bgu_ref[0]
        x_glu = jnp.minimum(hdn[:, :D_FF], SWIGLU_LIMIT)
        x_lin = jnp.clip(hdn[:, D_FF:], -SWIGLU_LIMIT, SWIGLU_LIMIT)
        act = x_glu * jax.nn.sigmoid(SWIGLU_ALPHA * x_glu) * (x_lin + 1.0)
        y = _dot(act.astype(BF16), wd_bf[...]) + bd_ref[0]
        rows = tile_ref[i] * bm + lax.broadcasted_iota(jnp.int32, (bm, 1), 0)
        mine = (rows >= lo_ref[i]) & (rows < hi_ref[i])

        @pl.when(first_ref[i] == 1)
        def _():
            o_ref[...] = jnp.where(mine, y, 0.0)

        @pl.when(first_ref[i] == 0)
        def _():
            o_ref[...] = jnp.where(mine, y, o_ref[...])


def _experts(xs, items, w, *, bm):
    r = xs.shape[0]
    n_items = items[0].shape[0]
    xmap = lambda i, tile, *_: (tile[i], 0)
    emap = lambda i, tile, e, *_: (e[i], 0, 0)
    return pl.pallas_call(
        _experts_kernel,
        grid_spec=pltpu.PrefetchScalarGridSpec(
            num_scalar_prefetch=len(items),
            grid=(n_items,),
            in_specs=[
                pl.BlockSpec((bm, PACKED_W), xmap),
                pl.BlockSpec((1, D_MODEL, 2 * D_FF), emap),
                pl.BlockSpec((1, 1, 2 * D_FF), emap),
                pl.BlockSpec((1, D_FF, D_MODEL), emap),
                pl.BlockSpec((1, 1, D_MODEL), emap),
            ],
            out_specs=pl.BlockSpec((bm, D_MODEL), xmap),
            scratch_shapes=[pltpu.VMEM((D_MODEL, 2 * D_FF), BF16), pltpu.VMEM((D_FF, D_MODEL), BF16)],
        ),
        out_shape=jax.ShapeDtypeStruct((r, D_MODEL), F32),
        compiler_params=_cparams("arbitrary"),
        name="experts",
    )(*items, xs, w["wgu"], w["bgu"], w["wd"], w["bd"])


def _expert_items(counts, n_rows, bm):
    n_tiles = n_rows // bm
    n_items = n_tiles + N_EXPERTS - 1
    ends = jnp.cumsum(counts)
    starts = ends - counts
    first_tile = starts // bm
    per_e = jnp.where(counts > 0, (ends - 1) // bm - first_tile + 1, 0)
    item_end = jnp.cumsum(per_e)
    i = jnp.arange(n_items, dtype=jnp.int32)
    valid = i < item_end[-1]
    e = jnp.sum(jnp.minimum(i, item_end[-1] - 1)[:, None] >= item_end[None, :], axis=1).astype(jnp.int32)
    of_e = e[:, None] == jnp.arange(N_EXPERTS, dtype=jnp.int32)[None, :]
    pick = lambda table: jnp.sum(jnp.where(of_e, table[None, :], 0), axis=1).astype(jnp.int32)
    tile = jnp.where(valid, pick(first_tile) + i - pick(item_end - per_e), n_tiles - 1).astype(jnp.int32)
    one = jnp.ones((1,), jnp.int32)
    first = jnp.concatenate([one, (tile[1:] != tile[:-1]).astype(jnp.int32)])
    new_e = jnp.concatenate([one, (e[1:] != e[:-1]).astype(jnp.int32)])
    return (tile, e, first, valid.astype(jnp.int32), pick(starts), pick(ends), new_e)


def _ple_kernel(h1_ref, yg_ref, route_ref, p_ref, gple_ref, wg_ref, wp_ref, o_ref):
    h2 = h1_ref[...]
    route = route_ref[...]
    w = SC_GATHER_TOKENS
    for k in range(TOP_K):
        yk = yg_ref[:, k * w:(k + 1) * w, :].reshape(h2.shape)
        h2 = h2 + yk * route[:, TOP_K + k:TOP_K + k + 1]
    gate = jax.nn.sigmoid(_dot(_rms(h2, gple_ref[...]).astype(BF16), wg_ref[...]))
    o_ref[...] = h2 + _dot(p_ref[...].astype(BF16), wp_ref[...]) * gate


def _ple(h1, yg, route, p, w, *, tm):
    n = h1.shape[0]
    row = lambda t: (t, 0)
    return pl.pallas_call(
        _ple_kernel,
        grid=(n // tm,),
        in_specs=[
            pl.BlockSpec((tm, D_MODEL), row),
            pl.BlockSpec((tm // SC_GATHER_TOKENS, TOP_K * SC_GATHER_TOKENS, D_MODEL), lambda t: (t, 0, 0)),
            pl.BlockSpec((tm, ROUTE_W), row),
            pl.BlockSpec((tm, PLE_DIM), row),
            _full(w["gple"].shape), _full(w["wpg"].shape), _full(w["wp"].shape),
        ],
        out_specs=pl.BlockSpec((tm, D_MODEL), row),
        out_shape=jax.ShapeDtypeStruct((n, D_MODEL), F32),
        compiler_params=_cparams("parallel"),
        name="ple",
    )(h1, yg, route, p, w["gple"], w["wpg"], w["wp"])


def _sc_mesh():
    return plsc.VectorSubcoreMesh(core_axis_name="c", subcore_axis_name="s")


def _window_index(dest_t, w):
    n = dest_t.shape[1]
    idx = dest_t.reshape(TOP_K, n // w, w).transpose(1, 0, 2).reshape(n // w, TOP_K * w)
    if TOP_K * w < SC_INDEX_ROW:
        idx = jnp.concatenate([idx, jnp.zeros((n // w, SC_INDEX_ROW - TOP_K * w), jnp.int32)], axis=-1)
    return idx


def _sc_dispatch(xn, idx):
    n = xn.shape[0]
    w = SC_SCATTER_TOKENS

    @pl.kernel(out_type=jax.ShapeDtypeStruct((n * TOP_K, xn.shape[1]), xn.dtype),
               mesh=_sc_mesh(), scratch_types=[pltpu.SemaphoreType.DMA])
    def scatter(x_hbm, i_hbm, o_hbm, sem):
        def body(x_vmem, i_vmem):
            copies = [pltpu.make_async_copy(x_vmem, o_hbm.at[i_vmem.at[0, pl.ds(k * w, w)]], sem)
                      for k in range(TOP_K)]
            for cp in copies:
                cp.start()
            for cp in copies:
                cp.wait()

        pltpu.emit_pipeline(
            body,
            grid=(n // w,),
            in_specs=[pl.BlockSpec((w, xn.shape[1]), lambda i: (i, 0)),
                      pl.BlockSpec((1, SC_INDEX_ROW), lambda i: (i, 0))],
            out_specs=[],
            core_axis_name=("c", "s"),
            dimension_semantics=(pltpu.PARALLEL,),
        )(x_hbm, i_hbm)

    return scatter(xn, idx)


def _sc_combine(ys, idx):
    w = SC_GATHER_TOKENS
    n = idx.shape[0] * w
    rows = TOP_K * w

    @pl.kernel(out_type=jax.ShapeDtypeStruct((n // w, rows, D_MODEL), ys.dtype),
               mesh=_sc_mesh(), scratch_types=[])
    def gather(y_hbm, i_hbm, o_hbm):
        def body(i_vmem, o_vmem):
            pltpu.sync_copy(y_hbm.at[i_vmem.at[0, pl.ds(0, rows)]], o_vmem.at[0])

        pltpu.emit_pipeline(
            body,
            grid=(n // w,),
            in_specs=[pl.BlockSpec((1, SC_INDEX_ROW), lambda i: (i, 0))],
            out_specs=[pl.BlockSpec((1, rows, D_MODEL), lambda i: (i, 0, 0))],
            core_axis_name=("c", "s"),
            dimension_semantics=(pltpu.PARALLEL,),
        )(i_hbm, o_hbm)

    return gather(ys, idx)


def _rope_tables(pos):
    inv = ROPE_THETA ** (-jnp.arange(0, ROPE_DIM, 2, dtype=F32) / ROPE_DIM)
    ang = pos.astype(F32)[:, None] * inv[None, :]
    cos, sin = jnp.cos(ang), jnp.sin(ang)
    n = pos.shape[0]
    pad = jnp.zeros((n, LANES - QK_DIM), F32)
    ct = jnp.concatenate([jnp.ones((n, NOPE_DIM), F32), cos, cos, pad], axis=-1)
    st = jnp.concatenate([jnp.zeros((n, NOPE_DIM), F32), sin, sin, pad], axis=-1)
    return ct, st


def _rope_partner(w_rope):
    return jnp.concatenate([-w_rope[..., ROPE_HALF:], w_rope[..., :ROPE_HALF]], axis=-1)


def _head_tile(nope, rope):
    pad = jnp.zeros(rope.shape[:-1] + (LANES - QK_DIM,), rope.dtype)
    return jnp.concatenate([nope, rope, pad], axis=-1)


def _block_diag(m):
    g, a, b = m.shape
    hg = g // 2
    eye = jnp.eye(hg, dtype=m.dtype)
    mh = m.reshape(2, hg, a, b)
    return jnp.einsum("kgab,gh->kgahb", mh, eye).reshape(2, hg * a, hg * b)


def _prep_weights(g_mix, w_in, s5_a_re, s5_a_im, s5_log_dt, s5_b_re, s5_b_im, s5_c_re, s5_c_im, s5_d,
                  w_glu, b_glu, g_q_lora, w_uq, g_kv_lora, w_uk, w_uv, g_qk_q, g_qk_k, g_out_ssm,
                  g_out_mla, w_out, g_ffn, w_router, b_router, w_gate_up, b_gate_up, w_down, b_down,
                  g_ple, w_ple_gate, w_ple):
    w = {}
    row = lambda v: v.reshape(1, -1).astype(F32)
    o = SSM_W + Q_LORA + KV_LORA
    w_kr = w_in[:, o:o + ROPE_DIM]
    zeros_n = jnp.zeros((D_MODEL, NOPE_DIM), F32)
    w["win"] = jnp.concatenate(
        [w_in[:, :o], _head_tile(zeros_n, w_kr), _head_tile(zeros_n, _rope_partner(w_kr))], axis=-1).astype(BF16)
    w["gmix"] = row(g_mix)
    w["gql"] = row(g_q_lora)
    w["gkv"] = row(g_kv_lora)

    wq = w_uq.reshape(Q_LORA, N_HEADS, QK_DIM)
    q1 = _head_tile(wq[..., :NOPE_DIM], wq[..., NOPE_DIM:])
    q2 = _head_tile(jnp.zeros_like(wq[..., :NOPE_DIM]), _rope_partner(wq[..., NOPE_DIM:]))
    w["wuq"] = jnp.concatenate([q1.reshape(Q_LORA, HP), q2.reshape(Q_LORA, HP)], axis=-1).astype(BF16)
    w["wuk"] = jnp.concatenate(
        [w_uk, jnp.zeros((KV_LORA, N_HEADS, LANES - NOPE_DIM), F32)], axis=-1).reshape(KV_LORA, HP).astype(BF16)
    w["wuv"] = jnp.concatenate(
        [w_uv, jnp.zeros((KV_LORA, N_HEADS, LANES - V_DIM), F32)], axis=-1).reshape(KV_LORA, HP).astype(BF16)
    w["wuv_h"] = w_uv.transpose(1, 0, 2).astype(BF16)
    wukT = w_uk.transpose(1, 2, 0)
    w["wukT"] = jnp.concatenate(
        [wukT, jnp.zeros((N_HEADS, LANES - NOPE_DIM, KV_LORA), F32)], axis=1).astype(BF16)
    gpad = jnp.zeros((LANES - QK_DIM,), F32)
    w["gq_p"] = row(jnp.concatenate([g_qk_q * (math.log2(math.e) / math.sqrt(QK_DIM)), gpad]))
    w["gq_s"] = row(jnp.concatenate([g_qk_q, gpad]))
    w["gk"] = row(jnp.concatenate([g_qk_k, gpad]))

    dt = jnp.exp(s5_log_dt.astype(F32))[:, None]
    ar, ai = s5_a_re.astype(F32), s5_a_im.astype(F32)
    mag = jnp.exp(dt * ar)
    abar_re, abar_im = mag * jnp.cos(dt * ai), mag * jnp.sin(dt * ai)
    den = ar * ar + ai * ai
    nr, ni = abar_re - 1.0, abar_im
    coef_re = (nr * ar + ni * ai) / den
    coef_im = (ni * ar - nr * ai) / den
    br, bi = s5_b_re.astype(F32), s5_b_im.astype(F32)
    bbar_re = coef_re[..., None] * br - coef_im[..., None] * bi
    bbar_im = coef_re[..., None] * bi + coef_im[..., None] * br
    bre = _block_diag(bbar_re.transpose(0, 2, 1))
    bim = _block_diag(bbar_im.transpose(0, 2, 1))
    w["s5_bw"] = jnp.stack([bre[0], bim[0], bre[1], bim[1]]).astype(BF16)
    cre = _block_diag(s5_c_re.astype(F32).transpose(0, 2, 1))
    cim = _block_diag(-s5_c_im.astype(F32).transpose(0, 2, 1))
    w["s5_cw"] = jnp.stack([cre[0], cim[0], cre[1], cim[1]]).astype(BF16)
    w["s5_are"] = jnp.broadcast_to(abar_re.reshape(1, SSM_S), (SUBLANES, SSM_S))
    w["s5_aim"] = jnp.broadcast_to(abar_im.reshape(1, SSM_S), (SUBLANES, SSM_S))
    w["s5_d"] = row(s5_d)
    w["wglu"] = w_glu.astype(BF16)
    w["bglu"] = row(b_glu)
    w["gout_ssm"] = row(g_out_ssm)
    w["gout_mla"] = row(g_out_mla)
    w["wout"] = w_out.astype(BF16)
    w["gffn"] = row(g_ffn)
    w["wrt"] = jnp.concatenate([w_router.T, jnp.zeros((LANES - N_EXPERTS, D_MODEL), F32)], axis=0).astype(BF16)
    w["brt"] = jnp.concatenate([b_router.astype(F32), jnp.zeros((LANES - N_EXPERTS,), F32)]).reshape(LANES, 1)
    w["wgu"] = w_gate_up
    w["bgu"] = b_gate_up.reshape(N_EXPERTS, 1, 2 * D_FF).astype(F32)
    w["wd"] = w_down
    w["bd"] = b_down.reshape(N_EXPERTS, 1, D_MODEL).astype(F32)
    w["gple"] = row(g_ple)
    w["wpg"] = w_ple_gate.astype(BF16)
    w["wp"] = w_ple.astype(BF16)
    return w


def _moe_experts(mixed, w, during_dispatch=()):
    h1, xn, _, route_t, cnt = mixed
    n = h1.shape[0]
    top_i = route_t[:TOP_K].astype(jnp.int32)
    rank = route_t[2 * TOP_K:3 * TOP_K].astype(jnp.int32)
    counts = cnt[:, 0].astype(jnp.int32)
    starts = jnp.cumsum(counts) - counts
    dest = rank
    for e in range(N_EXPERTS):
        dest = dest + jnp.where(top_i == e, starts[e], 0)
    gather_idx = _window_index(dest, SC_GATHER_TOKENS)
    xs = _after(_sc_dispatch(xn, _window_index(dest, SC_SCATTER_TOKENS)), (gather_idx, during_dispatch))
    bm = min(EXPERT_ROWS, max(LANES, n * TOP_K // N_EXPERTS))
    return _experts(xs, _expert_items(counts, n * TOP_K, bm), w, bm=bm), gather_idx


def _moe_combine(mixed, ys, gather_idx, p2, w, *, tm):
    h1, _, route, _, _ = mixed
    return _ple(h1, _sc_combine(ys, gather_idx), route, p2, w, tm=tm)


def _after(x, anchor):
    return lax.optimization_barrier((x, anchor))[0]


def kernel(x_prompt, x_sample, cache_kv_latent, cache_k_rope, cache_k_rstd, state_ssm_re, state_ssm_im,
           page_table, p_prompt, p_sample, g_mix, w_in, s5_a_re, s5_a_im, s5_log_dt, s5_b_re, s5_b_im,
           s5_c_re, s5_c_im, s5_d, w_glu, b_glu, g_q_lora, w_uq, g_kv_lora, w_uk, w_uv, g_qk_q, g_qk_k,
           g_out_ssm, g_out_mla, w_out, g_ffn, w_router, b_router, w_gate_up, b_gate_up, w_down, b_down,
           g_ple, w_ple_gate, w_ple):
    assert g_mix.shape[0] == 1, "single-layer step"
    nb, t, _ = x_prompt.shape
    db, ds, _ = x_sample.shape
    assert nb == SUBLANES and db % (2 * SUBLANES) == 0, "S5 scans 8 batch rows per vreg; attn_s splits rows in two"
    assert t % min(ATTN_P_TQ, t) == 0 and t % min(TOKEN_ROWS, t) == 0 and t % min(S5_STEPS, t) == 0
    assert (db * ds) % SC_SCATTER_TOKENS == 0 and page_table.shape[1] % min(ATTN_S_PAGES, page_table.shape[1]) == 0
    w = _prep_weights(g_mix[0], w_in[0], s5_a_re[0], s5_a_im[0], s5_log_dt[0], s5_b_re[0], s5_b_im[0],
                      s5_c_re[0], s5_c_im[0], s5_d[0], w_glu[0], b_glu[0], g_q_lora[0], w_uq[0],
                      g_kv_lora[0], w_uk[0], w_uv[0], g_qk_q[0], g_qk_k[0], g_out_ssm[0], g_out_mla[0],
                      w_out[0], g_ffn[0], w_router[0], b_router[0], w_gate_up[0], b_gate_up[0],
                      w_down[0], b_down[0], g_ple[0], w_ple_gate[0], w_ple[0])

    tm = min(TOKEN_ROWS, t)
    nt = t // tm
    xp = x_prompt.reshape(nb * t, D_MODEL)
    ct, st = _rope_tables(jnp.arange(t))
    u, q, k, v, c_p, kr_p, rstd_p = _proj(xp, ct, st, w, nb=nb, nt=nt, tm=tm, sample=False)
    steps = min(S5_STEPS, t)
    o_ssm, s_fin = _s5(u.reshape(nb, t, SSM_W), jnp.zeros((nb, 2 * SSM_S), F32), w,
                       ngroups=1, nchunks=t // steps, steps=steps)
    o_mla = _attn_p(q, k, v, nb=nb, t=t, tq=min(ATTN_P_TQ, t), tk=min(ATTN_P_TK, t))
    mixed_p = _mix(xp, o_ssm.reshape(nb * t, SSM_W), o_mla, w, nb=nb, nt=nt, tm=tm)
    sr_p = s_fin[:, :SSM_S].reshape(1, nb, SSM_G, SSM_P)
    si_p = s_fin[:, SSM_S:].reshape(1, nb, SSM_G, SSM_P)

    ns = db * ds
    ng = db // SUBLANES
    xs_ = _after(x_sample.reshape(ns, D_MODEL), mixed_p[2])
    ct_s, st_s = _rope_tables(PAST_LEN + jnp.tile(jnp.arange(ds), db))
    u_s, qf, qlat, c_s, kr_s, rstd_s = _proj(xs_, ct_s, st_s, w, nb=1, nt=1, tm=ns, sample=True)
    to_scan = lambda a: a.reshape(ng, SUBLANES, ds, -1).transpose(0, 2, 1, 3).reshape(ns, -1)
    from_scan = lambda a: a.reshape(ng, ds, SUBLANES, -1).transpose(0, 2, 1, 3).reshape(ns, -1)
    h0 = jnp.concatenate([state_ssm_re[0].reshape(db, SSM_S), state_ssm_im[0].reshape(db, SSM_S)],
                         axis=-1).astype(F32)
    o_ssm_s, s_fin_s = _s5(to_scan(u_s), h0, w, ngroups=ng, nchunks=1, steps=ds)
    pad_keys = lambda a: jnp.concatenate(
        [a, jnp.zeros((db, NEW_KEYS - ds, a.shape[-1]), a.dtype)], axis=1)
    key_major = lambda a: a.transpose(0, 2, 1)
    attn_in = (page_table,
               qlat.reshape(db, ds * N_HEADS, KV_LORA),
               qf.reshape(db, ds * N_HEADS, LANES)[:, :, NOPE_DIM:QK_DIM],
               pad_keys(c_s.reshape(db, ds, KV_LORA)),
               key_major(pad_keys(kr_s.reshape(db, ds, ROPE_DIM))),
               key_major(pad_keys(rstd_s.reshape(db, ds, N_HEADS))))
    caches = (cache_kv_latent[0], key_major(cache_k_rope[0]), key_major(cache_k_rstd[0]))

    def attn_rows(lo, hi, anchor=None):
        pt, *rest = (a[lo:hi] for a in attn_in)
        if anchor is not None:
            pt = _after(pt, anchor)
        return _attn_s(pt, *rest, *caches, dec_seq=ds)

    o_lat_a = attn_rows(0, db // 2)
    ys_p, dest_p = _moe_experts(mixed_p, w, during_dispatch=o_lat_a)
    o_lat_b = attn_rows(db // 2, db, anchor=ys_p)
    o_lat = jnp.concatenate([o_lat_a, o_lat_b], axis=0)
    o_mla_s = _latent_out(o_lat.reshape(ns, N_HEADS * KV_LORA), w["wuv_h"])

    y_p = _moe_combine(mixed_p, ys_p, dest_p, p_prompt[0].reshape(nb * t, PLE_DIM), w, tm=tm)
    mixed_s = _mix(xs_, from_scan(o_ssm_s), _after(o_mla_s, y_p), w, nb=1, nt=1, tm=ns)
    ys_s, dest_s = _moe_experts(mixed_s, w)
    y_s = _moe_combine(mixed_s, ys_s, dest_s, p_sample[0].reshape(ns, PLE_DIM), w, tm=ns)

    return (y_p.reshape(nb, t, D_MODEL), y_s.reshape(db, ds, D_MODEL),
            c_p.reshape(1, nb, t, KV_LORA), kr_p.reshape(1, nb, t, ROPE_DIM), rstd_p.reshape(1, nb, t, N_HEADS),
            sr_p, si_p,
            c_s.reshape(1, db, ds, KV_LORA), kr_s.reshape(1, db, ds, ROPE_DIM), rstd_s.reshape(1, db, ds, N_HEADS),
            s_fin_s[:, :SSM_S].reshape(1, db, SSM_G, SSM_P), s_fin_s[:, SSM_S:].reshape(1, db, SSM_G, SSM_P))
```

```python
import functools
import math

import jax
import jax.numpy as jnp
from jax import lax
from jax.experimental import pallas as pl
from jax.experimental.pallas import tpu as pltpu
from jax.experimental.pallas import tpu_sc as plsc

D_MODEL = 1024
SSM_W = 512
SSM_GC = 16
SSM_G = SSM_W // SSM_GC
SSM_P = 64
SSM_S = SSM_G * SSM_P
N_HEADS = 8
NOPE_DIM = 64
ROPE_DIM = 32
ROPE_HALF = ROPE_DIM // 2
QK_DIM = NOPE_DIM + ROPE_DIM
V_DIM = 64
Q_LORA = 384
KV_LORA = 256
ROPE_THETA = 10000.0
N_EXPERTS = 32
TOP_K = 4
D_FF = D_MODEL
SWIGLU_ALPHA = 1.702
SWIGLU_LIMIT = 7.0
PLE_DIM = 256
PAST_LEN = 16384
PAGE_SIZE = 128
EPS = 1e-6
NEG = -1e30

LANES = 128
SUBLANES = 8
VMEM_LIMIT = 56 * 1024 * 1024

HP = N_HEADS * LANES
BF16 = jnp.bfloat16
F32 = jnp.float32

TOKEN_ROWS = 512
S5_STEPS = 128
ATTN_P_HEADS = 4
ATTN_P_TQ = 1024
ATTN_P_TK = 512
ATTN_S_PAGES = 32
NEW_KEYS = 16
EXPERT_ROWS = 512
SC_INDEX_ROW = 128
SC_SCATTER_TOKENS = SC_INDEX_ROW // TOP_K
SC_GATHER_TOKENS = 8


def _cparams(*sem):
    return pltpu.CompilerParams(dimension_semantics=sem, vmem_limit_bytes=VMEM_LIMIT)


def _rms(x, g):
    r = lax.rsqrt(jnp.mean(x * x, axis=-1, keepdims=True) + EPS)
    return x * r * g


def _dot(a, b):
    return jnp.dot(a, b, preferred_element_type=F32)


def _dot_nt(a, b):
    return lax.dot_general(a, b, (((1,), (1,)), ((), ())), preferred_element_type=F32)


def _full(shape):
    nd = len(shape)
    return pl.BlockSpec(shape, lambda *_: (0,) * nd)


def _proj_kernel(x_ref, ct_ref, st_ref, gmix_ref, win_ref, gql_ref, wuq_ref, gkv_ref, wuk_ref,
                 w2_ref, gq_ref, gk_ref, u_ref, q_ref, k2_ref, *rest, sample):
    if sample:
        c_ref, kr_ref, rstd_ref = rest
    else:
        v2_ref, c_ref, kr_ref, rstd_ref = rest
    xn = _rms(x_ref[...], gmix_ref[...]).astype(BF16)
    z = _dot(xn, win_ref[...])
    u_ref[...] = z[:, :SSM_W]
    o = SSM_W
    cq = z[:, o:o + Q_LORA]
    o += Q_LORA
    ckv = z[:, o:o + KV_LORA]
    o += KV_LORA
    ct = ct_ref[...]
    st = st_ref[...]
    kblock = z[:, o:o + LANES] * ct + z[:, o + LANES:o + 2 * LANES] * st
    kr_ref[...] = kblock[:, NOPE_DIM:QK_DIM]

    qq = _dot(_rms(cq, gql_ref[...]).astype(BF16), wuq_ref[...])
    c = _rms(ckv, gkv_ref[...])
    c_ref[...] = c
    cb = c.astype(BF16)
    kn = _dot(cb, wuk_ref[...])
    if not sample:
        hl = lax.broadcasted_iota(jnp.int32, (1, HP), 1) & (LANES - 1)
        v2_ref[...] = (_dot(cb, w2_ref[...]) + jnp.where(hl >= V_DIM, 1.0, 0.0)).astype(BF16)

    lane = lax.broadcasted_iota(jnp.int32, ct.shape, 1)
    rstd_all = jnp.zeros(ct.shape, F32)
    gq = gq_ref[...]
    gk = gk_ref[...]
    for h in range(N_HEADS):
        sl = slice(h * LANES, (h + 1) * LANES)
        qr = qq[:, sl] * ct + qq[:, HP + h * LANES:HP + (h + 1) * LANES] * st
        qn = qr * lax.rsqrt(jnp.sum(qr * qr, axis=-1, keepdims=True) * (1.0 / QK_DIM) + EPS) * gq
        kh = kn[:, sl] + kblock
        rs = lax.rsqrt(jnp.sum(kh * kh, axis=-1, keepdims=True) * (1.0 / QK_DIM) + EPS)
        rstd_all = jnp.where(lane == h, rs, rstd_all)
        if sample:
            qf = (qn * gk).astype(BF16)
            q_ref[:, sl] = qf
            k2_ref[:, h * KV_LORA:(h + 1) * KV_LORA] = _dot(qf, w2_ref[h]).astype(BF16)
        else:
            q_ref[:, sl] = qn.astype(BF16)
            k2_ref[:, sl] = (kh * rs * gk).astype(BF16)
    rstd_ref[...] = rstd_all[:, :N_HEADS]


def _proj(x2, ct, st, w, *, nb, nt, tm, sample):
    n = x2.shape[0]
    row = lambda b, t: (b * nt + t, 0)
    rows = lambda width: pl.BlockSpec((tm, width), row)
    k2_w = N_HEADS * KV_LORA if sample else HP
    w2 = w["wukT"] if sample else w["wuv"]
    out_shape = [
        jax.ShapeDtypeStruct((n, SSM_W), F32),
        jax.ShapeDtypeStruct((n, HP), BF16),
        jax.ShapeDtypeStruct((n, k2_w), BF16),
        *([] if sample else [jax.ShapeDtypeStruct((n, HP), BF16)]),
        jax.ShapeDtypeStruct((n, KV_LORA), F32),
        jax.ShapeDtypeStruct((n, ROPE_DIM), F32),
        jax.ShapeDtypeStruct((n, N_HEADS), F32),
    ]
    out_specs = [
        rows(SSM_W), rows(HP), rows(k2_w), *([] if sample else [rows(HP)]),
        rows(KV_LORA), rows(ROPE_DIM), rows(N_HEADS),
    ]
    in_specs = [
        rows(D_MODEL),
        pl.BlockSpec((tm, LANES), lambda b, t: (t, 0)),
        pl.BlockSpec((tm, LANES), lambda b, t: (t, 0)),
        _full(w["gmix"].shape), _full(w["win"].shape), _full(w["gql"].shape), _full(w["wuq"].shape),
        _full(w["gkv"].shape), _full(w["wuk"].shape), _full(w2.shape),
        _full(w["gq_s" if sample else "gq_p"].shape), _full(w["gk"].shape),
    ]
    return pl.pallas_call(
        functools.partial(_proj_kernel, sample=sample),
        grid=(nb, nt),
        in_specs=in_specs,
        out_specs=out_specs,
        out_shape=out_shape,
        compiler_params=_cparams("parallel", "parallel"),
        name="proj_s" if sample else "proj_p",
    )(x2, ct, st, w["gmix"], w["win"], w["gql"], w["wuq"], w["gkv"], w["wuk"], w2,
      w["gq_s" if sample else "gq_p"], w["gk"])


S5_HALF = SSM_S // 2
S5_QUARTER = SSM_S // 4


def _s5_kernel(u_ref, h0_ref, are_ref, aim_ref, bw_ref, cw_ref, dsk_ref, wglu_ref, bglu_ref, gout_ref,
               o_ref, sfin_ref, xs_ref, st_ref, *il_ref, steps):
    @pl.when(pl.program_id(1) == 0)
    def _():
        st_ref[...] = h0_ref[...]

    il = il_ref[0] if il_ref else None
    lane_tiles = SSM_W // LANES
    if il is not None:
        for b in range(SUBLANES):
            for c in range(lane_tiles):
                il[c, pl.ds(b, steps, stride=SUBLANES), :] = u_ref[b, :, c * LANES:(c + 1) * LANES]
        u = jnp.concatenate([il[c] for c in range(lane_tiles)], axis=-1)
    else:
        u = u_ref[...]
    ub = u.astype(BF16)
    half_w = SSM_W // 2
    for k in range(2):
        uk = ub[:, k * half_w:(k + 1) * half_w]
        xs_ref[:, k * S5_HALF:(k + 1) * S5_HALF] = _dot(uk, bw_ref[2 * k])
        xs_ref[:, SSM_S + k * S5_HALF:SSM_S + (k + 1) * S5_HALF] = _dot(uk, bw_ref[2 * k + 1])

    for q in range(4):
        lr = slice(q * S5_QUARTER, (q + 1) * S5_QUARTER)
        li = slice(SSM_S + q * S5_QUARTER, SSM_S + (q + 1) * S5_QUARTER)
        ar = are_ref[:, lr]
        ai = aim_ref[:, lr]

        def step(t, carry):
            sr, si = carry
            r0 = pl.multiple_of(t * SUBLANES, SUBLANES)
            nr = ar * sr - ai * si + xs_ref[pl.ds(r0, SUBLANES), lr]
            ni = ar * si + ai * sr + xs_ref[pl.ds(r0, SUBLANES), li]
            xs_ref[pl.ds(r0, SUBLANES), lr] = nr
            xs_ref[pl.ds(r0, SUBLANES), li] = ni
            return nr, ni

        sr, si = lax.fori_loop(0, steps, step, (st_ref[:, lr], st_ref[:, li]), unroll=4)
        st_ref[:, lr] = sr
        st_ref[:, li] = si
    sfin_ref[...] = st_ref[...]

    ys = []
    for k in range(2):
        sre = xs_ref[:, k * S5_HALF:(k + 1) * S5_HALF].astype(BF16)
        sim = xs_ref[:, SSM_S + k * S5_HALF:SSM_S + (k + 1) * S5_HALF].astype(BF16)
        ys.append(_dot(sre, cw_ref[2 * k]) + _dot(sim, cw_ref[2 * k + 1]))
    y = jnp.concatenate(ys, axis=-1) + dsk_ref[...] * u
    g = jax.nn.gelu(y)
    out = g * jax.nn.sigmoid(_dot(g.astype(BF16), wglu_ref[...]) + bglu_ref[...])
    on = _rms(out, gout_ref[...])
    if il is not None:
        for c in range(lane_tiles):
            il[c] = on[:, c * LANES:(c + 1) * LANES]
        for b in range(SUBLANES):
            for c in range(lane_tiles):
                o_ref[b, :, c * LANES:(c + 1) * LANES] = il[c, pl.ds(b, steps, stride=SUBLANES), :].astype(BF16)
    else:
        o_ref[...] = on.astype(BF16)


def _s5(u, h0, w, *, ngroups, nchunks, steps):
    rows = steps * SUBLANES
    batch_major = u.ndim == 3
    if batch_major:
        assert ngroups == 1 and u.shape[0] == SUBLANES
        ublock = pl.BlockSpec((SUBLANES, steps, SSM_W), lambda g, t: (0, t, 0))
    else:
        ublock = pl.BlockSpec((rows, SSM_W), lambda g, t: (g * nchunks + t, 0))
    return pl.pallas_call(
        functools.partial(_s5_kernel, steps=steps),
        grid=(ngroups, nchunks),
        in_specs=[
            ublock,
            pl.BlockSpec((SUBLANES, 2 * SSM_S), lambda g, t: (g, 0)),
            _full(w["s5_are"].shape), _full(w["s5_aim"].shape), _full(w["s5_bw"].shape),
            _full(w["s5_cw"].shape), _full(w["s5_d"].shape), _full(w["wglu"].shape),
            _full(w["bglu"].shape), _full(w["gout_ssm"].shape),
        ],
        out_specs=[
            ublock,
            pl.BlockSpec((SUBLANES, 2 * SSM_S), lambda g, t: (g, 0)),
        ],
        out_shape=[
            jax.ShapeDtypeStruct(u.shape, BF16),
            jax.ShapeDtypeStruct(h0.shape, F32),
        ],
        scratch_shapes=[pltpu.VMEM((rows, 2 * SSM_S), F32), pltpu.VMEM((SUBLANES, 2 * SSM_S), F32)]
        + ([pltpu.VMEM((SSM_W // LANES, rows, LANES), F32)] if batch_major else []),
        compiler_params=_cparams("parallel", "arbitrary"),
        name="s5",
    )(u, h0, w["s5_are"], w["s5_aim"], w["s5_bw"], w["s5_cw"], w["s5_d"], w["wglu"], w["bglu"],
      w["gout_ssm"])


def _attn_p_kernel(q_ref, k_ref, v_ref, o_ref, *, tq, tk):
    i = pl.program_id(2)
    nd = tq // tk
    row = lax.broadcasted_iota(jnp.int32, (tk, tk), 0)
    col = lax.broadcasted_iota(jnp.int32, (tk, tk), 1)

    def block(j, carry, diag):
        k0 = pl.multiple_of(j * tk, tk)
        k2 = k_ref[pl.ds(k0, tk), :]
        v2 = v_ref[pl.ds(k0, tk), :]
        r0 = 0 if diag is None else diag * tk
        out = []
        for hh in range(ATTN_P_HEADS):
            m, acc = carry[hh]
            s = _dot_nt(q_ref[r0:, hh * LANES:(hh + 1) * LANES], k2[:, hh * LANES:(hh + 1) * LANES])
            if diag is not None:
                tri = jnp.where(col <= row, s[:tk], NEG)
                s = tri if r0 + tk == tq else jnp.concatenate([tri, s[tk:]], axis=0)
            m_new = jnp.maximum(m[r0:], jnp.max(s, axis=-1, keepdims=True))
            p = jnp.exp2(s - m_new).astype(BF16)
            acc_new = jnp.exp2(m[r0:] - m_new) * acc[r0:] + _dot(p, v2[:, hh * LANES:(hh + 1) * LANES])
            if r0:
                m_new = jnp.concatenate([m[:r0], m_new], axis=0)
                acc_new = jnp.concatenate([acc[:r0], acc_new], axis=0)
            out.append((m_new, acc_new))
        return tuple(out)

    init = tuple((jnp.full((tq, 1), NEG, F32), jnp.zeros((tq, LANES), F32)) for _ in range(ATTN_P_HEADS))
    carry = lax.fori_loop(0, i * nd, lambda j, c: block(j, c, None), init)
    for d in range(nd):
        carry = block(i * nd + d, carry, d)
    o = [acc / pltpu.roll(acc, V_DIM, 1) for _, acc in carry]
    lane = lax.broadcasted_iota(jnp.int32, (tq, LANES), 1)
    for pr in range(ATTN_P_HEADS // 2):
        o_ref[:, pr * LANES:(pr + 1) * LANES] = jnp.where(
            lane < V_DIM, o[2 * pr], pltpu.roll(o[2 * pr + 1], V_DIM, 1))


def _attn_p(q, k, v, *, nb, t, tq, tk):
    nq = t // tq
    hw = ATTN_P_HEADS * LANES
    return pl.pallas_call(
        functools.partial(_attn_p_kernel, tq=tq, tk=tk),
        grid=(nb, N_HEADS // ATTN_P_HEADS, nq),
        in_specs=[
            pl.BlockSpec((tq, hw), lambda b, h, i: (b * nq + i, h)),
            pl.BlockSpec((t, hw), lambda b, h, i: (b, h)),
            pl.BlockSpec((t, hw), lambda b, h, i: (b, h)),
        ],
        out_specs=pl.BlockSpec((tq, ATTN_P_HEADS * V_DIM), lambda b, h, i: (b * nq + i, h)),
        out_shape=jax.ShapeDtypeStruct((nb * t, N_HEADS * V_DIM), F32),
        compiler_params=_cparams("parallel", "parallel", "arbitrary"),
        name="attn_p",
    )(q, k, v)


def _attn_s_kernel(pt_ref, ql_ref, qr_ref, cn_ref, krn_ref, rnt_ref, cc_hbm, ckr_hbm, cr_hbm,
                   o_ref, cbuf, krbuf, rbuf, cb_ref, sc_ref, sems, *, n_pages, dec_seq):
    b = pl.program_id(0)
    nb = pl.num_programs(0)
    bp = min(ATTN_S_PAGES, n_pages)
    nblk = n_pages // bp
    keys = bp * PAGE_SIZE
    nrow = dec_seq * N_HEADS
    scale = 1.0 / math.sqrt(QK_DIM)
    slot = b % 2

    def start_block(bb, i, dst):
        for p in range(bp):
            pg = i * bp + p
            page = pt_ref[bb * n_pages + pg]
            lanes = pl.ds(pl.multiple_of(pg * PAGE_SIZE, PAGE_SIZE), PAGE_SIZE)
            pltpu.make_async_copy(cc_hbm.at[page], cbuf.at[dst, pg], sems.at[dst, 0]).start()
            pltpu.make_async_copy(ckr_hbm.at[page], krbuf.at[dst, :, lanes], sems.at[dst, 1]).start()
            pltpu.make_async_copy(cr_hbm.at[page], rbuf.at[dst, :, lanes], sems.at[dst, 2]).start()

    @pl.when(b == 0)
    def _():
        lax.fori_loop(0, nblk, lambda i, c: (start_block(0, i, 0), c)[1], 0)

    def wait_row(s):
        pltpu.make_async_copy(cbuf.at[s], cbuf.at[s], sems.at[s, 0]).wait()
        pltpu.make_async_copy(krbuf.at[s], krbuf.at[s], sems.at[s, 1]).wait()
        pltpu.make_async_copy(rbuf.at[s], rbuf.at[s], sems.at[s, 2]).wait()

    wait_row(slot)

    ql = ql_ref[0]
    qr = qr_ref[0]

    def tile_heads(r_t):
        return jnp.concatenate([r_t] * dec_seq, axis=0)

    cnb = cn_ref[0].astype(BF16)
    s_n = (_dot_nt(ql, cnb) + _dot(qr, krn_ref[0].astype(BF16))) * tile_heads(rnt_ref[0]) * scale
    qs = lax.broadcasted_iota(jnp.int32, (nrow, NEW_KEYS), 0) // N_HEADS
    kj = lax.broadcasted_iota(jnp.int32, (nrow, NEW_KEYS), 1)
    s_n = jnp.where(kj <= qs, s_n, NEG)

    def score_block(i, m):
        @pl.when(b + 1 < nb)
        def _():
            start_block(b + 1, i, 1 - slot)

        k0 = pl.multiple_of(i * keys, keys)
        cb = cbuf[slot, pl.ds(i * bp, bp)].reshape(keys, KV_LORA).astype(BF16)
        cb_ref[pl.ds(k0, keys), :] = cb
        raw = _dot_nt(ql, cb) + _dot(qr, krbuf[slot, :, pl.ds(k0, keys)].astype(BF16))
        s = raw * tile_heads(rbuf[slot, :, pl.ds(k0, keys)]) * scale
        sc_ref[:, pl.ds(k0, keys)] = s
        return jnp.maximum(m, jnp.max(s, axis=-1, keepdims=True))

    m = lax.fori_loop(0, nblk, score_block, jnp.max(s_n, axis=-1, keepdims=True))
    p_n = jnp.exp(s_n - m)

    def value_block(i, carry):
        l, acc = carry
        k0 = pl.multiple_of(i * keys, keys)
        p = jnp.exp(sc_ref[:, pl.ds(k0, keys)] - m)
        return (l + jnp.sum(p, axis=-1, keepdims=True),
                acc + _dot(p.astype(BF16), cb_ref[pl.ds(k0, keys), :]))

    l, acc = lax.fori_loop(0, nblk, value_block,
                           (jnp.sum(p_n, axis=-1, keepdims=True), _dot(p_n.astype(BF16), cnb)))
    o_ref[0] = acc / l


def _attn_s(page_table, ql, qr, cn, krn, rnt, cache_c, cache_kr, cache_r, *, dec_seq):
    nb, n_pages = page_table.shape
    nrow = dec_seq * N_HEADS
    b3 = lambda b, pt: (b, 0, 0)
    return pl.pallas_call(
        functools.partial(_attn_s_kernel, n_pages=n_pages, dec_seq=dec_seq),
        grid_spec=pltpu.PrefetchScalarGridSpec(
            num_scalar_prefetch=1,
            grid=(nb,),
            in_specs=[
                pl.BlockSpec((1, nrow, KV_LORA), b3),
                pl.BlockSpec((1, nrow, ROPE_DIM), b3),
                pl.BlockSpec((1, NEW_KEYS, KV_LORA), b3),
                pl.BlockSpec((1, ROPE_DIM, NEW_KEYS), b3),
                pl.BlockSpec((1, N_HEADS, NEW_KEYS), b3),
                pl.BlockSpec(memory_space=pl.ANY),
                pl.BlockSpec(memory_space=pl.ANY),
                pl.BlockSpec(memory_space=pl.ANY),
            ],
            out_specs=pl.BlockSpec((1, nrow, KV_LORA), b3),
            scratch_shapes=[
                pltpu.VMEM((2, n_pages, PAGE_SIZE, KV_LORA), F32),
                pltpu.VMEM((2, ROPE_DIM, n_pages * PAGE_SIZE), F32),
                pltpu.VMEM((2, N_HEADS, n_pages * PAGE_SIZE), F32),
                pltpu.VMEM((n_pages * PAGE_SIZE, KV_LORA), BF16),
                pltpu.VMEM((nrow, n_pages * PAGE_SIZE), F32),
                pltpu.SemaphoreType.DMA((2, 3)),
            ],
        ),
        out_shape=jax.ShapeDtypeStruct((nb, nrow, KV_LORA), F32),
        compiler_params=_cparams("arbitrary"),
        name="attn_s",
    )(page_table.reshape(-1), ql, qr, cn, krn, rnt, cache_c, cache_kr, cache_r)


def _latent_out_kernel(ol_ref, wuv_ref, o_ref):
    for h in range(N_HEADS):
        o_ref[:, h * V_DIM:(h + 1) * V_DIM] = _dot(
            ol_ref[:, h * KV_LORA:(h + 1) * KV_LORA].astype(BF16), wuv_ref[h])


def _latent_out(ol, wuv_h):
    n = ol.shape[0]
    return pl.pallas_call(
        _latent_out_kernel,
        grid=(1,),
        in_specs=[_full(ol.shape), _full(wuv_h.shape)],
        out_specs=_full((n, N_HEADS * V_DIM)),
        out_shape=jax.ShapeDtypeStruct((n, N_HEADS * V_DIM), F32),
        compiler_params=_cparams("arbitrary"),
        name="latent_out",
    )(ol, wuv_h)


ROUTE_W = 16
PACKED_W = D_MODEL // 2


def _pack_bf16_halves(xb):
    bits = lambda v: pltpu.bitcast(v.astype(F32), jnp.uint32)
    return (bits(xb[:, :PACKED_W]) >> 16) | bits(xb[:, PACKED_W:])


def _unpack_bf16_halves(words):
    lo = pltpu.bitcast(words << 16, F32)
    hi = pltpu.bitcast(words & jnp.uint32(0xFFFF0000), F32)
    return jnp.concatenate([lo, hi], axis=-1).astype(BF16)


def _mix_kernel(h_ref, os_ref, om_ref, gom_ref, wout_ref, gffn_ref, wrt_ref, brt_ref,
                h1_ref, xn_ref, route_ref, routet_ref, cnt_ref):
    @pl.when((pl.program_id(0) == 0) & (pl.program_id(1) == 0))
    def _():
        cnt_ref[...] = jnp.zeros_like(cnt_ref)

    tm = h_ref.shape[0]
    omn = _rms(om_ref[...], gom_ref[...]).astype(BF16)
    h1 = h_ref[...] + _dot(os_ref[...], wout_ref[:SSM_W, :]) + _dot(omn, wout_ref[SSM_W:, :])
    h1_ref[...] = h1
    xb = _rms(h1, gffn_ref[...]).astype(BF16)
    xn_ref[...] = _pack_bf16_halves(xb)
    work = (_dot_nt(wrt_ref[...], xb) + brt_ref[...])[:N_EXPERTS]

    eidx = lax.broadcasted_iota(jnp.int32, work.shape, 0).astype(F32)
    sels, vals, rows = [], [], []
    for k in range(TOP_K):
        mk = jnp.max(work, axis=0, keepdims=True)
        ik = jnp.min(jnp.where(work == mk, eidx, float(N_EXPERTS)), axis=0, keepdims=True)
        sel = eidx == ik
        work = jnp.where(sel, -jnp.inf, work)
        sels.append(sel)
        vals.append(mk)
        rows.append(ik)
    es = [jnp.exp(v - vals[0]) for v in vals]
    den = es[0] + es[1] + es[2] + es[3]
    rows += [e / den for e in es]
    onehot = jnp.zeros(eidx.shape, F32)
    for sel in sels:
        onehot = jnp.where(sel, 1.0, onehot)

    t_j = lax.broadcasted_iota(jnp.int32, (tm, tm), 0)
    t_i = lax.broadcasted_iota(jnp.int32, (tm, tm), 1)
    earlier = jnp.where(t_j < t_i, 1.0, 0.0).astype(BF16)
    cnt = cnt_ref[...]
    cum = _dot(onehot.astype(BF16), earlier) + jnp.concatenate([cnt] * pl.cdiv(tm, LANES), axis=1)[:, :tm]
    rows += [jnp.sum(jnp.where(sel, cum, 0.0), axis=0, keepdims=True) for sel in sels]
    cnt_ref[...] = cnt + jnp.sum(onehot, axis=1, keepdims=True)

    sub = lax.broadcasted_iota(jnp.int32, (ROUTE_W, tm), 0)
    rt = jnp.zeros((ROUTE_W, tm), F32)
    for r, v in enumerate(rows):
        rt = jnp.where(sub == r, v, rt)
    routet_ref[...] = rt
    route_ref[...] = jnp.concatenate([rt, jnp.zeros((LANES - ROUTE_W, tm), F32)], axis=0).T[:, :ROUTE_W]


def _mix(h2, o_ssm, o_mla, w, *, nb, nt, tm):
    n = h2.shape[0]
    row = lambda b, t: (b * nt + t, 0)
    return pl.pallas_call(
        _mix_kernel,
        grid=(nb, nt),
        in_specs=[
            pl.BlockSpec((tm, D_MODEL), row),
            pl.BlockSpec((tm, SSM_W), row),
            pl.BlockSpec((tm, N_HEADS * V_DIM), row),
            _full(w["gout_mla"].shape), _full(w["wout"].shape), _full(w["gffn"].shape),
            _full(w["wrt"].shape), _full(w["brt"].shape),
        ],
        out_specs=[
            pl.BlockSpec((tm, D_MODEL), row),
            pl.BlockSpec((tm, PACKED_W), row),
            pl.BlockSpec((tm, ROUTE_W), row),
            pl.BlockSpec((ROUTE_W, tm), lambda b, t: (0, b * nt + t)),
            _full((N_EXPERTS, LANES)),
        ],
        out_shape=[
            jax.ShapeDtypeStruct((n, D_MODEL), F32),
            jax.ShapeDtypeStruct((n, PACKED_W), jnp.uint32),
            jax.ShapeDtypeStruct((n, ROUTE_W), F32),
            jax.ShapeDtypeStruct((ROUTE_W, n), F32),
            jax.ShapeDtypeStruct((N_EXPERTS, LANES), F32),
        ],
        compiler_params=_cparams("arbitrary", "arbitrary"),
        name="mix",
    )(h2, o_ssm, o_mla, w["gout_mla"], w["wout"], w["gffn"], w["wrt"], w["brt"])


def _experts_kernel(tile_ref, e_ref, first_ref, valid_ref, lo_ref, hi_ref, newe_ref,
                    x_ref, wgu_ref, bgu_ref, wd_ref, bd_ref, o_ref, wgu_bf, wd_bf):
    i = pl.program_id(0)

    @pl.when(newe_ref[i] == 1)
    def _():
        wgu_bf[...] = wgu_ref[0].astype(BF16)
        wd_bf[...] = wd_ref[0].astype(BF16)

    @pl.when(valid_ref[i] == 1)
    def _():
        bm = x_ref.shape[0]
        hdn = _dot(_unpack_bf16_halves(x_ref[...]), wgu_bf[...]) + bgu_ref[0]
        x_glu = jnp.minimum(hdn[:, :D_FF], SWIGLU_LIMIT)
        x_lin = jnp.clip(hdn[:, D_FF:], -SWIGLU_LIMIT, SWIGLU_LIMIT)
        act = x_glu * jax.nn.sigmoid(SWIGLU_ALPHA * x_glu) * (x_lin + 1.0)
        y = _dot(act.astype(BF16), wd_bf[...]) + bd_ref[0]
        rows = tile_ref[i] * bm + lax.broadcasted_iota(jnp.int32, (bm, 1), 0)
        mine = (rows >= lo_ref[i]) & (rows < hi_ref[i])

        @pl.when(first_ref[i] == 1)
        def _():
            o_ref[...] = jnp.where(mine, y, 0.0)

        @pl.when(first_ref[i] == 0)
        def _():
            o_ref[...] = jnp.where(mine, y, o_ref[...])


def _experts(xs, items, w, *, bm):
    r = xs.shape[0]
    n_items = items[0].shape[0]
    xmap = lambda i, tile, *_: (tile[i], 0)
    emap = lambda i, tile, e, *_: (e[i], 0, 0)
    return pl.pallas_call(
        _experts_kernel,
        grid_spec=pltpu.PrefetchScalarGridSpec(
            num_scalar_prefetch=len(items),
            grid=(n_items,),
            in_specs=[
                pl.BlockSpec((bm, PACKED_W), xmap),
                pl.BlockSpec((1, D_MODEL, 2 * D_FF), emap),
                pl.BlockSpec((1, 1, 2 * D_FF), emap),
                pl.BlockSpec((1, D_FF, D_MODEL), emap),
                pl.BlockSpec((1, 1, D_MODEL), emap),
            ],
            out_specs=pl.BlockSpec((bm, D_MODEL), xmap),
            scratch_shapes=[pltpu.VMEM((D_MODEL, 2 * D_FF), BF16), pltpu.VMEM((D_FF, D_MODEL), BF16)],
        ),
        out_shape=jax.ShapeDtypeStruct((r, D_MODEL), F32),
        compiler_params=_cparams("arbitrary"),
        name="experts",
    )(*items, xs, w["wgu"], w["bgu"], w["wd"], w["bd"])


def _expert_items(counts, n_rows, bm):
    n_tiles = n_rows // bm
    n_items = n_tiles + N_EXPERTS - 1
    ends = jnp.cumsum(counts)
    starts = ends - counts
    first_tile = starts // bm
    per_e = jnp.where(counts > 0, (ends - 1) // bm - first_tile + 1, 0)
    item_end = jnp.cumsum(per_e)
    i = jnp.arange(n_items, dtype=jnp.int32)
    valid = i < item_end[-1]
    e = jnp.sum(jnp.minimum(i, item_end[-1] - 1)[:, None] >= item_end[None, :], axis=1).astype(jnp.int32)
    of_e = e[:, None] == jnp.arange(N_EXPERTS, dtype=jnp.int32)[None, :]
    pick = lambda table: jnp.sum(jnp.where(of_e, table[None, :], 0), axis=1).astype(jnp.int32)
    tile = jnp.where(valid, pick(first_tile) + i - pick(item_end - per_e), n_tiles - 1).astype(jnp.int32)
    one = jnp.ones((1,), jnp.int32)
    first = jnp.concatenate([one, (tile[1:] != tile[:-1]).astype(jnp.int32)])
    new_e = jnp.concatenate([one, (e[1:] != e[:-1]).astype(jnp.int32)])
    return (tile, e, first, valid.astype(jnp.int32), pick(starts), pick(ends), new_e)


PLE_RING = 3


def _ple_kernel(h1_ref, yg_hbm, route_ref, p_ref, gple_ref, wg_ref, wp_ref, o_ref, ybuf, sems):
    t = pl.program_id(0)
    nt = pl.num_programs(0)
    wins = ybuf.shape[1]

    def fetch(step, slot):
        return pltpu.make_async_copy(yg_hbm.at[pl.ds(step * wins, wins)], ybuf.at[slot], sems.at[slot])

    @pl.when(t == 0)
    def _():
        for s in range(PLE_RING - 1):
            @pl.when(s < nt)
            def _():
                fetch(s, s).start()

    ahead = t + PLE_RING - 1

    @pl.when(ahead < nt)
    def _():
        fetch(ahead, ahead % PLE_RING).start()

    slot = t % PLE_RING
    fetch(t, slot).wait()
    yg_ref = ybuf.at[slot]

    h2 = h1_ref[...]
    route = route_ref[...]
    w = SC_GATHER_TOKENS
    for k in range(TOP_K):
        yk = yg_ref[:, k * w:(k + 1) * w, :].reshape(h2.shape)
        h2 = h2 + yk * route[:, TOP_K + k:TOP_K + k + 1]
    gate = jax.nn.sigmoid(_dot(_rms(h2, gple_ref[...]).astype(BF16), wg_ref[...]))
    o_ref[...] = h2 + _dot(p_ref[...].astype(BF16), wp_ref[...]) * gate


def _ple(h1, yg, route, p, w, *, tm):
    n = h1.shape[0]
    row = lambda t: (t, 0)
    return pl.pallas_call(
        _ple_kernel,
        grid=(n // tm,),
        in_specs=[
            pl.BlockSpec((tm, D_MODEL), row),
            pl.BlockSpec(memory_space=pl.ANY),
            pl.BlockSpec((tm, ROUTE_W), row),
            pl.BlockSpec((tm, PLE_DIM), row),
            _full(w["gple"].shape), _full(w["wpg"].shape), _full(w["wp"].shape),
        ],
        out_specs=pl.BlockSpec((tm, D_MODEL), row),
        out_shape=jax.ShapeDtypeStruct((n, D_MODEL), F32),
        scratch_shapes=[
            pltpu.VMEM((PLE_RING, tm // SC_GATHER_TOKENS, TOP_K * SC_GATHER_TOKENS, D_MODEL), F32),
            pltpu.SemaphoreType.DMA((PLE_RING,)),
        ],
        compiler_params=_cparams("arbitrary"),
        name="ple",
    )(h1, yg, route, p, w["gple"], w["wpg"], w["wp"])


def _sc_mesh():
    return plsc.VectorSubcoreMesh(core_axis_name="c", subcore_axis_name="s")


def _window_index(dest_t, w):
    n = dest_t.shape[1]
    idx = dest_t.reshape(TOP_K, n // w, w).transpose(1, 0, 2).reshape(n // w, TOP_K * w)
    if TOP_K * w < SC_INDEX_ROW:
        idx = jnp.concatenate([idx, jnp.zeros((n // w, SC_INDEX_ROW - TOP_K * w), jnp.int32)], axis=-1)
    return idx


def _sc_dispatch(xn, idx):
    n = xn.shape[0]
    w = SC_SCATTER_TOKENS

    @pl.kernel(out_type=jax.ShapeDtypeStruct((n * TOP_K, xn.shape[1]), xn.dtype),
               mesh=_sc_mesh(), scratch_types=[pltpu.SemaphoreType.DMA])
    def scatter(x_hbm, i_hbm, o_hbm, sem):
        def body(x_vmem, i_vmem):
            copies = [pltpu.make_async_copy(x_vmem, o_hbm.at[i_vmem.at[0, pl.ds(k * w, w)]], sem)
                      for k in range(TOP_K)]
            for cp in copies:
                cp.start()
            for cp in copies:
                cp.wait()

        pltpu.emit_pipeline(
            body,
            grid=(n // w,),
            in_specs=[pl.BlockSpec((w, xn.shape[1]), lambda i: (i, 0)),
                      pl.BlockSpec((1, SC_INDEX_ROW), lambda i: (i, 0))],
            out_specs=[],
            core_axis_name=("c", "s"),
            dimension_semantics=(pltpu.PARALLEL,),
        )(x_hbm, i_hbm)

    return scatter(xn, idx)


def _sc_combine(ys, idx):
    w = SC_GATHER_TOKENS
    n = idx.shape[0] * w
    rows = TOP_K * w

    @pl.kernel(out_type=jax.ShapeDtypeStruct((n // w, rows, D_MODEL), ys.dtype),
               mesh=_sc_mesh(), scratch_types=[])
    def gather(y_hbm, i_hbm, o_hbm):
        def body(i_vmem, o_vmem):
            pltpu.sync_copy(y_hbm.at[i_vmem.at[0, pl.ds(0, rows)]], o_vmem.at[0])

        pltpu.emit_pipeline(
            body,
            grid=(n // w,),
            in_specs=[pl.BlockSpec((1, SC_INDEX_ROW), lambda i: (i, 0))],
            out_specs=[pl.BlockSpec((1, rows, D_MODEL), lambda i: (i, 0, 0))],
            core_axis_name=("c", "s"),
            dimension_semantics=(pltpu.PARALLEL,),
        )(i_hbm, o_hbm)

    return gather(ys, idx)


def _rope_tables(pos):
    inv = ROPE_THETA ** (-jnp.arange(0, ROPE_DIM, 2, dtype=F32) / ROPE_DIM)
    ang = pos.astype(F32)[:, None] * inv[None, :]
    cos, sin = jnp.cos(ang), jnp.sin(ang)
    n = pos.shape[0]
    pad = jnp.zeros((n, LANES - QK_DIM), F32)
    ct = jnp.concatenate([jnp.ones((n, NOPE_DIM), F32), cos, cos, pad], axis=-1)
    st = jnp.concatenate([jnp.zeros((n, NOPE_DIM), F32), sin, sin, pad], axis=-1)
    return ct, st


def _rope_partner(w_rope):
    return jnp.concatenate([-w_rope[..., ROPE_HALF:], w_rope[..., :ROPE_HALF]], axis=-1)


def _head_tile(nope, rope):
    pad = jnp.zeros(rope.shape[:-1] + (LANES - QK_DIM,), rope.dtype)
    return jnp.concatenate([nope, rope, pad], axis=-1)


def _block_diag(m):
    g, a, b = m.shape
    hg = g // 2
    eye = jnp.eye(hg, dtype=m.dtype)
    mh = m.reshape(2, hg, a, b)
    return jnp.einsum("kgab,gh->kgahb", mh, eye).reshape(2, hg * a, hg * b)


def _prep_weights(g_mix, w_in, s5_a_re, s5_a_im, s5_log_dt, s5_b_re, s5_b_im, s5_c_re, s5_c_im, s5_d,
                  w_glu, b_glu, g_q_lora, w_uq, g_kv_lora, w_uk, w_uv, g_qk_q, g_qk_k, g_out_ssm,
                  g_out_mla, w_out, g_ffn, w_router, b_router, w_gate_up, b_gate_up, w_down, b_down,
                  g_ple, w_ple_gate, w_ple):
    w = {}
    row = lambda v: v.reshape(1, -1).astype(F32)
    o = SSM_W + Q_LORA + KV_LORA
    w_kr = w_in[:, o:o + ROPE_DIM]
    zeros_n = jnp.zeros((D_MODEL, NOPE_DIM), F32)
    w["win"] = jnp.concatenate(
        [w_in[:, :o], _head_tile(zeros_n, w_kr), _head_tile(zeros_n, _rope_partner(w_kr))], axis=-1).astype(BF16)
    w["gmix"] = row(g_mix)
    w["gql"] = row(g_q_lora)
    w["gkv"] = row(g_kv_lora)

    wq = w_uq.reshape(Q_LORA, N_HEADS, QK_DIM)
    q1 = _head_tile(wq[..., :NOPE_DIM], wq[..., NOPE_DIM:])
    q2 = _head_tile(jnp.zeros_like(wq[..., :NOPE_DIM]), _rope_partner(wq[..., NOPE_DIM:]))
    w["wuq"] = jnp.concatenate([q1.reshape(Q_LORA, HP), q2.reshape(Q_LORA, HP)], axis=-1).astype(BF16)
    w["wuk"] = jnp.concatenate(
        [w_uk, jnp.zeros((KV_LORA, N_HEADS, LANES - NOPE_DIM), F32)], axis=-1).reshape(KV_LORA, HP).astype(BF16)
    w["wuv"] = jnp.concatenate(
        [w_uv, jnp.zeros((KV_LORA, N_HEADS, LANES - V_DIM), F32)], axis=-1).reshape(KV_LORA, HP).astype(BF16)
    w["wuv_h"] = w_uv.transpose(1, 0, 2).astype(BF16)
    wukT = w_uk.transpose(1, 2, 0)
    w["wukT"] = jnp.concatenate(
        [wukT, jnp.zeros((N_HEADS, LANES - NOPE_DIM, KV_LORA), F32)], axis=1).astype(BF16)
    gpad = jnp.zeros((LANES - QK_DIM,), F32)
    w["gq_p"] = row(jnp.concatenate([g_qk_q * (math.log2(math.e) / math.sqrt(QK_DIM)), gpad]))
    w["gq_s"] = row(jnp.concatenate([g_qk_q, gpad]))
    w["gk"] = row(jnp.concatenate([g_qk_k, gpad]))

    dt = jnp.exp(s5_log_dt.astype(F32))[:, None]
    ar, ai = s5_a_re.astype(F32), s5_a_im.astype(F32)
    mag = jnp.exp(dt * ar)
    abar_re, abar_im = mag * jnp.cos(dt * ai), mag * jnp.sin(dt * ai)
    den = ar * ar + ai * ai
    nr, ni = abar_re - 1.0, abar_im
    coef_re = (nr * ar + ni * ai) / den
    coef_im = (ni * ar - nr * ai) / den
    br, bi = s5_b_re.astype(F32), s5_b_im.astype(F32)
    bbar_re = coef_re[..., None] * br - coef_im[..., None] * bi
    bbar_im = coef_re[..., None] * bi + coef_im[..., None] * br
    bre = _block_diag(bbar_re.transpose(0, 2, 1))
    bim = _block_diag(bbar_im.transpose(0, 2, 1))
    w["s5_bw"] = jnp.stack([bre[0], bim[0], bre[1], bim[1]]).astype(BF16)
    cre = _block_diag(s5_c_re.astype(F32).transpose(0, 2, 1))
    cim = _block_diag(-s5_c_im.astype(F32).transpose(0, 2, 1))
    w["s5_cw"] = jnp.stack([cre[0], cim[0], cre[1], cim[1]]).astype(BF16)
    w["s5_are"] = jnp.broadcast_to(abar_re.reshape(1, SSM_S), (SUBLANES, SSM_S))
    w["s5_aim"] = jnp.broadcast_to(abar_im.reshape(1, SSM_S), (SUBLANES, SSM_S))
    w["s5_d"] = row(s5_d)
    w["wglu"] = w_glu.astype(BF16)
    w["bglu"] = row(b_glu)
    w["gout_ssm"] = row(g_out_ssm)
    w["gout_mla"] = row(g_out_mla)
    w["wout"] = w_out.astype(BF16)
    w["gffn"] = row(g_ffn)
    w["wrt"] = jnp.concatenate([w_router.T, jnp.zeros((LANES - N_EXPERTS, D_MODEL), F32)], axis=0).astype(BF16)
    w["brt"] = jnp.concatenate([b_router.astype(F32), jnp.zeros((LANES - N_EXPERTS,), F32)]).reshape(LANES, 1)
    w["wgu"] = w_gate_up
    w["bgu"] = b_gate_up.reshape(N_EXPERTS, 1, 2 * D_FF).astype(F32)
    w["wd"] = w_down
    w["bd"] = b_down.reshape(N_EXPERTS, 1, D_MODEL).astype(F32)
    w["gple"] = row(g_ple)
    w["wpg"] = w_ple_gate.astype(BF16)
    w["wp"] = w_ple.astype(BF16)
    return w


def _moe_experts(mixed, w, during_dispatch=()):
    h1, xn, _, route_t, cnt = mixed
    n = h1.shape[0]
    top_i = route_t[:TOP_K].astype(jnp.int32)
    rank = route_t[2 * TOP_K:3 * TOP_K].astype(jnp.int32)
    counts = cnt[:, 0].astype(jnp.int32)
    starts = jnp.cumsum(counts) - counts
    dest = rank
    for e in range(N_EXPERTS):
        dest = dest + jnp.where(top_i == e, starts[e], 0)
    gather_idx = _window_index(dest, SC_GATHER_TOKENS)
    xs = _after(_sc_dispatch(xn, _window_index(dest, SC_SCATTER_TOKENS)), (gather_idx, during_dispatch))
    bm = min(EXPERT_ROWS, max(LANES, n * TOP_K // N_EXPERTS))
    return _experts(xs, _expert_items(counts, n * TOP_K, bm), w, bm=bm), gather_idx


def _moe_combine(mixed, ys, gather_idx, p2, w, *, tm):
    h1, _, route, _, _ = mixed
    return _ple(h1, _sc_combine(ys, gather_idx), route, p2, w, tm=tm)


def _after(x, anchor):
    return lax.optimization_barrier((x, anchor))[0]


def kernel(x_prompt, x_sample, cache_kv_latent, cache_k_rope, cache_k_rstd, state_ssm_re, state_ssm_im,
           page_table, p_prompt, p_sample, g_mix, w_in, s5_a_re, s5_a_im, s5_log_dt, s5_b_re, s5_b_im,
           s5_c_re, s5_c_im, s5_d, w_glu, b_glu, g_q_lora, w_uq, g_kv_lora, w_uk, w_uv, g_qk_q, g_qk_k,
           g_out_ssm, g_out_mla, w_out, g_ffn, w_router, b_router, w_gate_up, b_gate_up, w_down, b_down,
           g_ple, w_ple_gate, w_ple):
    assert g_mix.shape[0] == 1, "single-layer step"
    nb, t, _ = x_prompt.shape
    db, ds, _ = x_sample.shape
    assert nb == SUBLANES and db % (2 * SUBLANES) == 0, "S5 scans 8 batch rows per vreg; attn_s splits rows in two"
    assert t % min(ATTN_P_TQ, t) == 0 and t % min(TOKEN_ROWS, t) == 0 and t % min(S5_STEPS, t) == 0
    assert (db * ds) % SC_SCATTER_TOKENS == 0 and page_table.shape[1] % min(ATTN_S_PAGES, page_table.shape[1]) == 0
    w = _prep_weights(g_mix[0], w_in[0], s5_a_re[0], s5_a_im[0], s5_log_dt[0], s5_b_re[0], s5_b_im[0],
                      s5_c_re[0], s5_c_im[0], s5_d[0], w_glu[0], b_glu[0], g_q_lora[0], w_uq[0],
                      g_kv_lora[0], w_uk[0], w_uv[0], g_qk_q[0], g_qk_k[0], g_out_ssm[0], g_out_mla[0],
                      w_out[0], g_ffn[0], w_router[0], b_router[0], w_gate_up[0], b_gate_up[0],
                      w_down[0], b_down[0], g_ple[0], w_ple_gate[0], w_ple[0])

    tm = min(TOKEN_ROWS, t)
    nt = t // tm
    xp = x_prompt.reshape(nb * t, D_MODEL)
    ct, st = _rope_tables(jnp.arange(t))
    u, q, k, v, c_p, kr_p, rstd_p = _proj(xp, ct, st, w, nb=nb, nt=nt, tm=tm, sample=False)
    steps = min(S5_STEPS, t)
    o_ssm, s_fin = _s5(u.reshape(nb, t, SSM_W), jnp.zeros((nb, 2 * SSM_S), F32), w,
                       ngroups=1, nchunks=t // steps, steps=steps)
    o_mla = _attn_p(q, k, v, nb=nb, t=t, tq=min(ATTN_P_TQ, t), tk=min(ATTN_P_TK, t))
    mixed_p = _mix(xp, o_ssm.reshape(nb * t, SSM_W), o_mla, w, nb=nb, nt=nt, tm=tm)
    sr_p = s_fin[:, :SSM_S].reshape(1, nb, SSM_G, SSM_P)
    si_p = s_fin[:, SSM_S:].reshape(1, nb, SSM_G, SSM_P)

    ns = db * ds
    ng = db // SUBLANES
    xs_ = _after(x_sample.reshape(ns, D_MODEL), mixed_p[2])
    ct_s, st_s = _rope_tables(PAST_LEN + jnp.tile(jnp.arange(ds), db))
    u_s, qf, qlat, c_s, kr_s, rstd_s = _proj(xs_, ct_s, st_s, w, nb=1, nt=1, tm=ns, sample=True)
    to_scan = lambda a: a.reshape(ng, SUBLANES, ds, -1).transpose(0, 2, 1, 3).reshape(ns, -1)
    from_scan = lambda a: a.reshape(ng, ds, SUBLANES, -1).transpose(0, 2, 1, 3).reshape(ns, -1)
    h0 = jnp.concatenate([state_ssm_re[0].reshape(db, SSM_S), state_ssm_im[0].reshape(db, SSM_S)],
                         axis=-1).astype(F32)
    o_ssm_s, s_fin_s = _s5(to_scan(u_s), h0, w, ngroups=ng, nchunks=1, steps=ds)
    pad_keys = lambda a: jnp.concatenate(
        [a, jnp.zeros((db, NEW_KEYS - ds, a.shape[-1]), a.dtype)], axis=1)
    key_major = lambda a: a.transpose(0, 2, 1)
    attn_in = (page_table,
               qlat.reshape(db, ds * N_HEADS, KV_LORA),
               qf.reshape(db, ds * N_HEADS, LANES)[:, :, NOPE_DIM:QK_DIM],
               pad_keys(c_s.reshape(db, ds, KV_LORA)),
               key_major(pad_keys(kr_s.reshape(db, ds, ROPE_DIM))),
               key_major(pad_keys(rstd_s.reshape(db, ds, N_HEADS))))
    caches = (cache_kv_latent[0], key_major(cache_k_rope[0]), key_major(cache_k_rstd[0]))

    def attn_rows(lo, hi, anchor=None):
        pt, *rest = (a[lo:hi] for a in attn_in)
        if anchor is not None:
            pt = _after(pt, anchor)
        return _attn_s(pt, *rest, *caches, dec_seq=ds)

    o_lat_a = attn_rows(0, db // 2)
    ys_p, dest_p = _moe_experts(mixed_p, w, during_dispatch=o_lat_a)
    o_lat_b = attn_rows(db // 2, db, anchor=ys_p)
    o_lat = jnp.concatenate([o_lat_a, o_lat_b], axis=0)
    o_mla_s = _latent_out(o_lat.reshape(ns, N_HEADS * KV_LORA), w["wuv_h"])

    y_p = _moe_combine(mixed_p, ys_p, dest_p, p_prompt[0].reshape(nb * t, PLE_DIM), w, tm=tm)
    mixed_s = _mix(xs_, from_scan(o_ssm_s), _after(o_mla_s, y_p), w, nb=1, nt=1, tm=ns)
    ys_s, dest_s = _moe_experts(mixed_s, w)
    y_s = _moe_combine(mixed_s, ys_s, dest_s, p_sample[0].reshape(ns, PLE_DIM), w, tm=ns)

    return (y_p.reshape(nb, t, D_MODEL), y_s.reshape(db, ds, D_MODEL),
            c_p.reshape(1, nb, t, KV_LORA), kr_p.reshape(1, nb, t, ROPE_DIM), rstd_p.reshape(1, nb, t, N_HEADS),
            sr_p, si_p,
            c_s.reshape(1, db, ds, KV_LORA), kr_s.reshape(1, db, ds, ROPE_DIM), rstd_s.reshape(1, db, ds, N_HEADS),
            s_fin_s[:, :SSM_S].reshape(1, db, SSM_G, SSM_P), s_fin_s[:, SSM_S:].reshape(1, db, SSM_G, SSM_P))
```
